```python
import math
import functools
import numpy as np
import jax
import jax.numpy as jnp
from jax import lax

D_MODEL = 1024
BATCH = 4
SEQ = 4096
DEPTH = 4
DEC_BATCH = 32
DEC_SEQ = 1
PAST_LEN = 8192
PAGE_SIZE = 128

EPS = 1e-6
N_HYB = (DEPTH + 1) // 2
N_GDN = DEPTH // 2

A_HEADS = 8
A_HEAD_DIM = 64
A_WIDTH = A_HEADS * A_HEAD_DIM
A_PATTERNS = ((128, 1), (512, 4), (2048, 16))
A_WIN_MAX = 2048
A_QBLOCK = 128
REL_BUCKETS = 32
REL_MAX_DIST = 2048

SSM_D_INNER = 1024
SSM_HEAD_DIM = 64
SSM_HEADS = SSM_D_INNER // SSM_HEAD_DIM
SSM_GROUPS = 2
SSM_STATE = 128
SSM_CONV = 4
SSM_CHUNK = 128
SSM_XBC = SSM_D_INNER + 2 * SSM_GROUPS * SSM_STATE

HYB_IN = 3 * A_WIDTH + SSM_D_INNER + SSM_XBC + SSM_HEADS
HYB_MIX = A_WIDTH + SSM_D_INNER

GDN_QK_HEADS = 8
GDN_V_HEADS = 16
GDN_DK = 128
GDN_DV = 128
GDN_CONV = 4
GDN_CHUNK = 64
GDN_QK_W = GDN_QK_HEADS * GDN_DK
GDN_VW = GDN_V_HEADS * GDN_DV
GDN_QKV = 2 * GDN_QK_W + GDN_VW
GDN_IN = GDN_QKV + GDN_VW + 2 * GDN_V_HEADS

D_FF = -(-8 * D_MODEL // (3 * 256)) * 256

kernel_name = 'hybrid_dilated_ssd_gdn_decoder_step'


def rmsnorm(x, g):
    xf = x.astype(jnp.float32)
    y = xf * lax.rsqrt(jnp.mean(xf * xf, axis=-1, keepdims=True) + EPS)
    return (y * g.astype(jnp.float32)).astype(x.dtype)


def l2norm(x):
    return x * lax.rsqrt(jnp.sum(x * x, axis=-1, keepdims=True) + EPS)


def causal_conv(x, buf, w):
    k = w.shape[0]
    L = x.shape[1]
    xp = jnp.concatenate([buf.astype(x.dtype), x], axis=1)
    y = xp[:, 0:L] * w[0]
    for i in range(1, k):
        y = y + xp[:, i:i + L] * w[i]
    return y, xp[:, L:]


def rel_buckets(dist):
    max_exact = REL_BUCKETS // 2
    n = np.maximum(dist, 1).astype(np.float32)
    large = max_exact + (np.log(n / max_exact) / math.log(REL_MAX_DIST / max_exact)
                         * (REL_BUCKETS - max_exact)).astype(np.int32)
    large = np.minimum(large, REL_BUCKETS - 1)
    return np.where(dist < max_exact, dist, large).astype(np.int32)


def dilated_window_attention(q, k_all, v_all, prefix, rel_bias):
    bsz, S, H, dh = q.shape
    qb = A_QBLOCK if S % A_QBLOCK == 0 else S
    nb = S // qb
    scale = dh ** -0.5
    pats = []
    for (w, d) in A_PATTERNS:
        offs = (np.arange(w // d + 1) * d).astype(np.int32)
        bias = rel_bias[rel_buckets(offs)].astype(jnp.float32)
        pats.append((offs, bias.T[None, :, None, :]))
    qblocks = jnp.moveaxis(q.reshape(bsz, nb, qb, H, dh), 1, 0)

    def block(args):
        qblk, b0 = args
        rows = prefix + b0 * qb + jnp.arange(qb, dtype=jnp.int32)
        qf = qblk.astype(jnp.float32) * scale
        ms, ss, nums = [], [], []
        for offs, bias in pats:
            idx = rows[:, None] - offs[None, :]
            valid = idx >= 0
            idxc = jnp.maximum(idx, 0)
            kg = jnp.take(k_all, idxc, axis=1).astype(jnp.float32)
            vg = jnp.take(v_all, idxc, axis=1).astype(jnp.float32)
            logits = jnp.einsum('bqhd,bqjhd->bhqj', qf, kg) + bias
            logits = jnp.where(valid[None, None], logits, -1e30)
            m = jnp.max(logits, axis=-1, keepdims=True)
            p = jnp.exp(logits - m)
            ms.append(m)
            ss.append(jnp.sum(p, axis=-1, keepdims=True))
            nums.append(jnp.einsum('bhqj,bqjhd->bhqd', p, vg))
        m_all = functools.reduce(jnp.maximum, ms)
        wts = [jnp.exp(m - m_all) for m in ms]
        num = functools.reduce(jnp.add, [w * n for w, n in zip(wts, nums)])
        den = functools.reduce(jnp.add, [w * s for w, s in zip(wts, ss)])
        return jnp.swapaxes(num / den, 1, 2)

    out = lax.map(block, (qblocks, jnp.arange(nb, dtype=jnp.int32)))
    return jnp.moveaxis(out, 0, 1).reshape(bsz, S, H, dh).astype(q.dtype)


def ssd_scan(x, dt, bm, cm, a_neg, h0):
    f32 = jnp.float32
    bsz, L = x.shape[0], x.shape[1]
    ch = min(SSM_CHUNK, L)
    nc = -(-L // ch)
    pad = nc * ch - L

    def padl(a):
        return jnp.pad(a.astype(f32), [(0, 0), (0, pad)] + [(0, 0)] * (a.ndim - 2))

    G, HG, P, N = SSM_GROUPS, SSM_HEADS // SSM_GROUPS, SSM_HEAD_DIM, SSM_STATE
    xc = padl(x).reshape(bsz, nc, ch, G, HG, P)
    dtc = padl(dt).reshape(bsz, nc, ch, G, HG)
    bc = padl(bm).reshape(bsz, nc, ch, G, N)
    cc = padl(cm).reshape(bsz, nc, ch, G, N)
    cs = jnp.cumsum(dtc * a_neg.reshape(G, HG), axis=2)
    cst = jnp.moveaxis(cs, 2, -1)
    causal = np.tril(np.ones((ch, ch), dtype=bool))
    seg = jnp.exp(jnp.where(causal, cst[..., :, None] - cst[..., None, :], -jnp.inf))
    cb = jnp.einsum('bcign,bcjgn->bcgij', cc, bc)
    mix = cb[:, :, :, None] * seg * jnp.moveaxis(dtc, 2, -1)[..., None, :]
    y_intra = jnp.einsum('bcghij,bcjghp->bcighp', mix, xc)
    last = cs[:, :, -1]
    w_end = jnp.exp(last[:, :, None] - cs) * dtc
    st = jnp.einsum('bcjgh,bcjgn,bcjghp->bcghpn', w_end, bc, xc)

    def step(h, inp):
        dec, s = inp
        return h * dec[..., None, None] + s, h

    h_fin, h_start = lax.scan(step, h0.astype(f32).reshape(bsz, G, HG, P, N),
                              (jnp.moveaxis(jnp.exp(last), 1, 0), jnp.moveaxis(st, 1, 0)))
    h_start = jnp.moveaxis(h_start, 0, 1)
    y_inter = jnp.einsum('bcign,bcghpn->bcighp', cc, h_start) * jnp.exp(cs)[..., None]
    y = (y_intra + y_inter).reshape(bsz, nc * ch, SSM_HEADS, P)[:, :L]
    return y, h_fin.reshape(bsz, SSM_HEADS, P, N)


def gated_delta_rule(q, k, v, g, beta, s0):
    f32 = jnp.float32
    bsz, L, H = v.shape[0], v.shape[1], v.shape[2]
    dv = v.shape[-1]
    ch = min(GDN_CHUNK, L)
    nc = -(-L // ch)
    pad = nc * ch - L

    def chunks(a):
        a = jnp.pad(a.astype(f32), [(0, 0), (0, pad)] + [(0, 0)] * (a.ndim - 2))
        return jnp.moveaxis(a.reshape((bsz, nc, ch) + a.shape[2:]), 3, 1)

    qc, kc, vc, gc, bc = (chunks(a) for a in (q, k, v, g, beta))
    gcum = jnp.cumsum(gc, axis=-1)
    incl = np.tril(np.ones((ch, ch), dtype=bool))
    strict = np.tril(np.ones((ch, ch), dtype=bool), k=-1)
    dec_incl = jnp.exp(jnp.where(incl, gcum[..., :, None] - gcum[..., None, :], -jnp.inf))
    dec_strict = jnp.where(strict, dec_incl, 0.0)
    kb = kc * bc[..., None]
    a_mat = jnp.einsum('bhcid,bhcjd->bhcij', kb, kc) * dec_strict + jnp.eye(ch, dtype=f32)
    rhs = jnp.concatenate([vc * bc[..., None], kb * jnp.exp(gcum)[..., None]], axis=-1)
    sol = lax.linalg.triangular_solve(a_mat, rhs, left_side=True, lower=True, unit_diagonal=True)
    u0, kcd = sol[..., :dv], sol[..., dv:]
    attn = jnp.einsum('bhcid,bhcjd->bhcij', qc, kc) * dec_incl
    qg = qc * jnp.exp(gcum)[..., None]
    kdec = kc * jnp.exp(gcum[..., -1:] - gcum)[..., None]
    glast = jnp.exp(gcum[..., -1])

    def step(s, inp):
        u0_c, kcd_c, attn_c, qg_c, kdec_c, gl_c = inp
        u = u0_c - jnp.einsum('bhik,bhkv->bhiv', kcd_c, s)
        o = jnp.einsum('bhik,bhkv->bhiv', qg_c, s) + jnp.einsum('bhij,bhjv->bhiv', attn_c, u)
        s = s * gl_c[..., None, None] + jnp.einsum('bhjk,bhjv->bhkv', kdec_c, u)
        return s, o

    xs = tuple(jnp.moveaxis(a, 2, 0) for a in (u0, kcd, attn, qg, kdec, glast))
    s_fin, o = lax.scan(step, s0.astype(f32), xs)
    o = jnp.moveaxis(jnp.moveaxis(o, 0, 2), 1, 3).reshape(bsz, nc * ch, H, dv)[:, :L]
    return o, s_fin


def hybrid_mixer(h, k_pre, v_pre, conv_buf, h0, keep, rel_bias, w_in, conv_w, conv_b,
                 dt_bias, a_log, d_skip, norm_w, w_out):
    f32 = jnp.float32
    bsz, L, _ = h.shape
    proj = h @ w_in
    q, k, v, z, xbc, dt_raw = jnp.split(
        proj, [A_WIDTH, 2 * A_WIDTH, 3 * A_WIDTH, 3 * A_WIDTH + SSM_D_INNER,
               3 * A_WIDTH + SSM_D_INNER + SSM_XBC], axis=-1)
    shp = (bsz, L, A_HEADS, A_HEAD_DIM)
    k_all = jnp.concatenate([k_pre.astype(h.dtype), k.reshape(shp)], axis=1)
    v_all = jnp.concatenate([v_pre.astype(h.dtype), v.reshape(shp)], axis=1)
    o_attn = dilated_window_attention(q.reshape(shp), k_all, v_all, k_pre.shape[1], rel_bias)
    xbc, new_conv = causal_conv(xbc, conv_buf, conv_w)
    xbc = jax.nn.silu((xbc + conv_b).astype(f32))
    xs, bm, cm = jnp.split(xbc, [SSM_D_INNER, SSM_D_INNER + SSM_GROUPS * SSM_STATE], axis=-1)
    xs = xs.reshape(bsz, L, SSM_HEADS, SSM_HEAD_DIM)
    dt = jax.nn.softplus(dt_raw.astype(f32) + dt_bias.astype(f32))
    y, h_fin = ssd_scan(xs, dt, bm.reshape(bsz, L, SSM_GROUPS, SSM_STATE),
                        cm.reshape(bsz, L, SSM_GROUPS, SSM_STATE), -jnp.exp(a_log.astype(f32)), h0)
    y = (y + d_skip.astype(f32)[:, None] * xs).reshape(bsz, L, SSM_D_INNER) * jax.nn.silu(z.astype(f32))
    y = y.reshape(bsz, L, SSM_GROUPS, SSM_D_INNER // SSM_GROUPS)
    y = y * lax.rsqrt(jnp.mean(y * y, axis=-1, keepdims=True) + EPS)
    y = y.reshape(bsz, L, SSM_D_INNER) * norm_w.astype(f32)
    mixed = jnp.concatenate([o_attn.reshape(bsz, L, A_WIDTH), y.astype(h.dtype)], axis=-1)
    return mixed @ w_out, k_all[:, -keep:], v_all[:, -keep:], new_conv, h_fin


def gdn_mixer(h, conv_buf, s0, w_in, conv_w, dt_bias, a_log, norm_w, w_out):
    f32 = jnp.float32
    bsz, L, _ = h.shape
    proj = h @ w_in
    qkv, z, b_raw, a_raw = jnp.split(
        proj, [GDN_QKV, GDN_QKV + GDN_VW, GDN_QKV + GDN_VW + GDN_V_HEADS], axis=-1)
    qkv, new_conv = causal_conv(qkv, conv_buf, conv_w)
    qkv = jax.nn.silu(qkv.astype(f32))
    q, k, v = jnp.split(qkv, [GDN_QK_W, 2 * GDN_QK_W], axis=-1)
    rep = GDN_V_HEADS // GDN_QK_HEADS
    q = jnp.repeat(l2norm(q.reshape(bsz, L, GDN_QK_HEADS, GDN_DK)) * GDN_DK ** -0.5, rep, axis=2)
    k = jnp.repeat(l2norm(k.reshape(bsz, L, GDN_QK_HEADS, GDN_DK)), rep, axis=2)
    v = v.reshape(bsz, L, GDN_V_HEADS, GDN_DV)
    beta = jax.nn.sigmoid(b_raw.astype(f32))
    g = -jnp.exp(a_log.astype(f32)) * jax.nn.softplus(a_raw.astype(f32) + dt_bias.astype(f32))
    o, s_fin = gated_delta_rule(q, k, v, g, beta, s0)
    o = rmsnorm(o, norm_w) * jax.nn.silu(z.astype(f32).reshape(bsz, L, GDN_V_HEADS, GDN_DV))
    return o.reshape(bsz, L, GDN_VW).astype(h.dtype) @ w_out, new_conv, s_fin


def swiglu(h, w_gate, w_up, w_down):
    return (jax.nn.silu(h @ w_gate) * (h @ w_up)) @ w_down


def setup_inputs(seed: int = 0) -> dict:
    key = jax.random.key(seed)
    ks = jax.random.split(key, 40)
    f32 = jnp.float32
    win_buf = min(A_WIN_MAX, PAST_LEN)

    def nrm(k, shape, scale):
        return jax.random.normal(k, shape, f32) * scale

    def gain(k, shape):
        return 1.0 + 0.02 * jax.random.normal(k, shape, f32)

    def dt_bias_init(k, shape):
        dt = jnp.exp(jax.random.uniform(k, shape, f32, math.log(1e-3), math.log(1e-1)))
        return dt + jnp.log(-jnp.expm1(-dt))

    def a_log_init(k, shape):
        return jnp.log(jax.random.uniform(k, shape, f32, 1.0, 16.0))

    return {
        'x_prompt': nrm(ks[0], (BATCH, SEQ, D_MODEL), 1.0),
        'x_sample': nrm(ks[1], (DEC_BATCH, DEC_SEQ, D_MODEL), 1.0),
        'cache_attn_k': nrm(ks[2], (N_HYB, DEC_BATCH, win_buf, A_HEADS, A_HEAD_DIM), 1.0),
        'cache_attn_v': nrm(ks[3], (N_HYB, DEC_BATCH, win_buf, A_HEADS, A_HEAD_DIM), 1.0),
        'state_ssm_conv': nrm(ks[4], (N_HYB, DEC_BATCH, SSM_CONV - 1, SSM_XBC), 1.0),
        'state_ssm': nrm(ks[5], (N_HYB, DEC_BATCH, SSM_HEADS, SSM_HEAD_DIM, SSM_STATE), 0.1),
        'state_gdn_conv': nrm(ks[6], (N_GDN, DEC_BATCH, GDN_CONV - 1, GDN_QKV), 1.0),
        'state_gdn': nrm(ks[7], (N_GDN, DEC_BATCH, GDN_V_HEADS, GDN_DK, GDN_DV), 0.1),
        'rel_bias': nrm(ks[8], (REL_BUCKETS, A_HEADS), 0.5),
        'norm_mix_pre': gain(ks[9], (DEPTH, D_MODEL)),
        'norm_mix_post': gain(ks[10], (DEPTH, D_MODEL)),
        'norm_ffn_pre': gain(ks[11], (DEPTH, D_MODEL)),
        'norm_ffn_post': gain(ks[12], (DEPTH, D_MODEL)),
        'w_hyb_in': nrm(ks[13], (N_HYB, D_MODEL, HYB_IN), D_MODEL ** -0.5),
        'ssm_conv_w': nrm(ks[14], (N_HYB, SSM_CONV, SSM_XBC), SSM_CONV ** -0.5),
        'ssm_conv_b': nrm(ks[15], (N_HYB, SSM_XBC), 0.02),
        'ssm_dt_bias': dt_bias_init(ks[16], (N_HYB, SSM_HEADS)),
        'ssm_a_log': a_log_init(ks[17], (N_HYB, SSM_HEADS)),
        'ssm_d': 1.0 + 0.1 * jax.random.normal(ks[18], (N_HYB, SSM_HEADS), f32),
        'ssm_norm_w': gain(ks[19], (N_HYB, SSM_D_INNER)),
        'w_hyb_out': nrm(ks[20], (N_HYB, HYB_MIX, D_MODEL), HYB_MIX ** -0.5),
        'w_gdn_in': nrm(ks[21], (N_GDN, D_MODEL, GDN_IN), D_MODEL ** -0.5),
        'gdn_conv_w': nrm(ks[22], (N_GDN, GDN_CONV, GDN_QKV), GDN_CONV ** -0.5),
        'gdn_dt_bias': dt_bias_init(ks[23], (N_GDN, GDN_V_HEADS)),
        'gdn_a_log': a_log_init(ks[24], (N_GDN, GDN_V_HEADS)),
        'gdn_norm_w': gain(ks[25], (N_GDN, GDN_DV)),
        'w_gdn_out': nrm(ks[26], (N_GDN, GDN_VW, D_MODEL), GDN_VW ** -0.5),
        'w_ffn_gate': nrm(ks[27], (DEPTH, D_MODEL, D_FF), D_MODEL ** -0.5),
        'w_ffn_up': nrm(ks[28], (DEPTH, D_MODEL, D_FF), D_MODEL ** -0.5),
        'w_ffn_down': nrm(ks[29], (DEPTH, D_FF, D_MODEL), D_FF ** -0.5),
    }


def reference(x_prompt, x_sample, cache_attn_k, cache_attn_v, state_ssm_conv, state_ssm,
              state_gdn_conv, state_gdn, rel_bias, norm_mix_pre, norm_mix_post, norm_ffn_pre,
              norm_ffn_post, w_hyb_in, ssm_conv_w, ssm_conv_b, ssm_dt_bias, ssm_a_log, ssm_d,
              ssm_norm_w, w_hyb_out, w_gdn_in, gdn_conv_w, gdn_dt_bias, gdn_a_log, gdn_norm_w,
              w_gdn_out, w_ffn_gate, w_ffn_up, w_ffn_down):

    def trunk(x, k_pre, v_pre, sconv, sssm, gconv, gstate, keep):
        nk, nv, nsc, nss, ngc, ngs = [], [], [], [], [], []
        for l in range(DEPTH):
            i = l // 2
            h = rmsnorm(x, norm_mix_pre[l])
            if l % 2 == 0:
                m, k_new, v_new, c_new, s_new = hybrid_mixer(
                    h, k_pre[i], v_pre[i], sconv[i], sssm[i], keep, rel_bias, w_hyb_in[i],
                    ssm_conv_w[i], ssm_conv_b[i], ssm_dt_bias[i], ssm_a_log[i], ssm_d[i],
                    ssm_norm_w[i], w_hyb_out[i])
                nk.append(k_new)
                nv.append(v_new)
                nsc.append(c_new)
                nss.append(s_new)
            else:
                m, c_new, s_new = gdn_mixer(h, gconv[i], gstate[i], w_gdn_in[i], gdn_conv_w[i],
                                            gdn_dt_bias[i], gdn_a_log[i], gdn_norm_w[i], w_gdn_out[i])
                ngc.append(c_new)
                ngs.append(s_new)
            x = x + rmsnorm(m, norm_mix_post[l])
            f = swiglu(rmsnorm(x, norm_ffn_pre[l]), w_ffn_gate[l], w_ffn_up[l], w_ffn_down[l])
            x = x + rmsnorm(f, norm_ffn_post[l])
        return (x, jnp.stack(nk), jnp.stack(nv), jnp.stack(nsc), jnp.stack(nss),
                jnp.stack(ngc), jnp.stack(ngs))

    bsz, s_len = x_prompt.shape[0], x_prompt.shape[1]
    dt_p = x_prompt.dtype
    p_k0 = jnp.zeros((N_HYB, bsz, 0, A_HEADS, A_HEAD_DIM), dt_p)
    p_sc0 = jnp.zeros((N_HYB, bsz, SSM_CONV - 1, SSM_XBC), dt_p)
    p_ss0 = jnp.zeros((N_HYB, bsz, SSM_HEADS, SSM_HEAD_DIM, SSM_STATE), jnp.float32)
    p_gc0 = jnp.zeros((N_GDN, bsz, GDN_CONV - 1, GDN_QKV), dt_p)
    p_gs0 = jnp.zeros((N_GDN, bsz, GDN_V_HEADS, GDN_DK, GDN_DV), jnp.float32)
    y_prompt, pk, pv, psc, pss, pgc, pgs = trunk(
        x_prompt, p_k0, p_k0, p_sc0, p_ss0, p_gc0, p_gs0, min(A_WIN_MAX, s_len))
    y_sample, sk, sv, ssc, sss, sgc, sgs = trunk(
        x_sample, cache_attn_k, cache_attn_v, state_ssm_conv, state_ssm, state_gdn_conv,
        state_gdn, cache_attn_k.shape[2])
    return (y_prompt, y_sample, pk, pv, psc, pss, pgc, pgs, sk, sv, ssc, sss, sgc, sgs)
```

```python
import functools
import math

import numpy as np
import jax
import jax.numpy as jnp
from jax import lax
from jax.experimental import pallas as pl
from jax.experimental.pallas import tpu as pltpu

F32 = jnp.float32
BF16 = jnp.bfloat16
EPS = 1e-6
NEG = -1e30
HIGHEST = lax.Precision.HIGHEST

VMEM_LIMIT_BYTES = 56 * 1024 * 1024
LANES = 128

A_HEADS = 8
A_HEAD_DIM = 64
A_WIDTH = A_HEADS * A_HEAD_DIM
A_PATTERNS = ((128, 1), (512, 4), (2048, 16))
A_BAND = 128
REL_BUCKETS = 32
REL_MAX_DIST = 2048

SSM_D_INNER = 1024
SSM_HEAD_DIM = 64
SSM_HEADS = SSM_D_INNER // SSM_HEAD_DIM
SSM_GROUPS = 2
SSM_STATE = 128
SSM_CONV = 4
SSM_CHUNK = 128
SSM_BC = 2 * SSM_GROUPS * SSM_STATE
SSM_XBC = SSM_D_INNER + SSM_BC

GDN_QK_HEADS = 8
GDN_V_HEADS = 16
GDN_DK = 128
GDN_DV = 128
GDN_CONV = 4
GDN_CHUNK = 64
GDN_QK_W = GDN_QK_HEADS * GDN_DK
GDN_VW = GDN_V_HEADS * GDN_DV
GDN_QKV = 2 * GDN_QK_W + GDN_VW

HYB_MAIN = 2 * SSM_D_INNER + 3 * A_WIDTH + SSM_BC
HYB_Q0 = 2 * SSM_D_INNER
HYB_K0 = HYB_Q0 + A_WIDTH
HYB_V0 = HYB_K0 + A_WIDTH
HYB_BC0 = HYB_V0 + A_WIDTH
GDN_MAIN = GDN_QKV + GDN_VW


def _cparams(*sem):
    return pltpu.CompilerParams(dimension_semantics=sem, vmem_limit_bytes=VMEM_LIMIT_BYTES)


def _rms(x):
    return x * lax.rsqrt(jnp.mean(x * x, axis=-1, keepdims=True) + EPS)


def _dot(a, b, **kw):
    return jnp.dot(a, b, preferred_element_type=F32, **kw)


def _dot_nt(a, b, **kw):
    return lax.dot_general(a, b, (((1,), (1,)), ((), ())), preferred_element_type=F32, **kw)


def _dot_tn(a, b, **kw):
    return lax.dot_general(a, b, (((0,), (0,)), ((), ())), preferred_element_type=F32, **kw)


def _iota(shape, dim):
    return lax.broadcasted_iota(jnp.int32, shape, dim)


def _inproj_kernel(x_ref, g_ref, w_ref, wt_ref, o_ref, t_ref, h_ref):
    @pl.when(pl.program_id(1) == 0)
    def _():
        hb = (_rms(x_ref[...]) * g_ref[...]).astype(BF16)
        h_ref[...] = hb
        t_ref[...] = _dot(hb, wt_ref[...])

    o_ref[...] = _dot(h_ref[...], w_ref[...])


def inproj(x, g, w_main, w_tail, *, tm, tn):
    m, d = x.shape
    n = w_main.shape[1]
    return pl.pallas_call(
        _inproj_kernel,
        grid=(m // tm, n // tn),
        in_specs=[
            pl.BlockSpec((tm, d), lambda i, j: (i, 0)),
            pl.BlockSpec((1, d), lambda i, j: (0, 0)),
            pl.BlockSpec((d, tn), lambda i, j: (0, j)),
            pl.BlockSpec((d, LANES), lambda i, j: (0, 0)),
        ],
        out_specs=[
            pl.BlockSpec((tm, tn), lambda i, j: (i, j)),
            pl.BlockSpec((tm, LANES), lambda i, j: (i, 0)),
        ],
        out_shape=[jax.ShapeDtypeStruct((m, n), F32), jax.ShapeDtypeStruct((m, LANES), F32)],
        scratch_shapes=[pltpu.VMEM((tm, d), BF16)],
        compiler_params=_cparams("parallel", "arbitrary"),
        name="inproj",
    )(x, g.reshape(1, d), w_main, w_tail)


def _outproj_kernel(*refs, n_in):
    a_refs, w_refs = refs[:n_in], refs[n_in:2 * n_in]
    x_ref, g_ref, o_ref = refs[2 * n_in:]
    acc = None
    for a_ref, w_ref in zip(a_refs, w_refs):
        t = _dot(a_ref[...].astype(BF16), w_ref[...])
        acc = t if acc is None else acc + t
    o_ref[...] = x_ref[...] + _rms(acc) * g_ref[...]


def outproj(acts, weights, x, g, *, tm):
    m, d = x.shape
    n_in = len(acts)
    in_specs = [pl.BlockSpec((tm, a.shape[1]), lambda i: (i, 0)) for a in acts]
    in_specs += [pl.BlockSpec(w.shape, lambda i: (0, 0)) for w in weights]
    in_specs += [pl.BlockSpec((tm, d), lambda i: (i, 0)), pl.BlockSpec((1, d), lambda i: (0, 0))]
    return pl.pallas_call(
        functools.partial(_outproj_kernel, n_in=n_in),
        grid=(m // tm,),
        in_specs=in_specs,
        out_specs=pl.BlockSpec((tm, d), lambda i: (i, 0)),
        out_shape=jax.ShapeDtypeStruct((m, d), F32),
        compiler_params=_cparams("parallel"),
        name="outproj",
    )(*acts, *weights, x, g.reshape(1, d))


def _ffn_kernel(x_ref, g1_ref, wg_ref, wu_ref, wd_ref, g2_ref, o_ref, h_ref, acc_ref):
    j = pl.program_id(1)

    @pl.when(j == 0)
    def _():
        h_ref[...] = (_rms(x_ref[...]) * g1_ref[...]).astype(BF16)
        acc_ref[...] = jnp.zeros_like(acc_ref)

    h = h_ref[...]
    a = jax.nn.silu(_dot(h, wg_ref[...])) * _dot(h, wu_ref[...])
    acc_ref[...] += _dot(a.astype(BF16), wd_ref[...])

    @pl.when(j == pl.num_programs(1) - 1)
    def _():
        o_ref[...] = x_ref[...] + _rms(acc_ref[...]) * g2_ref[...]


def ffn(x, g1, wg, wu, wd, g2, *, tm, tf):
    m, d = x.shape
    dff = wg.shape[1]
    return pl.pallas_call(
        _ffn_kernel,
        grid=(m // tm, dff // tf),
        in_specs=[
            pl.BlockSpec((tm, d), lambda i, j: (i, 0)),
            pl.BlockSpec((1, d), lambda i, j: (0, 0)),
            pl.BlockSpec((d, tf), lambda i, j: (0, j)),
            pl.BlockSpec((d, tf), lambda i, j: (0, j)),
            pl.BlockSpec((tf, d), lambda i, j: (j, 0)),
            pl.BlockSpec((1, d), lambda i, j: (0, 0)),
        ],
        out_specs=pl.BlockSpec((tm, d), lambda i, j: (i, 0)),
        out_shape=jax.ShapeDtypeStruct((m, d), F32),
        scratch_shapes=[pltpu.VMEM((tm, d), BF16), pltpu.VMEM((tm, d), F32)],
        compiler_params=_cparams("parallel", "arbitrary"),
        name="ffn",
    )(x, g1.reshape(1, d), wg, wu, wd, g2.reshape(1, d))


def _rel_buckets(dist):
    max_exact = REL_BUCKETS // 2
    n = np.maximum(dist, 1).astype(np.float32)
    large = max_exact + (np.log(n / max_exact) / math.log(REL_MAX_DIST / max_exact)
                         * (REL_BUCKETS - max_exact)).astype(np.int32)
    large = np.minimum(large, REL_BUCKETS - 1)
    return np.where(dist < max_exact, dist, large).astype(np.int32)


def _attn_bias_tiles(rel_bias):
    r = np.arange(A_BAND)[:, None]
    c = np.arange(2 * A_BAND)[None, :]
    t = np.where(c < A_BAND, A_BAND + r - c, r - (c - A_BAND))
    valid = (t >= 0) & (t <= A_BAND)
    tiles = []
    for (_, d) in A_PATTERNS:
        b = rel_bias[_rel_buckets(np.maximum(t, 0) * d)]
        tiles.append(jnp.where(valid[..., None], b.astype(F32), NEG))
    tl = jnp.stack(tiles)
    tl = jnp.transpose(tl, (3, 0, 1, 2))
    tl = tl.reshape(A_HEADS // 2, 2, len(A_PATTERNS), A_BAND, 2 * A_BAND)
    return jnp.transpose(tl, (0, 2, 1, 3, 4))


def _attn_kernel(q_ref, k_ref, v_ref, b_ref, o_ref, m0_ref, m1_ref, l0_ref, l1_ref, acc_ref, *, seq):
    scale = A_HEAD_DIM ** -0.5
    n_tiles = seq // A_BAND
    lane = _iota((A_BAND, LANES), 1)
    head0 = lane < A_HEAD_DIM
    m_refs, l_refs = (m0_ref, m1_ref), (l0_ref, l1_ref)
    for h in range(2):
        m_refs[h][...] = jnp.full(m_refs[h].shape, NEG, F32)
        l_refs[h][...] = jnp.zeros_like(l_refs[h])
    acc_ref[...] = jnp.zeros_like(acc_ref)

    for p, (_, d) in enumerate(A_PATTERNS):
        tiles_per_class = n_tiles // d

        def tile_step(idx, carry, p=p, d=d, tiles_per_class=tiles_per_class):
            r = idx // tiles_per_class
            t = idx % tiles_per_class
            start = r + t * (d * A_BAND)
            has_prev = t > 0
            prev = jnp.where(has_prev, start - d * A_BAND, start)
            if d > 1:
                rows, prows = pl.ds(start, A_BAND, stride=d), pl.ds(prev, A_BAND, stride=d)
            else:
                rows, prows = pl.ds(pl.multiple_of(start, A_BAND), A_BAND), pl.ds(pl.multiple_of(prev, A_BAND), A_BAND)
            q = q_ref[rows, :] * scale
            k2 = jnp.concatenate([k_ref[prows, :], k_ref[rows, :]], axis=0).astype(BF16)
            v2 = jnp.concatenate([v_ref[prows, :], v_ref[rows, :]], axis=0).astype(BF16)
            col = _iota((A_BAND, 2 * A_BAND), 1)
            dead = jnp.logical_and(col < A_BAND, jnp.logical_not(has_prev))
            acc_old = acc_ref[rows, :]
            new_acc = []
            for h in range(2):
                qh = jnp.where(head0 if h == 0 else jnp.logical_not(head0), q, 0.0).astype(BF16)
                s = _dot_nt(qh, k2) + jnp.where(dead, NEG, b_ref[0, p, h])
                m_old = m_refs[h][rows, :]
                l_old = l_refs[h][rows, :]
                m_new = jnp.maximum(m_old, jnp.max(s, axis=-1, keepdims=True))
                alpha = jnp.exp(m_old - m_new)
                pr = jnp.exp(s - m_new[:, :1])
                l_refs[h][rows, :] = alpha * l_old + jnp.sum(pr, axis=-1, keepdims=True)
                m_refs[h][rows, :] = m_new
                new_acc.append(alpha * acc_old + _dot(pr.astype(BF16), v2))
            acc_ref[rows, :] = jnp.where(head0, new_acc[0], new_acc[1])
            return carry

        lax.fori_loop(0, n_tiles, tile_step, 0)

    lane_s = _iota((seq, LANES), 1)
    o_ref[...] = acc_ref[...] / jnp.where(lane_s < A_HEAD_DIM, l0_ref[...], l1_ref[...])


def attention_prompt(proj, bias_tiles, *, bsz, seq):
    hp = A_HEADS // 2
    qb, kb, vb = HYB_Q0 // LANES, HYB_K0 // LANES, HYB_V0 // LANES
    return pl.pallas_call(
        functools.partial(_attn_kernel, seq=seq),
        grid=(bsz, hp),
        in_specs=[
            pl.BlockSpec((None, seq, LANES), lambda b, h: (b, 0, qb + h)),
            pl.BlockSpec((None, seq, LANES), lambda b, h: (b, 0, kb + h)),
            pl.BlockSpec((None, seq, LANES), lambda b, h: (b, 0, vb + h)),
            pl.BlockSpec((1,) + bias_tiles.shape[1:], lambda b, h: (h, 0, 0, 0, 0)),
        ],
        out_specs=pl.BlockSpec((None, seq, LANES), lambda b, h: (b, 0, h)),
        out_shape=jax.ShapeDtypeStruct((bsz, seq, A_WIDTH), F32),
        scratch_shapes=[pltpu.VMEM((seq, LANES), F32)] * 5,
        compiler_params=_cparams("parallel", "parallel"),
        name="attn_prompt",
    )(proj, proj, proj, bias_tiles)


def _attn_logw(rel_bias, past):
    dist = np.arange(past + 1)
    count = np.zeros(past + 1, np.float64)
    for (w, d) in A_PATTERNS:
        count += ((dist % d == 0) & (dist <= w)).astype(np.float64)
    logc = np.where(count > 0, np.log(np.maximum(count, 1.0)), 0.0).astype(np.float32)
    lw = rel_bias[_rel_buckets(dist)].astype(F32).T + logc[None, :]
    return jnp.where((count > 0)[None, :], lw, NEG)


def _attn_step_kernel(q_ref, kn_ref, vn_ref, kc_ref, vc_ref, lwc_ref, lwn_ref, o_ref, ko_ref, vo_ref, *, past):
    scale = A_HEAD_DIM ** -0.5
    headmask = _iota((A_HEADS, A_WIDTH), 1) // A_HEAD_DIM == _iota((A_HEADS, A_WIDTH), 0)
    q8 = jnp.where(headmask, q_ref[0] * scale, 0.0)
    kn = kn_ref[0]
    vn = vn_ref[0]
    s = _dot_nt(q8, kc_ref[0]) + lwc_ref[...]
    s_new = jnp.sum(q8 * kn, axis=-1, keepdims=True) + lwn_ref[:, :1]
    m = jnp.maximum(jnp.max(s, axis=-1, keepdims=True), s_new)
    p = jnp.exp(s - m)
    p_new = jnp.exp(s_new - m)
    den = jnp.sum(p, axis=-1, keepdims=True) + p_new
    o8 = (_dot(p, vc_ref[0]) + p_new * vn) / den
    o_ref[0] = jnp.sum(jnp.where(headmask, o8, 0.0), axis=0, keepdims=True)
    step = 256
    for r0 in range(0, past - 1, step):
        n = min(step, past - 1 - r0)
        ko_ref[0, pl.ds(r0, n), :] = kc_ref[0, pl.ds(r0 + 1, n), :]
        vo_ref[0, pl.ds(r0, n), :] = vc_ref[0, pl.ds(r0 + 1, n), :]
    ko_ref[0, pl.ds(past - 1, 1), :] = kn
    vo_ref[0, pl.ds(past - 1, 1), :] = vn


def attention_step(proj3, cache_k, cache_v, logw):
    bsz, past, w = cache_k.shape
    lw_cache = logw[:, past:0:-1]
    lw_new = jnp.broadcast_to(logw[:, :1], (A_HEADS, LANES))
    blk = lambda c0: pl.BlockSpec((1, 1, w), lambda b: (b, 0, c0 // w))
    cspec = pl.BlockSpec((1, past, w), lambda b: (b, 0, 0))
    return pl.pallas_call(
        functools.partial(_attn_step_kernel, past=past),
        grid=(bsz,),
        in_specs=[blk(HYB_Q0), blk(HYB_K0), blk(HYB_V0), cspec, cspec,
                  pl.BlockSpec((A_HEADS, past), lambda b: (0, 0)),
                  pl.BlockSpec((A_HEADS, LANES), lambda b: (0, 0))],
        out_specs=[pl.BlockSpec((1, 1, w), lambda b: (b, 0, 0)), cspec, cspec],
        out_shape=[jax.ShapeDtypeStruct((bsz, 1, w), F32),
                   jax.ShapeDtypeStruct((bsz, past, w), F32),
                   jax.ShapeDtypeStruct((bsz, past, w), F32)],
        compiler_params=_cparams("parallel"),
        name="attn_step",
    )(proj3, proj3, proj3, cache_k, cache_v, lw_cache, lw_new)


def _group_rms(y, w):
    half = SSM_D_INNER // SSM_GROUPS
    return [_rms(y[:, g * half:(g + 1) * half]) * w[:, g * half:(g + 1) * half] for g in range(SSM_GROUPS)]


def _ssd_kernel(z_ref, xs_ref, bc_ref, dt_ref, c0_ref, cw_ref, cb_ref, dtb_ref, alog_ref, d_ref, nw_ref,
                h0_ref, y_ref, co_ref, h_ref, xpad_ref, ys_ref):
    c = pl.program_id(1)
    ch = SSM_CHUNK
    base = 8
    lo = base - (SSM_CONV - 1)

    @pl.when(c == 0)
    def _():
        xpad_ref[lo:base, :] = c0_ref[0]
        h_ref[...] = h0_ref[...]

    xpad_ref[base:base + ch, 0:SSM_D_INNER] = xs_ref[0]
    xpad_ref[base:base + ch, SSM_D_INNER:SSM_XBC] = bc_ref[0]
    conv = cb_ref[...] + cw_ref[0:1, :] * xpad_ref[lo:lo + ch, :]
    for i in range(1, SSM_CONV):
        conv = conv + cw_ref[i:i + 1, :] * xpad_ref[lo + i:lo + i + ch, :]
    tail = xpad_ref[base + ch - (SSM_CONV - 1):base + ch, :]
    co_ref[0] = tail
    xpad_ref[lo:base, :] = tail
    xbc = jax.nn.silu(conv)
    xs = xbc[:, :SSM_D_INNER]

    dt = jax.nn.softplus(dt_ref[0] + dtb_ref[...])
    da = dt * (-jnp.exp(alog_ref[...]))
    row = _iota((ch, ch), 0)
    colv = _iota((ch, ch), 1)
    tril = (row >= colv).astype(F32)
    cs = _dot(tril, da, precision=HIGHEST)
    cs_t = cs.T
    causal = row >= colv

    for g in range(SSM_GROUPS):
        bm = xbc[:, SSM_D_INNER + g * SSM_STATE:SSM_D_INNER + (g + 1) * SSM_STATE]
        cm = xbc[:, SSM_D_INNER + (SSM_GROUPS + g) * SSM_STATE:SSM_D_INNER + (SSM_GROUPS + g + 1) * SSM_STATE]
        cb = _dot_nt(cm, bm)
        for hh in range(SSM_HEADS // SSM_GROUPS):
            h = g * (SSM_HEADS // SSM_GROUPS) + hh
            cs_col = cs[:, h:h + 1]
            cs_row = cs_t[h:h + 1, :]
            last = cs[ch - 1:ch, h:h + 1]
            seg = jnp.exp(jnp.where(causal, cs_col - cs_row, NEG))
            x_h = xs[:, h * SSM_HEAD_DIM:(h + 1) * SSM_HEAD_DIM]
            xdt = x_h * dt[:, h:h + 1]
            h_prev = h_ref[0, h]
            y_h = _dot(cb * seg, xdt) + _dot_nt(cm, h_prev) * jnp.exp(cs_col) + d_ref[:, h:h + 1] * x_h
            ys_ref[:, h * SSM_HEAD_DIM:(h + 1) * SSM_HEAD_DIM] = y_h
            st = _dot_tn(xdt * jnp.exp(last - cs_col), bm)
            h_ref[0, h] = h_prev * jnp.exp(last) + st

    y = ys_ref[...] * jax.nn.silu(z_ref[0])
    half = SSM_D_INNER // SSM_GROUPS
    for g, yg in enumerate(_group_rms(y, nw_ref[...])):
        y_ref[0, :, g * half:(g + 1) * half] = yg


def _small(a):
    return pl.BlockSpec(a.shape, lambda *_: (0,) * a.ndim)


def ssd_prompt(main, tail, conv0, h0, conv_w, conv_b, dt_bias, a_log, d_skip, norm_w, *, bsz, seq):
    nc = seq // SSM_CHUNK
    ch = SSM_CHUNK
    small = [conv_w, conv_b.reshape(1, -1), _pad_tail(dt_bias.reshape(1, -1)), _pad_tail(a_log.reshape(1, -1)),
             _pad_tail(d_skip.reshape(1, -1)), norm_w.reshape(1, -1)]
    return pl.pallas_call(
        _ssd_kernel,
        grid=(bsz, nc),
        in_specs=[
            pl.BlockSpec((1, ch, SSM_D_INNER), lambda b, c: (b, c, 0)),
            pl.BlockSpec((1, ch, SSM_D_INNER), lambda b, c: (b, c, 1)),
            pl.BlockSpec((1, ch, SSM_BC), lambda b, c: (b, c, HYB_BC0 // SSM_BC)),
            pl.BlockSpec((1, ch, LANES), lambda b, c: (b, c, 0)),
            pl.BlockSpec((1, SSM_CONV - 1, SSM_XBC), lambda b, c: (b, 0, 0)),
        ] + [_small(a) for a in small] + [
            pl.BlockSpec((1, SSM_HEADS, SSM_HEAD_DIM, SSM_STATE), lambda b, c: (b, 0, 0, 0)),
        ],
        out_specs=[
            pl.BlockSpec((1, ch, SSM_D_INNER), lambda b, c: (b, c, 0)),
            pl.BlockSpec((1, SSM_CONV - 1, SSM_XBC), lambda b, c: (b, 0, 0)),
            pl.BlockSpec((1, SSM_HEADS, SSM_HEAD_DIM, SSM_STATE), lambda b, c: (b, 0, 0, 0)),
        ],
        out_shape=[jax.ShapeDtypeStruct((bsz, seq, SSM_D_INNER), F32),
                   jax.ShapeDtypeStruct((bsz, SSM_CONV - 1, SSM_XBC), F32),
                   jax.ShapeDtypeStruct((bsz, SSM_HEADS, SSM_HEAD_DIM, SSM_STATE), F32)],
        scratch_shapes=[pltpu.VMEM((8 + ch, SSM_XBC), F32), pltpu.VMEM((ch, SSM_D_INNER), F32)],
        compiler_params=_cparams("parallel", "arbitrary"),
        name="ssd_prompt",
    )(main, main, main, tail, conv0, *small, h0)


def _row_to_col(row, eye):
    return jnp.sum(jnp.where(eye, row, 0.0), axis=1, keepdims=True)


def _col_to_row(col, eye):
    return jnp.sum(jnp.where(eye, col, 0.0), axis=0, keepdims=True)


def _conv_step(c0_ref, w_ref, x_row, c0, c1):
    acc = w_ref[SSM_CONV - 1:SSM_CONV, c0:c1] * x_row
    for i in range(SSM_CONV - 1):
        acc = acc + w_ref[i:i + 1, c0:c1] * c0_ref[0, i:i + 1, c0:c1]
    return acc


def _ssd_step_kernel(z_ref, xs_ref, bc_ref, dt_ref, c0_ref, cw_ref, cb_ref, dtb_ref, alog_ref, d_ref, nw_ref,
                     h0_ref, y_ref, co_ref, h_ref, ys_ref):
    xs_raw = xs_ref[0]
    bc_raw = bc_ref[0]
    xs = jax.nn.silu(_conv_step(c0_ref, cw_ref, xs_raw, 0, SSM_D_INNER) + cb_ref[:, 0:SSM_D_INNER])
    bc = jax.nn.silu(_conv_step(c0_ref, cw_ref, bc_raw, SSM_D_INNER, SSM_XBC) + cb_ref[:, SSM_D_INNER:SSM_XBC])
    co_ref[0, 0:SSM_CONV - 2, :] = c0_ref[0, 1:SSM_CONV - 1, :]
    co_ref[0, SSM_CONV - 2:SSM_CONV - 1, 0:SSM_D_INNER] = xs_raw
    co_ref[0, SSM_CONV - 2:SSM_CONV - 1, SSM_D_INNER:SSM_XBC] = bc_raw

    dt = jax.nn.softplus(dt_ref[0] + dtb_ref[...])
    dec = jnp.exp(dt * (-jnp.exp(alog_ref[...])))
    dskip = d_ref[...]
    eye = _iota((LANES, LANES), 0) == _iota((LANES, LANES), 1)
    upper = _iota((LANES, 1), 0) >= SSM_HEAD_DIM
    for j in range(SSM_HEADS // 2):
        h0i, h1i = 2 * j, 2 * j + 1
        g = h0i // (SSM_HEADS // SSM_GROUPS)
        bm = bc[:, g * SSM_STATE:(g + 1) * SSM_STATE]
        cm = bc[:, (SSM_GROUPS + g) * SSM_STATE:(SSM_GROUPS + g + 1) * SSM_STATE]
        x_row = xs[:, j * LANES:(j + 1) * LANES]
        x_col = _row_to_col(x_row, eye)
        pick = lambda v: jnp.where(upper, v[:, h1i:h1i + 1], v[:, h0i:h0i + 1])
        hp = h0_ref[0, h0i:h1i + 1].reshape(2 * SSM_HEAD_DIM, SSM_STATE)
        hn = hp * pick(dec) + (x_col * pick(dt)) * bm
        h_ref[0, h0i:h1i + 1] = hn.reshape(2, SSM_HEAD_DIM, SSM_STATE)
        y_col = jnp.sum(hn * cm, axis=1, keepdims=True) + pick(dskip) * x_col
        ys_ref[:, j * LANES:(j + 1) * LANES] = _col_to_row(y_col, eye)

    y = ys_ref[...] * jax.nn.silu(z_ref[0])
    half = SSM_D_INNER // SSM_GROUPS
    for g, yg in enumerate(_group_rms(y, nw_ref[...])):
        y_ref[0, :, g * half:(g + 1) * half] = yg


def ssd_step(main, tail, conv0, h0, conv_w, conv_b, dt_bias, a_log, d_skip, norm_w):
    bsz = main.shape[0]
    small = [conv_w, conv_b.reshape(1, -1), _pad_tail(dt_bias.reshape(1, -1)), _pad_tail(a_log.reshape(1, -1)),
             _pad_tail(d_skip.reshape(1, -1)), norm_w.reshape(1, -1)]
    hspec = pl.BlockSpec((1, SSM_HEADS, SSM_HEAD_DIM, SSM_STATE), lambda b: (b, 0, 0, 0))
    cspec = pl.BlockSpec((1, SSM_CONV - 1, SSM_XBC), lambda b: (b, 0, 0))
    return pl.pallas_call(
        _ssd_step_kernel,
        grid=(bsz,),
        in_specs=[
            pl.BlockSpec((1, 1, SSM_D_INNER), lambda b: (b, 0, 0)),
            pl.BlockSpec((1, 1, SSM_D_INNER), lambda b: (b, 0, 1)),
            pl.BlockSpec((1, 1, SSM_BC), lambda b: (b, 0, HYB_BC0 // SSM_BC)),
            pl.BlockSpec((1, 1, LANES), lambda b: (b, 0, 0)),
            cspec,
        ] + [_small(a) for a in small] + [hspec],
        out_specs=[pl.BlockSpec((1, 1, SSM_D_INNER), lambda b: (b, 0, 0)), cspec, hspec],
        out_shape=[jax.ShapeDtypeStruct((bsz, 1, SSM_D_INNER), F32),
                   jax.ShapeDtypeStruct(conv0.shape, F32),
                   jax.ShapeDtypeStruct(h0.shape, F32)],
        scratch_shapes=[pltpu.VMEM((1, SSM_D_INNER), F32)],
        compiler_params=_cparams("parallel"),
        name="ssd_step",
    )(main, main, main, tail, conv0, *small, h0)


def _l2norm(x):
    return x * lax.rsqrt(jnp.sum(x * x, axis=-1, keepdims=True) + EPS)


def _unit_lower_inverse(n_strict, eye):
    size = n_strict.shape[0]
    p = eye - n_strict
    m = _dot(n_strict, n_strict)
    power = 2
    while 2 * power < size:
        pm = _dot(jnp.concatenate([p, m], axis=0), m)
        p = p + pm[:size]
        m = pm[size:]
        power *= 2
    return p + _dot(p, m)


def _gdn_gates(ba, dtb_ref, alog_ref):
    beta = jax.nn.sigmoid(ba)
    g = -jnp.exp(alog_ref[...]) * jax.nn.softplus(ba + dtb_ref[...])
    return beta, g


def _gdn_kernel(q_ref, k_ref, v_ref, z_ref, ba_ref, c0_ref, cw_ref, dtb_ref, alog_ref, nw_ref, s0_ref,
                o_ref, co_ref, s_ref, xpad_ref):
    c = pl.program_id(1)
    ch = GDN_CHUNK
    base = 8
    lo = base - (GDN_CONV - 1)
    nh = GDN_V_HEADS

    @pl.when(c == 0)
    def _():
        xpad_ref[lo:base, :] = c0_ref[0]
        s_ref[...] = s0_ref[...]

    xpad_ref[base:base + ch, 0:GDN_QK_W] = q_ref[0]
    xpad_ref[base:base + ch, GDN_QK_W:2 * GDN_QK_W] = k_ref[0]
    xpad_ref[base:base + ch, 2 * GDN_QK_W:GDN_QKV] = v_ref[0]
    conv = cw_ref[0:1, :] * xpad_ref[lo:lo + ch, :]
    for i in range(1, GDN_CONV):
        conv = conv + cw_ref[i:i + 1, :] * xpad_ref[lo + i:lo + i + ch, :]
    tail = xpad_ref[base + ch - (GDN_CONV - 1):base + ch, :]
    co_ref[0] = tail
    xpad_ref[lo:base, :] = tail
    qkv = jax.nn.silu(conv)

    beta, g = _gdn_gates(ba_ref[0], dtb_ref, alog_ref)
    row = _iota((ch, ch), 0)
    colv = _iota((ch, ch), 1)
    incl = row >= colv
    strict = row > colv
    eye = (row == colv).astype(F32)
    gcum = _dot(incl.astype(F32), g, precision=HIGHEST)
    gcum_t = jnp.concatenate([gcum, jnp.zeros((LANES - ch, LANES), F32)], axis=0).T

    for j in range(GDN_QK_HEADS):
        qn = _l2norm(qkv[:, j * GDN_DK:(j + 1) * GDN_DK]) * (GDN_DK ** -0.5)
        kn = _l2norm(qkv[:, GDN_QK_W + j * GDN_DK:GDN_QK_W + (j + 1) * GDN_DK])
        kk = _dot_nt(kn, kn)
        qk = _dot_nt(qn, kn)
        for h in range(j * (nh // GDN_QK_HEADS), (j + 1) * (nh // GDN_QK_HEADS)):
            gc_col = gcum[:, nh + h:nh + h + 1]
            gc_row = gcum_t[nh + h:nh + h + 1, 0:ch]
            gc_last = gcum[ch - 1:ch, nh + h:nh + h + 1]
            beta_col = beta[:, h:h + 1]
            dec = jnp.exp(jnp.where(incl, gc_col - gc_row, NEG))
            n_strict = jnp.where(strict, kk * dec, 0.0) * beta_col
            t_inv = _unit_lower_inverse(n_strict, eye)
            eg = jnp.exp(gc_col)
            v_h = qkv[:, 2 * GDN_QK_W + h * GDN_DV:2 * GDN_QK_W + (h + 1) * GDN_DV]
            s_prev = s_ref[0, h]
            both = _dot(jnp.concatenate([kn * (beta_col * eg), qn * eg], axis=0), s_prev)
            u = _dot(t_inv, v_h * beta_col - both[:ch])
            o = both[ch:] + _dot(qk * dec, u)
            s_ref[0, h] = s_prev * jnp.exp(gc_last) + _dot_tn(kn * jnp.exp(gc_last - gc_col), u)
            z_h = z_ref[0, :, h * GDN_DV:(h + 1) * GDN_DV]
            o_ref[0, :, h * GDN_DV:(h + 1) * GDN_DV] = _rms(o) * nw_ref[...] * jax.nn.silu(z_h)


def _gdn_gate_params(dt_bias, a_log):
    nh = GDN_V_HEADS
    dtb = jnp.zeros((1, LANES), F32).at[0, nh:2 * nh].set(dt_bias)
    alog = jnp.zeros((1, LANES), F32).at[0, nh:2 * nh].set(a_log)
    return dtb, alog


def gdn_prompt(main, tail, conv0, s0, conv_w, dt_bias, a_log, norm_w, *, bsz, seq):
    ch = GDN_CHUNK
    nc = seq // ch
    dtb, alog = _gdn_gate_params(dt_bias, a_log)
    small = [conv_w, dtb, alog, norm_w.reshape(1, -1)]
    sspec = pl.BlockSpec((1, GDN_V_HEADS, GDN_DK, GDN_DV), lambda b, c: (b, 0, 0, 0))
    cspec = pl.BlockSpec((1, GDN_CONV - 1, GDN_QKV), lambda b, c: (b, 0, 0))
    return pl.pallas_call(
        _gdn_kernel,
        grid=(bsz, nc),
        in_specs=[
            pl.BlockSpec((1, ch, GDN_QK_W), lambda b, c: (b, c, 0)),
            pl.BlockSpec((1, ch, GDN_QK_W), lambda b, c: (b, c, 1)),
            pl.BlockSpec((1, ch, GDN_VW), lambda b, c: (b, c, 1)),
            pl.BlockSpec((1, ch, GDN_VW), lambda b, c: (b, c, 2)),
            pl.BlockSpec((1, ch, LANES), lambda b, c: (b, c, 0)),
            cspec,
        ] + [_small(a) for a in small] + [sspec],
        out_specs=[pl.BlockSpec((1, ch, GDN_VW), lambda b, c: (b, c, 0)), cspec, sspec],
        out_shape=[jax.ShapeDtypeStruct((bsz, seq, GDN_VW), F32),
                   jax.ShapeDtypeStruct((bsz, GDN_CONV - 1, GDN_QKV), F32),
                   jax.ShapeDtypeStruct((bsz, GDN_V_HEADS, GDN_DK, GDN_DV), F32)],
        scratch_shapes=[pltpu.VMEM((8 + ch, GDN_QKV), F32)],
        compiler_params=_cparams("parallel", "arbitrary"),
        name="gdn_prompt",
    )(main, main, main, main, tail, conv0, *small, s0)


def _gdn_conv_step(c0_ref, w_ref, x_row, c0, c1):
    acc = w_ref[GDN_CONV - 1:GDN_CONV, c0:c1] * x_row
    for i in range(GDN_CONV - 1):
        acc = acc + w_ref[i:i + 1, c0:c1] * c0_ref[0, i:i + 1, c0:c1]
    return acc


def _gdn_step_kernel(q_ref, k_ref, v_ref, z_ref, ba_ref, c0_ref, cw_ref, dtb_ref, alog_ref, nw_ref, s0_ref,
                     o_ref, co_ref, s_ref):
    nh = GDN_V_HEADS
    q_raw, k_raw, v_raw = q_ref[0], k_ref[0], v_ref[0]
    q = jax.nn.silu(_gdn_conv_step(c0_ref, cw_ref, q_raw, 0, GDN_QK_W))
    k = jax.nn.silu(_gdn_conv_step(c0_ref, cw_ref, k_raw, GDN_QK_W, 2 * GDN_QK_W))
    v = jax.nn.silu(_gdn_conv_step(c0_ref, cw_ref, v_raw, 2 * GDN_QK_W, GDN_QKV))
    co_ref[0, 0:GDN_CONV - 2, :] = c0_ref[0, 1:GDN_CONV - 1, :]
    co_ref[0, GDN_CONV - 2:GDN_CONV - 1, 0:GDN_QK_W] = q_raw
    co_ref[0, GDN_CONV - 2:GDN_CONV - 1, GDN_QK_W:2 * GDN_QK_W] = k_raw
    co_ref[0, GDN_CONV - 2:GDN_CONV - 1, 2 * GDN_QK_W:GDN_QKV] = v_raw

    beta, g = _gdn_gates(ba_ref[0], dtb_ref, alog_ref)
    eg_all = jnp.exp(g)
    eye = _iota((LANES, LANES), 0) == _iota((LANES, LANES), 1)
    for j in range(GDN_QK_HEADS):
        qn = _l2norm(q[:, j * GDN_DK:(j + 1) * GDN_DK]) * (GDN_DK ** -0.5)
        kn = _l2norm(k[:, j * GDN_DK:(j + 1) * GDN_DK])
        qk = jnp.sum(qn * kn, axis=-1, keepdims=True)
        q_col = _row_to_col(qn, eye)
        k_col = _row_to_col(kn, eye)
        for h in range(j * (nh // GDN_QK_HEADS), (j + 1) * (nh // GDN_QK_HEADS)):
            b_h = beta[:, h:h + 1]
            eg = eg_all[:, nh + h:nh + h + 1]
            s_prev = s0_ref[0, h]
            ks = jnp.sum(s_prev * k_col, axis=0, keepdims=True)
            qs = jnp.sum(s_prev * q_col, axis=0, keepdims=True)
            v_h = v[:, h * GDN_DV:(h + 1) * GDN_DV]
            u = v_h * b_h - (b_h * eg) * ks
            o = eg * qs + qk * u
            s_ref[0, h] = s_prev * eg + k_col * u
            z_h = z_ref[0, :, h * GDN_DV:(h + 1) * GDN_DV]
            o_ref[0, :, h * GDN_DV:(h + 1) * GDN_DV] = _rms(o) * nw_ref[...] * jax.nn.silu(z_h)


def gdn_step(main, tail, conv0, s0, conv_w, dt_bias, a_log, norm_w):
    bsz = main.shape[0]
    dtb, alog = _gdn_gate_params(dt_bias, a_log)
    small = [conv_w, dtb, alog, norm_w.reshape(1, -1)]
    sspec = pl.BlockSpec((1, GDN_V_HEADS, GDN_DK, GDN_DV), lambda b: (b, 0, 0, 0))
    cspec = pl.BlockSpec((1, GDN_CONV - 1, GDN_QKV), lambda b: (b, 0, 0))
    return pl.pallas_call(
        _gdn_step_kernel,
        grid=(bsz,),
        in_specs=[
            pl.BlockSpec((1, 1, GDN_QK_W), lambda b: (b, 0, 0)),
            pl.BlockSpec((1, 1, GDN_QK_W), lambda b: (b, 0, 1)),
            pl.BlockSpec((1, 1, GDN_VW), lambda b: (b, 0, 1)),
            pl.BlockSpec((1, 1, GDN_VW), lambda b: (b, 0, 2)),
            pl.BlockSpec((1, 1, LANES), lambda b: (b, 0, 0)),
            cspec,
        ] + [_small(a) for a in small] + [sspec],
        out_specs=[pl.BlockSpec((1, 1, GDN_VW), lambda b: (b, 0, 0)), cspec, sspec],
        out_shape=[jax.ShapeDtypeStruct((bsz, 1, GDN_VW), F32),
                   jax.ShapeDtypeStruct(conv0.shape, F32),
                   jax.ShapeDtypeStruct(s0.shape, F32)],
        compiler_params=_cparams("parallel"),
        name="gdn_step",
    )(main, main, main, main, tail, conv0, *small, s0)


def _pad_tail(w):
    return jnp.pad(w, ((0, 0), (0, LANES - w.shape[1])))


def _prep_hyb_in(w):
    a = A_WIDTH
    q, k, v = w[:, 0:a], w[:, a:2 * a], w[:, 2 * a:3 * a]
    z = w[:, 3 * a:3 * a + SSM_D_INNER]
    x0 = 3 * a + SSM_D_INNER
    xs = w[:, x0:x0 + SSM_D_INNER]
    bc = w[:, x0 + SSM_D_INNER:x0 + SSM_XBC]
    dt = w[:, x0 + SSM_XBC:]
    return jnp.concatenate([z, xs, q, k, v, bc], axis=1).astype(BF16), _pad_tail(dt).astype(BF16)


def _prep_gdn_in(w):
    return w[:, :GDN_MAIN].astype(BF16), _pad_tail(w[:, GDN_MAIN:]).astype(BF16)


def _row_tile(m, cap):
    return m if m <= cap else cap


def kernel(x_prompt, x_sample, cache_attn_k, cache_attn_v, state_ssm_conv, state_ssm, state_gdn_conv, state_gdn, rel_bias, norm_mix_pre, norm_mix_post, norm_ffn_pre, norm_ffn_post, w_hyb_in, ssm_conv_w, ssm_conv_b, ssm_dt_bias, ssm_a_log, ssm_d, ssm_norm_w, w_hyb_out, w_gdn_in, gdn_conv_w, gdn_dt_bias, gdn_a_log, gdn_norm_w, w_gdn_out, w_ffn_gate, w_ffn_up, w_ffn_down):
    depth = norm_mix_pre.shape[0]
    d_model = x_prompt.shape[-1]
    n_hyb, n_gdn = w_hyb_in.shape[0], w_gdn_in.shape[0]

    hyb_in = [_prep_hyb_in(w_hyb_in[i]) for i in range(n_hyb)]
    hyb_out = [(w_hyb_out[i, :A_WIDTH].astype(BF16), w_hyb_out[i, A_WIDTH:].astype(BF16)) for i in range(n_hyb)]
    gdn_in = [_prep_gdn_in(w_gdn_in[i]) for i in range(n_gdn)]
    gdn_out = [w_gdn_out[i].astype(BF16) for i in range(n_gdn)]
    ffn_w = [(w_ffn_gate[l].astype(BF16), w_ffn_up[l].astype(BF16), w_ffn_down[l].astype(BF16))
             for l in range(depth)]
    bias_tiles = _attn_bias_tiles(rel_bias)

    def trunk(x3, k_pre, v_pre, sconv, sssm, gconv, gstate):
        bsz, seq, _ = x3.shape
        m = bsz * seq
        step = seq == 1
        tm_big = _row_tile(m, 1024)
        tm = _row_tile(m, 512)
        x = x3.reshape(m, d_model)
        nk, nv, nsc, nss, ngc, ngs = [], [], [], [], [], []
        for l in range(depth):
            i = l // 2
            if l % 2 == 0:
                w_main, w_tail = hyb_in[i]
                main, tail = inproj(x, norm_mix_pre[l], w_main, w_tail, tm=tm_big, tn=512)
                main3 = main.reshape(bsz, seq, HYB_MAIN)
                tail3 = tail.reshape(bsz, seq, LANES)
                ssm_args = (ssm_conv_w[i], ssm_conv_b[i], ssm_dt_bias[i], ssm_a_log[i], ssm_d[i], ssm_norm_w[i])
                if step:
                    past = k_pre.shape[2]
                    o_attn, k_new, v_new = attention_step(
                        main3, k_pre[i].reshape(bsz, past, A_WIDTH), v_pre[i].reshape(bsz, past, A_WIDTH),
                        _attn_logw(rel_bias, past))
                    y, c_new, s_new = ssd_step(main3, tail3, sconv[i], sssm[i], *ssm_args)
                else:
                    o_attn = attention_prompt(main3, bias_tiles, bsz=bsz, seq=seq)
                    keep = min(A_PATTERNS[-1][0], seq)
                    k_new = main3[:, seq - keep:, HYB_K0:HYB_K0 + A_WIDTH]
                    v_new = main3[:, seq - keep:, HYB_V0:HYB_V0 + A_WIDTH]
                    y, c_new, s_new = ssd_prompt(main3, tail3, sconv[i], sssm[i], *ssm_args, bsz=bsz, seq=seq)
                nk.append(k_new.reshape(bsz, -1, A_HEADS, A_HEAD_DIM))
                nv.append(v_new.reshape(bsz, -1, A_HEADS, A_HEAD_DIM))
                nsc.append(c_new)
                nss.append(s_new)
                x = outproj([o_attn.reshape(m, A_WIDTH), y.reshape(m, SSM_D_INNER)], list(hyb_out[i]),
                            x, norm_mix_post[l], tm=tm)
            else:
                w_main, w_tail = gdn_in[i]
                main, tail = inproj(x, norm_mix_pre[l], w_main, w_tail, tm=tm_big, tn=512)
                main3 = main.reshape(bsz, seq, GDN_MAIN)
                tail3 = tail.reshape(bsz, seq, LANES)
                gdn_args = (gdn_conv_w[i], gdn_dt_bias[i], gdn_a_log[i], gdn_norm_w[i])
                if step:
                    o, c_new, s_new = gdn_step(main3, tail3, gconv[i], gstate[i], *gdn_args)
                else:
                    o, c_new, s_new = gdn_prompt(main3, tail3, gconv[i], gstate[i], *gdn_args, bsz=bsz, seq=seq)
                ngc.append(c_new)
                ngs.append(s_new)
                x = outproj([o.reshape(m, GDN_VW)], [gdn_out[i]], x, norm_mix_post[l], tm=tm)
            wg, wu, wd = ffn_w[l]
            x = ffn(x, norm_ffn_pre[l], wg, wu, wd, norm_ffn_post[l], tm=tm, tf=wg.shape[1] // 2)
        return (x.reshape(bsz, seq, d_model), jnp.stack(nk), jnp.stack(nv), jnp.stack(nsc), jnp.stack(nss),
                jnp.stack(ngc), jnp.stack(ngs))

    bsz = x_prompt.shape[0]
    dt_p = x_prompt.dtype
    p_sc0 = jnp.zeros((n_hyb, bsz, SSM_CONV - 1, SSM_XBC), dt_p)
    p_ss0 = jnp.zeros((n_hyb, bsz, SSM_HEADS, SSM_HEAD_DIM, SSM_STATE), F32)
    p_gc0 = jnp.zeros((n_gdn, bsz, GDN_CONV - 1, GDN_QKV), dt_p)
    p_gs0 = jnp.zeros((n_gdn, bsz, GDN_V_HEADS, GDN_DK, GDN_DV), F32)
    y_prompt, pk, pv, psc, pss, pgc, pgs = trunk(x_prompt, None, None, p_sc0, p_ss0, p_gc0, p_gs0)
    y_sample, sk, sv, ssc, sss, sgc, sgs = trunk(
        x_sample, cache_attn_k, cache_attn_v, state_ssm_conv, state_ssm, state_gdn_conv, state_gdn)
    return (y_prompt, y_sample, pk, pv, psc, pss, pgc, pgs, sk, sv, ssc, sss, sgc, sgs)
```

```python
import functools
import math

import numpy as np
import jax
import jax.numpy as jnp
from jax import lax
from jax.experimental import pallas as pl
from jax.experimental.pallas import tpu as pltpu

F32 = jnp.float32
BF16 = jnp.bfloat16
EPS = 1e-6
NEG = -1e30
HIGHEST = lax.Precision.HIGHEST

VMEM_LIMIT_BYTES = 56 * 1024 * 1024
LANES = 128

A_HEADS = 8
A_HEAD_DIM = 64
A_WIDTH = A_HEADS * A_HEAD_DIM
A_PATTERNS = ((128, 1), (512, 4), (2048, 16))
A_BAND = 128
ATTN_GROUP = 2
REL_BUCKETS = 32
REL_MAX_DIST = 2048

SSM_D_INNER = 1024
SSM_HEAD_DIM = 64
SSM_HEADS = SSM_D_INNER // SSM_HEAD_DIM
SSM_GROUPS = 2
SSM_STATE = 128
SSM_CONV = 4
SSM_CHUNK = 128
SSM_BC = 2 * SSM_GROUPS * SSM_STATE
SSM_XBC = SSM_D_INNER + SSM_BC

GDN_QK_HEADS = 8
GDN_V_HEADS = 16
GDN_DK = 128
GDN_DV = 128
GDN_CONV = 4
GDN_CHUNK = 64
GDN_QK_W = GDN_QK_HEADS * GDN_DK
GDN_VW = GDN_V_HEADS * GDN_DV
GDN_QKV = 2 * GDN_QK_W + GDN_VW

HYB_MAIN = 2 * SSM_D_INNER + 3 * A_WIDTH + SSM_BC
HYB_Q0 = 2 * SSM_D_INNER
HYB_K0 = HYB_Q0 + A_WIDTH
HYB_V0 = HYB_K0 + A_WIDTH
HYB_BC0 = HYB_V0 + A_WIDTH
GDN_MAIN = GDN_QKV + GDN_VW


def _cparams(*sem):
    return pltpu.CompilerParams(dimension_semantics=sem, vmem_limit_bytes=VMEM_LIMIT_BYTES)


def _rms(x):
    return x * lax.rsqrt(jnp.mean(x * x, axis=-1, keepdims=True) + EPS)


def _dot(a, b, **kw):
    return jnp.dot(a, b, preferred_element_type=F32, **kw)


def _dot_nt(a, b, **kw):
    return lax.dot_general(a, b, (((1,), (1,)), ((), ())), preferred_element_type=F32, **kw)


def _dot_tn(a, b, **kw):
    return lax.dot_general(a, b, (((0,), (0,)), ((), ())), preferred_element_type=F32, **kw)


def _iota(shape, dim):
    return lax.broadcasted_iota(jnp.int32, shape, dim)


def _inproj_kernel(x_ref, g_ref, w_ref, wt_ref, o_ref, t_ref, h_ref):
    @pl.when(pl.program_id(1) == 0)
    def _():
        hb = (_rms(x_ref[...]) * g_ref[...]).astype(BF16)
        h_ref[...] = hb
        t_ref[...] = _dot(hb, wt_ref[...])

    o_ref[...] = _dot(h_ref[...], w_ref[...])


def inproj(x, g, w_main, w_tail, *, tm, tn):
    m, d = x.shape
    n = w_main.shape[1]
    return pl.pallas_call(
        _inproj_kernel,
        grid=(m // tm, n // tn),
        in_specs=[
            pl.BlockSpec((tm, d), lambda i, j: (i, 0)),
            pl.BlockSpec((1, d), lambda i, j: (0, 0)),
            pl.BlockSpec((d, tn), lambda i, j: (0, j)),
            pl.BlockSpec((d, LANES), lambda i, j: (0, 0)),
        ],
        out_specs=[
            pl.BlockSpec((tm, tn), lambda i, j: (i, j)),
            pl.BlockSpec((tm, LANES), lambda i, j: (i, 0)),
        ],
        out_shape=[jax.ShapeDtypeStruct((m, n), F32), jax.ShapeDtypeStruct((m, LANES), F32)],
        scratch_shapes=[pltpu.VMEM((tm, d), BF16)],
        compiler_params=_cparams("parallel", "arbitrary"),
        name="inproj",
    )(x, g.reshape(1, d), w_main, w_tail)


def _outproj_kernel(*refs, n_in):
    a_refs, w_refs = refs[:n_in], refs[n_in:2 * n_in]
    x_ref, g_ref, o_ref = refs[2 * n_in:]
    acc = None
    for a_ref, w_ref in zip(a_refs, w_refs):
        t = _dot(a_ref[...].astype(BF16), w_ref[...])
        acc = t if acc is None else acc + t
    o_ref[...] = x_ref[...] + _rms(acc) * g_ref[...]


def outproj(acts, weights, x, g, *, tm):
    m, d = x.shape
    n_in = len(acts)
    in_specs = [pl.BlockSpec((tm, a.shape[1]), lambda i: (i, 0)) for a in acts]
    in_specs += [pl.BlockSpec(w.shape, lambda i: (0, 0)) for w in weights]
    in_specs += [pl.BlockSpec((tm, d), lambda i: (i, 0)), pl.BlockSpec((1, d), lambda i: (0, 0))]
    return pl.pallas_call(
        functools.partial(_outproj_kernel, n_in=n_in),
        grid=(m // tm,),
        in_specs=in_specs,
        out_specs=pl.BlockSpec((tm, d), lambda i: (i, 0)),
        out_shape=jax.ShapeDtypeStruct((m, d), F32),
        compiler_params=_cparams("parallel"),
        name="outproj",
    )(*acts, *weights, x, g.reshape(1, d))


def _ffn_kernel(x_ref, g1_ref, wg_ref, wu_ref, wd_ref, g2_ref, o_ref, h_ref, acc_ref):
    j = pl.program_id(1)

    @pl.when(j == 0)
    def _():
        h_ref[...] = (_rms(x_ref[...]) * g1_ref[...]).astype(BF16)
        acc_ref[...] = jnp.zeros_like(acc_ref)

    h = h_ref[...]
    a = jax.nn.silu(_dot(h, wg_ref[...])) * _dot(h, wu_ref[...])
    acc_ref[...] += _dot(a.astype(BF16), wd_ref[...])

    @pl.when(j == pl.num_programs(1) - 1)
    def _():
        o_ref[...] = x_ref[...] + _rms(acc_ref[...]) * g2_ref[...]


def ffn(x, g1, wg, wu, wd, g2, *, tm, tf):
    m, d = x.shape
    dff = wg.shape[1]
    return pl.pallas_call(
        _ffn_kernel,
        grid=(m // tm, dff // tf),
        in_specs=[
            pl.BlockSpec((tm, d), lambda i, j: (i, 0)),
            pl.BlockSpec((1, d), lambda i, j: (0, 0)),
            pl.BlockSpec((d, tf), lambda i, j: (0, j)),
            pl.BlockSpec((d, tf), lambda i, j: (0, j)),
            pl.BlockSpec((tf, d), lambda i, j: (j, 0)),
            pl.BlockSpec((1, d), lambda i, j: (0, 0)),
        ],
        out_specs=pl.BlockSpec((tm, d), lambda i, j: (i, 0)),
        out_shape=jax.ShapeDtypeStruct((m, d), F32),
        scratch_shapes=[pltpu.VMEM((tm, d), BF16), pltpu.VMEM((tm, d), F32)],
        compiler_params=_cparams("parallel", "arbitrary"),
        name="ffn",
    )(x, g1.reshape(1, d), wg, wu, wd, g2.reshape(1, d))


def _rel_buckets(dist):
    max_exact = REL_BUCKETS // 2
    n = np.maximum(dist, 1).astype(np.float32)
    large = max_exact + (np.log(n / max_exact) / math.log(REL_MAX_DIST / max_exact)
                         * (REL_BUCKETS - max_exact)).astype(np.int32)
    large = np.minimum(large, REL_BUCKETS - 1)
    return np.where(dist < max_exact, dist, large).astype(np.int32)


def _attn_bias_tiles(rel_bias):
    r = np.arange(A_BAND)[:, None]
    c = np.arange(2 * A_BAND)[None, :]
    t = np.where(c < A_BAND, A_BAND + r - c, r - (c - A_BAND))
    valid = (t >= 0) & (t <= A_BAND)
    tiles = []
    for (_, d) in A_PATTERNS:
        b = rel_bias[_rel_buckets(np.maximum(t, 0) * d)]
        tiles.append(jnp.where(valid[..., None], b.astype(F32), NEG))
    tl = jnp.stack(tiles)
    tl = jnp.transpose(tl, (3, 0, 1, 2))
    tl = tl.reshape(A_HEADS // 2, 2, len(A_PATTERNS), A_BAND, 2 * A_BAND)
    return jnp.transpose(tl, (0, 2, 1, 3, 4))


def _attn_kernel(q_ref, k_ref, v_ref, b_ref, o_ref, m0_ref, m1_ref, l0_ref, l1_ref, acc_ref, *, seq):
    scale = A_HEAD_DIM ** -0.5
    n_tiles = seq // A_BAND
    lane = _iota((A_BAND, LANES), 1)
    head0 = lane < A_HEAD_DIM
    m_refs, l_refs = (m0_ref, m1_ref), (l0_ref, l1_ref)
    for h in range(2):
        m_refs[h][...] = jnp.full(m_refs[h].shape, NEG, F32)
        l_refs[h][...] = jnp.zeros_like(l_refs[h])
    acc_ref[...] = jnp.zeros_like(acc_ref)

    for p, (_, d) in enumerate(A_PATTERNS):
        tiles_per_class = n_tiles // d

        def load_tile(idx, d=d, tiles_per_class=tiles_per_class):
            r = idx // tiles_per_class
            t = idx % tiles_per_class
            start = r + t * (d * A_BAND)
            has_prev = t > 0
            prev = jnp.where(has_prev, start - d * A_BAND, start)
            if d > 1:
                rows, prows = pl.ds(start, A_BAND, stride=d), pl.ds(prev, A_BAND, stride=d)
            else:
                rows, prows = pl.ds(pl.multiple_of(start, A_BAND), A_BAND), pl.ds(pl.multiple_of(prev, A_BAND), A_BAND)
            return dict(
                rows=rows, has_prev=has_prev, q=q_ref[rows, :] * scale,
                k2=jnp.concatenate([k_ref[prows, :], k_ref[rows, :]], axis=0).astype(BF16),
                v2=jnp.concatenate([v_ref[prows, :], v_ref[rows, :]], axis=0).astype(BF16),
                acc=acc_ref[rows, :], m=[m_refs[h][rows, :] for h in range(2)],
                l=[l_refs[h][rows, :] for h in range(2)])

        def update_tile(tile, p=p):
            col = _iota((A_BAND, 2 * A_BAND), 1)
            dead = jnp.logical_and(col < A_BAND, jnp.logical_not(tile["has_prev"]))
            m_out, l_out, acc_out = [], [], []
            for h in range(2):
                qh = jnp.where(head0 if h == 0 else jnp.logical_not(head0), tile["q"], 0.0).astype(BF16)
                s = _dot_nt(qh, tile["k2"]) + jnp.where(dead, NEG, b_ref[0, p, h])
                m_new = jnp.maximum(tile["m"][h], jnp.max(s, axis=-1, keepdims=True))
                alpha = jnp.exp(tile["m"][h] - m_new)
                pr = jnp.exp(s - m_new[:, :1])
                m_out.append(m_new)
                l_out.append(alpha * tile["l"][h] + jnp.sum(pr, axis=-1, keepdims=True))
                acc_out.append(alpha * tile["acc"] + _dot(pr.astype(BF16), tile["v2"]))
            return m_out, l_out, jnp.where(head0, acc_out[0], acc_out[1])

        def tile_group(idx, carry, load_tile=load_tile, update_tile=update_tile):
            tiles = [load_tile(idx + i * (n_tiles // ATTN_GROUP)) for i in range(ATTN_GROUP)]
            outs = [update_tile(tile) for tile in tiles]
            for tile, (m_out, l_out, acc_out) in zip(tiles, outs):
                for h in range(2):
                    m_refs[h][tile["rows"], :] = m_out[h]
                    l_refs[h][tile["rows"], :] = l_out[h]
                acc_ref[tile["rows"], :] = acc_out
            return carry

        lax.fori_loop(0, n_tiles // ATTN_GROUP, tile_group, 0)

    lane_s = _iota((seq, LANES), 1)
    o_ref[...] = acc_ref[...] / jnp.where(lane_s < A_HEAD_DIM, l0_ref[...], l1_ref[...])


def attention_prompt(proj, bias_tiles, *, bsz, seq):
    hp = A_HEADS // 2
    qb, kb, vb = HYB_Q0 // LANES, HYB_K0 // LANES, HYB_V0 // LANES
    return pl.pallas_call(
        functools.partial(_attn_kernel, seq=seq),
        grid=(bsz, hp),
        in_specs=[
            pl.BlockSpec((None, seq, LANES), lambda b, h: (b, 0, qb + h)),
            pl.BlockSpec((None, seq, LANES), lambda b, h: (b, 0, kb + h)),
            pl.BlockSpec((None, seq, LANES), lambda b, h: (b, 0, vb + h)),
            pl.BlockSpec((1,) + bias_tiles.shape[1:], lambda b, h: (h, 0, 0, 0, 0)),
        ],
        out_specs=pl.BlockSpec((None, seq, LANES), lambda b, h: (b, 0, h)),
        out_shape=jax.ShapeDtypeStruct((bsz, seq, A_WIDTH), F32),
        scratch_shapes=[pltpu.VMEM((seq, LANES), F32)] * 5,
        compiler_params=_cparams("parallel", "parallel"),
        name="attn_prompt",
    )(proj, proj, proj, bias_tiles)


def _attn_logw(rel_bias, past):
    dist = np.arange(past + 1)
    count = np.zeros(past + 1, np.float64)
    for (w, d) in A_PATTERNS:
        count += ((dist % d == 0) & (dist <= w)).astype(np.float64)
    logc = np.where(count > 0, np.log(np.maximum(count, 1.0)), 0.0).astype(np.float32)
    lw = rel_bias[_rel_buckets(dist)].astype(F32).T + logc[None, :]
    return jnp.where((count > 0)[None, :], lw, NEG)


def _attn_step_kernel(q_ref, kn_ref, vn_ref, kc_ref, vc_ref, lwc_ref, lwn_ref, o_ref, ko_ref, vo_ref, *, past):
    scale = A_HEAD_DIM ** -0.5
    headmask = _iota((A_HEADS, A_WIDTH), 1) // A_HEAD_DIM == _iota((A_HEADS, A_WIDTH), 0)
    q8 = jnp.where(headmask, q_ref[0] * scale, 0.0)
    kn = kn_ref[0]
    vn = vn_ref[0]
    s = _dot_nt(q8, kc_ref[0]) + lwc_ref[...]
    s_new = jnp.sum(q8 * kn, axis=-1, keepdims=True) + lwn_ref[:, :1]
    m = jnp.maximum(jnp.max(s, axis=-1, keepdims=True), s_new)
    p = jnp.exp(s - m)
    p_new = jnp.exp(s_new - m)
    den = jnp.sum(p, axis=-1, keepdims=True) + p_new
    o8 = (_dot(p, vc_ref[0]) + p_new * vn) / den
    o_ref[0] = jnp.sum(jnp.where(headmask, o8, 0.0), axis=0, keepdims=True)
    step = 256
    for r0 in range(0, past - 1, step):
        n = min(step, past - 1 - r0)
        ko_ref[0, pl.ds(r0, n), :] = kc_ref[0, pl.ds(r0 + 1, n), :]
        vo_ref[0, pl.ds(r0, n), :] = vc_ref[0, pl.ds(r0 + 1, n), :]
    ko_ref[0, pl.ds(past - 1, 1), :] = kn
    vo_ref[0, pl.ds(past - 1, 1), :] = vn


def attention_step(proj3, cache_k, cache_v, logw):
    bsz, past, w = cache_k.shape
    lw_cache = logw[:, past:0:-1]
    lw_new = jnp.broadcast_to(logw[:, :1], (A_HEADS, LANES))
    blk = lambda c0: pl.BlockSpec((1, 1, w), lambda b: (b, 0, c0 // w))
    cspec = pl.BlockSpec((1, past, w), lambda b: (b, 0, 0))
    return pl.pallas_call(
        functools.partial(_attn_step_kernel, past=past),
        grid=(bsz,),
        in_specs=[blk(HYB_Q0), blk(HYB_K0), blk(HYB_V0), cspec, cspec,
                  pl.BlockSpec((A_HEADS, past), lambda b: (0, 0)),
                  pl.BlockSpec((A_HEADS, LANES), lambda b: (0, 0))],
        out_specs=[pl.BlockSpec((1, 1, w), lambda b: (b, 0, 0)), cspec, cspec],
        out_shape=[jax.ShapeDtypeStruct((bsz, 1, w), F32),
                   jax.ShapeDtypeStruct((bsz, past, w), F32),
                   jax.ShapeDtypeStruct((bsz, past, w), F32)],
        compiler_params=_cparams("parallel"),
        name="attn_step",
    )(proj3, proj3, proj3, cache_k, cache_v, lw_cache, lw_new)


def _group_rms(y, w):
    half = SSM_D_INNER // SSM_GROUPS
    return [_rms(y[:, g * half:(g + 1) * half]) * w[:, g * half:(g + 1) * half] for g in range(SSM_GROUPS)]


def _ssd_kernel(z_ref, xs_ref, bc_ref, dt_ref, c0_ref, cw_ref, cb_ref, dtb_ref, alog_ref, d_ref, nw_ref,
                h0_ref, y_ref, co_ref, h_ref, xpad_ref, ys_ref):
    c = pl.program_id(1)
    ch = SSM_CHUNK
    base = 8
    lo = base - (SSM_CONV - 1)

    @pl.when(c == 0)
    def _():
        xpad_ref[lo:base, :] = c0_ref[0]
        h_ref[...] = h0_ref[...]

    xpad_ref[base:base + ch, 0:SSM_D_INNER] = xs_ref[0]
    xpad_ref[base:base + ch, SSM_D_INNER:SSM_XBC] = bc_ref[0]
    conv = cb_ref[...] + cw_ref[0:1, :] * xpad_ref[lo:lo + ch, :]
    for i in range(1, SSM_CONV):
        conv = conv + cw_ref[i:i + 1, :] * xpad_ref[lo + i:lo + i + ch, :]
    tail = xpad_ref[base + ch - (SSM_CONV - 1):base + ch, :]
    co_ref[0] = tail
    xpad_ref[lo:base, :] = tail
    xbc = jax.nn.silu(conv)
    xs = xbc[:, :SSM_D_INNER]

    dt = jax.nn.softplus(dt_ref[0] + dtb_ref[...])
    da = dt * (-jnp.exp(alog_ref[...]))
    row = _iota((ch, ch), 0)
    colv = _iota((ch, ch), 1)
    tril = (row >= colv).astype(F32)
    cs = _dot(tril, da, precision=HIGHEST)
    cs_t = cs.T
    causal = row >= colv

    heads = range(SSM_HEADS)
    hpg = SSM_HEADS // SSM_GROUPS
    bm = [xbc[:, SSM_D_INNER + g * SSM_STATE:SSM_D_INNER + (g + 1) * SSM_STATE] for g in range(SSM_GROUPS)]
    cm = [xbc[:, SSM_D_INNER + (SSM_GROUPS + g) * SSM_STATE:SSM_D_INNER + (SSM_GROUPS + g + 1) * SSM_STATE]
          for g in range(SSM_GROUPS)]
    cb = [_dot_nt(cm[g], bm[g]) for g in range(SSM_GROUPS)]
    cs_col = [cs[:, h:h + 1] for h in heads]
    last = [cs[ch - 1:ch, h:h + 1] for h in heads]
    seg = [jnp.exp(jnp.where(causal, cs_col[h] - cs_t[h:h + 1, :], NEG)) for h in heads]
    x_h = [xs[:, h * SSM_HEAD_DIM:(h + 1) * SSM_HEAD_DIM] for h in heads]
    xdt = [x_h[h] * dt[:, h:h + 1] for h in heads]
    h_prev = [h_ref[0, h] for h in heads]
    y_intra = [_dot(cb[h // hpg] * seg[h], xdt[h]) for h in heads]
    y_inter = [_dot_nt(cm[h // hpg], h_prev[h]) * jnp.exp(cs_col[h]) for h in heads]
    st = [_dot_tn(xdt[h] * jnp.exp(last[h] - cs_col[h]), bm[h // hpg]) for h in heads]
    for h in heads:
        ys_ref[:, h * SSM_HEAD_DIM:(h + 1) * SSM_HEAD_DIM] = y_intra[h] + y_inter[h] + d_ref[:, h:h + 1] * x_h[h]
        h_ref[0, h] = h_prev[h] * jnp.exp(last[h]) + st[h]

    y = ys_ref[...] * jax.nn.silu(z_ref[0])
    half = SSM_D_INNER // SSM_GROUPS
    for g, yg in enumerate(_group_rms(y, nw_ref[...])):
        y_ref[0, :, g * half:(g + 1) * half] = yg


def _small(a):
    return pl.BlockSpec(a.shape, lambda *_: (0,) * a.ndim)


def ssd_prompt(main, tail, conv0, h0, conv_w, conv_b, dt_bias, a_log, d_skip, norm_w, *, bsz, seq):
    nc = seq // SSM_CHUNK
    ch = SSM_CHUNK
    small = [conv_w, conv_b.reshape(1, -1), _pad_tail(dt_bias.reshape(1, -1)), _pad_tail(a_log.reshape(1, -1)),
             _pad_tail(d_skip.reshape(1, -1)), norm_w.reshape(1, -1)]
    return pl.pallas_call(
        _ssd_kernel,
        grid=(bsz, nc),
        in_specs=[
            pl.BlockSpec((1, ch, SSM_D_INNER), lambda b, c: (b, c, 0)),
            pl.BlockSpec((1, ch, SSM_D_INNER), lambda b, c: (b, c, 1)),
            pl.BlockSpec((1, ch, SSM_BC), lambda b, c: (b, c, HYB_BC0 // SSM_BC)),
            pl.BlockSpec((1, ch, LANES), lambda b, c: (b, c, 0)),
            pl.BlockSpec((1, SSM_CONV - 1, SSM_XBC), lambda b, c: (b, 0, 0)),
        ] + [_small(a) for a in small] + [
            pl.BlockSpec((1, SSM_HEADS, SSM_HEAD_DIM, SSM_STATE), lambda b, c: (b, 0, 0, 0)),
        ],
        out_specs=[
            pl.BlockSpec((1, ch, SSM_D_INNER), lambda b, c: (b, c, 0)),
            pl.BlockSpec((1, SSM_CONV - 1, SSM_XBC), lambda b, c: (b, 0, 0)),
            pl.BlockSpec((1, SSM_HEADS, SSM_HEAD_DIM, SSM_STATE), lambda b, c: (b, 0, 0, 0)),
        ],
        out_shape=[jax.ShapeDtypeStruct((bsz, seq, SSM_D_INNER), F32),
                   jax.ShapeDtypeStruct((bsz, SSM_CONV - 1, SSM_XBC), F32),
                   jax.ShapeDtypeStruct((bsz, SSM_HEADS, SSM_HEAD_DIM, SSM_STATE), F32)],
        scratch_shapes=[pltpu.VMEM((8 + ch, SSM_XBC), F32), pltpu.VMEM((ch, SSM_D_INNER), F32)],
        compiler_params=_cparams("parallel", "arbitrary"),
        name="ssd_prompt",
    )(main, main, main, tail, conv0, *small, h0)


def _row_to_col(row, eye):
    return jnp.sum(jnp.where(eye, row, 0.0), axis=1, keepdims=True)


def _col_to_row(col, eye):
    return jnp.sum(jnp.where(eye, col, 0.0), axis=0, keepdims=True)


def _conv_step(c0_ref, w_ref, x_row, c0, c1):
    acc = w_ref[SSM_CONV - 1:SSM_CONV, c0:c1] * x_row
    for i in range(SSM_CONV - 1):
        acc = acc + w_ref[i:i + 1, c0:c1] * c0_ref[0, i:i + 1, c0:c1]
    return acc


def _ssd_step_kernel(z_ref, xs_ref, bc_ref, dt_ref, c0_ref, cw_ref, cb_ref, dtb_ref, alog_ref, d_ref, nw_ref,
                     h0_ref, y_ref, co_ref, h_ref, ys_ref):
    xs_raw = xs_ref[0]
    bc_raw = bc_ref[0]
    xs = jax.nn.silu(_conv_step(c0_ref, cw_ref, xs_raw, 0, SSM_D_INNER) + cb_ref[:, 0:SSM_D_INNER])
    bc = jax.nn.silu(_conv_step(c0_ref, cw_ref, bc_raw, SSM_D_INNER, SSM_XBC) + cb_ref[:, SSM_D_INNER:SSM_XBC])
    co_ref[0, 0:SSM_CONV - 2, :] = c0_ref[0, 1:SSM_CONV - 1, :]
    co_ref[0, SSM_CONV - 2:SSM_CONV - 1, 0:SSM_D_INNER] = xs_raw
    co_ref[0, SSM_CONV - 2:SSM_CONV - 1, SSM_D_INNER:SSM_XBC] = bc_raw

    dt = jax.nn.softplus(dt_ref[0] + dtb_ref[...])
    dec = jnp.exp(dt * (-jnp.exp(alog_ref[...])))
    dskip = d_ref[...]
    eye = _iota((LANES, LANES), 0) == _iota((LANES, LANES), 1)
    upper = _iota((LANES, 1), 0) >= SSM_HEAD_DIM
    for j in range(SSM_HEADS // 2):
        h0i, h1i = 2 * j, 2 * j + 1
        g = h0i // (SSM_HEADS // SSM_GROUPS)
        bm = bc[:, g * SSM_STATE:(g + 1) * SSM_STATE]
        cm = bc[:, (SSM_GROUPS + g) * SSM_STATE:(SSM_GROUPS + g + 1) * SSM_STATE]
        x_row = xs[:, j * LANES:(j + 1) * LANES]
        x_col = _row_to_col(x_row, eye)
        pick = lambda v: jnp.where(upper, v[:, h1i:h1i + 1], v[:, h0i:h0i + 1])
        hp = h0_ref[0, h0i:h1i + 1].reshape(2 * SSM_HEAD_DIM, SSM_STATE)
        hn = hp * pick(dec) + (x_col * pick(dt)) * bm
        h_ref[0, h0i:h1i + 1] = hn.reshape(2, SSM_HEAD_DIM, SSM_STATE)
        y_col = jnp.sum(hn * cm, axis=1, keepdims=True) + pick(dskip) * x_col
        ys_ref[:, j * LANES:(j + 1) * LANES] = _col_to_row(y_col, eye)

    y = ys_ref[...] * jax.nn.silu(z_ref[0])
    half = SSM_D_INNER // SSM_GROUPS
    for g, yg in enumerate(_group_rms(y, nw_ref[...])):
        y_ref[0, :, g * half:(g + 1) * half] = yg


def ssd_step(main, tail, conv0, h0, conv_w, conv_b, dt_bias, a_log, d_skip, norm_w):
    bsz = main.shape[0]
    small = [conv_w, conv_b.reshape(1, -1), _pad_tail(dt_bias.reshape(1, -1)), _pad_tail(a_log.reshape(1, -1)),
             _pad_tail(d_skip.reshape(1, -1)), norm_w.reshape(1, -1)]
    hspec = pl.BlockSpec((1, SSM_HEADS, SSM_HEAD_DIM, SSM_STATE), lambda b: (b, 0, 0, 0))
    cspec = pl.BlockSpec((1, SSM_CONV - 1, SSM_XBC), lambda b: (b, 0, 0))
    return pl.pallas_call(
        _ssd_step_kernel,
        grid=(bsz,),
        in_specs=[
            pl.BlockSpec((1, 1, SSM_D_INNER), lambda b: (b, 0, 0)),
            pl.BlockSpec((1, 1, SSM_D_INNER), lambda b: (b, 0, 1)),
            pl.BlockSpec((1, 1, SSM_BC), lambda b: (b, 0, HYB_BC0 // SSM_BC)),
            pl.BlockSpec((1, 1, LANES), lambda b: (b, 0, 0)),
            cspec,
        ] + [_small(a) for a in small] + [hspec],
        out_specs=[pl.BlockSpec((1, 1, SSM_D_INNER), lambda b: (b, 0, 0)), cspec, hspec],
        out_shape=[jax.ShapeDtypeStruct((bsz, 1, SSM_D_INNER), F32),
                   jax.ShapeDtypeStruct(conv0.shape, F32),
                   jax.ShapeDtypeStruct(h0.shape, F32)],
        scratch_shapes=[pltpu.VMEM((1, SSM_D_INNER), F32)],
        compiler_params=_cparams("parallel"),
        name="ssd_step",
    )(main, main, main, tail, conv0, *small, h0)


def _l2norm(x):
    return x * lax.rsqrt(jnp.sum(x * x, axis=-1, keepdims=True) + EPS)


def _unit_lower_inverse(ns, eye):
    size = ns[0].shape[0]
    ps = [eye - n for n in ns]
    ms = [_dot(n, n) for n in ns]
    power = 2
    while 2 * power < size:
        pms = [_dot(jnp.concatenate([p, m], axis=0), m) for p, m in zip(ps, ms)]
        ps = [p + pm[:size] for p, pm in zip(ps, pms)]
        ms = [pm[size:] for pm in pms]
        power *= 2
    return [p + _dot(p, m) for p, m in zip(ps, ms)]


def _gdn_gates(ba, dtb_ref, alog_ref):
    beta = jax.nn.sigmoid(ba)
    g = -jnp.exp(alog_ref[...]) * jax.nn.softplus(ba + dtb_ref[...])
    return beta, g


def _gdn_kernel(q_ref, k_ref, v_ref, z_ref, ba_ref, c0_ref, cw_ref, dtb_ref, alog_ref, nw_ref, s0_ref,
                o_ref, co_ref, s_ref, xpad_ref):
    c = pl.program_id(1)
    ch = GDN_CHUNK
    base = 8
    lo = base - (GDN_CONV - 1)
    nh = GDN_V_HEADS

    @pl.when(c == 0)
    def _():
        xpad_ref[lo:base, :] = c0_ref[0]
        s_ref[...] = s0_ref[...]

    xpad_ref[base:base + ch, 0:GDN_QK_W] = q_ref[0]
    xpad_ref[base:base + ch, GDN_QK_W:2 * GDN_QK_W] = k_ref[0]
    xpad_ref[base:base + ch, 2 * GDN_QK_W:GDN_QKV] = v_ref[0]
    conv = cw_ref[0:1, :] * xpad_ref[lo:lo + ch, :]
    for i in range(1, GDN_CONV):
        conv = conv + cw_ref[i:i + 1, :] * xpad_ref[lo + i:lo + i + ch, :]
    tail = xpad_ref[base + ch - (GDN_CONV - 1):base + ch, :]
    co_ref[0] = tail
    xpad_ref[lo:base, :] = tail
    qkv = jax.nn.silu(conv)

    beta, g = _gdn_gates(ba_ref[0], dtb_ref, alog_ref)
    row = _iota((ch, ch), 0)
    colv = _iota((ch, ch), 1)
    incl = row >= colv
    strict = row > colv
    eye = (row == colv).astype(F32)
    gcum = _dot(incl.astype(F32), g, precision=HIGHEST)
    gcum_t = jnp.concatenate([gcum, jnp.zeros((LANES - ch, LANES), F32)], axis=0).T

    heads = range(nh)
    rep = nh // GDN_QK_HEADS
    qn = [_l2norm(qkv[:, j * GDN_DK:(j + 1) * GDN_DK]) * (GDN_DK ** -0.5) for j in range(GDN_QK_HEADS)]
    kn = [_l2norm(qkv[:, GDN_QK_W + j * GDN_DK:GDN_QK_W + (j + 1) * GDN_DK]) for j in range(GDN_QK_HEADS)]
    kk = [_dot_nt(k, k) for k in kn]
    qk = [_dot_nt(q, k) for q, k in zip(qn, kn)]
    gc_col = [gcum[:, nh + h:nh + h + 1] for h in heads]
    gc_last = [gcum[ch - 1:ch, nh + h:nh + h + 1] for h in heads]
    beta_col = [beta[:, h:h + 1] for h in heads]
    dec = [jnp.exp(jnp.where(incl, gc_col[h] - gcum_t[nh + h:nh + h + 1, 0:ch], NEG)) for h in heads]
    t_inv = _unit_lower_inverse(
        [jnp.where(strict, kk[h // rep] * dec[h], 0.0) * beta_col[h] for h in heads], eye)
    eg = [jnp.exp(gc_col[h]) for h in heads]
    s_prev = [s_ref[0, h] for h in heads]
    both = [_dot(jnp.concatenate([kn[h // rep] * (beta_col[h] * eg[h]), qn[h // rep] * eg[h]], axis=0), s_prev[h])
            for h in heads]
    u = [_dot(t_inv[h], qkv[:, 2 * GDN_QK_W + h * GDN_DV:2 * GDN_QK_W + (h + 1) * GDN_DV] * beta_col[h]
              - both[h][:ch]) for h in heads]
    o = [both[h][ch:] + _dot(qk[h // rep] * dec[h], u[h]) for h in heads]
    s_new = [s_prev[h] * jnp.exp(gc_last[h]) + _dot_tn(kn[h // rep] * jnp.exp(gc_last[h] - gc_col[h]), u[h])
             for h in heads]
    for h in heads:
        s_ref[0, h] = s_new[h]
        z_h = z_ref[0, :, h * GDN_DV:(h + 1) * GDN_DV]
        o_ref[0, :, h * GDN_DV:(h + 1) * GDN_DV] = _rms(o[h]) * nw_ref[...] * jax.nn.silu(z_h)


def _gdn_gate_params(dt_bias, a_log):
    nh = GDN_V_HEADS
    dtb = jnp.zeros((1, LANES), F32).at[0, nh:2 * nh].set(dt_bias)
    alog = jnp.zeros((1, LANES), F32).at[0, nh:2 * nh].set(a_log)
    return dtb, alog


def gdn_prompt(main, tail, conv0, s0, conv_w, dt_bias, a_log, norm_w, *, bsz, seq):
    ch = GDN_CHUNK
    nc = seq // ch
    dtb, alog = _gdn_gate_params(dt_bias, a_log)
    small = [conv_w, dtb, alog, norm_w.reshape(1, -1)]
    sspec = pl.BlockSpec((1, GDN_V_HEADS, GDN_DK, GDN_DV), lambda b, c: (b, 0, 0, 0))
    cspec = pl.BlockSpec((1, GDN_CONV - 1, GDN_QKV), lambda b, c: (b, 0, 0))
    return pl.pallas_call(
        _gdn_kernel,
        grid=(bsz, nc),
        in_specs=[
            pl.BlockSpec((1, ch, GDN_QK_W), lambda b, c: (b, c, 0)),
            pl.BlockSpec((1, ch, GDN_QK_W), lambda b, c: (b, c, 1)),
            pl.BlockSpec((1, ch, GDN_VW), lambda b, c: (b, c, 1)),
            pl.BlockSpec((1, ch, GDN_VW), lambda b, c: (b, c, 2)),
            pl.BlockSpec((1, ch, LANES), lambda b, c: (b, c, 0)),
            cspec,
        ] + [_small(a) for a in small] + [sspec],
        out_specs=[pl.BlockSpec((1, ch, GDN_VW), lambda b, c: (b, c, 0)), cspec, sspec],
        out_shape=[jax.ShapeDtypeStruct((bsz, seq, GDN_VW), F32),
                   jax.ShapeDtypeStruct((bsz, GDN_CONV - 1, GDN_QKV), F32),
                   jax.ShapeDtypeStruct((bsz, GDN_V_HEADS, GDN_DK, GDN_DV), F32)],
        scratch_shapes=[pltpu.VMEM((8 + ch, GDN_QKV), F32)],
        compiler_params=_cparams("parallel", "arbitrary"),
        name="gdn_prompt",
    )(main, main, main, main, tail, conv0, *small, s0)


def _gdn_conv_step(c0_ref, w_ref, x_row, c0, c1):
    acc = w_ref[GDN_CONV - 1:GDN_CONV, c0:c1] * x_row
    for i in range(GDN_CONV - 1):
        acc = acc + w_ref[i:i + 1, c0:c1] * c0_ref[0, i:i + 1, c0:c1]
    return acc


def _gdn_step_kernel(q_ref, k_ref, v_ref, z_ref, ba_ref, c0_ref, cw_ref, dtb_ref, alog_ref, nw_ref, s0_ref,
                     o_ref, co_ref, s_ref):
    nh = GDN_V_HEADS
    q_raw, k_raw, v_raw = q_ref[0], k_ref[0], v_ref[0]
    q = jax.nn.silu(_gdn_conv_step(c0_ref, cw_ref, q_raw, 0, GDN_QK_W))
    k = jax.nn.silu(_gdn_conv_step(c0_ref, cw_ref, k_raw, GDN_QK_W, 2 * GDN_QK_W))
    v = jax.nn.silu(_gdn_conv_step(c0_ref, cw_ref, v_raw, 2 * GDN_QK_W, GDN_QKV))
    co_ref[0, 0:GDN_CONV - 2, :] = c0_ref[0, 1:GDN_CONV - 1, :]
    co_ref[0, GDN_CONV - 2:GDN_CONV - 1, 0:GDN_QK_W] = q_raw
    co_ref[0, GDN_CONV - 2:GDN_CONV - 1, GDN_QK_W:2 * GDN_QK_W] = k_raw
    co_ref[0, GDN_CONV - 2:GDN_CONV - 1, 2 * GDN_QK_W:GDN_QKV] = v_raw

    beta, g = _gdn_gates(ba_ref[0], dtb_ref, alog_ref)
    eg_all = jnp.exp(g)
    eye = _iota((LANES, LANES), 0) == _iota((LANES, LANES), 1)
    for j in range(GDN_QK_HEADS):
        qn = _l2norm(q[:, j * GDN_DK:(j + 1) * GDN_DK]) * (GDN_DK ** -0.5)
        kn = _l2norm(k[:, j * GDN_DK:(j + 1) * GDN_DK])
        qk = jnp.sum(qn * kn, axis=-1, keepdims=True)
        q_col = _row_to_col(qn, eye)
        k_col = _row_to_col(kn, eye)
        for h in range(j * (nh // GDN_QK_HEADS), (j + 1) * (nh // GDN_QK_HEADS)):
            b_h = beta[:, h:h + 1]
            eg = eg_all[:, nh + h:nh + h + 1]
            s_prev = s0_ref[0, h]
            ks = jnp.sum(s_prev * k_col, axis=0, keepdims=True)
            qs = jnp.sum(s_prev * q_col, axis=0, keepdims=True)
            v_h = v[:, h * GDN_DV:(h + 1) * GDN_DV]
            u = v_h * b_h - (b_h * eg) * ks
            o = eg * qs + qk * u
            s_ref[0, h] = s_prev * eg + k_col * u
            z_h = z_ref[0, :, h * GDN_DV:(h + 1) * GDN_DV]
            o_ref[0, :, h * GDN_DV:(h + 1) * GDN_DV] = _rms(o) * nw_ref[...] * jax.nn.silu(z_h)


def gdn_step(main, tail, conv0, s0, conv_w, dt_bias, a_log, norm_w):
    bsz = main.shape[0]
    dtb, alog = _gdn_gate_params(dt_bias, a_log)
    small = [conv_w, dtb, alog, norm_w.reshape(1, -1)]
    sspec = pl.BlockSpec((1, GDN_V_HEADS, GDN_DK, GDN_DV), lambda b: (b, 0, 0, 0))
    cspec = pl.BlockSpec((1, GDN_CONV - 1, GDN_QKV), lambda b: (b, 0, 0))
    return pl.pallas_call(
        _gdn_step_kernel,
        grid=(bsz,),
        in_specs=[
            pl.BlockSpec((1, 1, GDN_QK_W), lambda b: (b, 0, 0)),
            pl.BlockSpec((1, 1, GDN_QK_W), lambda b: (b, 0, 1)),
            pl.BlockSpec((1, 1, GDN_VW), lambda b: (b, 0, 1)),
            pl.BlockSpec((1, 1, GDN_VW), lambda b: (b, 0, 2)),
            pl.BlockSpec((1, 1, LANES), lambda b: (b, 0, 0)),
            cspec,
        ] + [_small(a) for a in small] + [sspec],
        out_specs=[pl.BlockSpec((1, 1, GDN_VW), lambda b: (b, 0, 0)), cspec, sspec],
        out_shape=[jax.ShapeDtypeStruct((bsz, 1, GDN_VW), F32),
                   jax.ShapeDtypeStruct(conv0.shape, F32),
                   jax.ShapeDtypeStruct(s0.shape, F32)],
        compiler_params=_cparams("parallel"),
        name="gdn_step",
    )(main, main, main, main, tail, conv0, *small, s0)


def _pad_tail(w):
    return jnp.pad(w, ((0, 0), (0, LANES - w.shape[1])))


def _prep_hyb_in(w):
    a = A_WIDTH
    q, k, v = w[:, 0:a], w[:, a:2 * a], w[:, 2 * a:3 * a]
    z = w[:, 3 * a:3 * a + SSM_D_INNER]
    x0 = 3 * a + SSM_D_INNER
    xs = w[:, x0:x0 + SSM_D_INNER]
    bc = w[:, x0 + SSM_D_INNER:x0 + SSM_XBC]
    dt = w[:, x0 + SSM_XBC:]
    return jnp.concatenate([z, xs, q, k, v, bc], axis=1).astype(BF16), _pad_tail(dt).astype(BF16)


def _prep_gdn_in(w):
    return w[:, :GDN_MAIN].astype(BF16), _pad_tail(w[:, GDN_MAIN:]).astype(BF16)


def _row_tile(m, cap):
    return m if m <= cap else cap


def kernel(x_prompt, x_sample, cache_attn_k, cache_attn_v, state_ssm_conv, state_ssm, state_gdn_conv, state_gdn, rel_bias, norm_mix_pre, norm_mix_post, norm_ffn_pre, norm_ffn_post, w_hyb_in, ssm_conv_w, ssm_conv_b, ssm_dt_bias, ssm_a_log, ssm_d, ssm_norm_w, w_hyb_out, w_gdn_in, gdn_conv_w, gdn_dt_bias, gdn_a_log, gdn_norm_w, w_gdn_out, w_ffn_gate, w_ffn_up, w_ffn_down):
    depth = norm_mix_pre.shape[0]
    d_model = x_prompt.shape[-1]
    n_hyb, n_gdn = w_hyb_in.shape[0], w_gdn_in.shape[0]

    hyb_in = [_prep_hyb_in(w_hyb_in[i]) for i in range(n_hyb)]
    hyb_out = [(w_hyb_out[i, :A_WIDTH].astype(BF16), w_hyb_out[i, A_WIDTH:].astype(BF16)) for i in range(n_hyb)]
    gdn_in = [_prep_gdn_in(w_gdn_in[i]) for i in range(n_gdn)]
    gdn_out = [w_gdn_out[i].astype(BF16) for i in range(n_gdn)]
    ffn_w = [(w_ffn_gate[l].astype(BF16), w_ffn_up[l].astype(BF16), w_ffn_down[l].astype(BF16))
             for l in range(depth)]
    bias_tiles = _attn_bias_tiles(rel_bias)

    def trunk(x3, k_pre, v_pre, sconv, sssm, gconv, gstate):
        bsz, seq, _ = x3.shape
        m = bsz * seq
        step = seq == 1
        tm_big = _row_tile(m, 1024)
        tm = _row_tile(m, 512)
        x = x3.reshape(m, d_model)
        nk, nv, nsc, nss, ngc, ngs = [], [], [], [], [], []
        for l in range(depth):
            i = l // 2
            if l % 2 == 0:
                w_main, w_tail = hyb_in[i]
                main, tail = inproj(x, norm_mix_pre[l], w_main, w_tail, tm=tm_big, tn=512)
                main3 = main.reshape(bsz, seq, HYB_MAIN)
                tail3 = tail.reshape(bsz, seq, LANES)
                ssm_args = (ssm_conv_w[i], ssm_conv_b[i], ssm_dt_bias[i], ssm_a_log[i], ssm_d[i], ssm_norm_w[i])
                if step:
                    past = k_pre.shape[2]
                    o_attn, k_new, v_new = attention_step(
                        main3, k_pre[i].reshape(bsz, past, A_WIDTH), v_pre[i].reshape(bsz, past, A_WIDTH),
                        _attn_logw(rel_bias, past))
                    y, c_new, s_new = ssd_step(main3, tail3, sconv[i], sssm[i], *ssm_args)
                else:
                    o_attn = attention_prompt(main3, bias_tiles, bsz=bsz, seq=seq)
                    keep = min(A_PATTERNS[-1][0], seq)
                    k_new = main3[:, seq - keep:, HYB_K0:HYB_K0 + A_WIDTH]
                    v_new = main3[:, seq - keep:, HYB_V0:HYB_V0 + A_WIDTH]
                    y, c_new, s_new = ssd_prompt(main3, tail3, sconv[i], sssm[i], *ssm_args, bsz=bsz, seq=seq)
                nk.append(k_new.reshape(bsz, -1, A_HEADS, A_HEAD_DIM))
                nv.append(v_new.reshape(bsz, -1, A_HEADS, A_HEAD_DIM))
                nsc.append(c_new)
                nss.append(s_new)
                x = outproj([o_attn.reshape(m, A_WIDTH), y.reshape(m, SSM_D_INNER)], list(hyb_out[i]),
                            x, norm_mix_post[l], tm=tm)
            else:
                w_main, w_tail = gdn_in[i]
                main, tail = inproj(x, norm_mix_pre[l], w_main, w_tail, tm=tm_big, tn=512)
                main3 = main.reshape(bsz, seq, GDN_MAIN)
                tail3 = tail.reshape(bsz, seq, LANES)
                gdn_args = (gdn_conv_w[i], gdn_dt_bias[i], gdn_a_log[i], gdn_norm_w[i])
                if step:
                    o, c_new, s_new = gdn_step(main3, tail3, gconv[i], gstate[i], *gdn_args)
                else:
                    o, c_new, s_new = gdn_prompt(main3, tail3, gconv[i], gstate[i], *gdn_args, bsz=bsz, seq=seq)
                ngc.append(c_new)
                ngs.append(s_new)
                x = outproj([o.reshape(m, GDN_VW)], [gdn_out[i]], x, norm_mix_post[l], tm=tm)
            wg, wu, wd = ffn_w[l]
            x = ffn(x, norm_ffn_pre[l], wg, wu, wd, norm_ffn_post[l], tm=tm, tf=wg.shape[1] // 2)
        return (x.reshape(bsz, seq, d_model), jnp.stack(nk), jnp.stack(nv), jnp.stack(nsc), jnp.stack(nss),
                jnp.stack(ngc), jnp.stack(ngs))

    bsz = x_prompt.shape[0]
    dt_p = x_prompt.dtype
    p_sc0 = jnp.zeros((n_hyb, bsz, SSM_CONV - 1, SSM_XBC), dt_p)
    p_ss0 = jnp.zeros((n_hyb, bsz, SSM_HEADS, SSM_HEAD_DIM, SSM_STATE), F32)
    p_gc0 = jnp.zeros((n_gdn, bsz, GDN_CONV - 1, GDN_QKV), dt_p)
    p_gs0 = jnp.zeros((n_gdn, bsz, GDN_V_HEADS, GDN_DK, GDN_DV), F32)
    y_prompt, pk, pv, psc, pss, pgc, pgs = trunk(x_prompt, None, None, p_sc0, p_ss0, p_gc0, p_gs0)
    y_sample, sk, sv, ssc, sss, sgc, sgs = trunk(
        x_sample, cache_attn_k, cache_attn_v, state_ssm_conv, state_ssm, state_gdn_conv, state_gdn)
    return (y_prompt, y_sample, pk, pv, psc, pss, pgc, pgs, sk, sv, ssc, sss, sgc, sgs)
```

```python
import functools
import math

import numpy as np
import jax
import jax.numpy as jnp
from jax import lax
from jax.experimental import pallas as pl
from jax.experimental.pallas import tpu as pltpu

F32 = jnp.float32
BF16 = jnp.bfloat16
EPS = 1e-6
NEG = -1e30
HIGHEST = lax.Precision.HIGHEST

VMEM_LIMIT_BYTES = 56 * 1024 * 1024
LANES = 128

A_HEADS = 8
A_HEAD_DIM = 64
A_WIDTH = A_HEADS * A_HEAD_DIM
A_PATTERNS = ((128, 1), (512, 4), (2048, 16))
A_BAND = 128
ATTN_GROUP = 4
REL_BUCKETS = 32
REL_MAX_DIST = 2048

SSM_D_INNER = 1024
SSM_HEAD_DIM = 64
SSM_HEADS = SSM_D_INNER // SSM_HEAD_DIM
SSM_GROUPS = 2
SSM_STATE = 128
SSM_CONV = 4
SSM_CHUNK = 128
SSM_BC = 2 * SSM_GROUPS * SSM_STATE
SSM_XBC = SSM_D_INNER + SSM_BC

GDN_QK_HEADS = 8
GDN_V_HEADS = 16
GDN_DK = 128
GDN_DV = 128
GDN_CONV = 4
GDN_CHUNK = 64
GDN_QK_W = GDN_QK_HEADS * GDN_DK
GDN_VW = GDN_V_HEADS * GDN_DV
GDN_QKV = 2 * GDN_QK_W + GDN_VW

HYB_MAIN = 2 * SSM_D_INNER + 3 * A_WIDTH + SSM_BC
HYB_Q0 = 2 * SSM_D_INNER
HYB_K0 = HYB_Q0 + A_WIDTH
HYB_V0 = HYB_K0 + A_WIDTH
HYB_BC0 = HYB_V0 + A_WIDTH
GDN_MAIN = GDN_QKV + GDN_VW


def _cparams(*sem):
    return pltpu.CompilerParams(dimension_semantics=sem, vmem_limit_bytes=VMEM_LIMIT_BYTES)


def _rms(x):
    return x * lax.rsqrt(jnp.mean(x * x, axis=-1, keepdims=True) + EPS)


def _dot(a, b, **kw):
    return jnp.dot(a, b, preferred_element_type=F32, **kw)


def _dot_nt(a, b, **kw):
    return lax.dot_general(a, b, (((1,), (1,)), ((), ())), preferred_element_type=F32, **kw)


def _dot_tn(a, b, **kw):
    return lax.dot_general(a, b, (((0,), (0,)), ((), ())), preferred_element_type=F32, **kw)


def _iota(shape, dim):
    return lax.broadcasted_iota(jnp.int32, shape, dim)


def _inproj_kernel(x_ref, g_ref, w_ref, wt_ref, o_ref, t_ref, h_ref):
    @pl.when(pl.program_id(1) == 0)
    def _():
        hb = (_rms(x_ref[...]) * g_ref[...]).astype(BF16)
        h_ref[...] = hb
        t_ref[...] = _dot(hb, wt_ref[...])

    o_ref[...] = _dot(h_ref[...], w_ref[...])


def inproj(x, g, w_main, w_tail, *, tm, tn):
    m, d = x.shape
    n = w_main.shape[1]
    return pl.pallas_call(
        _inproj_kernel,
        grid=(m // tm, n // tn),
        in_specs=[
            pl.BlockSpec((tm, d), lambda i, j: (i, 0)),
            pl.BlockSpec((1, d), lambda i, j: (0, 0)),
            pl.BlockSpec((d, tn), lambda i, j: (0, j)),
            pl.BlockSpec((d, LANES), lambda i, j: (0, 0)),
        ],
        out_specs=[
            pl.BlockSpec((tm, tn), lambda i, j: (i, j)),
            pl.BlockSpec((tm, LANES), lambda i, j: (i, 0)),
        ],
        out_shape=[jax.ShapeDtypeStruct((m, n), F32), jax.ShapeDtypeStruct((m, LANES), F32)],
        scratch_shapes=[pltpu.VMEM((tm, d), BF16)],
        compiler_params=_cparams("parallel", "arbitrary"),
        name="inproj",
    )(x, g.reshape(1, d), w_main, w_tail)


def _outproj_kernel(*refs, n_in):
    a_refs, w_refs = refs[:n_in], refs[n_in:2 * n_in]
    x_ref, g_ref, o_ref = refs[2 * n_in:]
    acc = None
    for a_ref, w_ref in zip(a_refs, w_refs):
        t = _dot(a_ref[...].astype(BF16), w_ref[...])
        acc = t if acc is None else acc + t
    o_ref[...] = x_ref[...] + _rms(acc) * g_ref[...]


def outproj(acts, weights, x, g, *, tm):
    m, d = x.shape
    n_in = len(acts)
    in_specs = [pl.BlockSpec((tm, a.shape[1]), lambda i: (i, 0)) for a in acts]
    in_specs += [pl.BlockSpec(w.shape, lambda i: (0, 0)) for w in weights]
    in_specs += [pl.BlockSpec((tm, d), lambda i: (i, 0)), pl.BlockSpec((1, d), lambda i: (0, 0))]
    return pl.pallas_call(
        functools.partial(_outproj_kernel, n_in=n_in),
        grid=(m // tm,),
        in_specs=in_specs,
        out_specs=pl.BlockSpec((tm, d), lambda i: (i, 0)),
        out_shape=jax.ShapeDtypeStruct((m, d), F32),
        compiler_params=_cparams("parallel"),
        name="outproj",
    )(*acts, *weights, x, g.reshape(1, d))


def _ffn_kernel(x_ref, g1_ref, wg_ref, wu_ref, wd_ref, g2_ref, o_ref, h_ref, acc_ref):
    j = pl.program_id(1)

    @pl.when(j == 0)
    def _():
        h_ref[...] = (_rms(x_ref[...]) * g1_ref[...]).astype(BF16)
        acc_ref[...] = jnp.zeros_like(acc_ref)

    h = h_ref[...]
    a = jax.nn.silu(_dot(h, wg_ref[...])) * _dot(h, wu_ref[...])
    acc_ref[...] += _dot(a.astype(BF16), wd_ref[...])

    @pl.when(j == pl.num_programs(1) - 1)
    def _():
        o_ref[...] = x_ref[...] + _rms(acc_ref[...]) * g2_ref[...]


def ffn(x, g1, wg, wu, wd, g2, *, tm, tf):
    m, d = x.shape
    dff = wg.shape[1]
    return pl.pallas_call(
        _ffn_kernel,
        grid=(m // tm, dff // tf),
        in_specs=[
            pl.BlockSpec((tm, d), lambda i, j: (i, 0)),
            pl.BlockSpec((1, d), lambda i, j: (0, 0)),
            pl.BlockSpec((d, tf), lambda i, j: (0, j)),
            pl.BlockSpec((d, tf), lambda i, j: (0, j)),
            pl.BlockSpec((tf, d), lambda i, j: (j, 0)),
            pl.BlockSpec((1, d), lambda i, j: (0, 0)),
        ],
        out_specs=pl.BlockSpec((tm, d), lambda i, j: (i, 0)),
        out_shape=jax.ShapeDtypeStruct((m, d), F32),
        scratch_shapes=[pltpu.VMEM((tm, d), BF16), pltpu.VMEM((tm, d), F32)],
        compiler_params=_cparams("parallel", "arbitrary"),
        name="ffn",
    )(x, g1.reshape(1, d), wg, wu, wd, g2.reshape(1, d))


def _rel_buckets(dist):
    max_exact = REL_BUCKETS // 2
    n = np.maximum(dist, 1).astype(np.float32)
    large = max_exact + (np.log(n / max_exact) / math.log(REL_MAX_DIST / max_exact)
                         * (REL_BUCKETS - max_exact)).astype(np.int32)
    large = np.minimum(large, REL_BUCKETS - 1)
    return np.where(dist < max_exact, dist, large).astype(np.int32)


def _attn_bias_rows(rel_bias):
    u = np.arange(2 * A_BAND)
    valid = u <= A_BAND
    rows = []
    for (_, d) in A_PATTERNS:
        b = rel_bias[_rel_buckets(np.where(valid, A_BAND - u, 0) * d)]
        rows.append(jnp.where(valid[:, None], b.astype(F32), NEG))
    tl = jnp.transpose(jnp.stack(rows), (2, 0, 1))
    tl = tl.reshape(A_HEADS // 2, 2, len(A_PATTERNS), 2 * A_BAND)
    tl = jnp.transpose(tl, (0, 2, 1, 3))[:, :, :, None, :]
    return jnp.broadcast_to(tl, tl.shape[:3] + (8, 2 * A_BAND))


def _attn_kernel(q_ref, k_ref, v_ref, brow_ref, o_ref, m0_ref, m1_ref, l0_ref, l1_ref, acc_ref, b_ref, *, seq):
    scale = A_HEAD_DIM ** -0.5
    n_tiles = seq // A_BAND
    lane = _iota((A_BAND, LANES), 1)
    head0 = lane < A_HEAD_DIM
    m_refs, l_refs = (m0_ref, m1_ref), (l0_ref, l1_ref)
    for p in range(len(A_PATTERNS)):
        for h in range(2):
            row = jnp.broadcast_to(brow_ref[0, p, h, 0:1, :], (A_BAND, 2 * A_BAND))
            b_ref[0, p, h] = pltpu.roll(row, 0, 1, stride=1, stride_axis=0)
    for h in range(2):
        m_refs[h][...] = jnp.full(m_refs[h].shape, NEG, F32)
        l_refs[h][...] = jnp.zeros_like(l_refs[h])
    acc_ref[...] = jnp.zeros_like(acc_ref)

    for p, (_, d) in enumerate(A_PATTERNS):
        tiles_per_class = n_tiles // d

        def load_tile(idx, d=d, tiles_per_class=tiles_per_class):
            r = idx // tiles_per_class
            t = idx % tiles_per_class
            start = r + t * (d * A_BAND)
            has_prev = t > 0
            prev = jnp.where(has_prev, start - d * A_BAND, start)
            if d > 1:
                rows, prows = pl.ds(start, A_BAND, stride=d), pl.ds(prev, A_BAND, stride=d)
            else:
                rows, prows = pl.ds(pl.multiple_of(start, A_BAND), A_BAND), pl.ds(pl.multiple_of(prev, A_BAND), A_BAND)
            return dict(
                rows=rows, has_prev=has_prev, q=q_ref[rows, :] * scale,
                k2=jnp.concatenate([k_ref[prows, :], k_ref[rows, :]], axis=0).astype(BF16),
                v2=jnp.concatenate([v_ref[prows, :], v_ref[rows, :]], axis=0).astype(BF16),
                acc=acc_ref[rows, :], m=[m_refs[h][rows, :] for h in range(2)],
                l=[l_refs[h][rows, :] for h in range(2)])

        def tile_group(idx, carry, load_tile=load_tile, p=p):
            tiles = [load_tile(idx + i * (n_tiles // ATTN_GROUP)) for i in range(ATTN_GROUP)]
            chains = [(tile, h) for tile in tiles for h in range(2)]
            col = _iota((A_BAND, 2 * A_BAND), 1)
            qh = [jnp.where(head0 if h == 0 else jnp.logical_not(head0), tile["q"], 0.0).astype(BF16)
                  for tile, h in chains]
            s = [_dot_nt(qh[c], tile["k2"])
                 + jnp.where(jnp.logical_and(col < A_BAND, jnp.logical_not(tile["has_prev"])), NEG, b_ref[0, p, h])
                 for c, (tile, h) in enumerate(chains)]
            m_new = [jnp.maximum(tile["m"][h], jnp.max(s[c], axis=-1, keepdims=True))
                     for c, (tile, h) in enumerate(chains)]
            alpha = [jnp.exp(tile["m"][h] - m_new[c]) for c, (tile, h) in enumerate(chains)]
            pr = [jnp.exp(s[c] - jnp.concatenate([m_new[c], m_new[c]], axis=1)) for c in range(len(chains))]
            l_new = [alpha[c] * tile["l"][h] + jnp.sum(pr[c], axis=-1, keepdims=True)
                     for c, (tile, h) in enumerate(chains)]
            acc_new = [alpha[c] * tile["acc"] + _dot(pr[c].astype(BF16), tile["v2"])
                       for c, (tile, h) in enumerate(chains)]
            for c, (tile, h) in enumerate(chains):
                m_refs[h][tile["rows"], :] = m_new[c]
                l_refs[h][tile["rows"], :] = l_new[c]
                if h == 1:
                    acc_ref[tile["rows"], :] = jnp.where(head0, acc_new[c - 1], acc_new[c])
            return carry

        lax.fori_loop(0, n_tiles // ATTN_GROUP, tile_group, 0)

    lane_s = _iota((seq, LANES), 1)
    o_ref[...] = acc_ref[...] / jnp.where(lane_s < A_HEAD_DIM, l0_ref[...], l1_ref[...])


def attention_prompt(proj, bias_tiles, *, bsz, seq):
    hp = A_HEADS // 2
    qb, kb, vb = HYB_Q0 // LANES, HYB_K0 // LANES, HYB_V0 // LANES
    return pl.pallas_call(
        functools.partial(_attn_kernel, seq=seq),
        grid=(bsz, hp),
        in_specs=[
            pl.BlockSpec((None, seq, LANES), lambda b, h: (b, 0, qb + h)),
            pl.BlockSpec((None, seq, LANES), lambda b, h: (b, 0, kb + h)),
            pl.BlockSpec((None, seq, LANES), lambda b, h: (b, 0, vb + h)),
            pl.BlockSpec((1,) + bias_tiles.shape[1:], lambda b, h: (h, 0, 0, 0, 0)),
        ],
        out_specs=pl.BlockSpec((None, seq, LANES), lambda b, h: (b, 0, h)),
        out_shape=jax.ShapeDtypeStruct((bsz, seq, A_WIDTH), F32),
        scratch_shapes=[pltpu.VMEM((seq, LANES), F32)] * 5
        + [pltpu.VMEM((1, len(A_PATTERNS), 2, A_BAND, 2 * A_BAND), F32)],
        compiler_params=_cparams("parallel", "parallel"),
        name="attn_prompt",
    )(proj, proj, proj, bias_tiles)


def _attn_logw(rel_bias, past):
    dist = np.arange(past + 1)
    count = np.zeros(past + 1, np.float64)
    for (w, d) in A_PATTERNS:
        count += ((dist % d == 0) & (dist <= w)).astype(np.float64)
    logc = np.where(count > 0, np.log(np.maximum(count, 1.0)), 0.0).astype(np.float32)
    lw = rel_bias[_rel_buckets(dist)].astype(F32).T + logc[None, :]
    return jnp.where((count > 0)[None, :], lw, NEG)


ROLL_SPLIT = 8


def _roll_copies(k_ref, v_ref, ko_ref, vo_ref, sem):
    n_layers, bsz, past = k_ref.shape[:3]
    nb = bsz // ROLL_SPLIT
    copies = []
    for t, (src, dst) in enumerate(((k_ref, ko_ref), (v_ref, vo_ref))):
        for i in range(n_layers):
            for j in range(ROLL_SPLIT):
                bs = pl.ds(j * nb, nb)
                copies.append(pltpu.make_async_copy(
                    src.at[i, bs, pl.ds(1, past - 1)], dst.at[i, bs, pl.ds(0, past - 1)],
                    sem.at[0, t, i, j]))
                copies.append(pltpu.make_async_copy(
                    src.at[i, bs, pl.ds(past - 1, 1)], dst.at[i, bs, pl.ds(past - 1, 1)],
                    sem.at[1, t, i, j]))
    return copies


def _cache_roll_kernel(k_ref, v_ref, ko_ref, vo_ref, sem):
    copies = _roll_copies(k_ref, v_ref, ko_ref, vo_ref, sem)
    for c in copies:
        c.start()
    for c in copies:
        c.wait()


def cache_roll(cache_k, cache_v):
    n_layers, bsz = cache_k.shape[:2]
    assert bsz % ROLL_SPLIT == 0
    any_spec = pl.BlockSpec(memory_space=pl.ANY)
    return pl.pallas_call(
        _cache_roll_kernel,
        in_specs=[any_spec, any_spec],
        out_specs=[any_spec, any_spec],
        out_shape=[jax.ShapeDtypeStruct(cache_k.shape, cache_k.dtype),
                   jax.ShapeDtypeStruct(cache_v.shape, cache_v.dtype)],
        scratch_shapes=[pltpu.SemaphoreType.DMA((2, 2, n_layers, ROLL_SPLIT))],
        name="cache_roll",
    )(cache_k, cache_v)


def _attn_step_rows(past):
    runs, dists, lo = [], [], 0
    for (w, d) in sorted(A_PATTERNS, key=lambda wd: wd[1]):
        assert lo % d == 0 and past % d == 0 and w <= past
        n = (w - lo) // d
        runs.append((d, (past - w) // d, n))
        dists.append(w - d * np.arange(n))
        lo = w
    return runs, np.concatenate(dists)


def _attn_step_kernel(q_ref, kn_ref, vn_ref, bias_ref, b0_ref, knh_ref, vnh_ref, *rest, layer, past, runs):
    nv = len(runs)
    k_views, v_views = rest[:nv], rest[nv:2 * nv]
    ko_in, vo_in, o_ref, ko_ref, vo_ref, kbuf, vbuf, gsem, asem = rest[2 * nv:]
    del ko_in, vo_in
    b = pl.program_id(0)
    nb = pl.num_programs(0)

    def gathers(bb, slot):
        copies, off = [], 0
        for t, (views, buf) in enumerate(((k_views, kbuf), (v_views, vbuf))):
            off = 0
            for r, (d, first, n) in enumerate(runs):
                src = views[r].at[layer, bb, pl.ds(first, n)] if d == 1 else views[r].at[layer, bb, pl.ds(first, n), 0]
                copies.append(pltpu.make_async_copy(src, buf.at[slot, pl.ds(off, n)], gsem.at[slot, t, r]))
                off += n
        return copies

    def appends():
        return [pltpu.make_async_copy(knh_ref, ko_ref.at[layer, :, past - 1], asem.at[0]),
                pltpu.make_async_copy(vnh_ref, vo_ref.at[layer, :, past - 1], asem.at[1])]

    @pl.when(b == 0)
    def _():
        for c in gathers(0, 0) + appends():
            c.start()

    @pl.when(b + 1 < nb)
    def _():
        for c in gathers(b + 1, (b + 1) % 2):
            c.start()

    slot = b % 2
    for c in gathers(b, slot):
        c.wait()

    q8 = q_ref[0] * (A_HEAD_DIM ** -0.5)
    s = jnp.sum(kbuf[slot] * q8[None], axis=-1, keepdims=True) + bias_ref[...]
    s_new = jnp.sum(kn_ref[0] * q8, axis=-1, keepdims=True) + b0_ref[...]
    m = jnp.maximum(jnp.max(s, axis=0), s_new)
    p = jnp.exp(s - m[None])
    p_new = jnp.exp(s_new - m)
    den = jnp.sum(p, axis=0) + p_new
    o_ref[0] = (jnp.sum(p * vbuf[slot], axis=0) + p_new * vn_ref[0]) / den

    @pl.when(b == nb - 1)
    def _():
        for c in appends():
            c.wait()


def attention_step(q3, kn3, vn3, cache_k, cache_v, rolled_k, rolled_v, logw, *, layer):
    n_layers, bsz, past = cache_k.shape[:3]
    runs, dists = _attn_step_rows(past)
    n_rows = len(dists)
    bias = jnp.broadcast_to(logw[:, dists].T[:, :, None], (n_rows, A_HEADS, A_HEAD_DIM))
    bias0 = jnp.broadcast_to(logw[:, :1], (A_HEADS, A_HEAD_DIM))
    views = lambda c: [c if d == 1 else c.reshape(n_layers, bsz, past // d, d, A_HEADS, A_HEAD_DIM)
                       for (d, _, _) in runs]
    row = pl.BlockSpec((1, A_HEADS, A_HEAD_DIM), lambda b: (b, 0, 0))
    any_spec = pl.BlockSpec(memory_space=pl.ANY)
    n_any = 2 + 2 * len(runs) + 2
    n_in = 5 + n_any
    return pl.pallas_call(
        functools.partial(_attn_step_kernel, layer=layer, past=past, runs=tuple(runs)),
        grid=(bsz,),
        in_specs=[row, row, row, _small(bias), _small(bias0)] + [any_spec] * n_any,
        out_specs=[row, any_spec, any_spec],
        out_shape=[jax.ShapeDtypeStruct((bsz, A_HEADS, A_HEAD_DIM), F32),
                   jax.ShapeDtypeStruct(rolled_k.shape, rolled_k.dtype),
                   jax.ShapeDtypeStruct(rolled_v.shape, rolled_v.dtype)],
        scratch_shapes=[pltpu.VMEM((2, n_rows, A_HEADS, A_HEAD_DIM), F32),
                        pltpu.VMEM((2, n_rows, A_HEADS, A_HEAD_DIM), F32),
                        pltpu.SemaphoreType.DMA((2, 2, len(runs))),
                        pltpu.SemaphoreType.DMA((2,))],
        input_output_aliases={n_in - 2: 1, n_in - 1: 2},
        compiler_params=_cparams("arbitrary"),
        name="attn_step",
    )(q3, kn3, vn3, bias, bias0, kn3, vn3, *views(cache_k), *views(cache_v), rolled_k, rolled_v)


def _group_rms(y, w):
    half = SSM_D_INNER // SSM_GROUPS
    return [_rms(y[:, g * half:(g + 1) * half]) * w[:, g * half:(g + 1) * half] for g in range(SSM_GROUPS)]


def _ssd_kernel(z_ref, xs_ref, bc_ref, dt_ref, c0_ref, cw_ref, cb_ref, dtb_ref, alog_ref, d_ref, nw_ref,
                h0_ref, y_ref, co_ref, h_ref, xpad_ref, ys_ref):
    c = pl.program_id(1)
    ch = SSM_CHUNK
    base = 8
    lo = base - (SSM_CONV - 1)

    @pl.when(c == 0)
    def _():
        xpad_ref[lo:base, :] = c0_ref[0]
        h_ref[...] = h0_ref[...]

    xpad_ref[base:base + ch, 0:SSM_D_INNER] = xs_ref[0]
    xpad_ref[base:base + ch, SSM_D_INNER:SSM_XBC] = bc_ref[0]
    conv = cb_ref[...] + cw_ref[0:1, :] * xpad_ref[lo:lo + ch, :]
    for i in range(1, SSM_CONV):
        conv = conv + cw_ref[i:i + 1, :] * xpad_ref[lo + i:lo + i + ch, :]
    tail = xpad_ref[base + ch - (SSM_CONV - 1):base + ch, :]
    co_ref[0] = tail
    xpad_ref[lo:base, :] = tail
    xbc = jax.nn.silu(conv)
    xs = xbc[:, :SSM_D_INNER]

    dt = jax.nn.softplus(dt_ref[0] + dtb_ref[...])
    da = dt * (-jnp.exp(alog_ref[...]))
    row = _iota((ch, ch), 0)
    colv = _iota((ch, ch), 1)
    tril = (row >= colv).astype(F32)
    cs = _dot(tril, da, precision=HIGHEST)
    cs_t = cs.T
    causal = row >= colv

    heads = range(SSM_HEADS)
    hpg = SSM_HEADS // SSM_GROUPS
    bm = [xbc[:, SSM_D_INNER + g * SSM_STATE:SSM_D_INNER + (g + 1) * SSM_STATE] for g in range(SSM_GROUPS)]
    cm = [xbc[:, SSM_D_INNER + (SSM_GROUPS + g) * SSM_STATE:SSM_D_INNER + (SSM_GROUPS + g + 1) * SSM_STATE]
          for g in range(SSM_GROUPS)]
    cb = [_dot_nt(cm[g], bm[g]) for g in range(SSM_GROUPS)]
    dt_t = dt.T
    xs_t = xs.T
    w_t = dt_t * jnp.exp(cs_t[:, ch - 1:ch] - cs_t)
    e_last = jnp.exp(cs[ch - 1:ch, :])
    lane_lo = _iota((ch, LANES), 1) < SSM_HEAD_DIM
    row_lo = _iota((LANES, ch), 0) < SSM_HEAD_DIM
    pairs = range(SSM_HEADS // 2)
    csb = [jnp.broadcast_to(cs[:, h:h + 1], (ch, ch)) for h in heads]
    mix = [cb[h // hpg] * jnp.exp(jnp.where(causal, csb[h] - cs_t[h:h + 1, :], NEG)) * dt_t[h:h + 1, :]
           for h in heads]
    x_pair = [xs[:, j * LANES:(j + 1) * LANES] for j in pairs]
    y_intra = [jnp.where(lane_lo, _dot(mix[2 * j], x_pair[j]), _dot(mix[2 * j + 1], x_pair[j])) for j in pairs]
    h_pair = [h_ref[0, 2 * j:2 * j + 2].reshape(2 * SSM_HEAD_DIM, SSM_STATE) for j in pairs]
    y_inter = [_dot_nt(cm[2 * j // hpg], h_pair[j]) * jnp.exp(jnp.where(lane_lo, csb[2 * j], csb[2 * j + 1]))
               for j in pairs]
    xw_t = [xs_t[j * LANES:(j + 1) * LANES, :] * jnp.where(row_lo, w_t[2 * j:2 * j + 1, :], w_t[2 * j + 1:2 * j + 2, :])
            for j in pairs]
    st = [_dot(xw_t[j], bm[2 * j // hpg]) for j in pairs]
    for j in pairs:
        cols = slice(j * LANES, (j + 1) * LANES)
        ys_ref[:, cols] = y_intra[j] + y_inter[j] + d_ref[:, cols] * x_pair[j]
        decay = jnp.where(row_lo, e_last[:, 2 * j:2 * j + 1], e_last[:, 2 * j + 1:2 * j + 2])
        h_ref[0, 2 * j:2 * j + 2] = (h_pair[j] * decay + st[j]).reshape(2, SSM_HEAD_DIM, SSM_STATE)

    y = ys_ref[...] * jax.nn.silu(z_ref[0])
    half = SSM_D_INNER // SSM_GROUPS
    for g, yg in enumerate(_group_rms(y, nw_ref[...])):
        y_ref[0, :, g * half:(g + 1) * half] = yg


def _small(a):
    return pl.BlockSpec(a.shape, lambda *_: (0,) * a.ndim)


def ssd_prompt(main, tail, conv0, h0, conv_w, conv_b, dt_bias, a_log, d_skip, norm_w, *, bsz, seq):
    nc = seq // SSM_CHUNK
    ch = SSM_CHUNK
    small = [conv_w, conv_b.reshape(1, -1), _pad_tail(dt_bias.reshape(1, -1)), _pad_tail(a_log.reshape(1, -1)),
             jnp.repeat(d_skip, SSM_HEAD_DIM).reshape(1, -1), norm_w.reshape(1, -1)]
    return pl.pallas_call(
        _ssd_kernel,
        grid=(bsz, nc),
        in_specs=[
            pl.BlockSpec((1, ch, SSM_D_INNER), lambda b, c: (b, c, 0)),
            pl.BlockSpec((1, ch, SSM_D_INNER), lambda b, c: (b, c, 1)),
            pl.BlockSpec((1, ch, SSM_BC), lambda b, c: (b, c, HYB_BC0 // SSM_BC)),
            pl.BlockSpec((1, ch, LANES), lambda b, c: (b, c, 0)),
            pl.BlockSpec((1, SSM_CONV - 1, SSM_XBC), lambda b, c: (b, 0, 0)),
        ] + [_small(a) for a in small] + [
            pl.BlockSpec((1, SSM_HEADS, SSM_HEAD_DIM, SSM_STATE), lambda b, c: (b, 0, 0, 0)),
        ],
        out_specs=[
            pl.BlockSpec((1, ch, SSM_D_INNER), lambda b, c: (b, c, 0)),
            pl.BlockSpec((1, SSM_CONV - 1, SSM_XBC), lambda b, c: (b, 0, 0)),
            pl.BlockSpec((1, SSM_HEADS, SSM_HEAD_DIM, SSM_STATE), lambda b, c: (b, 0, 0, 0)),
        ],
        out_shape=[jax.ShapeDtypeStruct((bsz, seq, SSM_D_INNER), F32),
                   jax.ShapeDtypeStruct((bsz, SSM_CONV - 1, SSM_XBC), F32),
                   jax.ShapeDtypeStruct((bsz, SSM_HEADS, SSM_HEAD_DIM, SSM_STATE), F32)],
        scratch_shapes=[pltpu.VMEM((8 + ch, SSM_XBC), F32), pltpu.VMEM((ch, SSM_D_INNER), F32)],
        compiler_params=_cparams("parallel", "arbitrary"),
        name="ssd_prompt",
    )(main, main, main, tail, conv0, *small, h0)


def _row_to_col(row, eye):
    return jnp.sum(jnp.where(eye, row, 0.0), axis=1, keepdims=True)


def _col_to_row(col, eye):
    return jnp.sum(jnp.where(eye, col, 0.0), axis=0, keepdims=True)


def _conv_step(c0_ref, w_ref, x_row, c0, c1):
    acc = w_ref[SSM_CONV - 1:SSM_CONV, c0:c1] * x_row
    for i in range(SSM_CONV - 1):
        acc = acc + w_ref[i:i + 1, c0:c1] * c0_ref[0, i:i + 1, c0:c1]
    return acc


def _ssd_step_kernel(z_ref, xs_ref, bc_ref, dt_ref, c0_ref, cw_ref, cb_ref, dtb_ref, alog_ref, d_ref, nw_ref,
                     h0_ref, y_ref, co_ref, h_ref, ys_ref):
    xs_raw = xs_ref[0]
    bc_raw = bc_ref[0]
    xs = jax.nn.silu(_conv_step(c0_ref, cw_ref, xs_raw, 0, SSM_D_INNER) + cb_ref[:, 0:SSM_D_INNER])
    bc = jax.nn.silu(_conv_step(c0_ref, cw_ref, bc_raw, SSM_D_INNER, SSM_XBC) + cb_ref[:, SSM_D_INNER:SSM_XBC])
    co_ref[0, 0:SSM_CONV - 2, :] = c0_ref[0, 1:SSM_CONV - 1, :]
    co_ref[0, SSM_CONV - 2:SSM_CONV - 1, 0:SSM_D_INNER] = xs_raw
    co_ref[0, SSM_CONV - 2:SSM_CONV - 1, SSM_D_INNER:SSM_XBC] = bc_raw

    dt = jax.nn.softplus(dt_ref[0] + dtb_ref[...])
    dec = jnp.exp(dt * (-jnp.exp(alog_ref[...])))
    dskip = d_ref[...]
    eye = _iota((LANES, LANES), 0) == _iota((LANES, LANES), 1)
    upper = _iota((LANES, 1), 0) >= SSM_HEAD_DIM
    for j in range(SSM_HEADS // 2):
        h0i, h1i = 2 * j, 2 * j + 1
        g = h0i // (SSM_HEADS // SSM_GROUPS)
        bm = bc[:, g * SSM_STATE:(g + 1) * SSM_STATE]
        cm = bc[:, (SSM_GROUPS + g) * SSM_STATE:(SSM_GROUPS + g + 1) * SSM_STATE]
        x_row = xs[:, j * LANES:(j + 1) * LANES]
        x_col = _row_to_col(x_row, eye)
        pick = lambda v: jnp.where(upper, v[:, h1i:h1i + 1], v[:, h0i:h0i + 1])
        hp = h0_ref[0, h0i:h1i + 1].reshape(2 * SSM_HEAD_DIM, SSM_STATE)
        hn = hp * pick(dec) + (x_col * pick(dt)) * bm
        h_ref[0, h0i:h1i + 1] = hn.reshape(2, SSM_HEAD_DIM, SSM_STATE)
        y_col = jnp.sum(hn * cm, axis=1, keepdims=True) + pick(dskip) * x_col
        ys_ref[:, j * LANES:(j + 1) * LANES] = _col_to_row(y_col, eye)

    y = ys_ref[...] * jax.nn.silu(z_ref[0])
    half = SSM_D_INNER // SSM_GROUPS
    for g, yg in enumerate(_group_rms(y, nw_ref[...])):
        y_ref[0, :, g * half:(g + 1) * half] = yg


def ssd_step(main, tail, conv0, h0, conv_w, conv_b, dt_bias, a_log, d_skip, norm_w):
    bsz = main.shape[0]
    small = [conv_w, conv_b.reshape(1, -1), _pad_tail(dt_bias.reshape(1, -1)), _pad_tail(a_log.reshape(1, -1)),
             _pad_tail(d_skip.reshape(1, -1)), norm_w.reshape(1, -1)]
    hspec = pl.BlockSpec((1, SSM_HEADS, SSM_HEAD_DIM, SSM_STATE), lambda b: (b, 0, 0, 0))
    cspec = pl.BlockSpec((1, SSM_CONV - 1, SSM_XBC), lambda b: (b, 0, 0))
    return pl.pallas_call(
        _ssd_step_kernel,
        grid=(bsz,),
        in_specs=[
            pl.BlockSpec((1, 1, SSM_D_INNER), lambda b: (b, 0, 0)),
            pl.BlockSpec((1, 1, SSM_D_INNER), lambda b: (b, 0, 1)),
            pl.BlockSpec((1, 1, SSM_BC), lambda b: (b, 0, HYB_BC0 // SSM_BC)),
            pl.BlockSpec((1, 1, LANES), lambda b: (b, 0, 0)),
            cspec,
        ] + [_small(a) for a in small] + [hspec],
        out_specs=[pl.BlockSpec((1, 1, SSM_D_INNER), lambda b: (b, 0, 0)), cspec, hspec],
        out_shape=[jax.ShapeDtypeStruct((bsz, 1, SSM_D_INNER), F32),
                   jax.ShapeDtypeStruct(conv0.shape, F32),
                   jax.ShapeDtypeStruct(h0.shape, F32)],
        scratch_shapes=[pltpu.VMEM((1, SSM_D_INNER), F32)],
        compiler_params=_cparams("parallel"),
        name="ssd_step",
    )(main, main, main, tail, conv0, *small, h0)


def _l2norm(x):
    return x * lax.rsqrt(jnp.sum(x * x, axis=-1, keepdims=True) + EPS)


def _unit_lower_inverse(ns, eye):
    size = ns[0].shape[0]
    ps = [eye - n for n in ns]
    ms = [_dot(n, n) for n in ns]
    power = 2
    while 2 * power < size:
        pms = [_dot(jnp.concatenate([p, m], axis=0), m) for p, m in zip(ps, ms)]
        ps = [p + pm[:size] for p, pm in zip(ps, pms)]
        ms = [pm[size:] for pm in pms]
        power *= 2
    return [p + _dot(p, m) for p, m in zip(ps, ms)]


def _gdn_gates(ba, dtb_ref, alog_ref):
    beta = jax.nn.sigmoid(ba)
    g = -jnp.exp(alog_ref[...]) * jax.nn.softplus(ba + dtb_ref[...])
    return beta, g


def _gdn_kernel(q_ref, k_ref, v_ref, z_ref, ba_ref, c0_ref, cw_ref, dtb_ref, alog_ref, nw_ref, s0_ref,
                o_ref, co_ref, s_ref, xpad_ref):
    c = pl.program_id(1)
    ch = GDN_CHUNK
    base = 8
    lo = base - (GDN_CONV - 1)
    nh = GDN_V_HEADS

    @pl.when(c == 0)
    def _():
        xpad_ref[lo:base, :] = c0_ref[0]
        s_ref[...] = s0_ref[...]

    xpad_ref[base:base + ch, 0:GDN_QK_W] = q_ref[0]
    xpad_ref[base:base + ch, GDN_QK_W:2 * GDN_QK_W] = k_ref[0]
    xpad_ref[base:base + ch, 2 * GDN_QK_W:GDN_QKV] = v_ref[0]
    conv = cw_ref[0:1, :] * xpad_ref[lo:lo + ch, :]
    for i in range(1, GDN_CONV):
        conv = conv + cw_ref[i:i + 1, :] * xpad_ref[lo + i:lo + i + ch, :]
    tail = xpad_ref[base + ch - (GDN_CONV - 1):base + ch, :]
    co_ref[0] = tail
    xpad_ref[lo:base, :] = tail
    qkv = jax.nn.silu(conv)

    beta, g = _gdn_gates(ba_ref[0], dtb_ref, alog_ref)
    row = _iota((ch, ch), 0)
    colv = _iota((ch, ch), 1)
    incl = row >= colv
    strict = row > colv
    eye = (row == colv).astype(F32)
    gcum = _dot(incl.astype(F32), g, precision=HIGHEST)
    gcum_t = jnp.concatenate([gcum, jnp.zeros((LANES - ch, LANES), F32)], axis=0).T

    heads = range(nh)
    rep = nh // GDN_QK_HEADS
    qn = [_l2norm(qkv[:, j * GDN_DK:(j + 1) * GDN_DK]) * (GDN_DK ** -0.5) for j in range(GDN_QK_HEADS)]
    kn = [_l2norm(qkv[:, GDN_QK_W + j * GDN_DK:GDN_QK_W + (j + 1) * GDN_DK]) for j in range(GDN_QK_HEADS)]
    kk = [_dot_nt(k, k) for k in kn]
    qk = [_dot_nt(q, k) for q, k in zip(qn, kn)]
    gc_col = [gcum[:, nh + h:nh + h + 1] for h in heads]
    gc_last = [gcum[ch - 1:ch, nh + h:nh + h + 1] for h in heads]
    beta_col = [beta[:, h:h + 1] for h in heads]
    dec = [jnp.exp(jnp.where(incl, gc_col[h] - gcum_t[nh + h:nh + h + 1, 0:ch], NEG)) for h in heads]
    t_inv = _unit_lower_inverse(
        [jnp.where(strict, kk[h // rep] * dec[h], 0.0) * beta_col[h] for h in heads], eye)
    eg = [jnp.exp(gc_col[h]) for h in heads]
    s_prev = [s_ref[0, h] for h in heads]
    both = [_dot(jnp.concatenate([kn[h // rep] * (beta_col[h] * eg[h]), qn[h // rep] * eg[h]], axis=0), s_prev[h])
            for h in heads]
    u = [_dot(t_inv[h], qkv[:, 2 * GDN_QK_W + h * GDN_DV:2 * GDN_QK_W + (h + 1) * GDN_DV] * beta_col[h]
              - both[h][:ch]) for h in heads]
    o = [both[h][ch:] + _dot(qk[h // rep] * dec[h], u[h]) for h in heads]
    s_new = [s_prev[h] * jnp.exp(gc_last[h]) + _dot_tn(kn[h // rep] * jnp.exp(gc_last[h] - gc_col[h]), u[h])
             for h in heads]
    for h in heads:
        s_ref[0, h] = s_new[h]
        z_h = z_ref[0, :, h * GDN_DV:(h + 1) * GDN_DV]
        o_ref[0, :, h * GDN_DV:(h + 1) * GDN_DV] = _rms(o[h]) * nw_ref[...] * jax.nn.silu(z_h)


def _gdn_gate_params(dt_bias, a_log):
    nh = GDN_V_HEADS
    dtb = jnp.zeros((1, LANES), F32).at[0, nh:2 * nh].set(dt_bias)
    alog = jnp.zeros((1, LANES), F32).at[0, nh:2 * nh].set(a_log)
    return dtb, alog


def gdn_prompt(main, tail, conv0, s0, conv_w, dt_bias, a_log, norm_w, *, bsz, seq):
    ch = GDN_CHUNK
    nc = seq // ch
    dtb, alog = _gdn_gate_params(dt_bias, a_log)
    small = [conv_w, dtb, alog, norm_w.reshape(1, -1)]
    sspec = pl.BlockSpec((1, GDN_V_HEADS, GDN_DK, GDN_DV), lambda b, c: (b, 0, 0, 0))
    cspec = pl.BlockSpec((1, GDN_CONV - 1, GDN_QKV), lambda b, c: (b, 0, 0))
    return pl.pallas_call(
        _gdn_kernel,
        grid=(bsz, nc),
        in_specs=[
            pl.BlockSpec((1, ch, GDN_QK_W), lambda b, c: (b, c, 0)),
            pl.BlockSpec((1, ch, GDN_QK_W), lambda b, c: (b, c, 1)),
            pl.BlockSpec((1, ch, GDN_VW), lambda b, c: (b, c, 1)),
            pl.BlockSpec((1, ch, GDN_VW), lambda b, c: (b, c, 2)),
            pl.BlockSpec((1, ch, LANES), lambda b, c: (b, c, 0)),
            cspec,
        ] + [_small(a) for a in small] + [sspec],
        out_specs=[pl.BlockSpec((1, ch, GDN_VW), lambda b, c: (b, c, 0)), cspec, sspec],
        out_shape=[jax.ShapeDtypeStruct((bsz, seq, GDN_VW), F32),
                   jax.ShapeDtypeStruct((bsz, GDN_CONV - 1, GDN_QKV), F32),
                   jax.ShapeDtypeStruct((bsz, GDN_V_HEADS, GDN_DK, GDN_DV), F32)],
        scratch_shapes=[pltpu.VMEM((8 + ch, GDN_QKV), F32)],
        compiler_params=_cparams("parallel", "arbitrary"),
        name="gdn_prompt",
    )(main, main, main, main, tail, conv0, *small, s0)


def _gdn_conv_step(c0_ref, w_ref, x_row, c0, c1):
    acc = w_ref[GDN_CONV - 1:GDN_CONV, c0:c1] * x_row
    for i in range(GDN_CONV - 1):
        acc = acc + w_ref[i:i + 1, c0:c1] * c0_ref[0, i:i + 1, c0:c1]
    return acc


def _gdn_step_kernel(q_ref, k_ref, v_ref, z_ref, ba_ref, c0_ref, cw_ref, dtb_ref, alog_ref, nw_ref, s0_ref,
                     o_ref, co_ref, s_ref):
    nh = GDN_V_HEADS
    q_raw, k_raw, v_raw = q_ref[0], k_ref[0], v_ref[0]
    q = jax.nn.silu(_gdn_conv_step(c0_ref, cw_ref, q_raw, 0, GDN_QK_W))
    k = jax.nn.silu(_gdn_conv_step(c0_ref, cw_ref, k_raw, GDN_QK_W, 2 * GDN_QK_W))
    v = jax.nn.silu(_gdn_conv_step(c0_ref, cw_ref, v_raw, 2 * GDN_QK_W, GDN_QKV))
    co_ref[0, 0:GDN_CONV - 2, :] = c0_ref[0, 1:GDN_CONV - 1, :]
    co_ref[0, GDN_CONV - 2:GDN_CONV - 1, 0:GDN_QK_W] = q_raw
    co_ref[0, GDN_CONV - 2:GDN_CONV - 1, GDN_QK_W:2 * GDN_QK_W] = k_raw
    co_ref[0, GDN_CONV - 2:GDN_CONV - 1, 2 * GDN_QK_W:GDN_QKV] = v_raw

    beta, g = _gdn_gates(ba_ref[0], dtb_ref, alog_ref)
    eg_all = jnp.exp(g)
    eye = _iota((LANES, LANES), 0) == _iota((LANES, LANES), 1)
    for j in range(GDN_QK_HEADS):
        qn = _l2norm(q[:, j * GDN_DK:(j + 1) * GDN_DK]) * (GDN_DK ** -0.5)
        kn = _l2norm(k[:, j * GDN_DK:(j + 1) * GDN_DK])
        qk = jnp.sum(qn * kn, axis=-1, keepdims=True)
        q_col = _row_to_col(qn, eye)
        k_col = _row_to_col(kn, eye)
        for h in range(j * (nh // GDN_QK_HEADS), (j + 1) * (nh // GDN_QK_HEADS)):
            b_h = beta[:, h:h + 1]
            eg = eg_all[:, nh + h:nh + h + 1]
            s_prev = s0_ref[0, h]
            ks = jnp.sum(s_prev * k_col, axis=0, keepdims=True)
            qs = jnp.sum(s_prev * q_col, axis=0, keepdims=True)
            v_h = v[:, h * GDN_DV:(h + 1) * GDN_DV]
            u = v_h * b_h - (b_h * eg) * ks
            o = eg * qs + qk * u
            s_ref[0, h] = s_prev * eg + k_col * u
            z_h = z_ref[0, :, h * GDN_DV:(h + 1) * GDN_DV]
            o_ref[0, :, h * GDN_DV:(h + 1) * GDN_DV] = _rms(o) * nw_ref[...] * jax.nn.silu(z_h)


def gdn_step(main, tail, conv0, s0, conv_w, dt_bias, a_log, norm_w):
    bsz = main.shape[0]
    dtb, alog = _gdn_gate_params(dt_bias, a_log)
    small = [conv_w, dtb, alog, norm_w.reshape(1, -1)]
    sspec = pl.BlockSpec((1, GDN_V_HEADS, GDN_DK, GDN_DV), lambda b: (b, 0, 0, 0))
    cspec = pl.BlockSpec((1, GDN_CONV - 1, GDN_QKV), lambda b: (b, 0, 0))
    return pl.pallas_call(
        _gdn_step_kernel,
        grid=(bsz,),
        in_specs=[
            pl.BlockSpec((1, 1, GDN_QK_W), lambda b: (b, 0, 0)),
            pl.BlockSpec((1, 1, GDN_QK_W), lambda b: (b, 0, 1)),
            pl.BlockSpec((1, 1, GDN_VW), lambda b: (b, 0, 1)),
            pl.BlockSpec((1, 1, GDN_VW), lambda b: (b, 0, 2)),
            pl.BlockSpec((1, 1, LANES), lambda b: (b, 0, 0)),
            cspec,
        ] + [_small(a) for a in small] + [sspec],
        out_specs=[pl.BlockSpec((1, 1, GDN_VW), lambda b: (b, 0, 0)), cspec, sspec],
        out_shape=[jax.ShapeDtypeStruct((bsz, 1, GDN_VW), F32),
                   jax.ShapeDtypeStruct(conv0.shape, F32),
                   jax.ShapeDtypeStruct(s0.shape, F32)],
        compiler_params=_cparams("parallel"),
        name="gdn_step",
    )(main, main, main, main, tail, conv0, *small, s0)


def _pad_tail(w):
    return jnp.pad(w, ((0, 0), (0, LANES - w.shape[1])))


def _prep_hyb_in(w):
    a = A_WIDTH
    q, k, v = w[:, 0:a], w[:, a:2 * a], w[:, 2 * a:3 * a]
    z = w[:, 3 * a:3 * a + SSM_D_INNER]
    x0 = 3 * a + SSM_D_INNER
    xs = w[:, x0:x0 + SSM_D_INNER]
    bc = w[:, x0 + SSM_D_INNER:x0 + SSM_XBC]
    dt = w[:, x0 + SSM_XBC:]
    return jnp.concatenate([z, xs, q, k, v, bc], axis=1).astype(BF16), _pad_tail(dt).astype(BF16)


def _prep_gdn_in(w):
    return w[:, :GDN_MAIN].astype(BF16), _pad_tail(w[:, GDN_MAIN:]).astype(BF16)


def _row_tile(m, cap):
    return m if m <= cap else cap


def kernel(x_prompt, x_sample, cache_attn_k, cache_attn_v, state_ssm_conv, state_ssm, state_gdn_conv, state_gdn, rel_bias, norm_mix_pre, norm_mix_post, norm_ffn_pre, norm_ffn_post, w_hyb_in, ssm_conv_w, ssm_conv_b, ssm_dt_bias, ssm_a_log, ssm_d, ssm_norm_w, w_hyb_out, w_gdn_in, gdn_conv_w, gdn_dt_bias, gdn_a_log, gdn_norm_w, w_gdn_out, w_ffn_gate, w_ffn_up, w_ffn_down):
    depth = norm_mix_pre.shape[0]
    d_model = x_prompt.shape[-1]
    n_hyb, n_gdn = w_hyb_in.shape[0], w_gdn_in.shape[0]

    hyb_in = [_prep_hyb_in(w_hyb_in[i]) for i in range(n_hyb)]
    hyb_out = [(w_hyb_out[i, :A_WIDTH].astype(BF16), w_hyb_out[i, A_WIDTH:].astype(BF16)) for i in range(n_hyb)]
    gdn_in = [_prep_gdn_in(w_gdn_in[i]) for i in range(n_gdn)]
    gdn_out = [w_gdn_out[i].astype(BF16) for i in range(n_gdn)]
    ffn_w = [(w_ffn_gate[l].astype(BF16), w_ffn_up[l].astype(BF16), w_ffn_down[l].astype(BF16))
             for l in range(depth)]
    bias_tiles = _attn_bias_rows(rel_bias)

    def trunk(x3, k_pre, v_pre, sconv, sssm, gconv, gstate):
        bsz, seq, _ = x3.shape
        m = bsz * seq
        step = seq == 1
        tm_big = _row_tile(m, 1024)
        tm = _row_tile(m, 512)
        x = x3.reshape(m, d_model)
        nk, nv, nsc, nss, ngc, ngs = [], [], [], [], [], []
        rolled = list(cache_roll(k_pre, v_pre)) if step else None
        for l in range(depth):
            i = l // 2
            if l % 2 == 0:
                w_main, w_tail = hyb_in[i]
                main, tail = inproj(x, norm_mix_pre[l], w_main, w_tail, tm=tm_big, tn=512)
                main3 = main.reshape(bsz, seq, HYB_MAIN)
                tail3 = tail.reshape(bsz, seq, LANES)
                ssm_args = (ssm_conv_w[i], ssm_conv_b[i], ssm_dt_bias[i], ssm_a_log[i], ssm_d[i], ssm_norm_w[i])
                if step:
                    heads3 = lambda c0: main[:, c0:c0 + A_WIDTH].reshape(bsz, A_HEADS, A_HEAD_DIM)
                    o_attn, rolled[0], rolled[1] = attention_step(
                        heads3(HYB_Q0), heads3(HYB_K0), heads3(HYB_V0), k_pre, v_pre, rolled[0], rolled[1],
                        _attn_logw(rel_bias, k_pre.shape[2]), layer=i)
                    y, c_new, s_new = ssd_step(main3, tail3, sconv[i], sssm[i], *ssm_args)
                else:
                    o_attn = attention_prompt(main3, bias_tiles, bsz=bsz, seq=seq)
                    keep = min(A_PATTERNS[-1][0], seq)
                    k_new = main3[:, seq - keep:, HYB_K0:HYB_K0 + A_WIDTH]
                    v_new = main3[:, seq - keep:, HYB_V0:HYB_V0 + A_WIDTH]
                    y, c_new, s_new = ssd_prompt(main3, tail3, sconv[i], sssm[i], *ssm_args, bsz=bsz, seq=seq)
                    nk.append(k_new.reshape(bsz, -1, A_HEADS, A_HEAD_DIM))
                    nv.append(v_new.reshape(bsz, -1, A_HEADS, A_HEAD_DIM))
                nsc.append(c_new)
                nss.append(s_new)
                x = outproj([o_attn.reshape(m, A_WIDTH), y.reshape(m, SSM_D_INNER)], list(hyb_out[i]),
                            x, norm_mix_post[l], tm=tm)
            else:
                w_main, w_tail = gdn_in[i]
                main, tail = inproj(x, norm_mix_pre[l], w_main, w_tail, tm=tm_big, tn=512)
                main3 = main.reshape(bsz, seq, GDN_MAIN)
                tail3 = tail.reshape(bsz, seq, LANES)
                gdn_args = (gdn_conv_w[i], gdn_dt_bias[i], gdn_a_log[i], gdn_norm_w[i])
                if step:
                    o, c_new, s_new = gdn_step(main3, tail3, gconv[i], gstate[i], *gdn_args)
                else:
                    o, c_new, s_new = gdn_prompt(main3, tail3, gconv[i], gstate[i], *gdn_args, bsz=bsz, seq=seq)
                ngc.append(c_new)
                ngs.append(s_new)
                x = outproj([o.reshape(m, GDN_VW)], [gdn_out[i]], x, norm_mix_post[l], tm=tm)
            wg, wu, wd = ffn_w[l]
            x = ffn(x, norm_ffn_pre[l], wg, wu, wd, norm_ffn_post[l], tm=tm, tf=wg.shape[1] // 2)
        k_out, v_out = rolled if step else (jnp.stack(nk), jnp.stack(nv))
        return (x.reshape(bsz, seq, d_model), k_out, v_out, jnp.stack(nsc), jnp.stack(nss),
                jnp.stack(ngc), jnp.stack(ngs))

    bsz = x_prompt.shape[0]
    dt_p = x_prompt.dtype
    p_sc0 = jnp.zeros((n_hyb, bsz, SSM_CONV - 1, SSM_XBC), dt_p)
    p_ss0 = jnp.zeros((n_hyb, bsz, SSM_HEADS, SSM_HEAD_DIM, SSM_STATE), F32)
    p_gc0 = jnp.zeros((n_gdn, bsz, GDN_CONV - 1, GDN_QKV), dt_p)
    p_gs0 = jnp.zeros((n_gdn, bsz, GDN_V_HEADS, GDN_DK, GDN_DV), F32)
    y_prompt, pk, pv, psc, pss, pgc, pgs = trunk(x_prompt, None, None, p_sc0, p_ss0, p_gc0, p_gs0)
    y_sample, sk, sv, ssc, sss, sgc, sgs = trunk(
        x_sample, cache_attn_k, cache_attn_v, state_ssm_conv, state_ssm, state_gdn_conv, state_gdn)
    return (y_prompt, y_sample, pk, pv, psc, pss, pgc, pgs, sk, sv, ssc, sss, sgc, sgs)
```

```python
import functools
import math

import numpy as np
import jax
import jax.numpy as jnp
from jax import lax
from jax.experimental import pallas as pl
from jax.experimental.pallas import tpu as pltpu

F32 = jnp.float32
BF16 = jnp.bfloat16
EPS = 1e-6
NEG = -1e30
HIGHEST = lax.Precision.HIGHEST

VMEM_LIMIT_BYTES = 56 * 1024 * 1024
LANES = 128

A_HEADS = 8
A_HEAD_DIM = 64
A_WIDTH = A_HEADS * A_HEAD_DIM
A_PATTERNS = ((128, 1), (512, 4), (2048, 16))
A_BAND = 128
ATTN_GROUP = 4
REL_BUCKETS = 32
REL_MAX_DIST = 2048

SSM_D_INNER = 1024
SSM_HEAD_DIM = 64
SSM_HEADS = SSM_D_INNER // SSM_HEAD_DIM
SSM_GROUPS = 2
SSM_STATE = 128
SSM_CONV = 4
SSM_CHUNK = 128
SSM_BC = 2 * SSM_GROUPS * SSM_STATE
SSM_XBC = SSM_D_INNER + SSM_BC

GDN_QK_HEADS = 8
GDN_V_HEADS = 16
GDN_DK = 128
GDN_DV = 128
GDN_CONV = 4
GDN_CHUNK = 64
GDN_QK_W = GDN_QK_HEADS * GDN_DK
GDN_VW = GDN_V_HEADS * GDN_DV
GDN_QKV = 2 * GDN_QK_W + GDN_VW

HYB_MAIN = 2 * SSM_D_INNER + 3 * A_WIDTH + SSM_BC
HYB_Q0 = 2 * SSM_D_INNER
HYB_K0 = HYB_Q0 + A_WIDTH
HYB_V0 = HYB_K0 + A_WIDTH
HYB_BC0 = HYB_V0 + A_WIDTH
GDN_MAIN = GDN_QKV + GDN_VW


def _cparams(*sem):
    return pltpu.CompilerParams(dimension_semantics=sem, vmem_limit_bytes=VMEM_LIMIT_BYTES)


def _rms(x):
    return x * lax.rsqrt(jnp.mean(x * x, axis=-1, keepdims=True) + EPS)


def _dot(a, b, **kw):
    return jnp.dot(a, b, preferred_element_type=F32, **kw)


def _dot_nt(a, b, **kw):
    return lax.dot_general(a, b, (((1,), (1,)), ((), ())), preferred_element_type=F32, **kw)


def _dot_tn(a, b, **kw):
    return lax.dot_general(a, b, (((0,), (0,)), ((), ())), preferred_element_type=F32, **kw)


def _iota(shape, dim):
    return lax.broadcasted_iota(jnp.int32, shape, dim)


def _inproj_kernel(x_ref, g_ref, w_ref, wt_ref, o_ref, t_ref, h_ref):
    @pl.when(pl.program_id(1) == 0)
    def _():
        hb = (_rms(x_ref[...]) * g_ref[...]).astype(BF16)
        h_ref[...] = hb
        t_ref[...] = _dot(hb, wt_ref[...])

    o_ref[...] = _dot(h_ref[...], w_ref[...])


def inproj(x, g, w_main, w_tail, *, tm, tn):
    m, d = x.shape
    n = w_main.shape[1]
    return pl.pallas_call(
        _inproj_kernel,
        grid=(m // tm, n // tn),
        in_specs=[
            pl.BlockSpec((tm, d), lambda i, j: (i, 0)),
            pl.BlockSpec((1, d), lambda i, j: (0, 0)),
            pl.BlockSpec((d, tn), lambda i, j: (0, j)),
            pl.BlockSpec((d, LANES), lambda i, j: (0, 0)),
        ],
        out_specs=[
            pl.BlockSpec((tm, tn), lambda i, j: (i, j)),
            pl.BlockSpec((tm, LANES), lambda i, j: (i, 0)),
        ],
        out_shape=[jax.ShapeDtypeStruct((m, n), F32), jax.ShapeDtypeStruct((m, LANES), F32)],
        scratch_shapes=[pltpu.VMEM((tm, d), BF16)],
        compiler_params=_cparams("parallel", "arbitrary"),
        name="inproj",
    )(x, g.reshape(1, d), w_main, w_tail)


def _outproj_kernel(*refs, n_in):
    a_refs, w_refs = refs[:n_in], refs[n_in:2 * n_in]
    x_ref, g_ref, o_ref = refs[2 * n_in:]
    acc = None
    for a_ref, w_ref in zip(a_refs, w_refs):
        t = _dot(a_ref[...].astype(BF16), w_ref[...])
        acc = t if acc is None else acc + t
    o_ref[...] = x_ref[...] + _rms(acc) * g_ref[...]


def outproj(acts, weights, x, g, *, tm):
    m, d = x.shape
    n_in = len(acts)
    in_specs = [pl.BlockSpec((tm, a.shape[1]), lambda i: (i, 0)) for a in acts]
    in_specs += [pl.BlockSpec(w.shape, lambda i: (0, 0)) for w in weights]
    in_specs += [pl.BlockSpec((tm, d), lambda i: (i, 0)), pl.BlockSpec((1, d), lambda i: (0, 0))]
    return pl.pallas_call(
        functools.partial(_outproj_kernel, n_in=n_in),
        grid=(m // tm,),
        in_specs=in_specs,
        out_specs=pl.BlockSpec((tm, d), lambda i: (i, 0)),
        out_shape=jax.ShapeDtypeStruct((m, d), F32),
        compiler_params=_cparams("parallel"),
        name="outproj",
    )(*acts, *weights, x, g.reshape(1, d))


def _ffn_kernel(x_ref, g1_ref, wg_ref, wu_ref, wd_ref, g2_ref, o_ref, h_ref, acc_ref):
    j = pl.program_id(1)

    @pl.when(j == 0)
    def _():
        h_ref[...] = (_rms(x_ref[...]) * g1_ref[...]).astype(BF16)
        acc_ref[...] = jnp.zeros_like(acc_ref)

    h = h_ref[...]
    a = jax.nn.silu(_dot(h, wg_ref[...])) * _dot(h, wu_ref[...])
    acc_ref[...] += _dot(a.astype(BF16), wd_ref[...])

    @pl.when(j == pl.num_programs(1) - 1)
    def _():
        o_ref[...] = x_ref[...] + _rms(acc_ref[...]) * g2_ref[...]


def ffn(x, g1, wg, wu, wd, g2, *, tm, tf):
    m, d = x.shape
    dff = wg.shape[1]
    return pl.pallas_call(
        _ffn_kernel,
        grid=(m // tm, dff // tf),
        in_specs=[
            pl.BlockSpec((tm, d), lambda i, j: (i, 0)),
            pl.BlockSpec((1, d), lambda i, j: (0, 0)),
            pl.BlockSpec((d, tf), lambda i, j: (0, j)),
            pl.BlockSpec((d, tf), lambda i, j: (0, j)),
            pl.BlockSpec((tf, d), lambda i, j: (j, 0)),
            pl.BlockSpec((1, d), lambda i, j: (0, 0)),
        ],
        out_specs=pl.BlockSpec((tm, d), lambda i, j: (i, 0)),
        out_shape=jax.ShapeDtypeStruct((m, d), F32),
        scratch_shapes=[pltpu.VMEM((tm, d), BF16), pltpu.VMEM((tm, d), F32)],
        compiler_params=_cparams("parallel", "arbitrary"),
        name="ffn",
    )(x, g1.reshape(1, d), wg, wu, wd, g2.reshape(1, d))


def _rel_buckets(dist):
    max_exact = REL_BUCKETS // 2
    n = np.maximum(dist, 1).astype(np.float32)
    large = max_exact + (np.log(n / max_exact) / math.log(REL_MAX_DIST / max_exact)
                         * (REL_BUCKETS - max_exact)).astype(np.int32)
    large = np.minimum(large, REL_BUCKETS - 1)
    return np.where(dist < max_exact, dist, large).astype(np.int32)


def _attn_bias_rows(rel_bias):
    u = np.arange(2 * A_BAND)
    valid = u <= A_BAND
    rows = []
    for (_, d) in A_PATTERNS:
        b = rel_bias[_rel_buckets(np.where(valid, A_BAND - u, 0) * d)]
        rows.append(jnp.where(valid[:, None], b.astype(F32), NEG))
    tl = jnp.transpose(jnp.stack(rows), (2, 0, 1))
    tl = tl.reshape(A_HEADS // 2, 2, len(A_PATTERNS), 2 * A_BAND)
    tl = jnp.transpose(tl, (0, 2, 1, 3))[:, :, :, None, :]
    return jnp.broadcast_to(tl, tl.shape[:3] + (8, 2 * A_BAND))


def _attn_kernel(q_ref, k_ref, v_ref, brow_ref, o_ref, m0_ref, m1_ref, l0_ref, l1_ref, acc_ref, b_ref, *, seq):
    scale = A_HEAD_DIM ** -0.5
    n_tiles = seq // A_BAND
    lane = _iota((A_BAND, LANES), 1)
    head0 = lane < A_HEAD_DIM
    m_refs, l_refs = (m0_ref, m1_ref), (l0_ref, l1_ref)
    for p in range(len(A_PATTERNS)):
        for h in range(2):
            row = jnp.broadcast_to(brow_ref[0, p, h, 0:1, :], (A_BAND, 2 * A_BAND))
            b_ref[0, p, h] = pltpu.roll(row, 0, 1, stride=1, stride_axis=0)
    for h in range(2):
        m_refs[h][...] = jnp.full(m_refs[h].shape, NEG, F32)
        l_refs[h][...] = jnp.zeros_like(l_refs[h])
    acc_ref[...] = jnp.zeros_like(acc_ref)

    for p, (_, d) in enumerate(A_PATTERNS):
        tiles_per_class = n_tiles // d

        def load_tile(idx, d=d, tiles_per_class=tiles_per_class):
            r = idx // tiles_per_class
            t = idx % tiles_per_class
            start = r + t * (d * A_BAND)
            has_prev = t > 0
            prev = jnp.where(has_prev, start - d * A_BAND, start)
            if d > 1:
                rows, prows = pl.ds(start, A_BAND, stride=d), pl.ds(prev, A_BAND, stride=d)
            else:
                rows, prows = pl.ds(pl.multiple_of(start, A_BAND), A_BAND), pl.ds(pl.multiple_of(prev, A_BAND), A_BAND)
            return dict(
                rows=rows, has_prev=has_prev, q=q_ref[rows, :] * scale,
                k2=jnp.concatenate([k_ref[prows, :], k_ref[rows, :]], axis=0).astype(BF16),
                v2=jnp.concatenate([v_ref[prows, :], v_ref[rows, :]], axis=0).astype(BF16),
                acc=acc_ref[rows, :], m=[m_refs[h][rows, :] for h in range(2)],
                l=[l_refs[h][rows, :] for h in range(2)])

        def tile_group(idx, carry, load_tile=load_tile, p=p):
            tiles = [load_tile(idx + i * (n_tiles // ATTN_GROUP)) for i in range(ATTN_GROUP)]
            chains = [(tile, h) for tile in tiles for h in range(2)]
            col = _iota((A_BAND, 2 * A_BAND), 1)
            qh = [jnp.where(head0 if h == 0 else jnp.logical_not(head0), tile["q"], 0.0).astype(BF16)
                  for tile, h in chains]
            s = [_dot_nt(qh[c], tile["k2"])
                 + jnp.where(jnp.logical_and(col < A_BAND, jnp.logical_not(tile["has_prev"])), NEG, b_ref[0, p, h])
                 for c, (tile, h) in enumerate(chains)]
            m_new = [jnp.maximum(tile["m"][h], jnp.max(s[c], axis=-1, keepdims=True))
                     for c, (tile, h) in enumerate(chains)]
            alpha = [jnp.exp(tile["m"][h] - m_new[c]) for c, (tile, h) in enumerate(chains)]
            pr = [jnp.exp(s[c] - jnp.concatenate([m_new[c], m_new[c]], axis=1)) for c in range(len(chains))]
            l_new = [alpha[c] * tile["l"][h] + jnp.sum(pr[c], axis=-1, keepdims=True)
                     for c, (tile, h) in enumerate(chains)]
            acc_new = [alpha[c] * tile["acc"] + _dot(pr[c].astype(BF16), tile["v2"])
                       for c, (tile, h) in enumerate(chains)]
            for c, (tile, h) in enumerate(chains):
                m_refs[h][tile["rows"], :] = m_new[c]
                l_refs[h][tile["rows"], :] = l_new[c]
                if h == 1:
                    acc_ref[tile["rows"], :] = jnp.where(head0, acc_new[c - 1], acc_new[c])
            return carry

        lax.fori_loop(0, n_tiles // ATTN_GROUP, tile_group, 0)

    lane_s = _iota((seq, LANES), 1)
    o_ref[...] = acc_ref[...] / jnp.where(lane_s < A_HEAD_DIM, l0_ref[...], l1_ref[...])


def attention_prompt(proj, bias_tiles, *, bsz, seq):
    hp = A_HEADS // 2
    qb, kb, vb = HYB_Q0 // LANES, HYB_K0 // LANES, HYB_V0 // LANES
    return pl.pallas_call(
        functools.partial(_attn_kernel, seq=seq),
        grid=(bsz, hp),
        in_specs=[
            pl.BlockSpec((None, seq, LANES), lambda b, h: (b, 0, qb + h)),
            pl.BlockSpec((None, seq, LANES), lambda b, h: (b, 0, kb + h)),
            pl.BlockSpec((None, seq, LANES), lambda b, h: (b, 0, vb + h)),
            pl.BlockSpec((1,) + bias_tiles.shape[1:], lambda b, h: (h, 0, 0, 0, 0)),
        ],
        out_specs=pl.BlockSpec((None, seq, LANES), lambda b, h: (b, 0, h)),
        out_shape=jax.ShapeDtypeStruct((bsz, seq, A_WIDTH), F32),
        scratch_shapes=[pltpu.VMEM((seq, LANES), F32)] * 5
        + [pltpu.VMEM((1, len(A_PATTERNS), 2, A_BAND, 2 * A_BAND), F32)],
        compiler_params=_cparams("parallel", "parallel"),
        name="attn_prompt",
    )(proj, proj, proj, bias_tiles)


def _attn_logw(rel_bias, past):
    dist = np.arange(past + 1)
    count = np.zeros(past + 1, np.float64)
    for (w, d) in A_PATTERNS:
        count += ((dist % d == 0) & (dist <= w)).astype(np.float64)
    logc = np.where(count > 0, np.log(np.maximum(count, 1.0)), 0.0).astype(np.float32)
    lw = rel_bias[_rel_buckets(dist)].astype(F32).T + logc[None, :]
    return jnp.where((count > 0)[None, :], lw, NEG)


def _step_scores(xk_ref, q_col, kn_col, lw_ref, lw0_ref, s_ref):
    qs = q_col * (A_HEAD_DIM ** -0.5)
    s_new = []
    for h in range(A_HEADS):
        rows = slice(h * A_HEAD_DIM, (h + 1) * A_HEAD_DIM)
        s_ref[h:h + 1, :] = jnp.sum(xk_ref[0, 0, rows, :] * qs[rows], axis=0, keepdims=True)
        s_new.append(jnp.sum(kn_col[rows] * qs[rows], axis=0, keepdims=True))
    s = s_ref[...] + lw_ref[...]
    s_new = jnp.concatenate(s_new, axis=0) + lw0_ref[:, :1]
    m = jnp.maximum(jnp.max(s, axis=-1, keepdims=True), s_new)
    p = jnp.exp(s - m)
    p_new = jnp.exp(s_new - m)
    den = jnp.sum(p, axis=-1, keepdims=True) + p_new
    return p, p_new, den


def _step_output(xv_ref, vn_col, p, p_new, den, o_ref):
    for h in range(A_HEADS):
        rows = slice(h * A_HEAD_DIM, (h + 1) * A_HEAD_DIM)
        pv = jnp.sum(xv_ref[0, 0, rows, :] * p[h:h + 1, :], axis=-1, keepdims=True)
        o_ref[0, rows, :] = (pv + p_new[h:h + 1, :] * vn_col[rows]) / den[h:h + 1, :]


def _attn_step_roll_kernel(q_ref, kn_ref, vn_ref, lw_ref, lw0_ref, xk_ref, xv_ref, o_ref, ko_ref, vo_ref, s_ref,
                           *, layer, past):
    is_layer = pl.program_id(0) == layer
    newest = _iota((A_HEAD_DIM, past), 1) == past - 1
    for x_ref, n_ref, out_ref in ((xk_ref, kn_ref, ko_ref), (xv_ref, vn_ref, vo_ref)):
        for h in range(A_HEADS):
            rows = slice(h * A_HEAD_DIM, (h + 1) * A_HEAD_DIM)
            rolled = pltpu.roll(x_ref[0, 0, rows, :], past - 1, 1)
            out_ref[0, 0, rows, :] = jnp.where(jnp.logical_and(newest, is_layer), n_ref[0, rows, :], rolled)

    @pl.when(is_layer)
    def _():
        p, p_new, den = _step_scores(xk_ref, q_ref[0], kn_ref[0], lw_ref, lw0_ref, s_ref)
        _step_output(xv_ref, vn_ref[0], p, p_new, den, o_ref.at[0])

    @pl.when(jnp.logical_not(is_layer))
    def _():
        o_ref[...] = jnp.zeros_like(o_ref)


def _attn_step_append_kernel(q_ref, kn_ref, vn_ref, lw_ref, lw0_ref, xk_ref, xv_ref, ko_in, vo_in,
                             o_ref, ko_ref, vo_ref, s_ref, *, past):
    del ko_in, vo_in
    p, p_new, den = _step_scores(xk_ref, q_ref[0], kn_ref[0], lw_ref, lw0_ref, s_ref)
    _step_output(xv_ref, vn_ref[0], p, p_new, den, o_ref)
    newest = _iota((A_WIDTH, LANES), 1) == LANES - 1
    for x_ref, n_ref, out_ref in ((xk_ref, kn_ref, ko_ref), (xv_ref, vn_ref, vo_ref)):
        rolled = pltpu.roll(x_ref[0, 0, :, past - LANES:past], LANES - 1, 1)
        out_ref[0, 0] = jnp.where(newest, n_ref[0], rolled)


def attention_step(q_col, kn_col, vn_col, cache_k, cache_v, rolled, logw, *, layer):
    n_layers, bsz, w, past = cache_k.shape
    lw_cache = logw[:, past:0:-1]
    lw_new = jnp.broadcast_to(logw[:, :1], (A_HEADS, LANES))
    out_shape = [jax.ShapeDtypeStruct((bsz, w, 1), F32),
                 jax.ShapeDtypeStruct(cache_k.shape, cache_k.dtype),
                 jax.ShapeDtypeStruct(cache_v.shape, cache_v.dtype)]
    scratch = [pltpu.VMEM((A_HEADS, past), F32)]
    if rolled is None:
        assert layer == 0
        col = pl.BlockSpec((1, w, 1), lambda l, b: (b, 0, 0))
        win = pl.BlockSpec((1, 1, w, past), lambda l, b: (l, b, 0, 0))
        o_all, rolled_k, rolled_v = pl.pallas_call(
            functools.partial(_attn_step_roll_kernel, layer=layer, past=past),
            grid=(n_layers, bsz),
            in_specs=[col, col, col, _small(lw_cache), _small(lw_new), win, win],
            out_specs=[pl.BlockSpec((1, 1, w, 1), lambda l, b: (l, b, 0, 0)), win, win],
            out_shape=[jax.ShapeDtypeStruct((n_layers, bsz, w, 1), F32)] + out_shape[1:],
            scratch_shapes=scratch,
            compiler_params=_cparams("arbitrary", "arbitrary"),
            name="attn_step_roll",
        )(q_col, kn_col, vn_col, lw_cache, lw_new, cache_k, cache_v)
        return o_all[layer], rolled_k, rolled_v
    col = pl.BlockSpec((1, w, 1), lambda b: (b, 0, 0))
    win = pl.BlockSpec((1, 1, w, past), lambda b: (layer, b, 0, 0))
    tail = pl.BlockSpec((1, 1, w, LANES), lambda b: (layer, b, 0, past // LANES - 1))
    return pl.pallas_call(
        functools.partial(_attn_step_append_kernel, past=past),
        grid=(bsz,),
        in_specs=[col, col, col, _small(lw_cache), _small(lw_new), win, win, tail, tail],
        out_specs=[col, tail, tail],
        out_shape=out_shape,
        scratch_shapes=scratch,
        input_output_aliases={7: 1, 8: 2},
        compiler_params=_cparams("arbitrary"),
        name="attn_step_append",
    )(q_col, kn_col, vn_col, lw_cache, lw_new, cache_k, cache_v, *rolled)


def _group_rms(y, w):
    half = SSM_D_INNER // SSM_GROUPS
    return [_rms(y[:, g * half:(g + 1) * half]) * w[:, g * half:(g + 1) * half] for g in range(SSM_GROUPS)]


def _ssd_kernel(z_ref, xs_ref, bc_ref, dt_ref, c0_ref, cw_ref, cb_ref, dtb_ref, alog_ref, d_ref, nw_ref,
                h0_ref, y_ref, co_ref, h_ref, xpad_ref, ys_ref):
    c = pl.program_id(1)
    ch = SSM_CHUNK
    base = 8
    lo = base - (SSM_CONV - 1)

    @pl.when(c == 0)
    def _():
        xpad_ref[lo:base, :] = c0_ref[0]
        h_ref[...] = h0_ref[...]

    xpad_ref[base:base + ch, 0:SSM_D_INNER] = xs_ref[0]
    xpad_ref[base:base + ch, SSM_D_INNER:SSM_XBC] = bc_ref[0]
    conv = cb_ref[...] + cw_ref[0:1, :] * xpad_ref[lo:lo + ch, :]
    for i in range(1, SSM_CONV):
        conv = conv + cw_ref[i:i + 1, :] * xpad_ref[lo + i:lo + i + ch, :]
    tail = xpad_ref[base + ch - (SSM_CONV - 1):base + ch, :]
    co_ref[0] = tail
    xpad_ref[lo:base, :] = tail
    xbc = jax.nn.silu(conv)
    xs = xbc[:, :SSM_D_INNER]

    dt = jax.nn.softplus(dt_ref[0] + dtb_ref[...])
    da = dt * (-jnp.exp(alog_ref[...]))
    row = _iota((ch, ch), 0)
    colv = _iota((ch, ch), 1)
    tril = (row >= colv).astype(F32)
    cs = _dot(tril, da, precision=HIGHEST)
    cs_t = cs.T
    causal = row >= colv

    heads = range(SSM_HEADS)
    hpg = SSM_HEADS // SSM_GROUPS
    bm = [xbc[:, SSM_D_INNER + g * SSM_STATE:SSM_D_INNER + (g + 1) * SSM_STATE] for g in range(SSM_GROUPS)]
    cm = [xbc[:, SSM_D_INNER + (SSM_GROUPS + g) * SSM_STATE:SSM_D_INNER + (SSM_GROUPS + g + 1) * SSM_STATE]
          for g in range(SSM_GROUPS)]
    cb = [_dot_nt(cm[g], bm[g]) for g in range(SSM_GROUPS)]
    dt_t = dt.T
    xs_t = xs.T
    w_t = dt_t * jnp.exp(cs_t[:, ch - 1:ch] - cs_t)
    e_last = jnp.exp(cs[ch - 1:ch, :])
    lane_lo = _iota((ch, LANES), 1) < SSM_HEAD_DIM
    row_lo = _iota((LANES, ch), 0) < SSM_HEAD_DIM
    pairs = range(SSM_HEADS // 2)
    csb = [jnp.broadcast_to(cs[:, h:h + 1], (ch, ch)) for h in heads]
    mix = [cb[h // hpg] * jnp.exp(jnp.where(causal, csb[h] - cs_t[h:h + 1, :], NEG)) * dt_t[h:h + 1, :]
           for h in heads]
    x_pair = [xs[:, j * LANES:(j + 1) * LANES] for j in pairs]
    y_intra = [jnp.where(lane_lo, _dot(mix[2 * j], x_pair[j]), _dot(mix[2 * j + 1], x_pair[j])) for j in pairs]
    h_pair = [h_ref[0, 2 * j:2 * j + 2].reshape(2 * SSM_HEAD_DIM, SSM_STATE) for j in pairs]
    y_inter = [_dot_nt(cm[2 * j // hpg], h_pair[j]) * jnp.exp(jnp.where(lane_lo, csb[2 * j], csb[2 * j + 1]))
               for j in pairs]
    xw_t = [xs_t[j * LANES:(j + 1) * LANES, :] * jnp.where(row_lo, w_t[2 * j:2 * j + 1, :], w_t[2 * j + 1:2 * j + 2, :])
            for j in pairs]
    st = [_dot(xw_t[j], bm[2 * j // hpg]) for j in pairs]
    for j in pairs:
        cols = slice(j * LANES, (j + 1) * LANES)
        ys_ref[:, cols] = y_intra[j] + y_inter[j] + d_ref[:, cols] * x_pair[j]
        decay = jnp.where(row_lo, e_last[:, 2 * j:2 * j + 1], e_last[:, 2 * j + 1:2 * j + 2])
        h_ref[0, 2 * j:2 * j + 2] = (h_pair[j] * decay + st[j]).reshape(2, SSM_HEAD_DIM, SSM_STATE)

    y = ys_ref[...] * jax.nn.silu(z_ref[0])
    half = SSM_D_INNER // SSM_GROUPS
    for g, yg in enumerate(_group_rms(y, nw_ref[...])):
        y_ref[0, :, g * half:(g + 1) * half] = yg


def _small(a):
    return pl.BlockSpec(a.shape, lambda *_: (0,) * a.ndim)


def ssd_prompt(main, tail, conv0, h0, conv_w, conv_b, dt_bias, a_log, d_skip, norm_w, *, bsz, seq):
    nc = seq // SSM_CHUNK
    ch = SSM_CHUNK
    small = [conv_w, conv_b.reshape(1, -1), _pad_tail(dt_bias.reshape(1, -1)), _pad_tail(a_log.reshape(1, -1)),
             jnp.repeat(d_skip, SSM_HEAD_DIM).reshape(1, -1), norm_w.reshape(1, -1)]
    return pl.pallas_call(
        _ssd_kernel,
        grid=(bsz, nc),
        in_specs=[
            pl.BlockSpec((1, ch, SSM_D_INNER), lambda b, c: (b, c, 0)),
            pl.BlockSpec((1, ch, SSM_D_INNER), lambda b, c: (b, c, 1)),
            pl.BlockSpec((1, ch, SSM_BC), lambda b, c: (b, c, HYB_BC0 // SSM_BC)),
            pl.BlockSpec((1, ch, LANES), lambda b, c: (b, c, 0)),
            pl.BlockSpec((1, SSM_CONV - 1, SSM_XBC), lambda b, c: (b, 0, 0)),
        ] + [_small(a) for a in small] + [
            pl.BlockSpec((1, SSM_HEADS, SSM_HEAD_DIM, SSM_STATE), lambda b, c: (b, 0, 0, 0)),
        ],
        out_specs=[
            pl.BlockSpec((1, ch, SSM_D_INNER), lambda b, c: (b, c, 0)),
            pl.BlockSpec((1, SSM_CONV - 1, SSM_XBC), lambda b, c: (b, 0, 0)),
            pl.BlockSpec((1, SSM_HEADS, SSM_HEAD_DIM, SSM_STATE), lambda b, c: (b, 0, 0, 0)),
        ],
        out_shape=[jax.ShapeDtypeStruct((bsz, seq, SSM_D_INNER), F32),
                   jax.ShapeDtypeStruct((bsz, SSM_CONV - 1, SSM_XBC), F32),
                   jax.ShapeDtypeStruct((bsz, SSM_HEADS, SSM_HEAD_DIM, SSM_STATE), F32)],
        scratch_shapes=[pltpu.VMEM((8 + ch, SSM_XBC), F32), pltpu.VMEM((ch, SSM_D_INNER), F32)],
        compiler_params=_cparams("parallel", "arbitrary"),
        name="ssd_prompt",
    )(main, main, main, tail, conv0, *small, h0)


def _row_to_col(row, eye):
    return jnp.sum(jnp.where(eye, row, 0.0), axis=1, keepdims=True)


def _col_to_row(col, eye):
    return jnp.sum(jnp.where(eye, col, 0.0), axis=0, keepdims=True)


def _conv_step(c0_ref, w_ref, x_row, c0, c1):
    acc = w_ref[SSM_CONV - 1:SSM_CONV, c0:c1] * x_row
    for i in range(SSM_CONV - 1):
        acc = acc + w_ref[i:i + 1, c0:c1] * c0_ref[0, i:i + 1, c0:c1]
    return acc


def _ssd_step_kernel(z_ref, xs_ref, bc_ref, dt_ref, c0_ref, cw_ref, cb_ref, dtb_ref, alog_ref, d_ref, nw_ref,
                     h0_ref, y_ref, co_ref, h_ref, ys_ref):
    xs_raw = xs_ref[0]
    bc_raw = bc_ref[0]
    xs = jax.nn.silu(_conv_step(c0_ref, cw_ref, xs_raw, 0, SSM_D_INNER) + cb_ref[:, 0:SSM_D_INNER])
    bc = jax.nn.silu(_conv_step(c0_ref, cw_ref, bc_raw, SSM_D_INNER, SSM_XBC) + cb_ref[:, SSM_D_INNER:SSM_XBC])
    co_ref[0, 0:SSM_CONV - 2, :] = c0_ref[0, 1:SSM_CONV - 1, :]
    co_ref[0, SSM_CONV - 2:SSM_CONV - 1, 0:SSM_D_INNER] = xs_raw
    co_ref[0, SSM_CONV - 2:SSM_CONV - 1, SSM_D_INNER:SSM_XBC] = bc_raw

    dt = jax.nn.softplus(dt_ref[0] + dtb_ref[...])
    dec = jnp.exp(dt * (-jnp.exp(alog_ref[...])))
    dskip = d_ref[...]
    eye = _iota((LANES, LANES), 0) == _iota((LANES, LANES), 1)
    upper = _iota((LANES, 1), 0) >= SSM_HEAD_DIM
    for j in range(SSM_HEADS // 2):
        h0i, h1i = 2 * j, 2 * j + 1
        g = h0i // (SSM_HEADS // SSM_GROUPS)
        bm = bc[:, g * SSM_STATE:(g + 1) * SSM_STATE]
        cm = bc[:, (SSM_GROUPS + g) * SSM_STATE:(SSM_GROUPS + g + 1) * SSM_STATE]
        x_row = xs[:, j * LANES:(j + 1) * LANES]
        x_col = _row_to_col(x_row, eye)
        pick = lambda v: jnp.where(upper, v[:, h1i:h1i + 1], v[:, h0i:h0i + 1])
        hp = h0_ref[0, h0i:h1i + 1].reshape(2 * SSM_HEAD_DIM, SSM_STATE)
        hn = hp * pick(dec) + (x_col * pick(dt)) * bm
        h_ref[0, h0i:h1i + 1] = hn.reshape(2, SSM_HEAD_DIM, SSM_STATE)
        y_col = jnp.sum(hn * cm, axis=1, keepdims=True) + pick(dskip) * x_col
        ys_ref[:, j * LANES:(j + 1) * LANES] = _col_to_row(y_col, eye)

    y = ys_ref[...] * jax.nn.silu(z_ref[0])
    half = SSM_D_INNER // SSM_GROUPS
    for g, yg in enumerate(_group_rms(y, nw_ref[...])):
        y_ref[0, :, g * half:(g + 1) * half] = yg


def ssd_step(main, tail, conv0, h0, conv_w, conv_b, dt_bias, a_log, d_skip, norm_w):
    bsz = main.shape[0]
    small = [conv_w, conv_b.reshape(1, -1), _pad_tail(dt_bias.reshape(1, -1)), _pad_tail(a_log.reshape(1, -1)),
             _pad_tail(d_skip.reshape(1, -1)), norm_w.reshape(1, -1)]
    hspec = pl.BlockSpec((1, SSM_HEADS, SSM_HEAD_DIM, SSM_STATE), lambda b: (b, 0, 0, 0))
    cspec = pl.BlockSpec((1, SSM_CONV - 1, SSM_XBC), lambda b: (b, 0, 0))
    return pl.pallas_call(
        _ssd_step_kernel,
        grid=(bsz,),
        in_specs=[
            pl.BlockSpec((1, 1, SSM_D_INNER), lambda b: (b, 0, 0)),
            pl.BlockSpec((1, 1, SSM_D_INNER), lambda b: (b, 0, 1)),
            pl.BlockSpec((1, 1, SSM_BC), lambda b: (b, 0, HYB_BC0 // SSM_BC)),
            pl.BlockSpec((1, 1, LANES), lambda b: (b, 0, 0)),
            cspec,
        ] + [_small(a) for a in small] + [hspec],
        out_specs=[pl.BlockSpec((1, 1, SSM_D_INNER), lambda b: (b, 0, 0)), cspec, hspec],
        out_shape=[jax.ShapeDtypeStruct((bsz, 1, SSM_D_INNER), F32),
                   jax.ShapeDtypeStruct(conv0.shape, F32),
                   jax.ShapeDtypeStruct(h0.shape, F32)],
        scratch_shapes=[pltpu.VMEM((1, SSM_D_INNER), F32)],
        compiler_params=_cparams("parallel"),
        name="ssd_step",
    )(main, main, main, tail, conv0, *small, h0)


def _l2norm(x):
    return x * lax.rsqrt(jnp.sum(x * x, axis=-1, keepdims=True) + EPS)


def _unit_lower_inverse(ns, eye):
    size = ns[0].shape[0]
    ps = [eye - n for n in ns]
    ms = [_dot(n, n) for n in ns]
    power = 2
    while 2 * power < size:
        pms = [_dot(jnp.concatenate([p, m], axis=0), m) for p, m in zip(ps, ms)]
        ps = [p + pm[:size] for p, pm in zip(ps, pms)]
        ms = [pm[size:] for pm in pms]
        power *= 2
    return [p + _dot(p, m) for p, m in zip(ps, ms)]


def _gdn_gates(ba, dtb_ref, alog_ref):
    beta = jax.nn.sigmoid(ba)
    g = -jnp.exp(alog_ref[...]) * jax.nn.softplus(ba + dtb_ref[...])
    return beta, g


def _gdn_kernel(q_ref, k_ref, v_ref, z_ref, ba_ref, c0_ref, cw_ref, dtb_ref, alog_ref, nw_ref, s0_ref,
                o_ref, co_ref, s_ref, xpad_ref):
    c = pl.program_id(1)
    ch = GDN_CHUNK
    base = 8
    lo = base - (GDN_CONV - 1)
    nh = GDN_V_HEADS

    @pl.when(c == 0)
    def _():
        xpad_ref[lo:base, :] = c0_ref[0]
        s_ref[...] = s0_ref[...]

    xpad_ref[base:base + ch, 0:GDN_QK_W] = q_ref[0]
    xpad_ref[base:base + ch, GDN_QK_W:2 * GDN_QK_W] = k_ref[0]
    xpad_ref[base:base + ch, 2 * GDN_QK_W:GDN_QKV] = v_ref[0]
    conv = cw_ref[0:1, :] * xpad_ref[lo:lo + ch, :]
    for i in range(1, GDN_CONV):
        conv = conv + cw_ref[i:i + 1, :] * xpad_ref[lo + i:lo + i + ch, :]
    tail = xpad_ref[base + ch - (GDN_CONV - 1):base + ch, :]
    co_ref[0] = tail
    xpad_ref[lo:base, :] = tail
    qkv = jax.nn.silu(conv)

    beta, g = _gdn_gates(ba_ref[0], dtb_ref, alog_ref)
    row = _iota((ch, ch), 0)
    colv = _iota((ch, ch), 1)
    incl = row >= colv
    strict = row > colv
    eye = (row == colv).astype(F32)
    gcum = _dot(incl.astype(F32), g, precision=HIGHEST)
    gcum_t = jnp.concatenate([gcum, jnp.zeros((LANES - ch, LANES), F32)], axis=0).T

    heads = range(nh)
    rep = nh // GDN_QK_HEADS
    qn = [_l2norm(qkv[:, j * GDN_DK:(j + 1) * GDN_DK]) * (GDN_DK ** -0.5) for j in range(GDN_QK_HEADS)]
    kn = [_l2norm(qkv[:, GDN_QK_W + j * GDN_DK:GDN_QK_W + (j + 1) * GDN_DK]) for j in range(GDN_QK_HEADS)]
    kk = [_dot_nt(k, k) for k in kn]
    qk = [_dot_nt(q, k) for q, k in zip(qn, kn)]
    gc_col = [gcum[:, nh + h:nh + h + 1] for h in heads]
    gc_last = [gcum[ch - 1:ch, nh + h:nh + h + 1] for h in heads]
    beta_col = [beta[:, h:h + 1] for h in heads]
    dec = [jnp.exp(jnp.where(incl, gc_col[h] - gcum_t[nh + h:nh + h + 1, 0:ch], NEG)) for h in heads]
    t_inv = _unit_lower_inverse(
        [jnp.where(strict, kk[h // rep] * dec[h], 0.0) * beta_col[h] for h in heads], eye)
    eg = [jnp.exp(gc_col[h]) for h in heads]
    s_prev = [s_ref[0, h] for h in heads]
    both = [_dot(jnp.concatenate([kn[h // rep] * (beta_col[h] * eg[h]), qn[h // rep] * eg[h]], axis=0), s_prev[h])
            for h in heads]
    u = [_dot(t_inv[h], qkv[:, 2 * GDN_QK_W + h * GDN_DV:2 * GDN_QK_W + (h + 1) * GDN_DV] * beta_col[h]
              - both[h][:ch]) for h in heads]
    o = [both[h][ch:] + _dot(qk[h // rep] * dec[h], u[h]) for h in heads]
    s_new = [s_prev[h] * jnp.exp(gc_last[h]) + _dot_tn(kn[h // rep] * jnp.exp(gc_last[h] - gc_col[h]), u[h])
             for h in heads]
    for h in heads:
        s_ref[0, h] = s_new[h]
        z_h = z_ref[0, :, h * GDN_DV:(h + 1) * GDN_DV]
        o_ref[0, :, h * GDN_DV:(h + 1) * GDN_DV] = _rms(o[h]) * nw_ref[...] * jax.nn.silu(z_h)


def _gdn_gate_params(dt_bias, a_log):
    nh = GDN_V_HEADS
    dtb = jnp.zeros((1, LANES), F32).at[0, nh:2 * nh].set(dt_bias)
    alog = jnp.zeros((1, LANES), F32).at[0, nh:2 * nh].set(a_log)
    return dtb, alog


def gdn_prompt(main, tail, conv0, s0, conv_w, dt_bias, a_log, norm_w, *, bsz, seq):
    ch = GDN_CHUNK
    nc = seq // ch
    dtb, alog = _gdn_gate_params(dt_bias, a_log)
    small = [conv_w, dtb, alog, norm_w.reshape(1, -1)]
    sspec = pl.BlockSpec((1, GDN_V_HEADS, GDN_DK, GDN_DV), lambda b, c: (b, 0, 0, 0))
    cspec = pl.BlockSpec((1, GDN_CONV - 1, GDN_QKV), lambda b, c: (b, 0, 0))
    return pl.pallas_call(
        _gdn_kernel,
        grid=(bsz, nc),
        in_specs=[
            pl.BlockSpec((1, ch, GDN_QK_W), lambda b, c: (b, c, 0)),
            pl.BlockSpec((1, ch, GDN_QK_W), lambda b, c: (b, c, 1)),
            pl.BlockSpec((1, ch, GDN_VW), lambda b, c: (b, c, 1)),
            pl.BlockSpec((1, ch, GDN_VW), lambda b, c: (b, c, 2)),
            pl.BlockSpec((1, ch, LANES), lambda b, c: (b, c, 0)),
            cspec,
        ] + [_small(a) for a in small] + [sspec],
        out_specs=[pl.BlockSpec((1, ch, GDN_VW), lambda b, c: (b, c, 0)), cspec, sspec],
        out_shape=[jax.ShapeDtypeStruct((bsz, seq, GDN_VW), F32),
                   jax.ShapeDtypeStruct((bsz, GDN_CONV - 1, GDN_QKV), F32),
                   jax.ShapeDtypeStruct((bsz, GDN_V_HEADS, GDN_DK, GDN_DV), F32)],
        scratch_shapes=[pltpu.VMEM((8 + ch, GDN_QKV), F32)],
        compiler_params=_cparams("parallel", "arbitrary"),
        name="gdn_prompt",
    )(main, main, main, main, tail, conv0, *small, s0)


def _gdn_conv_step(c0_ref, w_ref, x_row, c0, c1):
    acc = w_ref[GDN_CONV - 1:GDN_CONV, c0:c1] * x_row
    for i in range(GDN_CONV - 1):
        acc = acc + w_ref[i:i + 1, c0:c1] * c0_ref[0, i:i + 1, c0:c1]
    return acc


def _gdn_step_kernel(q_ref, k_ref, v_ref, z_ref, ba_ref, c0_ref, cw_ref, dtb_ref, alog_ref, nw_ref, s0_ref,
                     o_ref, co_ref, s_ref):
    nh = GDN_V_HEADS
    q_raw, k_raw, v_raw = q_ref[0], k_ref[0], v_ref[0]
    q = jax.nn.silu(_gdn_conv_step(c0_ref, cw_ref, q_raw, 0, GDN_QK_W))
    k = jax.nn.silu(_gdn_conv_step(c0_ref, cw_ref, k_raw, GDN_QK_W, 2 * GDN_QK_W))
    v = jax.nn.silu(_gdn_conv_step(c0_ref, cw_ref, v_raw, 2 * GDN_QK_W, GDN_QKV))
    co_ref[0, 0:GDN_CONV - 2, :] = c0_ref[0, 1:GDN_CONV - 1, :]
    co_ref[0, GDN_CONV - 2:GDN_CONV - 1, 0:GDN_QK_W] = q_raw
    co_ref[0, GDN_CONV - 2:GDN_CONV - 1, GDN_QK_W:2 * GDN_QK_W] = k_raw
    co_ref[0, GDN_CONV - 2:GDN_CONV - 1, 2 * GDN_QK_W:GDN_QKV] = v_raw

    beta, g = _gdn_gates(ba_ref[0], dtb_ref, alog_ref)
    eg_all = jnp.exp(g)
    eye = _iota((LANES, LANES), 0) == _iota((LANES, LANES), 1)
    for j in range(GDN_QK_HEADS):
        qn = _l2norm(q[:, j * GDN_DK:(j + 1) * GDN_DK]) * (GDN_DK ** -0.5)
        kn = _l2norm(k[:, j * GDN_DK:(j + 1) * GDN_DK])
        qk = jnp.sum(qn * kn, axis=-1, keepdims=True)
        q_col = _row_to_col(qn, eye)
        k_col = _row_to_col(kn, eye)
        for h in range(j * (nh // GDN_QK_HEADS), (j + 1) * (nh // GDN_QK_HEADS)):
            b_h = beta[:, h:h + 1]
            eg = eg_all[:, nh + h:nh + h + 1]
            s_prev = s0_ref[0, h]
            ks = jnp.sum(s_prev * k_col, axis=0, keepdims=True)
            qs = jnp.sum(s_prev * q_col, axis=0, keepdims=True)
            v_h = v[:, h * GDN_DV:(h + 1) * GDN_DV]
            u = v_h * b_h - (b_h * eg) * ks
            o = eg * qs + qk * u
            s_ref[0, h] = s_prev * eg + k_col * u
            z_h = z_ref[0, :, h * GDN_DV:(h + 1) * GDN_DV]
            o_ref[0, :, h * GDN_DV:(h + 1) * GDN_DV] = _rms(o) * nw_ref[...] * jax.nn.silu(z_h)


def gdn_step(main, tail, conv0, s0, conv_w, dt_bias, a_log, norm_w):
    bsz = main.shape[0]
    dtb, alog = _gdn_gate_params(dt_bias, a_log)
    small = [conv_w, dtb, alog, norm_w.reshape(1, -1)]
    sspec = pl.BlockSpec((1, GDN_V_HEADS, GDN_DK, GDN_DV), lambda b: (b, 0, 0, 0))
    cspec = pl.BlockSpec((1, GDN_CONV - 1, GDN_QKV), lambda b: (b, 0, 0))
    return pl.pallas_call(
        _gdn_step_kernel,
        grid=(bsz,),
        in_specs=[
            pl.BlockSpec((1, 1, GDN_QK_W), lambda b: (b, 0, 0)),
            pl.BlockSpec((1, 1, GDN_QK_W), lambda b: (b, 0, 1)),
            pl.BlockSpec((1, 1, GDN_VW), lambda b: (b, 0, 1)),
            pl.BlockSpec((1, 1, GDN_VW), lambda b: (b, 0, 2)),
            pl.BlockSpec((1, 1, LANES), lambda b: (b, 0, 0)),
            cspec,
        ] + [_small(a) for a in small] + [sspec],
        out_specs=[pl.BlockSpec((1, 1, GDN_VW), lambda b: (b, 0, 0)), cspec, sspec],
        out_shape=[jax.ShapeDtypeStruct((bsz, 1, GDN_VW), F32),
                   jax.ShapeDtypeStruct(conv0.shape, F32),
                   jax.ShapeDtypeStruct(s0.shape, F32)],
        compiler_params=_cparams("parallel"),
        name="gdn_step",
    )(main, main, main, main, tail, conv0, *small, s0)


def _pad_tail(w):
    return jnp.pad(w, ((0, 0), (0, LANES - w.shape[1])))


def _prep_hyb_in(w):
    a = A_WIDTH
    q, k, v = w[:, 0:a], w[:, a:2 * a], w[:, 2 * a:3 * a]
    z = w[:, 3 * a:3 * a + SSM_D_INNER]
    x0 = 3 * a + SSM_D_INNER
    xs = w[:, x0:x0 + SSM_D_INNER]
    bc = w[:, x0 + SSM_D_INNER:x0 + SSM_XBC]
    dt = w[:, x0 + SSM_XBC:]
    return jnp.concatenate([z, xs, q, k, v, bc], axis=1).astype(BF16), _pad_tail(dt).astype(BF16)


def _prep_gdn_in(w):
    return w[:, :GDN_MAIN].astype(BF16), _pad_tail(w[:, GDN_MAIN:]).astype(BF16)


def _window_to_lanes(c):
    n, b, past, h, dh = c.shape
    return jnp.transpose(c, (0, 1, 3, 4, 2)).reshape(n, b, h * dh, past)


def _window_from_lanes(c):
    n, b, _, past = c.shape
    return jnp.transpose(c.reshape(n, b, A_HEADS, A_HEAD_DIM, past), (0, 1, 4, 2, 3))


def _row_tile(m, cap):
    return m if m <= cap else cap


def kernel(x_prompt, x_sample, cache_attn_k, cache_attn_v, state_ssm_conv, state_ssm, state_gdn_conv, state_gdn, rel_bias, norm_mix_pre, norm_mix_post, norm_ffn_pre, norm_ffn_post, w_hyb_in, ssm_conv_w, ssm_conv_b, ssm_dt_bias, ssm_a_log, ssm_d, ssm_norm_w, w_hyb_out, w_gdn_in, gdn_conv_w, gdn_dt_bias, gdn_a_log, gdn_norm_w, w_gdn_out, w_ffn_gate, w_ffn_up, w_ffn_down):
    depth = norm_mix_pre.shape[0]
    d_model = x_prompt.shape[-1]
    n_hyb, n_gdn = w_hyb_in.shape[0], w_gdn_in.shape[0]

    hyb_in = [_prep_hyb_in(w_hyb_in[i]) for i in range(n_hyb)]
    hyb_out = [(w_hyb_out[i, :A_WIDTH].astype(BF16), w_hyb_out[i, A_WIDTH:].astype(BF16)) for i in range(n_hyb)]
    gdn_in = [_prep_gdn_in(w_gdn_in[i]) for i in range(n_gdn)]
    gdn_out = [w_gdn_out[i].astype(BF16) for i in range(n_gdn)]
    ffn_w = [(w_ffn_gate[l].astype(BF16), w_ffn_up[l].astype(BF16), w_ffn_down[l].astype(BF16))
             for l in range(depth)]
    bias_tiles = _attn_bias_rows(rel_bias)

    def trunk(x3, k_pre, v_pre, sconv, sssm, gconv, gstate):
        bsz, seq, _ = x3.shape
        m = bsz * seq
        step = seq == 1
        tm_big = _row_tile(m, 1024)
        tm = _row_tile(m, 512)
        x = x3.reshape(m, d_model)
        nk, nv, nsc, nss, ngc, ngs = [], [], [], [], [], []
        rolled = None
        for l in range(depth):
            i = l // 2
            if l % 2 == 0:
                w_main, w_tail = hyb_in[i]
                main, tail = inproj(x, norm_mix_pre[l], w_main, w_tail, tm=tm_big, tn=512)
                main3 = main.reshape(bsz, seq, HYB_MAIN)
                tail3 = tail.reshape(bsz, seq, LANES)
                ssm_args = (ssm_conv_w[i], ssm_conv_b[i], ssm_dt_bias[i], ssm_a_log[i], ssm_d[i], ssm_norm_w[i])
                if step:
                    col = lambda c0: main[:, c0:c0 + A_WIDTH].reshape(bsz, A_WIDTH, 1)
                    o_attn, *rolled = attention_step(
                        col(HYB_Q0), col(HYB_K0), col(HYB_V0), k_pre, v_pre, rolled,
                        _attn_logw(rel_bias, k_pre.shape[-1]), layer=i)
                    y, c_new, s_new = ssd_step(main3, tail3, sconv[i], sssm[i], *ssm_args)
                else:
                    o_attn = attention_prompt(main3, bias_tiles, bsz=bsz, seq=seq)
                    keep = min(A_PATTERNS[-1][0], seq)
                    k_new = main3[:, seq - keep:, HYB_K0:HYB_K0 + A_WIDTH]
                    v_new = main3[:, seq - keep:, HYB_V0:HYB_V0 + A_WIDTH]
                    y, c_new, s_new = ssd_prompt(main3, tail3, sconv[i], sssm[i], *ssm_args, bsz=bsz, seq=seq)
                    nk.append(k_new.reshape(bsz, -1, A_HEADS, A_HEAD_DIM))
                    nv.append(v_new.reshape(bsz, -1, A_HEADS, A_HEAD_DIM))
                nsc.append(c_new)
                nss.append(s_new)
                x = outproj([o_attn.reshape(m, A_WIDTH), y.reshape(m, SSM_D_INNER)], list(hyb_out[i]),
                            x, norm_mix_post[l], tm=tm)
            else:
                w_main, w_tail = gdn_in[i]
                main, tail = inproj(x, norm_mix_pre[l], w_main, w_tail, tm=tm_big, tn=512)
                main3 = main.reshape(bsz, seq, GDN_MAIN)
                tail3 = tail.reshape(bsz, seq, LANES)
                gdn_args = (gdn_conv_w[i], gdn_dt_bias[i], gdn_a_log[i], gdn_norm_w[i])
                if step:
                    o, c_new, s_new = gdn_step(main3, tail3, gconv[i], gstate[i], *gdn_args)
                else:
                    o, c_new, s_new = gdn_prompt(main3, tail3, gconv[i], gstate[i], *gdn_args, bsz=bsz, seq=seq)
                ngc.append(c_new)
                ngs.append(s_new)
                x = outproj([o.reshape(m, GDN_VW)], [gdn_out[i]], x, norm_mix_post[l], tm=tm)
            wg, wu, wd = ffn_w[l]
            x = ffn(x, norm_ffn_pre[l], wg, wu, wd, norm_ffn_post[l], tm=tm, tf=wg.shape[1] // 2)
        k_out, v_out = [_window_from_lanes(r) for r in rolled] if step else (jnp.stack(nk), jnp.stack(nv))
        return (x.reshape(bsz, seq, d_model), k_out, v_out, jnp.stack(nsc), jnp.stack(nss),
                jnp.stack(ngc), jnp.stack(ngs))

    bsz = x_prompt.shape[0]
    dt_p = x_prompt.dtype
    p_sc0 = jnp.zeros((n_hyb, bsz, SSM_CONV - 1, SSM_XBC), dt_p)
    p_ss0 = jnp.zeros((n_hyb, bsz, SSM_HEADS, SSM_HEAD_DIM, SSM_STATE), F32)
    p_gc0 = jnp.zeros((n_gdn, bsz, GDN_CONV - 1, GDN_QKV), dt_p)
    p_gs0 = jnp.zeros((n_gdn, bsz, GDN_V_HEADS, GDN_DK, GDN_DV), F32)
    y_prompt, pk, pv, psc, pss, pgc, pgs = trunk(x_prompt, None, None, p_sc0, p_ss0, p_gc0, p_gs0)
    y_sample, sk, sv, ssc, sss, sgc, sgs = trunk(
        x_sample, _window_to_lanes(cache_attn_k), _window_to_lanes(cache_attn_v), state_ssm_conv, state_ssm,
        state_gdn_conv, state_gdn)
    return (y_prompt, y_sample, pk, pv, psc, pss, pgc, pgs, sk, sv, ssc, sss, sgc, sgs)
```

```python
import functools
import math

import numpy as np
import jax
import jax.numpy as jnp
from jax import lax
from jax.experimental import pallas as pl
from jax.experimental.pallas import tpu as pltpu

F32 = jnp.float32
BF16 = jnp.bfloat16
EPS = 1e-6
NEG = -1e30
HIGHEST = lax.Precision.HIGHEST

VMEM_LIMIT_BYTES = 56 * 1024 * 1024
LANES = 128

A_HEADS = 8
A_HEAD_DIM = 64
A_WIDTH = A_HEADS * A_HEAD_DIM
A_PATTERNS = ((128, 1), (512, 4), (2048, 16))
A_BAND = 128
ATTN_GROUP = 4
REL_BUCKETS = 32
REL_MAX_DIST = 2048

SSM_D_INNER = 1024
SSM_HEAD_DIM = 64
SSM_HEADS = SSM_D_INNER // SSM_HEAD_DIM
SSM_GROUPS = 2
SSM_STATE = 128
SSM_CONV = 4
SSM_CHUNK = 128
SSM_BC = 2 * SSM_GROUPS * SSM_STATE
SSM_XBC = SSM_D_INNER + SSM_BC

GDN_QK_HEADS = 8
GDN_V_HEADS = 16
GDN_DK = 128
GDN_DV = 128
GDN_CONV = 4
GDN_CHUNK = 64
GDN_QK_W = GDN_QK_HEADS * GDN_DK
GDN_VW = GDN_V_HEADS * GDN_DV
GDN_QKV = 2 * GDN_QK_W + GDN_VW

HYB_MAIN = 2 * SSM_D_INNER + 3 * A_WIDTH + SSM_BC
HYB_Q0 = 2 * SSM_D_INNER
HYB_K0 = HYB_Q0 + A_WIDTH
HYB_V0 = HYB_K0 + A_WIDTH
HYB_BC0 = HYB_V0 + A_WIDTH
GDN_MAIN = GDN_QKV + GDN_VW


def _cparams(*sem):
    return pltpu.CompilerParams(dimension_semantics=sem, vmem_limit_bytes=VMEM_LIMIT_BYTES)


def _rms(x):
    return x * lax.rsqrt(jnp.mean(x * x, axis=-1, keepdims=True) + EPS)


def _dot(a, b, **kw):
    return jnp.dot(a, b, preferred_element_type=F32, **kw)


def _dot_nt(a, b, **kw):
    return lax.dot_general(a, b, (((1,), (1,)), ((), ())), preferred_element_type=F32, **kw)


def _dot_tn(a, b, **kw):
    return lax.dot_general(a, b, (((0,), (0,)), ((), ())), preferred_element_type=F32, **kw)


def _iota(shape, dim):
    return lax.broadcasted_iota(jnp.int32, shape, dim)


def _inproj_kernel(x_ref, g_ref, w_ref, wt_ref, o_ref, t_ref, h_ref):
    @pl.when(pl.program_id(1) == 0)
    def _():
        hb = (_rms(x_ref[...]) * g_ref[...]).astype(BF16)
        h_ref[...] = hb
        t_ref[...] = _dot(hb, wt_ref[...])

    o_ref[...] = _dot(h_ref[...], w_ref[...])


def inproj(x, g, w_main, w_tail, *, tm, tn):
    m, d = x.shape
    n = w_main.shape[1]
    return pl.pallas_call(
        _inproj_kernel,
        grid=(m // tm, n // tn),
        in_specs=[
            pl.BlockSpec((tm, d), lambda i, j: (i, 0)),
            pl.BlockSpec((1, d), lambda i, j: (0, 0)),
            pl.BlockSpec((d, tn), lambda i, j: (0, j)),
            pl.BlockSpec((d, LANES), lambda i, j: (0, 0)),
        ],
        out_specs=[
            pl.BlockSpec((tm, tn), lambda i, j: (i, j)),
            pl.BlockSpec((tm, LANES), lambda i, j: (i, 0)),
        ],
        out_shape=[jax.ShapeDtypeStruct((m, n), F32), jax.ShapeDtypeStruct((m, LANES), F32)],
        scratch_shapes=[pltpu.VMEM((tm, d), BF16)],
        compiler_params=_cparams("parallel", "arbitrary"),
        name="inproj",
    )(x, g.reshape(1, d), w_main, w_tail)


CONV_TAPS = 4
CONV_BASE = 8


def _inproj_conv_kernel(x_ref, g_ref, w_ref, wt_ref, cw_ref, cb_ref, c0_ref, o_ref, t_ref, so_ref,
                        h_ref, xp_ref, carry_ref, *, conv_tiles, tiles_per_seq):
    i, j = pl.program_id(0), pl.program_id(1)
    tm = o_ref.shape[0]
    lo = CONV_BASE - (CONV_TAPS - 1)

    @pl.when(j == 0)
    def _():
        hb = (_rms(x_ref[...]) * g_ref[...]).astype(BF16)
        h_ref[...] = hb
        t_ref[...] = _dot(hb, wt_ref[...])

    xp_ref[CONV_BASE:CONV_BASE + tm, :] = _dot(h_ref[...], w_ref[...])
    tail = xp_ref[CONV_BASE + tm - (CONV_TAPS - 1):CONV_BASE + tm, :]
    so_ref[0] = tail
    is_conv = functools.reduce(jnp.logical_or, [jnp.logical_and(j >= a, j < b) for a, b in conv_tiles])

    @pl.when(jnp.logical_not(is_conv))
    def _():
        o_ref[...] = xp_ref[CONV_BASE:CONV_BASE + tm, :].astype(o_ref.dtype)

    @pl.when(is_conv)
    def _():
        first = i % tiles_per_seq == 0

        @pl.when(first)
        def _():
            xp_ref[lo:CONV_BASE, :] = c0_ref[0]

        @pl.when(jnp.logical_not(first))
        def _():
            xp_ref[lo:CONV_BASE, :] = carry_ref[j]

        conv = cb_ref[...] + cw_ref[0:1, :] * xp_ref[lo:lo + tm, :]
        for tap in range(1, CONV_TAPS):
            conv = conv + cw_ref[tap:tap + 1, :] * xp_ref[lo + tap:lo + tap + tm, :]
        o_ref[...] = jax.nn.silu(conv).astype(o_ref.dtype)
        carry_ref[j] = tail


def inproj_conv(x, g, w_main, w_tail, conv_w, conv_b, conv0, *, seq, conv_cols, tm, tn):
    m, d = x.shape
    n = w_main.shape[1]
    bsz = m // seq
    assert seq % tm == 0 and all(a % tn == 0 and b % tn == 0 for a, b in conv_cols)
    tiles_per_seq = seq // tm
    conv_tiles = tuple((a // tn, b // tn) for a, b in conv_cols)
    hist = CONV_TAPS - 1
    return pl.pallas_call(
        functools.partial(_inproj_conv_kernel, conv_tiles=conv_tiles, tiles_per_seq=tiles_per_seq),
        grid=(m // tm, n // tn),
        in_specs=[
            pl.BlockSpec((tm, d), lambda i, j: (i, 0)),
            pl.BlockSpec((1, d), lambda i, j: (0, 0)),
            pl.BlockSpec((d, tn), lambda i, j: (0, j)),
            pl.BlockSpec((d, LANES), lambda i, j: (0, 0)),
            pl.BlockSpec((CONV_TAPS, tn), lambda i, j: (0, j)),
            pl.BlockSpec((1, tn), lambda i, j: (0, j)),
            pl.BlockSpec((1, hist, tn), lambda i, j: (i // tiles_per_seq, 0, j)),
        ],
        out_specs=[
            pl.BlockSpec((tm, tn), lambda i, j: (i, j)),
            pl.BlockSpec((tm, LANES), lambda i, j: (i, 0)),
            pl.BlockSpec((1, hist, tn), lambda i, j: (i // tiles_per_seq, 0, j)),
        ],
        out_shape=[jax.ShapeDtypeStruct((m, n), BF16), jax.ShapeDtypeStruct((m, LANES), F32),
                   jax.ShapeDtypeStruct((bsz, hist, n), F32)],
        scratch_shapes=[pltpu.VMEM((tm, d), BF16), pltpu.VMEM((CONV_BASE + tm, tn), F32),
                        pltpu.VMEM((n // tn, hist, tn), F32)],
        compiler_params=_cparams("arbitrary", "arbitrary"),
        name="inproj_conv",
    )(x, g.reshape(1, d), w_main, w_tail, conv_w, conv_b, conv0)


def _outproj_kernel(*refs, n_in):
    a_refs, w_refs = refs[:n_in], refs[n_in:2 * n_in]
    x_ref, g_ref, o_ref = refs[2 * n_in:]
    acc = None
    for a_ref, w_ref in zip(a_refs, w_refs):
        t = _dot(a_ref[...].astype(BF16), w_ref[...])
        acc = t if acc is None else acc + t
    o_ref[...] = x_ref[...] + _rms(acc) * g_ref[...]


def outproj(acts, weights, x, g, *, tm):
    m, d = x.shape
    n_in = len(acts)
    in_specs = [pl.BlockSpec((tm, a.shape[1]), lambda i: (i, 0)) for a in acts]
    in_specs += [pl.BlockSpec(w.shape, lambda i: (0, 0)) for w in weights]
    in_specs += [pl.BlockSpec((tm, d), lambda i: (i, 0)), pl.BlockSpec((1, d), lambda i: (0, 0))]
    return pl.pallas_call(
        functools.partial(_outproj_kernel, n_in=n_in),
        grid=(m // tm,),
        in_specs=in_specs,
        out_specs=pl.BlockSpec((tm, d), lambda i: (i, 0)),
        out_shape=jax.ShapeDtypeStruct((m, d), F32),
        compiler_params=_cparams("parallel"),
        name="outproj",
    )(*acts, *weights, x, g.reshape(1, d))


def _ffn_kernel(x_ref, g1_ref, wg_ref, wu_ref, wd_ref, g2_ref, o_ref, h_ref, acc_ref):
    j = pl.program_id(1)

    @pl.when(j == 0)
    def _():
        h_ref[...] = (_rms(x_ref[...]) * g1_ref[...]).astype(BF16)
        acc_ref[...] = jnp.zeros_like(acc_ref)

    h = h_ref[...]
    a = jax.nn.silu(_dot(h, wg_ref[...])) * _dot(h, wu_ref[...])
    acc_ref[...] += _dot(a.astype(BF16), wd_ref[...])

    @pl.when(j == pl.num_programs(1) - 1)
    def _():
        o_ref[...] = x_ref[...] + _rms(acc_ref[...]) * g2_ref[...]


def ffn(x, g1, wg, wu, wd, g2, *, tm, tf):
    m, d = x.shape
    dff = wg.shape[1]
    return pl.pallas_call(
        _ffn_kernel,
        grid=(m // tm, dff // tf),
        in_specs=[
            pl.BlockSpec((tm, d), lambda i, j: (i, 0)),
            pl.BlockSpec((1, d), lambda i, j: (0, 0)),
            pl.BlockSpec((d, tf), lambda i, j: (0, j)),
            pl.BlockSpec((d, tf), lambda i, j: (0, j)),
            pl.BlockSpec((tf, d), lambda i, j: (j, 0)),
            pl.BlockSpec((1, d), lambda i, j: (0, 0)),
        ],
        out_specs=pl.BlockSpec((tm, d), lambda i, j: (i, 0)),
        out_shape=jax.ShapeDtypeStruct((m, d), F32),
        scratch_shapes=[pltpu.VMEM((tm, d), BF16), pltpu.VMEM((tm, d), F32)],
        compiler_params=_cparams("parallel", "arbitrary"),
        name="ffn",
    )(x, g1.reshape(1, d), wg, wu, wd, g2.reshape(1, d))


def _rel_buckets(dist):
    max_exact = REL_BUCKETS // 2
    n = np.maximum(dist, 1).astype(np.float32)
    large = max_exact + (np.log(n / max_exact) / math.log(REL_MAX_DIST / max_exact)
                         * (REL_BUCKETS - max_exact)).astype(np.int32)
    large = np.minimum(large, REL_BUCKETS - 1)
    return np.where(dist < max_exact, dist, large).astype(np.int32)


def _attn_bias_rows(rel_bias):
    u = np.arange(2 * A_BAND)
    valid = u <= A_BAND
    rows = []
    for (_, d) in A_PATTERNS:
        b = rel_bias[_rel_buckets(np.where(valid, A_BAND - u, 0) * d)]
        rows.append(jnp.where(valid[:, None], b.astype(F32), NEG))
    tl = jnp.transpose(jnp.stack(rows), (2, 0, 1))
    tl = tl.reshape(A_HEADS // 2, 2, len(A_PATTERNS), 2 * A_BAND)
    tl = jnp.transpose(tl, (0, 2, 1, 3))[:, :, :, None, :]
    return jnp.broadcast_to(tl, tl.shape[:3] + (8, 2 * A_BAND))


def _attn_kernel(qin_ref, kin_ref, vin_ref, brow_ref, o_ref, m0_ref, m1_ref, l0_ref, l1_ref, acc_ref,
                 q_ref, k_ref, v_ref, b_ref, *, seq):
    n_tiles = seq // A_BAND
    lane = _iota((A_BAND, LANES), 1)
    head0 = lane < A_HEAD_DIM
    m_refs, l_refs = (m0_ref, m1_ref), (l0_ref, l1_ref)
    q_ref[...] = qin_ref[...].astype(F32) * (A_HEAD_DIM ** -0.5)
    k_ref[...] = kin_ref[...].astype(F32)
    v_ref[...] = vin_ref[...].astype(F32)
    for p in range(len(A_PATTERNS)):
        for h in range(2):
            row = jnp.broadcast_to(brow_ref[0, p, h, 0:1, :], (A_BAND, 2 * A_BAND))
            b_ref[0, p, h] = pltpu.roll(row, 0, 1, stride=1, stride_axis=0)
    for h in range(2):
        m_refs[h][...] = jnp.full(m_refs[h].shape, NEG, F32)
        l_refs[h][...] = jnp.zeros_like(l_refs[h])
    acc_ref[...] = jnp.zeros_like(acc_ref)

    for p, (_, d) in enumerate(A_PATTERNS):
        tiles_per_class = n_tiles // d

        def load_tile(idx, d=d, tiles_per_class=tiles_per_class):
            r = idx // tiles_per_class
            t = idx % tiles_per_class
            start = r + t * (d * A_BAND)
            has_prev = t > 0
            prev = jnp.where(has_prev, start - d * A_BAND, start)
            if d > 1:
                rows, prows = pl.ds(start, A_BAND, stride=d), pl.ds(prev, A_BAND, stride=d)
            else:
                rows, prows = pl.ds(pl.multiple_of(start, A_BAND), A_BAND), pl.ds(pl.multiple_of(prev, A_BAND), A_BAND)
            return dict(
                rows=rows, has_prev=has_prev, q=q_ref[rows, :],
                k2=jnp.concatenate([k_ref[prows, :], k_ref[rows, :]], axis=0).astype(BF16),
                v2=jnp.concatenate([v_ref[prows, :], v_ref[rows, :]], axis=0).astype(BF16),
                acc=acc_ref[rows, :], m=[m_refs[h][rows, :] for h in range(2)],
                l=[l_refs[h][rows, :] for h in range(2)])

        def tile_group(idx, carry, load_tile=load_tile, p=p):
            tiles = [load_tile(idx + i * (n_tiles // ATTN_GROUP)) for i in range(ATTN_GROUP)]
            chains = [(tile, h) for tile in tiles for h in range(2)]
            col = _iota((A_BAND, 2 * A_BAND), 1)
            qh = [jnp.where(head0 if h == 0 else jnp.logical_not(head0), tile["q"], 0.0).astype(BF16)
                  for tile, h in chains]
            s = [_dot_nt(qh[c], tile["k2"])
                 + jnp.where(jnp.logical_and(col < A_BAND, jnp.logical_not(tile["has_prev"])), NEG, b_ref[0, p, h])
                 for c, (tile, h) in enumerate(chains)]
            m_new = [jnp.maximum(tile["m"][h], jnp.max(s[c], axis=-1, keepdims=True))
                     for c, (tile, h) in enumerate(chains)]
            alpha = [jnp.exp(tile["m"][h] - m_new[c]) for c, (tile, h) in enumerate(chains)]
            pr = [jnp.exp(s[c] - jnp.concatenate([m_new[c], m_new[c]], axis=1)) for c in range(len(chains))]
            l_new = [alpha[c] * tile["l"][h] + jnp.sum(pr[c], axis=-1, keepdims=True)
                     for c, (tile, h) in enumerate(chains)]
            acc_new = [alpha[c] * tile["acc"] + _dot(pr[c].astype(BF16), tile["v2"])
                       for c, (tile, h) in enumerate(chains)]
            for c, (tile, h) in enumerate(chains):
                m_refs[h][tile["rows"], :] = m_new[c]
                l_refs[h][tile["rows"], :] = l_new[c]
                if h == 1:
                    acc_ref[tile["rows"], :] = jnp.where(head0, acc_new[c - 1], acc_new[c])
            return carry

        lax.fori_loop(0, n_tiles // ATTN_GROUP, tile_group, 0)

    lane_s = _iota((seq, LANES), 1)
    o_ref[...] = (acc_ref[...] / jnp.where(lane_s < A_HEAD_DIM, l0_ref[...], l1_ref[...])).astype(o_ref.dtype)


def attention_prompt(proj, bias_tiles, *, bsz, seq):
    hp = A_HEADS // 2
    qb, kb, vb = HYB_Q0 // LANES, HYB_K0 // LANES, HYB_V0 // LANES
    return pl.pallas_call(
        functools.partial(_attn_kernel, seq=seq),
        grid=(bsz, hp),
        in_specs=[
            pl.BlockSpec((None, seq, LANES), lambda b, h: (b, 0, qb + h)),
            pl.BlockSpec((None, seq, LANES), lambda b, h: (b, 0, kb + h)),
            pl.BlockSpec((None, seq, LANES), lambda b, h: (b, 0, vb + h)),
            pl.BlockSpec((1,) + bias_tiles.shape[1:], lambda b, h: (h, 0, 0, 0, 0)),
        ],
        out_specs=pl.BlockSpec((None, seq, LANES), lambda b, h: (b, 0, h)),
        out_shape=jax.ShapeDtypeStruct((bsz, seq, A_WIDTH), BF16),
        scratch_shapes=[pltpu.VMEM((seq, LANES), F32)] * 8
        + [pltpu.VMEM((1, len(A_PATTERNS), 2, A_BAND, 2 * A_BAND), F32)],
        compiler_params=_cparams("parallel", "parallel"),
        name="attn_prompt",
    )(proj, proj, proj, bias_tiles)


def _attn_logw(rel_bias, past):
    dist = np.arange(past + 1)
    count = np.zeros(past + 1, np.float64)
    for (w, d) in A_PATTERNS:
        count += ((dist % d == 0) & (dist <= w)).astype(np.float64)
    logc = np.where(count > 0, np.log(np.maximum(count, 1.0)), 0.0).astype(np.float32)
    lw = rel_bias[_rel_buckets(dist)].astype(F32).T + logc[None, :]
    return jnp.where((count > 0)[None, :], lw, NEG)


def _step_scores(xk_ref, q_col, kn_col, lw_ref, lw0_ref, s_ref):
    qs = q_col * (A_HEAD_DIM ** -0.5)
    s_new = []
    for h in range(A_HEADS):
        rows = slice(h * A_HEAD_DIM, (h + 1) * A_HEAD_DIM)
        s_ref[h:h + 1, :] = jnp.sum(xk_ref[0, 0, rows, :] * qs[rows], axis=0, keepdims=True)
        s_new.append(jnp.sum(kn_col[rows] * qs[rows], axis=0, keepdims=True))
    s = s_ref[...] + lw_ref[...]
    s_new = jnp.concatenate(s_new, axis=0) + lw0_ref[:, :1]
    m = jnp.maximum(jnp.max(s, axis=-1, keepdims=True), s_new)
    p = jnp.exp(s - m)
    p_new = jnp.exp(s_new - m)
    den = jnp.sum(p, axis=-1, keepdims=True) + p_new
    return p, p_new, den


def _step_output(xv_ref, vn_col, p, p_new, den, o_ref):
    for h in range(A_HEADS):
        rows = slice(h * A_HEAD_DIM, (h + 1) * A_HEAD_DIM)
        pv = jnp.sum(xv_ref[0, 0, rows, :] * p[h:h + 1, :], axis=-1, keepdims=True)
        o_ref[0, rows, :] = (pv + p_new[h:h + 1, :] * vn_col[rows]) / den[h:h + 1, :]


def _attn_step_roll_kernel(q_ref, kn_ref, vn_ref, lw_ref, lw0_ref, xk_ref, xv_ref, o_ref, ko_ref, vo_ref, s_ref,
                           *, layer, past):
    is_layer = pl.program_id(0) == layer
    newest = _iota((A_HEAD_DIM, past), 1) == past - 1
    for x_ref, n_ref, out_ref in ((xk_ref, kn_ref, ko_ref), (xv_ref, vn_ref, vo_ref)):
        for h in range(A_HEADS):
            rows = slice(h * A_HEAD_DIM, (h + 1) * A_HEAD_DIM)
            rolled = pltpu.roll(x_ref[0, 0, rows, :], past - 1, 1)
            out_ref[0, 0, rows, :] = jnp.where(jnp.logical_and(newest, is_layer), n_ref[0, rows, :], rolled)

    @pl.when(is_layer)
    def _():
        p, p_new, den = _step_scores(xk_ref, q_ref[0], kn_ref[0], lw_ref, lw0_ref, s_ref)
        _step_output(xv_ref, vn_ref[0], p, p_new, den, o_ref.at[0])

    @pl.when(jnp.logical_not(is_layer))
    def _():
        o_ref[...] = jnp.zeros_like(o_ref)


def _attn_step_append_kernel(q_ref, kn_ref, vn_ref, lw_ref, lw0_ref, xk_ref, xv_ref, ko_in, vo_in,
                             o_ref, ko_ref, vo_ref, s_ref, *, past):
    del ko_in, vo_in
    p, p_new, den = _step_scores(xk_ref, q_ref[0], kn_ref[0], lw_ref, lw0_ref, s_ref)
    _step_output(xv_ref, vn_ref[0], p, p_new, den, o_ref)
    newest = _iota((A_WIDTH, LANES), 1) == LANES - 1
    for x_ref, n_ref, out_ref in ((xk_ref, kn_ref, ko_ref), (xv_ref, vn_ref, vo_ref)):
        rolled = pltpu.roll(x_ref[0, 0, :, past - LANES:past], LANES - 1, 1)
        out_ref[0, 0] = jnp.where(newest, n_ref[0], rolled)


def attention_step(q_col, kn_col, vn_col, cache_k, cache_v, rolled, logw, *, layer):
    n_layers, bsz, w, past = cache_k.shape
    lw_cache = logw[:, past:0:-1]
    lw_new = jnp.broadcast_to(logw[:, :1], (A_HEADS, LANES))
    out_shape = [jax.ShapeDtypeStruct((bsz, w, 1), F32),
                 jax.ShapeDtypeStruct(cache_k.shape, cache_k.dtype),
                 jax.ShapeDtypeStruct(cache_v.shape, cache_v.dtype)]
    scratch = [pltpu.VMEM((A_HEADS, past), F32)]
    if rolled is None:
        assert layer == 0
        col = pl.BlockSpec((1, w, 1), lambda l, b: (b, 0, 0))
        win = pl.BlockSpec((1, 1, w, past), lambda l, b: (l, b, 0, 0))
        o_all, rolled_k, rolled_v = pl.pallas_call(
            functools.partial(_attn_step_roll_kernel, layer=layer, past=past),
            grid=(n_layers, bsz),
            in_specs=[col, col, col, _small(lw_cache), _small(lw_new), win, win],
            out_specs=[pl.BlockSpec((1, 1, w, 1), lambda l, b: (l, b, 0, 0)), win, win],
            out_shape=[jax.ShapeDtypeStruct((n_layers, bsz, w, 1), F32)] + out_shape[1:],
            scratch_shapes=scratch,
            compiler_params=_cparams("arbitrary", "arbitrary"),
            name="attn_step_roll",
        )(q_col, kn_col, vn_col, lw_cache, lw_new, cache_k, cache_v)
        return o_all[layer], rolled_k, rolled_v
    col = pl.BlockSpec((1, w, 1), lambda b: (b, 0, 0))
    win = pl.BlockSpec((1, 1, w, past), lambda b: (layer, b, 0, 0))
    tail = pl.BlockSpec((1, 1, w, LANES), lambda b: (layer, b, 0, past // LANES - 1))
    return pl.pallas_call(
        functools.partial(_attn_step_append_kernel, past=past),
        grid=(bsz,),
        in_specs=[col, col, col, _small(lw_cache), _small(lw_new), win, win, tail, tail],
        out_specs=[col, tail, tail],
        out_shape=out_shape,
        scratch_shapes=scratch,
        input_output_aliases={7: 1, 8: 2},
        compiler_params=_cparams("arbitrary"),
        name="attn_step_append",
    )(q_col, kn_col, vn_col, lw_cache, lw_new, cache_k, cache_v, *rolled)


def _group_rms(y, w):
    half = SSM_D_INNER // SSM_GROUPS
    return [_rms(y[:, g * half:(g + 1) * half]) * w[:, g * half:(g + 1) * half] for g in range(SSM_GROUPS)]


def _ssd_kernel(z_ref, xs_ref, bc_ref, dt_ref, dtb_ref, alog_ref, d_ref, nw_ref, h0_ref, y_ref, h_ref, ys_ref):
    c = pl.program_id(1)
    ch = SSM_CHUNK

    @pl.when(c == 0)
    def _():
        h_ref[...] = h0_ref[...]

    xs = xs_ref[0].astype(F32)
    bc = bc_ref[0].astype(F32)

    dt = jax.nn.softplus(dt_ref[0] + dtb_ref[...])
    da = dt * (-jnp.exp(alog_ref[...]))
    row = _iota((ch, ch), 0)
    colv = _iota((ch, ch), 1)
    tril = (row >= colv).astype(F32)
    cs = _dot(tril, da, precision=HIGHEST)
    cs_t = cs.T
    causal = row >= colv

    heads = range(SSM_HEADS)
    hpg = SSM_HEADS // SSM_GROUPS
    bm = [bc[:, g * SSM_STATE:(g + 1) * SSM_STATE] for g in range(SSM_GROUPS)]
    cm = [bc[:, (SSM_GROUPS + g) * SSM_STATE:(SSM_GROUPS + g + 1) * SSM_STATE] for g in range(SSM_GROUPS)]
    cb = [_dot_nt(cm[g], bm[g]) for g in range(SSM_GROUPS)]
    dt_t = dt.T
    xs_t = xs.T
    w_t = dt_t * jnp.exp(cs_t[:, ch - 1:ch] - cs_t)
    e_last = jnp.exp(cs[ch - 1:ch, :])
    lane_lo = _iota((ch, LANES), 1) < SSM_HEAD_DIM
    row_lo = _iota((LANES, ch), 0) < SSM_HEAD_DIM
    pairs = range(SSM_HEADS // 2)
    csb = [jnp.broadcast_to(cs[:, h:h + 1], (ch, ch)) for h in heads]
    mix = [cb[h // hpg] * jnp.exp(jnp.where(causal, csb[h] - cs_t[h:h + 1, :], NEG)) * dt_t[h:h + 1, :]
           for h in heads]
    x_pair = [xs[:, j * LANES:(j + 1) * LANES] for j in pairs]
    y_intra = [jnp.where(lane_lo, _dot(mix[2 * j], x_pair[j]), _dot(mix[2 * j + 1], x_pair[j])) for j in pairs]
    h_pair = [h_ref[0, 2 * j:2 * j + 2].reshape(2 * SSM_HEAD_DIM, SSM_STATE) for j in pairs]
    y_inter = [_dot_nt(cm[2 * j // hpg], h_pair[j]) * jnp.exp(jnp.where(lane_lo, csb[2 * j], csb[2 * j + 1]))
               for j in pairs]
    xw_t = [xs_t[j * LANES:(j + 1) * LANES, :] * jnp.where(row_lo, w_t[2 * j:2 * j + 1, :], w_t[2 * j + 1:2 * j + 2, :])
            for j in pairs]
    st = [_dot(xw_t[j], bm[2 * j // hpg]) for j in pairs]
    for j in pairs:
        cols = slice(j * LANES, (j + 1) * LANES)
        ys_ref[:, cols] = y_intra[j] + y_inter[j] + d_ref[:, cols] * x_pair[j]
        decay = jnp.where(row_lo, e_last[:, 2 * j:2 * j + 1], e_last[:, 2 * j + 1:2 * j + 2])
        h_ref[0, 2 * j:2 * j + 2] = (h_pair[j] * decay + st[j]).reshape(2, SSM_HEAD_DIM, SSM_STATE)

    y = ys_ref[...] * jax.nn.silu(z_ref[0].astype(F32))
    half = SSM_D_INNER // SSM_GROUPS
    for g, yg in enumerate(_group_rms(y, nw_ref[...])):
        y_ref[0, :, g * half:(g + 1) * half] = yg.astype(y_ref.dtype)


def _small(a):
    return pl.BlockSpec(a.shape, lambda *_: (0,) * a.ndim)


def ssd_prompt(main, tail, h0, dt_bias, a_log, d_skip, norm_w, *, bsz, seq):
    nc = seq // SSM_CHUNK
    ch = SSM_CHUNK
    small = [_pad_tail(dt_bias.reshape(1, -1)), _pad_tail(a_log.reshape(1, -1)),
             jnp.repeat(d_skip, SSM_HEAD_DIM).reshape(1, -1), norm_w.reshape(1, -1)]
    return pl.pallas_call(
        _ssd_kernel,
        grid=(bsz, nc),
        in_specs=[
            pl.BlockSpec((1, ch, SSM_D_INNER), lambda b, c: (b, c, 0)),
            pl.BlockSpec((1, ch, SSM_D_INNER), lambda b, c: (b, c, 1)),
            pl.BlockSpec((1, ch, SSM_BC), lambda b, c: (b, c, HYB_BC0 // SSM_BC)),
            pl.BlockSpec((1, ch, LANES), lambda b, c: (b, c, 0)),
        ] + [_small(a) for a in small] + [
            pl.BlockSpec((1, SSM_HEADS, SSM_HEAD_DIM, SSM_STATE), lambda b, c: (b, 0, 0, 0)),
        ],
        out_specs=[
            pl.BlockSpec((1, ch, SSM_D_INNER), lambda b, c: (b, c, 0)),
            pl.BlockSpec((1, SSM_HEADS, SSM_HEAD_DIM, SSM_STATE), lambda b, c: (b, 0, 0, 0)),
        ],
        out_shape=[jax.ShapeDtypeStruct((bsz, seq, SSM_D_INNER), BF16),
                   jax.ShapeDtypeStruct((bsz, SSM_HEADS, SSM_HEAD_DIM, SSM_STATE), F32)],
        scratch_shapes=[pltpu.VMEM((ch, SSM_D_INNER), F32)],
        compiler_params=_cparams("parallel", "arbitrary"),
        name="ssd_prompt",
    )(main, main, main, tail, *small, h0)


def _row_to_col(row, eye):
    return jnp.sum(jnp.where(eye, row, 0.0), axis=1, keepdims=True)


def _col_to_row(col, eye):
    return jnp.sum(jnp.where(eye, col, 0.0), axis=0, keepdims=True)


def _conv_step(c0_ref, w_ref, x_row, c0, c1):
    acc = w_ref[SSM_CONV - 1:SSM_CONV, c0:c1] * x_row
    for i in range(SSM_CONV - 1):
        acc = acc + w_ref[i:i + 1, c0:c1] * c0_ref[0, i:i + 1, c0:c1]
    return acc


def _ssd_step_kernel(z_ref, xs_ref, bc_ref, dt_ref, c0_ref, cw_ref, cb_ref, dtb_ref, alog_ref, d_ref, nw_ref,
                     h0_ref, y_ref, co_ref, h_ref, ys_ref):
    xs_raw = xs_ref[0]
    bc_raw = bc_ref[0]
    xs = jax.nn.silu(_conv_step(c0_ref, cw_ref, xs_raw, 0, SSM_D_INNER) + cb_ref[:, 0:SSM_D_INNER])
    bc = jax.nn.silu(_conv_step(c0_ref, cw_ref, bc_raw, SSM_D_INNER, SSM_XBC) + cb_ref[:, SSM_D_INNER:SSM_XBC])
    co_ref[0, 0:SSM_CONV - 2, :] = c0_ref[0, 1:SSM_CONV - 1, :]
    co_ref[0, SSM_CONV - 2:SSM_CONV - 1, 0:SSM_D_INNER] = xs_raw
    co_ref[0, SSM_CONV - 2:SSM_CONV - 1, SSM_D_INNER:SSM_XBC] = bc_raw

    dt = jax.nn.softplus(dt_ref[0] + dtb_ref[...])
    dec = jnp.exp(dt * (-jnp.exp(alog_ref[...])))
    dskip = d_ref[...]
    eye = _iota((LANES, LANES), 0) == _iota((LANES, LANES), 1)
    upper = _iota((LANES, 1), 0) >= SSM_HEAD_DIM
    for j in range(SSM_HEADS // 2):
        h0i, h1i = 2 * j, 2 * j + 1
        g = h0i // (SSM_HEADS // SSM_GROUPS)
        bm = bc[:, g * SSM_STATE:(g + 1) * SSM_STATE]
        cm = bc[:, (SSM_GROUPS + g) * SSM_STATE:(SSM_GROUPS + g + 1) * SSM_STATE]
        x_row = xs[:, j * LANES:(j + 1) * LANES]
        x_col = _row_to_col(x_row, eye)
        pick = lambda v: jnp.where(upper, v[:, h1i:h1i + 1], v[:, h0i:h0i + 1])
        hp = h0_ref[0, h0i:h1i + 1].reshape(2 * SSM_HEAD_DIM, SSM_STATE)
        hn = hp * pick(dec) + (x_col * pick(dt)) * bm
        h_ref[0, h0i:h1i + 1] = hn.reshape(2, SSM_HEAD_DIM, SSM_STATE)
        y_col = jnp.sum(hn * cm, axis=1, keepdims=True) + pick(dskip) * x_col
        ys_ref[:, j * LANES:(j + 1) * LANES] = _col_to_row(y_col, eye)

    y = ys_ref[...] * jax.nn.silu(z_ref[0])
    half = SSM_D_INNER // SSM_GROUPS
    for g, yg in enumerate(_group_rms(y, nw_ref[...])):
        y_ref[0, :, g * half:(g + 1) * half] = yg


def ssd_step(main, tail, conv0, h0, conv_w, conv_b, dt_bias, a_log, d_skip, norm_w):
    bsz = main.shape[0]
    small = [conv_w, conv_b.reshape(1, -1), _pad_tail(dt_bias.reshape(1, -1)), _pad_tail(a_log.reshape(1, -1)),
             _pad_tail(d_skip.reshape(1, -1)), norm_w.reshape(1, -1)]
    hspec = pl.BlockSpec((1, SSM_HEADS, SSM_HEAD_DIM, SSM_STATE), lambda b: (b, 0, 0, 0))
    cspec = pl.BlockSpec((1, SSM_CONV - 1, SSM_XBC), lambda b: (b, 0, 0))
    return pl.pallas_call(
        _ssd_step_kernel,
        grid=(bsz,),
        in_specs=[
            pl.BlockSpec((1, 1, SSM_D_INNER), lambda b: (b, 0, 0)),
            pl.BlockSpec((1, 1, SSM_D_INNER), lambda b: (b, 0, 1)),
            pl.BlockSpec((1, 1, SSM_BC), lambda b: (b, 0, HYB_BC0 // SSM_BC)),
            pl.BlockSpec((1, 1, LANES), lambda b: (b, 0, 0)),
            cspec,
        ] + [_small(a) for a in small] + [hspec],
        out_specs=[pl.BlockSpec((1, 1, SSM_D_INNER), lambda b: (b, 0, 0)), cspec, hspec],
        out_shape=[jax.ShapeDtypeStruct((bsz, 1, SSM_D_INNER), F32),
                   jax.ShapeDtypeStruct(conv0.shape, F32),
                   jax.ShapeDtypeStruct(h0.shape, F32)],
        scratch_shapes=[pltpu.VMEM((1, SSM_D_INNER), F32)],
        compiler_params=_cparams("parallel"),
        name="ssd_step",
    )(main, main, main, tail, conv0, *small, h0)


def _l2norm(x):
    return x * lax.rsqrt(jnp.sum(x * x, axis=-1, keepdims=True) + EPS)


def _unit_lower_inverse(ns, eye):
    size = ns[0].shape[0]
    ps = [eye - n for n in ns]
    ms = [_dot(n, n) for n in ns]
    power = 2
    while 2 * power < size:
        pms = [_dot(jnp.concatenate([p, m], axis=0), m) for p, m in zip(ps, ms)]
        ps = [p + pm[:size] for p, pm in zip(ps, pms)]
        ms = [pm[size:] for pm in pms]
        power *= 2
    return [p + _dot(p, m) for p, m in zip(ps, ms)]


def _gdn_gates(ba, dtb_ref, alog_ref):
    beta = jax.nn.sigmoid(ba)
    g = -jnp.exp(alog_ref[...]) * jax.nn.softplus(ba + dtb_ref[...])
    return beta, g


def _gdn_kernel(q_ref, k_ref, v_ref, z_ref, ba_ref, dtb_ref, alog_ref, nw_ref, s0_ref, o_ref, s_ref):
    c = pl.program_id(1)
    ch = GDN_CHUNK
    nh = GDN_V_HEADS

    @pl.when(c == 0)
    def _():
        s_ref[...] = s0_ref[...]

    beta, g = _gdn_gates(ba_ref[0], dtb_ref, alog_ref)
    row = _iota((ch, ch), 0)
    colv = _iota((ch, ch), 1)
    incl = row >= colv
    strict = row > colv
    eye = (row == colv).astype(F32)
    gcum = _dot(incl.astype(F32), g, precision=HIGHEST)
    gcum_t = jnp.concatenate([gcum, jnp.zeros((LANES - ch, LANES), F32)], axis=0).T

    heads = range(nh)
    rep = nh // GDN_QK_HEADS
    qn = [_l2norm(q_ref[0, :, j * GDN_DK:(j + 1) * GDN_DK].astype(F32)) * (GDN_DK ** -0.5)
          for j in range(GDN_QK_HEADS)]
    kn = [_l2norm(k_ref[0, :, j * GDN_DK:(j + 1) * GDN_DK].astype(F32)) for j in range(GDN_QK_HEADS)]
    kk = [_dot_nt(k, k) for k in kn]
    qk = [_dot_nt(q, k) for q, k in zip(qn, kn)]
    gc_col = [gcum[:, nh + h:nh + h + 1] for h in heads]
    gc_last = [gcum[ch - 1:ch, nh + h:nh + h + 1] for h in heads]
    beta_col = [beta[:, h:h + 1] for h in heads]
    dec = [jnp.exp(jnp.where(incl, gc_col[h] - gcum_t[nh + h:nh + h + 1, 0:ch], NEG)) for h in heads]
    t_inv = _unit_lower_inverse(
        [jnp.where(strict, kk[h // rep] * dec[h], 0.0) * beta_col[h] for h in heads], eye)
    eg = [jnp.exp(gc_col[h]) for h in heads]
    s_prev = [s_ref[0, h] for h in heads]
    both = [_dot(jnp.concatenate([kn[h // rep] * (beta_col[h] * eg[h]), qn[h // rep] * eg[h]], axis=0), s_prev[h])
            for h in heads]
    u = [_dot(t_inv[h], v_ref[0, :, h * GDN_DV:(h + 1) * GDN_DV].astype(F32) * beta_col[h] - both[h][:ch])
         for h in heads]
    o = [both[h][ch:] + _dot(qk[h // rep] * dec[h], u[h]) for h in heads]
    s_new = [s_prev[h] * jnp.exp(gc_last[h]) + _dot_tn(kn[h // rep] * jnp.exp(gc_last[h] - gc_col[h]), u[h])
             for h in heads]
    for h in heads:
        s_ref[0, h] = s_new[h]
        z_h = z_ref[0, :, h * GDN_DV:(h + 1) * GDN_DV].astype(F32)
        o_ref[0, :, h * GDN_DV:(h + 1) * GDN_DV] = (_rms(o[h]) * nw_ref[...] * jax.nn.silu(z_h)).astype(o_ref.dtype)


def _gdn_gate_params(dt_bias, a_log):
    nh = GDN_V_HEADS
    dtb = jnp.zeros((1, LANES), F32).at[0, nh:2 * nh].set(dt_bias)
    alog = jnp.zeros((1, LANES), F32).at[0, nh:2 * nh].set(a_log)
    return dtb, alog


def gdn_prompt(main, tail, s0, dt_bias, a_log, norm_w, *, bsz, seq):
    ch = GDN_CHUNK
    nc = seq // ch
    dtb, alog = _gdn_gate_params(dt_bias, a_log)
    small = [dtb, alog, norm_w.reshape(1, -1)]
    sspec = pl.BlockSpec((1, GDN_V_HEADS, GDN_DK, GDN_DV), lambda b, c: (b, 0, 0, 0))
    return pl.pallas_call(
        _gdn_kernel,
        grid=(bsz, nc),
        in_specs=[
            pl.BlockSpec((1, ch, GDN_QK_W), lambda b, c: (b, c, 0)),
            pl.BlockSpec((1, ch, GDN_QK_W), lambda b, c: (b, c, 1)),
            pl.BlockSpec((1, ch, GDN_VW), lambda b, c: (b, c, 1)),
            pl.BlockSpec((1, ch, GDN_VW), lambda b, c: (b, c, 2)),
            pl.BlockSpec((1, ch, LANES), lambda b, c: (b, c, 0)),
        ] + [_small(a) for a in small] + [sspec],
        out_specs=[pl.BlockSpec((1, ch, GDN_VW), lambda b, c: (b, c, 0)), sspec],
        out_shape=[jax.ShapeDtypeStruct((bsz, seq, GDN_VW), BF16),
                   jax.ShapeDtypeStruct((bsz, GDN_V_HEADS, GDN_DK, GDN_DV), F32)],
        compiler_params=_cparams("parallel", "arbitrary"),
        name="gdn_prompt",
    )(main, main, main, main, tail, *small, s0)


def _gdn_conv_step(c0_ref, w_ref, x_row, c0, c1):
    acc = w_ref[GDN_CONV - 1:GDN_CONV, c0:c1] * x_row
    for i in range(GDN_CONV - 1):
        acc = acc + w_ref[i:i + 1, c0:c1] * c0_ref[0, i:i + 1, c0:c1]
    return acc


def _gdn_step_kernel(q_ref, k_ref, v_ref, z_ref, ba_ref, c0_ref, cw_ref, dtb_ref, alog_ref, nw_ref, s0_ref,
                     o_ref, co_ref, s_ref):
    nh = GDN_V_HEADS
    q_raw, k_raw, v_raw = q_ref[0], k_ref[0], v_ref[0]
    q = jax.nn.silu(_gdn_conv_step(c0_ref, cw_ref, q_raw, 0, GDN_QK_W))
    k = jax.nn.silu(_gdn_conv_step(c0_ref, cw_ref, k_raw, GDN_QK_W, 2 * GDN_QK_W))
    v = jax.nn.silu(_gdn_conv_step(c0_ref, cw_ref, v_raw, 2 * GDN_QK_W, GDN_QKV))
    co_ref[0, 0:GDN_CONV - 2, :] = c0_ref[0, 1:GDN_CONV - 1, :]
    co_ref[0, GDN_CONV - 2:GDN_CONV - 1, 0:GDN_QK_W] = q_raw
    co_ref[0, GDN_CONV - 2:GDN_CONV - 1, GDN_QK_W:2 * GDN_QK_W] = k_raw
    co_ref[0, GDN_CONV - 2:GDN_CONV - 1, 2 * GDN_QK_W:GDN_QKV] = v_raw

    beta, g = _gdn_gates(ba_ref[0], dtb_ref, alog_ref)
    eg_all = jnp.exp(g)
    eye = _iota((LANES, LANES), 0) == _iota((LANES, LANES), 1)
    for j in range(GDN_QK_HEADS):
        qn = _l2norm(q[:, j * GDN_DK:(j + 1) * GDN_DK]) * (GDN_DK ** -0.5)
        kn = _l2norm(k[:, j * GDN_DK:(j + 1) * GDN_DK])
        qk = jnp.sum(qn * kn, axis=-1, keepdims=True)
        q_col = _row_to_col(qn, eye)
        k_col = _row_to_col(kn, eye)
        for h in range(j * (nh // GDN_QK_HEADS), (j + 1) * (nh // GDN_QK_HEADS)):
            b_h = beta[:, h:h + 1]
            eg = eg_all[:, nh + h:nh + h + 1]
            s_prev = s0_ref[0, h]
            ks = jnp.sum(s_prev * k_col, axis=0, keepdims=True)
            qs = jnp.sum(s_prev * q_col, axis=0, keepdims=True)
            v_h = v[:, h * GDN_DV:(h + 1) * GDN_DV]
            u = v_h * b_h - (b_h * eg) * ks
            o = eg * qs + qk * u
            s_ref[0, h] = s_prev * eg + k_col * u
            z_h = z_ref[0, :, h * GDN_DV:(h + 1) * GDN_DV]
            o_ref[0, :, h * GDN_DV:(h + 1) * GDN_DV] = _rms(o) * nw_ref[...] * jax.nn.silu(z_h)


def gdn_step(main, tail, conv0, s0, conv_w, dt_bias, a_log, norm_w):
    bsz = main.shape[0]
    dtb, alog = _gdn_gate_params(dt_bias, a_log)
    small = [conv_w, dtb, alog, norm_w.reshape(1, -1)]
    sspec = pl.BlockSpec((1, GDN_V_HEADS, GDN_DK, GDN_DV), lambda b: (b, 0, 0, 0))
    cspec = pl.BlockSpec((1, GDN_CONV - 1, GDN_QKV), lambda b: (b, 0, 0))
    return pl.pallas_call(
        _gdn_step_kernel,
        grid=(bsz,),
        in_specs=[
            pl.BlockSpec((1, 1, GDN_QK_W), lambda b: (b, 0, 0)),
            pl.BlockSpec((1, 1, GDN_QK_W), lambda b: (b, 0, 1)),
            pl.BlockSpec((1, 1, GDN_VW), lambda b: (b, 0, 1)),
            pl.BlockSpec((1, 1, GDN_VW), lambda b: (b, 0, 2)),
            pl.BlockSpec((1, 1, LANES), lambda b: (b, 0, 0)),
            cspec,
        ] + [_small(a) for a in small] + [sspec],
        out_specs=[pl.BlockSpec((1, 1, GDN_VW), lambda b: (b, 0, 0)), cspec, sspec],
        out_shape=[jax.ShapeDtypeStruct((bsz, 1, GDN_VW), F32),
                   jax.ShapeDtypeStruct(conv0.shape, F32),
                   jax.ShapeDtypeStruct(s0.shape, F32)],
        compiler_params=_cparams("parallel"),
        name="gdn_step",
    )(main, main, main, main, tail, conv0, *small, s0)


def _pad_tail(w):
    return jnp.pad(w, ((0, 0), (0, LANES - w.shape[1])))


def _prep_hyb_in(w):
    a = A_WIDTH
    q, k, v = w[:, 0:a], w[:, a:2 * a], w[:, 2 * a:3 * a]
    z = w[:, 3 * a:3 * a + SSM_D_INNER]
    x0 = 3 * a + SSM_D_INNER
    xs = w[:, x0:x0 + SSM_D_INNER]
    bc = w[:, x0 + SSM_D_INNER:x0 + SSM_XBC]
    dt = w[:, x0 + SSM_XBC:]
    return jnp.concatenate([z, xs, q, k, v, bc], axis=1).astype(BF16), _pad_tail(dt).astype(BF16)


def _prep_gdn_in(w):
    return w[:, :GDN_MAIN].astype(BF16), _pad_tail(w[:, GDN_MAIN:]).astype(BF16)


HYB_CONV_COLS = ((SSM_D_INNER, 2 * SSM_D_INNER), (HYB_BC0, HYB_MAIN))


def _hyb_cols(a):
    out = jnp.zeros(a.shape[:-1] + (HYB_MAIN,), F32)
    (x0, x1), (b0, b1) = HYB_CONV_COLS
    return out.at[..., x0:x1].set(a[..., :SSM_D_INNER]).at[..., b0:b1].set(a[..., SSM_D_INNER:])


def _window_to_lanes(c):
    n, b, past, h, dh = c.shape
    return jnp.transpose(c, (0, 1, 3, 4, 2)).reshape(n, b, h * dh, past)


def _window_from_lanes(c):
    n, b, _, past = c.shape
    return jnp.transpose(c.reshape(n, b, A_HEADS, A_HEAD_DIM, past), (0, 1, 4, 2, 3))


def _row_tile(m, cap):
    return m if m <= cap else cap


def kernel(x_prompt, x_sample, cache_attn_k, cache_attn_v, state_ssm_conv, state_ssm, state_gdn_conv, state_gdn, rel_bias, norm_mix_pre, norm_mix_post, norm_ffn_pre, norm_ffn_post, w_hyb_in, ssm_conv_w, ssm_conv_b, ssm_dt_bias, ssm_a_log, ssm_d, ssm_norm_w, w_hyb_out, w_gdn_in, gdn_conv_w, gdn_dt_bias, gdn_a_log, gdn_norm_w, w_gdn_out, w_ffn_gate, w_ffn_up, w_ffn_down):
    depth = norm_mix_pre.shape[0]
    d_model = x_prompt.shape[-1]
    n_hyb, n_gdn = w_hyb_in.shape[0], w_gdn_in.shape[0]

    hyb_in = [_prep_hyb_in(w_hyb_in[i]) for i in range(n_hyb)]
    hyb_out = [(w_hyb_out[i, :A_WIDTH].astype(BF16), w_hyb_out[i, A_WIDTH:].astype(BF16)) for i in range(n_hyb)]
    gdn_in = [_prep_gdn_in(w_gdn_in[i]) for i in range(n_gdn)]
    gdn_out = [w_gdn_out[i].astype(BF16) for i in range(n_gdn)]
    ffn_w = [(w_ffn_gate[l].astype(BF16), w_ffn_up[l].astype(BF16), w_ffn_down[l].astype(BF16))
             for l in range(depth)]
    bias_tiles = _attn_bias_rows(rel_bias)

    def trunk(x3, k_pre, v_pre, sconv, sssm, gconv, gstate):
        bsz, seq, _ = x3.shape
        m = bsz * seq
        step = seq == 1
        tm_big = _row_tile(m, 1024)
        tm = _row_tile(m, 512)
        x = x3.reshape(m, d_model)
        nk, nv, nsc, nss, ngc, ngs = [], [], [], [], [], []
        rolled = None
        for l in range(depth):
            i = l // 2
            if l % 2 == 0:
                w_main, w_tail = hyb_in[i]
                ssm_args = (ssm_dt_bias[i], ssm_a_log[i], ssm_d[i], ssm_norm_w[i])
                if step:
                    main, tail = inproj(x, norm_mix_pre[l], w_main, w_tail, tm=tm_big, tn=512)
                    main3 = main.reshape(bsz, seq, HYB_MAIN)
                    tail3 = tail.reshape(bsz, seq, LANES)
                    col = lambda c0: main[:, c0:c0 + A_WIDTH].reshape(bsz, A_WIDTH, 1)
                    o_attn, *rolled = attention_step(
                        col(HYB_Q0), col(HYB_K0), col(HYB_V0), k_pre, v_pre, rolled,
                        _attn_logw(rel_bias, k_pre.shape[-1]), layer=i)
                    y, c_new, s_new = ssd_step(main3, tail3, sconv[i], sssm[i], ssm_conv_w[i], ssm_conv_b[i],
                                               *ssm_args)
                else:
                    main, tail, hist = inproj_conv(
                        x, norm_mix_pre[l], w_main, w_tail, _hyb_cols(ssm_conv_w[i]),
                        _hyb_cols(ssm_conv_b[i][None]), _hyb_cols(sconv[i]), seq=seq, conv_cols=HYB_CONV_COLS,
                        tm=tm_big, tn=512)
                    main3 = main.reshape(bsz, seq, HYB_MAIN)
                    tail3 = tail.reshape(bsz, seq, LANES)
                    c_new = jnp.concatenate([hist[..., a:b] for a, b in HYB_CONV_COLS], axis=-1)
                    o_attn = attention_prompt(main3, bias_tiles, bsz=bsz, seq=seq)
                    keep = min(A_PATTERNS[-1][0], seq)
                    k_new = main3[:, seq - keep:, HYB_K0:HYB_K0 + A_WIDTH].astype(F32)
                    v_new = main3[:, seq - keep:, HYB_V0:HYB_V0 + A_WIDTH].astype(F32)
                    y, s_new = ssd_prompt(main3, tail3, sssm[i], *ssm_args, bsz=bsz, seq=seq)
                    nk.append(k_new.reshape(bsz, -1, A_HEADS, A_HEAD_DIM))
                    nv.append(v_new.reshape(bsz, -1, A_HEADS, A_HEAD_DIM))
                nsc.append(c_new)
                nss.append(s_new)
                x = outproj([o_attn.reshape(m, A_WIDTH), y.reshape(m, SSM_D_INNER)], list(hyb_out[i]),
                            x, norm_mix_post[l], tm=tm)
            else:
                w_main, w_tail = gdn_in[i]
                gdn_args = (gdn_dt_bias[i], gdn_a_log[i], gdn_norm_w[i])
                if step:
                    main, tail = inproj(x, norm_mix_pre[l], w_main, w_tail, tm=tm_big, tn=512)
                    o, c_new, s_new = gdn_step(main.reshape(bsz, seq, GDN_MAIN), tail.reshape(bsz, seq, LANES),
                                               gconv[i], gstate[i], gdn_conv_w[i], *gdn_args)
                else:
                    pad = lambda a: jnp.pad(a, [(0, 0)] * (a.ndim - 1) + [(0, GDN_MAIN - GDN_QKV)])
                    main, tail, hist = inproj_conv(
                        x, norm_mix_pre[l], w_main, w_tail, pad(gdn_conv_w[i]), jnp.zeros((1, GDN_MAIN), F32),
                        pad(gconv[i]), seq=seq, conv_cols=((0, GDN_QKV),), tm=tm_big, tn=512)
                    c_new = hist[..., :GDN_QKV]
                    o, s_new = gdn_prompt(main.reshape(bsz, seq, GDN_MAIN), tail.reshape(bsz, seq, LANES),
                                          gstate[i], *gdn_args, bsz=bsz, seq=seq)
                ngc.append(c_new)
                ngs.append(s_new)
                x = outproj([o.reshape(m, GDN_VW)], [gdn_out[i]], x, norm_mix_post[l], tm=tm)
            wg, wu, wd = ffn_w[l]
            x = ffn(x, norm_ffn_pre[l], wg, wu, wd, norm_ffn_post[l], tm=tm, tf=wg.shape[1] // 2)
        k_out, v_out = [_window_from_lanes(r) for r in rolled] if step else (jnp.stack(nk), jnp.stack(nv))
        return (x.reshape(bsz, seq, d_model), k_out, v_out, jnp.stack(nsc), jnp.stack(nss),
                jnp.stack(ngc), jnp.stack(ngs))

    bsz = x_prompt.shape[0]
    dt_p = x_prompt.dtype
    p_sc0 = jnp.zeros((n_hyb, bsz, SSM_CONV - 1, SSM_XBC), dt_p)
    p_ss0 = jnp.zeros((n_hyb, bsz, SSM_HEADS, SSM_HEAD_DIM, SSM_STATE), F32)
    p_gc0 = jnp.zeros((n_gdn, bsz, GDN_CONV - 1, GDN_QKV), dt_p)
    p_gs0 = jnp.zeros((n_gdn, bsz, GDN_V_HEADS, GDN_DK, GDN_DV), F32)
    y_prompt, pk, pv, psc, pss, pgc, pgs = trunk(x_prompt, None, None, p_sc0, p_ss0, p_gc0, p_gs0)
    y_sample, sk, sv, ssc, sss, sgc, sgs = trunk(
        x_sample, _window_to_lanes(cache_attn_k), _window_to_lanes(cache_attn_v), state_ssm_conv, state_ssm,
        state_gdn_conv, state_gdn)
    return (y_prompt, y_sample, pk, pv, psc, pss, pgc, pgs, sk, sv, ssc, sss, sgc, sgs)
```

```python
import functools
import math

import numpy as np
import jax
import jax.numpy as jnp
from jax import lax
from jax.experimental import pallas as pl
from jax.experimental.pallas import tpu as pltpu

F32 = jnp.float32
BF16 = jnp.bfloat16
EPS = 1e-6
NEG = -1e30
HIGHEST = lax.Precision.HIGHEST

VMEM_LIMIT_BYTES = 56 * 1024 * 1024
LANES = 128

A_HEADS = 8
A_HEAD_DIM = 64
A_WIDTH = A_HEADS * A_HEAD_DIM
A_PATTERNS = ((128, 1), (512, 4), (2048, 16))
A_BAND = 128
ATTN_GROUP = 4
REL_BUCKETS = 32
REL_MAX_DIST = 2048

SSM_D_INNER = 1024
SSM_HEAD_DIM = 64
SSM_HEADS = SSM_D_INNER // SSM_HEAD_DIM
SSM_GROUPS = 2
SSM_STATE = 128
SSM_CONV = 4
SSM_CHUNK = 128
SSM_BC = 2 * SSM_GROUPS * SSM_STATE
SSM_XBC = SSM_D_INNER + SSM_BC

GDN_QK_HEADS = 8
GDN_V_HEADS = 16
GDN_DK = 128
GDN_DV = 128
GDN_CONV = 4
GDN_CHUNK = 64
GDN_HEAD_GROUP = 16
GDN_QK_W = GDN_QK_HEADS * GDN_DK
GDN_VW = GDN_V_HEADS * GDN_DV
GDN_QKV = 2 * GDN_QK_W + GDN_VW

HYB_MAIN = 2 * SSM_D_INNER + 3 * A_WIDTH + SSM_BC
HYB_Q0 = 2 * SSM_D_INNER
HYB_K0 = HYB_Q0 + A_WIDTH
HYB_V0 = HYB_K0 + A_WIDTH
HYB_BC0 = HYB_V0 + A_WIDTH
GDN_MAIN = GDN_QKV + GDN_VW


def _cparams(*sem):
    return pltpu.CompilerParams(dimension_semantics=sem, vmem_limit_bytes=VMEM_LIMIT_BYTES)


def _rms(x):
    return x * lax.rsqrt(jnp.mean(x * x, axis=-1, keepdims=True) + EPS)


def _dot(a, b, **kw):
    return jnp.dot(a, b, preferred_element_type=F32, **kw)


def _dot_nt(a, b, **kw):
    return lax.dot_general(a, b, (((1,), (1,)), ((), ())), preferred_element_type=F32, **kw)


def _dot_tn(a, b, **kw):
    return lax.dot_general(a, b, (((0,), (0,)), ((), ())), preferred_element_type=F32, **kw)


def _iota(shape, dim):
    return lax.broadcasted_iota(jnp.int32, shape, dim)


def _inproj_kernel(x_ref, g_ref, w_ref, wt_ref, o_ref, t_ref, h_ref):
    @pl.when(pl.program_id(1) == 0)
    def _():
        hb = (_rms(x_ref[...]) * g_ref[...]).astype(BF16)
        h_ref[...] = hb
        t_ref[...] = _dot(hb, wt_ref[...])

    o_ref[...] = _dot(h_ref[...], w_ref[...])


def inproj(x, g, w_main, w_tail, *, tm, tn):
    m, d = x.shape
    n = w_main.shape[1]
    return pl.pallas_call(
        _inproj_kernel,
        grid=(m // tm, n // tn),
        in_specs=[
            pl.BlockSpec((tm, d), lambda i, j: (i, 0)),
            pl.BlockSpec((1, d), lambda i, j: (0, 0)),
            pl.BlockSpec((d, tn), lambda i, j: (0, j)),
            pl.BlockSpec((d, LANES), lambda i, j: (0, 0)),
        ],
        out_specs=[
            pl.BlockSpec((tm, tn), lambda i, j: (i, j)),
            pl.BlockSpec((tm, LANES), lambda i, j: (i, 0)),
        ],
        out_shape=[jax.ShapeDtypeStruct((m, n), F32), jax.ShapeDtypeStruct((m, LANES), F32)],
        scratch_shapes=[pltpu.VMEM((tm, d), BF16)],
        compiler_params=_cparams("parallel", "arbitrary"),
        name="inproj",
    )(x, g.reshape(1, d), w_main, w_tail)


CONV_TAPS = 4
CONV_ROWS = 64
CONV_BASE = 8


def _inproj_conv_kernel(x_ref, g_ref, w_ref, wt_ref, cw_ref, cb_ref, c0_ref, o_ref, t_ref, so_ref,
                        h_ref, xp0_ref, xp1_ref, carry_ref, *, n_col, conv_tiles, tiles_per_seq):
    s = pl.program_id(0)
    n_tiles = pl.num_programs(0) - 1
    tm = o_ref.shape[0]
    lo = CONV_BASE - (CONV_TAPS - 1)
    cur = jnp.minimum(s, n_tiles - 1)
    prv = jnp.maximum(s - 1, 0)
    pi, pj = prv // n_col, prv % n_col
    prv_conv = functools.reduce(jnp.logical_or, [jnp.logical_and(pj >= a, pj < b) for a, b in conv_tiles])
    first = pi % tiles_per_seq == 0
    both = lambda a, b: jnp.logical_and(a, b)

    @pl.when(both(cur % n_col == 0, s < n_tiles))
    def _():
        hb = (_rms(x_ref[...]) * g_ref[...]).astype(BF16)
        h_ref[...] = hb
        t_ref[...] = _dot(hb, wt_ref[...])

    for parity, (cur_ref, prv_ref) in enumerate(((xp0_ref, xp1_ref), (xp1_ref, xp0_ref))):
        here = s % 2 == parity

        def project(cur_ref=cur_ref):
            cur_ref[CONV_BASE:CONV_BASE + tm, :] = _dot(h_ref[...], w_ref[...])

        def raw_tail(prv_ref=prv_ref):
            tail = prv_ref[CONV_BASE + tm - (CONV_TAPS - 1):CONV_BASE + tm, :]
            so_ref[0] = tail
            return tail

        if parity == 0:
            @pl.when(s == 0)
            def _():
                project()

        @pl.when(both(here, both(s > 0, jnp.logical_not(prv_conv))))
        def _():
            project()
            raw_tail()
            o_ref[...] = prv_ref[CONV_BASE:CONV_BASE + tm, :].astype(o_ref.dtype)

        @pl.when(both(here, both(both(s > 0, prv_conv), first)))
        def _():
            prv_ref[lo:CONV_BASE, :] = c0_ref[0]

        @pl.when(both(here, both(both(s > 0, prv_conv), jnp.logical_not(first))))
        def _():
            prv_ref[lo:CONV_BASE, :] = carry_ref[pj]

        @pl.when(both(here, both(s > 0, prv_conv)))
        def _():
            project()
            for r0 in range(0, tm, CONV_ROWS):
                ext = prv_ref[r0:r0 + CONV_BASE + CONV_ROWS, :]
                conv = cb_ref[...] + cw_ref[CONV_TAPS - 1:CONV_TAPS, :] * ext[CONV_BASE:]
                for back in range(1, CONV_TAPS):
                    tap = CONV_TAPS - 1 - back
                    conv = conv + cw_ref[tap:tap + 1, :] * pltpu.roll(ext, back, 0)[CONV_BASE:]
                o_ref[r0:r0 + CONV_ROWS, :] = jax.nn.silu(conv).astype(o_ref.dtype)
            carry_ref[pj] = raw_tail()


def inproj_conv(x, g, w_main, w_tail, conv_w, conv_b, conv0, *, seq, conv_cols, tm, tn):
    m, d = x.shape
    n = w_main.shape[1]
    assert seq % tm == 0 and all(a % tn == 0 and b % tn == 0 for a, b in conv_cols)
    tiles_per_seq = seq // tm
    n_row, n_col = m // tm, n // tn
    n_tiles = n_row * n_col
    conv_tiles = tuple((a // tn, b // tn) for a, b in conv_cols)
    hist = CONV_TAPS - 1
    cur = lambda s: jnp.minimum(s, n_tiles - 1)
    prv = lambda s: jnp.maximum(s - 1, 0)
    main, tail, hist_rows = pl.pallas_call(
        functools.partial(_inproj_conv_kernel, n_col=n_col, conv_tiles=conv_tiles, tiles_per_seq=tiles_per_seq),
        grid=(n_tiles + 1,),
        in_specs=[
            pl.BlockSpec((tm, d), lambda s: (cur(s) // n_col, 0)),
            pl.BlockSpec((1, d), lambda s: (0, 0)),
            pl.BlockSpec((d, tn), lambda s: (0, cur(s) % n_col)),
            pl.BlockSpec((d, LANES), lambda s: (0, 0)),
            pl.BlockSpec((CONV_TAPS, tn), lambda s: (0, prv(s) % n_col)),
            pl.BlockSpec((1, tn), lambda s: (0, prv(s) % n_col)),
            pl.BlockSpec((1, hist, tn), lambda s: (prv(s) // n_col // tiles_per_seq, 0, prv(s) % n_col)),
        ],
        out_specs=[
            pl.BlockSpec((tm, tn), lambda s: (prv(s) // n_col, prv(s) % n_col)),
            pl.BlockSpec((tm, LANES), lambda s: (cur(s) // n_col, 0)),
            pl.BlockSpec((1, hist, tn), lambda s: (prv(s) // n_col, 0, prv(s) % n_col)),
        ],
        out_shape=[jax.ShapeDtypeStruct((m, n), BF16), jax.ShapeDtypeStruct((m, LANES), F32),
                   jax.ShapeDtypeStruct((n_row, hist, n), F32)],
        scratch_shapes=[pltpu.VMEM((tm, d), BF16), pltpu.VMEM((CONV_BASE + tm, tn), F32),
                        pltpu.VMEM((CONV_BASE + tm, tn), F32), pltpu.VMEM((n_col, hist, tn), F32)],
        compiler_params=_cparams("arbitrary"),
        name="inproj_conv",
    )(x, g.reshape(1, d), w_main, w_tail, conv_w, conv_b, conv0)
    return main, tail, hist_rows[tiles_per_seq - 1::tiles_per_seq]


def _outproj_kernel(*refs, n_in):
    a_refs, w_refs = refs[:n_in], refs[n_in:2 * n_in]
    x_ref, g_ref, o_ref = refs[2 * n_in:]
    acc = None
    for a_ref, w_ref in zip(a_refs, w_refs):
        t = _dot(a_ref[...].astype(BF16), w_ref[...])
        acc = t if acc is None else acc + t
    o_ref[...] = x_ref[...] + _rms(acc) * g_ref[...]


def outproj(acts, weights, x, g, *, tm):
    m, d = x.shape
    n_in = len(acts)
    in_specs = [pl.BlockSpec((tm, a.shape[1]), lambda i: (i, 0)) for a in acts]
    in_specs += [pl.BlockSpec(w.shape, lambda i: (0, 0)) for w in weights]
    in_specs += [pl.BlockSpec((tm, d), lambda i: (i, 0)), pl.BlockSpec((1, d), lambda i: (0, 0))]
    return pl.pallas_call(
        functools.partial(_outproj_kernel, n_in=n_in),
        grid=(m // tm,),
        in_specs=in_specs,
        out_specs=pl.BlockSpec((tm, d), lambda i: (i, 0)),
        out_shape=jax.ShapeDtypeStruct((m, d), F32),
        compiler_params=_cparams("parallel"),
        name="outproj",
    )(*acts, *weights, x, g.reshape(1, d))


def _ffn_kernel(x_ref, g1_ref, wg_ref, wu_ref, wd_ref, g2_ref, o_ref, h_ref, acc_ref):
    j = pl.program_id(1)

    @pl.when(j == 0)
    def _():
        h_ref[...] = (_rms(x_ref[...]) * g1_ref[...]).astype(BF16)
        acc_ref[...] = jnp.zeros_like(acc_ref)

    h = h_ref[...]
    a = jax.nn.silu(_dot(h, wg_ref[...])) * _dot(h, wu_ref[...])
    acc_ref[...] += _dot(a.astype(BF16), wd_ref[...])

    @pl.when(j == pl.num_programs(1) - 1)
    def _():
        o_ref[...] = x_ref[...] + _rms(acc_ref[...]) * g2_ref[...]


def ffn(x, g1, wg, wu, wd, g2, *, tm, tf):
    m, d = x.shape
    dff = wg.shape[1]
    return pl.pallas_call(
        _ffn_kernel,
        grid=(m // tm, dff // tf),
        in_specs=[
            pl.BlockSpec((tm, d), lambda i, j: (i, 0)),
            pl.BlockSpec((1, d), lambda i, j: (0, 0)),
            pl.BlockSpec((d, tf), lambda i, j: (0, j)),
            pl.BlockSpec((d, tf), lambda i, j: (0, j)),
            pl.BlockSpec((tf, d), lambda i, j: (j, 0)),
            pl.BlockSpec((1, d), lambda i, j: (0, 0)),
        ],
        out_specs=pl.BlockSpec((tm, d), lambda i, j: (i, 0)),
        out_shape=jax.ShapeDtypeStruct((m, d), F32),
        scratch_shapes=[pltpu.VMEM((tm, d), BF16), pltpu.VMEM((tm, d), F32)],
        compiler_params=_cparams("parallel", "arbitrary"),
        name="ffn",
    )(x, g1.reshape(1, d), wg, wu, wd, g2.reshape(1, d))


def _rel_buckets(dist):
    max_exact = REL_BUCKETS // 2
    n = np.maximum(dist, 1).astype(np.float32)
    large = max_exact + (np.log(n / max_exact) / math.log(REL_MAX_DIST / max_exact)
                         * (REL_BUCKETS - max_exact)).astype(np.int32)
    large = np.minimum(large, REL_BUCKETS - 1)
    return np.where(dist < max_exact, dist, large).astype(np.int32)


def _attn_bias_rows(rel_bias):
    u = np.arange(2 * A_BAND)
    valid = u <= A_BAND
    rows = []
    for (_, d) in A_PATTERNS:
        b = rel_bias[_rel_buckets(np.where(valid, A_BAND - u, 0) * d)]
        rows.append(jnp.where(valid[:, None], b.astype(F32), NEG))
    tl = jnp.transpose(jnp.stack(rows), (2, 0, 1))
    tl = tl.reshape(A_HEADS // 2, 2, len(A_PATTERNS), 2 * A_BAND)
    tl = jnp.transpose(tl, (0, 2, 1, 3))[:, :, :, None, :]
    return jnp.broadcast_to(tl, tl.shape[:3] + (8, 2 * A_BAND))


def _attn_kernel(qin_ref, kin_ref, vin_ref, brow_ref, o_ref, m0_ref, m1_ref, l0_ref, l1_ref, acc_ref,
                 q_ref, k_ref, v_ref, b_ref, *, seq):
    n_tiles = seq // A_BAND
    lane = _iota((A_BAND, LANES), 1)
    head0 = lane < A_HEAD_DIM
    m_refs, l_refs = (m0_ref, m1_ref), (l0_ref, l1_ref)
    q_ref[...] = qin_ref[...].astype(F32) * (A_HEAD_DIM ** -0.5)
    k_ref[...] = kin_ref[...].astype(F32)
    v_ref[...] = vin_ref[...].astype(F32)
    for p in range(len(A_PATTERNS)):
        for h in range(2):
            row = jnp.broadcast_to(brow_ref[0, p, h, 0:1, :], (A_BAND, 2 * A_BAND))
            b_ref[0, p, h] = pltpu.roll(row, 0, 1, stride=1, stride_axis=0)
    for h in range(2):
        m_refs[h][...] = jnp.full(m_refs[h].shape, NEG, F32)
        l_refs[h][...] = jnp.zeros_like(l_refs[h])
    acc_ref[...] = jnp.zeros_like(acc_ref)

    for p, (_, d) in enumerate(A_PATTERNS):
        tiles_per_class = n_tiles // d

        def load_tile(idx, d=d, tiles_per_class=tiles_per_class):
            r = idx // tiles_per_class
            t = idx % tiles_per_class
            start = r + t * (d * A_BAND)
            has_prev = t > 0
            prev = jnp.where(has_prev, start - d * A_BAND, start)
            if d > 1:
                rows, prows = pl.ds(start, A_BAND, stride=d), pl.ds(prev, A_BAND, stride=d)
            else:
                rows, prows = pl.ds(pl.multiple_of(start, A_BAND), A_BAND), pl.ds(pl.multiple_of(prev, A_BAND), A_BAND)
            return dict(
                rows=rows, has_prev=has_prev, q=q_ref[rows, :],
                k2=jnp.concatenate([k_ref[prows, :], k_ref[rows, :]], axis=0).astype(BF16),
                v2=jnp.concatenate([v_ref[prows, :], v_ref[rows, :]], axis=0).astype(BF16),
                acc=acc_ref[rows, :], m=[m_refs[h][rows, :] for h in range(2)],
                l=[l_refs[h][rows, :] for h in range(2)])

        def tile_group(idx, carry, load_tile=load_tile, p=p):
            tiles = [load_tile(idx + i * (n_tiles // ATTN_GROUP)) for i in range(ATTN_GROUP)]
            chains = [(tile, h) for tile in tiles for h in range(2)]
            col = _iota((A_BAND, 2 * A_BAND), 1)
            qh = [jnp.where(head0 if h == 0 else jnp.logical_not(head0), tile["q"], 0.0).astype(BF16)
                  for tile, h in chains]
            s = [_dot_nt(qh[c], tile["k2"])
                 + jnp.where(jnp.logical_and(col < A_BAND, jnp.logical_not(tile["has_prev"])), NEG, b_ref[0, p, h])
                 for c, (tile, h) in enumerate(chains)]
            m_new = [jnp.maximum(tile["m"][h], jnp.max(s[c], axis=-1, keepdims=True))
                     for c, (tile, h) in enumerate(chains)]
            alpha = [jnp.exp(tile["m"][h] - m_new[c]) for c, (tile, h) in enumerate(chains)]
            pr = [jnp.exp(s[c] - jnp.concatenate([m_new[c], m_new[c]], axis=1)) for c in range(len(chains))]
            l_new = [alpha[c] * tile["l"][h] + jnp.sum(pr[c], axis=-1, keepdims=True)
                     for c, (tile, h) in enumerate(chains)]
            acc_new = [alpha[c] * tile["acc"] + _dot(pr[c].astype(BF16), tile["v2"])
                       for c, (tile, h) in enumerate(chains)]
            for c, (tile, h) in enumerate(chains):
                m_refs[h][tile["rows"], :] = m_new[c]
                l_refs[h][tile["rows"], :] = l_new[c]
                if h == 1:
                    acc_ref[tile["rows"], :] = jnp.where(head0, acc_new[c - 1], acc_new[c])
            return carry

        lax.fori_loop(0, n_tiles // ATTN_GROUP, tile_group, 0)

    lane_s = _iota((seq, LANES), 1)
    o_ref[...] = (acc_ref[...] / jnp.where(lane_s < A_HEAD_DIM, l0_ref[...], l1_ref[...])).astype(o_ref.dtype)


def attention_prompt(proj, bias_tiles, *, bsz, seq):
    hp = A_HEADS // 2
    qb, kb, vb = HYB_Q0 // LANES, HYB_K0 // LANES, HYB_V0 // LANES
    return pl.pallas_call(
        functools.partial(_attn_kernel, seq=seq),
        grid=(bsz, hp),
        in_specs=[
            pl.BlockSpec((None, seq, LANES), lambda b, h: (b, 0, qb + h)),
            pl.BlockSpec((None, seq, LANES), lambda b, h: (b, 0, kb + h)),
            pl.BlockSpec((None, seq, LANES), lambda b, h: (b, 0, vb + h)),
            pl.BlockSpec((1,) + bias_tiles.shape[1:], lambda b, h: (h, 0, 0, 0, 0)),
        ],
        out_specs=pl.BlockSpec((None, seq, LANES), lambda b, h: (b, 0, h)),
        out_shape=jax.ShapeDtypeStruct((bsz, seq, A_WIDTH), BF16),
        scratch_shapes=[pltpu.VMEM((seq, LANES), F32)] * 8
        + [pltpu.VMEM((1, len(A_PATTERNS), 2, A_BAND, 2 * A_BAND), F32)],
        compiler_params=_cparams("parallel", "parallel"),
        name="attn_prompt",
    )(proj, proj, proj, bias_tiles)


def _attn_logw(rel_bias, past):
    dist = np.arange(past + 1)
    count = np.zeros(past + 1, np.float64)
    for (w, d) in A_PATTERNS:
        count += ((dist % d == 0) & (dist <= w)).astype(np.float64)
    logc = np.where(count > 0, np.log(np.maximum(count, 1.0)), 0.0).astype(np.float32)
    lw = rel_bias[_rel_buckets(dist)].astype(F32).T + logc[None, :]
    return jnp.where((count > 0)[None, :], lw, NEG)


def _step_scores(xk_ref, q_col, kn_col, lw_ref, lw0_ref, s_ref):
    qs = q_col * (A_HEAD_DIM ** -0.5)
    s_new = []
    for h in range(A_HEADS):
        rows = slice(h * A_HEAD_DIM, (h + 1) * A_HEAD_DIM)
        s_ref[h:h + 1, :] = jnp.sum(xk_ref[0, 0, rows, :] * qs[rows], axis=0, keepdims=True)
        s_new.append(jnp.sum(kn_col[rows] * qs[rows], axis=0, keepdims=True))
    s = s_ref[...] + lw_ref[...]
    s_new = jnp.concatenate(s_new, axis=0) + lw0_ref[:, :1]
    m = jnp.maximum(jnp.max(s, axis=-1, keepdims=True), s_new)
    p = jnp.exp(s - m)
    p_new = jnp.exp(s_new - m)
    den = jnp.sum(p, axis=-1, keepdims=True) + p_new
    return p, p_new, den


def _step_output(xv_ref, vn_col, p, p_new, den, o_ref):
    for h in range(A_HEADS):
        rows = slice(h * A_HEAD_DIM, (h + 1) * A_HEAD_DIM)
        pv = jnp.sum(xv_ref[0, 0, rows, :] * p[h:h + 1, :], axis=-1, keepdims=True)
        o_ref[0, rows, :] = (pv + p_new[h:h + 1, :] * vn_col[rows]) / den[h:h + 1, :]


def _attn_step_roll_kernel(q_ref, kn_ref, vn_ref, lw_ref, lw0_ref, xk_ref, xv_ref, o_ref, ko_ref, vo_ref, s_ref,
                           *, layer, past):
    is_layer = pl.program_id(0) == layer
    newest = _iota((A_HEAD_DIM, past), 1) == past - 1
    for x_ref, n_ref, out_ref in ((xk_ref, kn_ref, ko_ref), (xv_ref, vn_ref, vo_ref)):
        for h in range(A_HEADS):
            rows = slice(h * A_HEAD_DIM, (h + 1) * A_HEAD_DIM)
            rolled = pltpu.roll(x_ref[0, 0, rows, :], past - 1, 1)
            out_ref[0, 0, rows, :] = jnp.where(jnp.logical_and(newest, is_layer), n_ref[0, rows, :], rolled)

    @pl.when(is_layer)
    def _():
        p, p_new, den = _step_scores(xk_ref, q_ref[0], kn_ref[0], lw_ref, lw0_ref, s_ref)
        _step_output(xv_ref, vn_ref[0], p, p_new, den, o_ref.at[0])

    @pl.when(jnp.logical_not(is_layer))
    def _():
        o_ref[...] = jnp.zeros_like(o_ref)


def _attn_step_append_kernel(q_ref, kn_ref, vn_ref, lw_ref, lw0_ref, xk_ref, xv_ref, ko_in, vo_in,
                             o_ref, ko_ref, vo_ref, s_ref, *, past):
    del ko_in, vo_in
    p, p_new, den = _step_scores(xk_ref, q_ref[0], kn_ref[0], lw_ref, lw0_ref, s_ref)
    _step_output(xv_ref, vn_ref[0], p, p_new, den, o_ref)
    newest = _iota((A_WIDTH, LANES), 1) == LANES - 1
    for x_ref, n_ref, out_ref in ((xk_ref, kn_ref, ko_ref), (xv_ref, vn_ref, vo_ref)):
        rolled = pltpu.roll(x_ref[0, 0, :, past - LANES:past], LANES - 1, 1)
        out_ref[0, 0] = jnp.where(newest, n_ref[0], rolled)


def attention_step(q_col, kn_col, vn_col, cache_k, cache_v, rolled, logw, *, layer):
    n_layers, bsz, w, past = cache_k.shape
    lw_cache = logw[:, past:0:-1]
    lw_new = jnp.broadcast_to(logw[:, :1], (A_HEADS, LANES))
    out_shape = [jax.ShapeDtypeStruct((bsz, w, 1), F32),
                 jax.ShapeDtypeStruct(cache_k.shape, cache_k.dtype),
                 jax.ShapeDtypeStruct(cache_v.shape, cache_v.dtype)]
    scratch = [pltpu.VMEM((A_HEADS, past), F32)]
    if rolled is None:
        assert layer == 0
        col = pl.BlockSpec((1, w, 1), lambda l, b: (b, 0, 0))
        win = pl.BlockSpec((1, 1, w, past), lambda l, b: (l, b, 0, 0))
        o_all, rolled_k, rolled_v = pl.pallas_call(
            functools.partial(_attn_step_roll_kernel, layer=layer, past=past),
            grid=(n_layers, bsz),
            in_specs=[col, col, col, _small(lw_cache), _small(lw_new), win, win],
            out_specs=[pl.BlockSpec((1, 1, w, 1), lambda l, b: (l, b, 0, 0)), win, win],
            out_shape=[jax.ShapeDtypeStruct((n_layers, bsz, w, 1), F32)] + out_shape[1:],
            scratch_shapes=scratch,
            compiler_params=_cparams("arbitrary", "arbitrary"),
            name="attn_step_roll",
        )(q_col, kn_col, vn_col, lw_cache, lw_new, cache_k, cache_v)
        return o_all[layer], rolled_k, rolled_v
    col = pl.BlockSpec((1, w, 1), lambda b: (b, 0, 0))
    win = pl.BlockSpec((1, 1, w, past), lambda b: (layer, b, 0, 0))
    tail = pl.BlockSpec((1, 1, w, LANES), lambda b: (layer, b, 0, past // LANES - 1))
    return pl.pallas_call(
        functools.partial(_attn_step_append_kernel, past=past),
        grid=(bsz,),
        in_specs=[col, col, col, _small(lw_cache), _small(lw_new), win, win, tail, tail],
        out_specs=[col, tail, tail],
        out_shape=out_shape,
        scratch_shapes=scratch,
        input_output_aliases={7: 1, 8: 2},
        compiler_params=_cparams("arbitrary"),
        name="attn_step_append",
    )(q_col, kn_col, vn_col, lw_cache, lw_new, cache_k, cache_v, *rolled)


def _group_rms(y, w):
    half = SSM_D_INNER // SSM_GROUPS
    return [_rms(y[:, g * half:(g + 1) * half]) * w[:, g * half:(g + 1) * half] for g in range(SSM_GROUPS)]


def _ssd_kernel(z_ref, xs_ref, bc_ref, dt_ref, dtb_ref, alog_ref, d_ref, nw_ref, h0_ref, y_ref, h_ref, ys_ref):
    c = pl.program_id(1)
    ch = SSM_CHUNK

    @pl.when(c == 0)
    def _():
        h_ref[...] = h0_ref[...]

    xs = xs_ref[0].astype(F32)
    bc = bc_ref[0].astype(F32)

    dt = jax.nn.softplus(dt_ref[0] + dtb_ref[...])
    da = dt * (-jnp.exp(alog_ref[...]))
    row = _iota((ch, ch), 0)
    colv = _iota((ch, ch), 1)
    tril = (row >= colv).astype(F32)
    cs = _dot(tril, da, precision=HIGHEST)
    cs_t = cs.T
    causal = row >= colv

    heads = range(SSM_HEADS)
    hpg = SSM_HEADS // SSM_GROUPS
    bm = [bc[:, g * SSM_STATE:(g + 1) * SSM_STATE] for g in range(SSM_GROUPS)]
    cm = [bc[:, (SSM_GROUPS + g) * SSM_STATE:(SSM_GROUPS + g + 1) * SSM_STATE] for g in range(SSM_GROUPS)]
    cb = [_dot_nt(cm[g], bm[g]) for g in range(SSM_GROUPS)]
    dt_t = dt.T
    xs_t = xs.T
    w_t = dt_t * jnp.exp(cs_t[:, ch - 1:ch] - cs_t)
    e_last = jnp.exp(cs[ch - 1:ch, :])
    lane_lo = _iota((ch, LANES), 1) < SSM_HEAD_DIM
    row_lo = _iota((LANES, ch), 0) < SSM_HEAD_DIM
    pairs = range(SSM_HEADS // 2)
    csb = [jnp.broadcast_to(cs[:, h:h + 1], (ch, ch)) for h in heads]
    mix = [cb[h // hpg] * jnp.exp(jnp.where(causal, csb[h] - cs_t[h:h + 1, :], NEG)) * dt_t[h:h + 1, :]
           for h in heads]
    x_pair = [xs[:, j * LANES:(j + 1) * LANES] for j in pairs]
    y_intra = [jnp.where(lane_lo, _dot(mix[2 * j], x_pair[j]), _dot(mix[2 * j + 1], x_pair[j])) for j in pairs]
    h_pair = [h_ref[0, 2 * j:2 * j + 2].reshape(2 * SSM_HEAD_DIM, SSM_STATE) for j in pairs]
    y_inter = [_dot_nt(cm[2 * j // hpg], h_pair[j]) * jnp.exp(jnp.where(lane_lo, csb[2 * j], csb[2 * j + 1]))
               for j in pairs]
    xw_t = [xs_t[j * LANES:(j + 1) * LANES, :] * jnp.where(row_lo, w_t[2 * j:2 * j + 1, :], w_t[2 * j + 1:2 * j + 2, :])
            for j in pairs]
    st = [_dot(xw_t[j], bm[2 * j // hpg]) for j in pairs]
    for j in pairs:
        cols = slice(j * LANES, (j + 1) * LANES)
        ys_ref[:, cols] = y_intra[j] + y_inter[j] + d_ref[:, cols] * x_pair[j]
        decay = jnp.where(row_lo, e_last[:, 2 * j:2 * j + 1], e_last[:, 2 * j + 1:2 * j + 2])
        h_ref[0, 2 * j:2 * j + 2] = (h_pair[j] * decay + st[j]).reshape(2, SSM_HEAD_DIM, SSM_STATE)

    y = ys_ref[...] * jax.nn.silu(z_ref[0].astype(F32))
    half = SSM_D_INNER // SSM_GROUPS
    for g, yg in enumerate(_group_rms(y, nw_ref[...])):
        y_ref[0, :, g * half:(g + 1) * half] = yg.astype(y_ref.dtype)


def _small(a):
    return pl.BlockSpec(a.shape, lambda *_: (0,) * a.ndim)


def ssd_prompt(main, tail, h0, dt_bias, a_log, d_skip, norm_w, *, bsz, seq):
    nc = seq // SSM_CHUNK
    ch = SSM_CHUNK
    small = [_pad_tail(dt_bias.reshape(1, -1)), _pad_tail(a_log.reshape(1, -1)),
             jnp.repeat(d_skip, SSM_HEAD_DIM).reshape(1, -1), norm_w.reshape(1, -1)]
    return pl.pallas_call(
        _ssd_kernel,
        grid=(bsz, nc),
        in_specs=[
            pl.BlockSpec((1, ch, SSM_D_INNER), lambda b, c: (b, c, 0)),
            pl.BlockSpec((1, ch, SSM_D_INNER), lambda b, c: (b, c, 1)),
            pl.BlockSpec((1, ch, SSM_BC), lambda b, c: (b, c, HYB_BC0 // SSM_BC)),
            pl.BlockSpec((1, ch, LANES), lambda b, c: (b, c, 0)),
        ] + [_small(a) for a in small] + [
            pl.BlockSpec((1, SSM_HEADS, SSM_HEAD_DIM, SSM_STATE), lambda b, c: (b, 0, 0, 0)),
        ],
        out_specs=[
            pl.BlockSpec((1, ch, SSM_D_INNER), lambda b, c: (b, c, 0)),
            pl.BlockSpec((1, SSM_HEADS, SSM_HEAD_DIM, SSM_STATE), lambda b, c: (b, 0, 0, 0)),
        ],
        out_shape=[jax.ShapeDtypeStruct((bsz, seq, SSM_D_INNER), BF16),
                   jax.ShapeDtypeStruct((bsz, SSM_HEADS, SSM_HEAD_DIM, SSM_STATE), F32)],
        scratch_shapes=[pltpu.VMEM((ch, SSM_D_INNER), F32)],
        compiler_params=_cparams("parallel", "arbitrary"),
        name="ssd_prompt",
    )(main, main, main, tail, *small, h0)


def _row_to_col(row, eye):
    return jnp.sum(jnp.where(eye, row, 0.0), axis=1, keepdims=True)


def _col_to_row(col, eye):
    return jnp.sum(jnp.where(eye, col, 0.0), axis=0, keepdims=True)


def _conv_step(c0_ref, w_ref, x_row, c0, c1):
    acc = w_ref[SSM_CONV - 1:SSM_CONV, c0:c1] * x_row
    for i in range(SSM_CONV - 1):
        acc = acc + w_ref[i:i + 1, c0:c1] * c0_ref[0, i:i + 1, c0:c1]
    return acc


def _ssd_step_kernel(z_ref, xs_ref, bc_ref, dt_ref, c0_ref, cw_ref, cb_ref, dtb_ref, alog_ref, d_ref, nw_ref,
                     h0_ref, y_ref, co_ref, h_ref, ys_ref):
    xs_raw = xs_ref[0]
    bc_raw = bc_ref[0]
    xs = jax.nn.silu(_conv_step(c0_ref, cw_ref, xs_raw, 0, SSM_D_INNER) + cb_ref[:, 0:SSM_D_INNER])
    bc = jax.nn.silu(_conv_step(c0_ref, cw_ref, bc_raw, SSM_D_INNER, SSM_XBC) + cb_ref[:, SSM_D_INNER:SSM_XBC])
    co_ref[0, 0:SSM_CONV - 2, :] = c0_ref[0, 1:SSM_CONV - 1, :]
    co_ref[0, SSM_CONV - 2:SSM_CONV - 1, 0:SSM_D_INNER] = xs_raw
    co_ref[0, SSM_CONV - 2:SSM_CONV - 1, SSM_D_INNER:SSM_XBC] = bc_raw

    dt = jax.nn.softplus(dt_ref[0] + dtb_ref[...])
    dec = jnp.exp(dt * (-jnp.exp(alog_ref[...])))
    dskip = d_ref[...]
    eye = _iota((LANES, LANES), 0) == _iota((LANES, LANES), 1)
    upper = _iota((LANES, 1), 0) >= SSM_HEAD_DIM
    for j in range(SSM_HEADS // 2):
        h0i, h1i = 2 * j, 2 * j + 1
        g = h0i // (SSM_HEADS // SSM_GROUPS)
        bm = bc[:, g * SSM_STATE:(g + 1) * SSM_STATE]
        cm = bc[:, (SSM_GROUPS + g) * SSM_STATE:(SSM_GROUPS + g + 1) * SSM_STATE]
        x_row = xs[:, j * LANES:(j + 1) * LANES]
        x_col = _row_to_col(x_row, eye)
        pick = lambda v: jnp.where(upper, v[:, h1i:h1i + 1], v[:, h0i:h0i + 1])
        hp = h0_ref[0, h0i:h1i + 1].reshape(2 * SSM_HEAD_DIM, SSM_STATE)
        hn = hp * pick(dec) + (x_col * pick(dt)) * bm
        h_ref[0, h0i:h1i + 1] = hn.reshape(2, SSM_HEAD_DIM, SSM_STATE)
        y_col = jnp.sum(hn * cm, axis=1, keepdims=True) + pick(dskip) * x_col
        ys_ref[:, j * LANES:(j + 1) * LANES] = _col_to_row(y_col, eye)

    y = ys_ref[...] * jax.nn.silu(z_ref[0])
    half = SSM_D_INNER // SSM_GROUPS
    for g, yg in enumerate(_group_rms(y, nw_ref[...])):
        y_ref[0, :, g * half:(g + 1) * half] = yg


def ssd_step(main, tail, conv0, h0, conv_w, conv_b, dt_bias, a_log, d_skip, norm_w):
    bsz = main.shape[0]
    small = [conv_w, conv_b.reshape(1, -1), _pad_tail(dt_bias.reshape(1, -1)), _pad_tail(a_log.reshape(1, -1)),
             _pad_tail(d_skip.reshape(1, -1)), norm_w.reshape(1, -1)]
    hspec = pl.BlockSpec((1, SSM_HEADS, SSM_HEAD_DIM, SSM_STATE), lambda b: (b, 0, 0, 0))
    cspec = pl.BlockSpec((1, SSM_CONV - 1, SSM_XBC), lambda b: (b, 0, 0))
    return pl.pallas_call(
        _ssd_step_kernel,
        grid=(bsz,),
        in_specs=[
            pl.BlockSpec((1, 1, SSM_D_INNER), lambda b: (b, 0, 0)),
            pl.BlockSpec((1, 1, SSM_D_INNER), lambda b: (b, 0, 1)),
            pl.BlockSpec((1, 1, SSM_BC), lambda b: (b, 0, HYB_BC0 // SSM_BC)),
            pl.BlockSpec((1, 1, LANES), lambda b: (b, 0, 0)),
            cspec,
        ] + [_small(a) for a in small] + [hspec],
        out_specs=[pl.BlockSpec((1, 1, SSM_D_INNER), lambda b: (b, 0, 0)), cspec, hspec],
        out_shape=[jax.ShapeDtypeStruct((bsz, 1, SSM_D_INNER), F32),
                   jax.ShapeDtypeStruct(conv0.shape, F32),
                   jax.ShapeDtypeStruct(h0.shape, F32)],
        scratch_shapes=[pltpu.VMEM((1, SSM_D_INNER), F32)],
        compiler_params=_cparams("parallel"),
        name="ssd_step",
    )(main, main, main, tail, conv0, *small, h0)


def _l2norm(x):
    return x * lax.rsqrt(jnp.sum(x * x, axis=-1, keepdims=True) + EPS)


def _unit_lower_inverse(ns, eye):
    size = ns[0].shape[0]
    ps = [eye - n for n in ns]
    ms = [_dot(n, n) for n in ns]
    power = 2
    while 2 * power < size:
        pms = [_dot(jnp.concatenate([p, m], axis=0), m) for p, m in zip(ps, ms)]
        ps = [p + pm[:size] for p, pm in zip(ps, pms)]
        ms = [pm[size:] for pm in pms]
        power *= 2
    return [p + _dot(p, m) for p, m in zip(ps, ms)]


def _gdn_gates(ba, dtb_ref, alog_ref):
    beta = jax.nn.sigmoid(ba)
    g = -jnp.exp(alog_ref[...]) * jax.nn.softplus(ba + dtb_ref[...])
    return beta, g


def _gdn_kernel(q_ref, k_ref, v_ref, z_ref, ba_ref, dtb_ref, alog_ref, nw_ref, s0_ref, o_ref, s_ref):
    c = pl.program_id(1)
    ch = GDN_CHUNK
    nh = GDN_V_HEADS

    @pl.when(c == 0)
    def _():
        s_ref[...] = s0_ref[...]

    beta, g = _gdn_gates(ba_ref[0], dtb_ref, alog_ref)
    row = _iota((ch, ch), 0)
    colv = _iota((ch, ch), 1)
    incl = row >= colv
    strict = row > colv
    eye = (row == colv).astype(F32)
    gcum = _dot(incl.astype(F32), g, precision=HIGHEST)
    gcum_t = jnp.concatenate([gcum, jnp.zeros((LANES - ch, LANES), F32)], axis=0).T

    rep = nh // GDN_QK_HEADS
    for h0 in range(0, nh, GDN_HEAD_GROUP):
        heads = range(h0, h0 + GDN_HEAD_GROUP)
        qk_heads = range(h0 // rep, (h0 + GDN_HEAD_GROUP) // rep)
        qn = {j: _l2norm(q_ref[0, :, j * GDN_DK:(j + 1) * GDN_DK].astype(F32)) * (GDN_DK ** -0.5) for j in qk_heads}
        kn = {j: _l2norm(k_ref[0, :, j * GDN_DK:(j + 1) * GDN_DK].astype(F32)) for j in qk_heads}
        kk = {j: _dot_nt(kn[j], kn[j]) for j in qk_heads}
        qk = {j: _dot_nt(qn[j], kn[j]) for j in qk_heads}
        gc_col = {h: gcum[:, nh + h:nh + h + 1] for h in heads}
        gc_last = {h: gcum[ch - 1:ch, nh + h:nh + h + 1] for h in heads}
        beta_col = {h: beta[:, h:h + 1] for h in heads}
        dec = {h: jnp.exp(jnp.where(incl, gc_col[h] - gcum_t[nh + h:nh + h + 1, 0:ch], NEG)) for h in heads}
        t_inv = dict(zip(heads, _unit_lower_inverse(
            [jnp.where(strict, kk[h // rep] * dec[h], 0.0) * beta_col[h] for h in heads], eye)))
        eg = {h: jnp.exp(gc_col[h]) for h in heads}
        s_prev = {h: s_ref[0, h] for h in heads}
        both = {h: _dot(jnp.concatenate([kn[h // rep] * (beta_col[h] * eg[h]), qn[h // rep] * eg[h]], axis=0),
                        s_prev[h]) for h in heads}
        u = {h: _dot(t_inv[h], v_ref[0, :, h * GDN_DV:(h + 1) * GDN_DV].astype(F32) * beta_col[h] - both[h][:ch])
             for h in heads}
        o = {h: both[h][ch:] + _dot(qk[h // rep] * dec[h], u[h]) for h in heads}
        s_new = {h: s_prev[h] * jnp.exp(gc_last[h])
                 + _dot_tn(kn[h // rep] * jnp.exp(gc_last[h] - gc_col[h]), u[h]) for h in heads}
        for h in heads:
            s_ref[0, h] = s_new[h]
            z_h = z_ref[0, :, h * GDN_DV:(h + 1) * GDN_DV].astype(F32)
            o_ref[0, :, h * GDN_DV:(h + 1) * GDN_DV] = (
                _rms(o[h]) * nw_ref[...] * jax.nn.silu(z_h)).astype(o_ref.dtype)


def _gdn_gate_params(dt_bias, a_log):
    nh = GDN_V_HEADS
    dtb = jnp.zeros((1, LANES), F32).at[0, nh:2 * nh].set(dt_bias)
    alog = jnp.zeros((1, LANES), F32).at[0, nh:2 * nh].set(a_log)
    return dtb, alog


def gdn_prompt(main, tail, s0, dt_bias, a_log, norm_w, *, bsz, seq):
    ch = GDN_CHUNK
    nc = seq // ch
    dtb, alog = _gdn_gate_params(dt_bias, a_log)
    small = [dtb, alog, norm_w.reshape(1, -1)]
    sspec = pl.BlockSpec((1, GDN_V_HEADS, GDN_DK, GDN_DV), lambda b, c: (b, 0, 0, 0))
    return pl.pallas_call(
        _gdn_kernel,
        grid=(bsz, nc),
        in_specs=[
            pl.BlockSpec((1, ch, GDN_QK_W), lambda b, c: (b, c, 0)),
            pl.BlockSpec((1, ch, GDN_QK_W), lambda b, c: (b, c, 1)),
            pl.BlockSpec((1, ch, GDN_VW), lambda b, c: (b, c, 1)),
            pl.BlockSpec((1, ch, GDN_VW), lambda b, c: (b, c, 2)),
            pl.BlockSpec((1, ch, LANES), lambda b, c: (b, c, 0)),
        ] + [_small(a) for a in small] + [sspec],
        out_specs=[pl.BlockSpec((1, ch, GDN_VW), lambda b, c: (b, c, 0)), sspec],
        out_shape=[jax.ShapeDtypeStruct((bsz, seq, GDN_VW), BF16),
                   jax.ShapeDtypeStruct((bsz, GDN_V_HEADS, GDN_DK, GDN_DV), F32)],
        compiler_params=_cparams("parallel", "arbitrary"),
        name="gdn_prompt",
    )(main, main, main, main, tail, *small, s0)


def _gdn_conv_step(c0_ref, w_ref, x_row, c0, c1):
    acc = w_ref[GDN_CONV - 1:GDN_CONV, c0:c1] * x_row
    for i in range(GDN_CONV - 1):
        acc = acc + w_ref[i:i + 1, c0:c1] * c0_ref[0, i:i + 1, c0:c1]
    return acc


def _gdn_step_kernel(q_ref, k_ref, v_ref, z_ref, ba_ref, c0_ref, cw_ref, dtb_ref, alog_ref, nw_ref, s0_ref,
                     o_ref, co_ref, s_ref):
    nh = GDN_V_HEADS
    q_raw, k_raw, v_raw = q_ref[0], k_ref[0], v_ref[0]
    q = jax.nn.silu(_gdn_conv_step(c0_ref, cw_ref, q_raw, 0, GDN_QK_W))
    k = jax.nn.silu(_gdn_conv_step(c0_ref, cw_ref, k_raw, GDN_QK_W, 2 * GDN_QK_W))
    v = jax.nn.silu(_gdn_conv_step(c0_ref, cw_ref, v_raw, 2 * GDN_QK_W, GDN_QKV))
    co_ref[0, 0:GDN_CONV - 2, :] = c0_ref[0, 1:GDN_CONV - 1, :]
    co_ref[0, GDN_CONV - 2:GDN_CONV - 1, 0:GDN_QK_W] = q_raw
    co_ref[0, GDN_CONV - 2:GDN_CONV - 1, GDN_QK_W:2 * GDN_QK_W] = k_raw
    co_ref[0, GDN_CONV - 2:GDN_CONV - 1, 2 * GDN_QK_W:GDN_QKV] = v_raw

    beta, g = _gdn_gates(ba_ref[0], dtb_ref, alog_ref)
    eg_all = jnp.exp(g)
    eye = _iota((LANES, LANES), 0) == _iota((LANES, LANES), 1)
    for j in range(GDN_QK_HEADS):
        qn = _l2norm(q[:, j * GDN_DK:(j + 1) * GDN_DK]) * (GDN_DK ** -0.5)
        kn = _l2norm(k[:, j * GDN_DK:(j + 1) * GDN_DK])
        qk = jnp.sum(qn * kn, axis=-1, keepdims=True)
        q_col = _row_to_col(qn, eye)
        k_col = _row_to_col(kn, eye)
        for h in range(j * (nh // GDN_QK_HEADS), (j + 1) * (nh // GDN_QK_HEADS)):
            b_h = beta[:, h:h + 1]
            eg = eg_all[:, nh + h:nh + h + 1]
            s_prev = s0_ref[0, h]
            ks = jnp.sum(s_prev * k_col, axis=0, keepdims=True)
            qs = jnp.sum(s_prev * q_col, axis=0, keepdims=True)
            v_h = v[:, h * GDN_DV:(h + 1) * GDN_DV]
            u = v_h * b_h - (b_h * eg) * ks
            o = eg * qs + qk * u
            s_ref[0, h] = s_prev * eg + k_col * u
            z_h = z_ref[0, :, h * GDN_DV:(h + 1) * GDN_DV]
            o_ref[0, :, h * GDN_DV:(h + 1) * GDN_DV] = _rms(o) * nw_ref[...] * jax.nn.silu(z_h)


def gdn_step(main, tail, conv0, s0, conv_w, dt_bias, a_log, norm_w):
    bsz = main.shape[0]
    dtb, alog = _gdn_gate_params(dt_bias, a_log)
    small = [conv_w, dtb, alog, norm_w.reshape(1, -1)]
    sspec = pl.BlockSpec((1, GDN_V_HEADS, GDN_DK, GDN_DV), lambda b: (b, 0, 0, 0))
    cspec = pl.BlockSpec((1, GDN_CONV - 1, GDN_QKV), lambda b: (b, 0, 0))
    return pl.pallas_call(
        _gdn_step_kernel,
        grid=(bsz,),
        in_specs=[
            pl.BlockSpec((1, 1, GDN_QK_W), lambda b: (b, 0, 0)),
            pl.BlockSpec((1, 1, GDN_QK_W), lambda b: (b, 0, 1)),
            pl.BlockSpec((1, 1, GDN_VW), lambda b: (b, 0, 1)),
            pl.BlockSpec((1, 1, GDN_VW), lambda b: (b, 0, 2)),
            pl.BlockSpec((1, 1, LANES), lambda b: (b, 0, 0)),
            cspec,
        ] + [_small(a) for a in small] + [sspec],
        out_specs=[pl.BlockSpec((1, 1, GDN_VW), lambda b: (b, 0, 0)), cspec, sspec],
        out_shape=[jax.ShapeDtypeStruct((bsz, 1, GDN_VW), F32),
                   jax.ShapeDtypeStruct(conv0.shape, F32),
                   jax.ShapeDtypeStruct(s0.shape, F32)],
        compiler_params=_cparams("parallel"),
        name="gdn_step",
    )(main, main, main, main, tail, conv0, *small, s0)


def _pad_tail(w):
    return jnp.pad(w, ((0, 0), (0, LANES - w.shape[1])))


def _prep_hyb_in(w):
    a = A_WIDTH
    q, k, v = w[:, 0:a], w[:, a:2 * a], w[:, 2 * a:3 * a]
    z = w[:, 3 * a:3 * a + SSM_D_INNER]
    x0 = 3 * a + SSM_D_INNER
    xs = w[:, x0:x0 + SSM_D_INNER]
    bc = w[:, x0 + SSM_D_INNER:x0 + SSM_XBC]
    dt = w[:, x0 + SSM_XBC:]
    return jnp.concatenate([z, xs, q, k, v, bc], axis=1).astype(BF16), _pad_tail(dt).astype(BF16)


def _prep_gdn_in(w):
    return w[:, :GDN_MAIN].astype(BF16), _pad_tail(w[:, GDN_MAIN:]).astype(BF16)


HYB_CONV_COLS = ((SSM_D_INNER, 2 * SSM_D_INNER), (HYB_BC0, HYB_MAIN))


def _hyb_cols(a):
    out = jnp.zeros(a.shape[:-1] + (HYB_MAIN,), F32)
    (x0, x1), (b0, b1) = HYB_CONV_COLS
    return out.at[..., x0:x1].set(a[..., :SSM_D_INNER]).at[..., b0:b1].set(a[..., SSM_D_INNER:])


def _window_to_lanes(c):
    n, b, past, h, dh = c.shape
    return jnp.transpose(c, (0, 1, 3, 4, 2)).reshape(n, b, h * dh, past)


def _window_from_lanes(c):
    n, b, _, past = c.shape
    return jnp.transpose(c.reshape(n, b, A_HEADS, A_HEAD_DIM, past), (0, 1, 4, 2, 3))


def _row_tile(m, cap):
    return m if m <= cap else cap


def kernel(x_prompt, x_sample, cache_attn_k, cache_attn_v, state_ssm_conv, state_ssm, state_gdn_conv, state_gdn, rel_bias, norm_mix_pre, norm_mix_post, norm_ffn_pre, norm_ffn_post, w_hyb_in, ssm_conv_w, ssm_conv_b, ssm_dt_bias, ssm_a_log, ssm_d, ssm_norm_w, w_hyb_out, w_gdn_in, gdn_conv_w, gdn_dt_bias, gdn_a_log, gdn_norm_w, w_gdn_out, w_ffn_gate, w_ffn_up, w_ffn_down):
    depth = norm_mix_pre.shape[0]
    d_model = x_prompt.shape[-1]
    n_hyb, n_gdn = w_hyb_in.shape[0], w_gdn_in.shape[0]

    hyb_in = [_prep_hyb_in(w_hyb_in[i]) for i in range(n_hyb)]
    hyb_out = [(w_hyb_out[i, :A_WIDTH].astype(BF16), w_hyb_out[i, A_WIDTH:].astype(BF16)) for i in range(n_hyb)]
    gdn_in = [_prep_gdn_in(w_gdn_in[i]) for i in range(n_gdn)]
    gdn_out = [w_gdn_out[i].astype(BF16) for i in range(n_gdn)]
    ffn_w = [(w_ffn_gate[l].astype(BF16), w_ffn_up[l].astype(BF16), w_ffn_down[l].astype(BF16))
             for l in range(depth)]
    bias_tiles = _attn_bias_rows(rel_bias)

    def trunk(x3, k_pre, v_pre, sconv, sssm, gconv, gstate):
        bsz, seq, _ = x3.shape
        m = bsz * seq
        step = seq == 1
        tm_big = _row_tile(m, 1024)
        tm = _row_tile(m, 512)
        x = x3.reshape(m, d_model)
        nk, nv, nsc, nss, ngc, ngs = [], [], [], [], [], []
        rolled = None
        for l in range(depth):
            i = l // 2
            if l % 2 == 0:
                w_main, w_tail = hyb_in[i]
                ssm_args = (ssm_dt_bias[i], ssm_a_log[i], ssm_d[i], ssm_norm_w[i])
                if step:
                    main, tail = inproj(x, norm_mix_pre[l], w_main, w_tail, tm=tm_big, tn=512)
                    main3 = main.reshape(bsz, seq, HYB_MAIN)
                    tail3 = tail.reshape(bsz, seq, LANES)
                    col = lambda c0: main[:, c0:c0 + A_WIDTH].reshape(bsz, A_WIDTH, 1)
                    o_attn, *rolled = attention_step(
                        col(HYB_Q0), col(HYB_K0), col(HYB_V0), k_pre, v_pre, rolled,
                        _attn_logw(rel_bias, k_pre.shape[-1]), layer=i)
                    y, c_new, s_new = ssd_step(main3, tail3, sconv[i], sssm[i], ssm_conv_w[i], ssm_conv_b[i],
                                               *ssm_args)
                else:
                    main, tail, hist = inproj_conv(
                        x, norm_mix_pre[l], w_main, w_tail, _hyb_cols(ssm_conv_w[i]),
                        _hyb_cols(ssm_conv_b[i][None]), _hyb_cols(sconv[i]), seq=seq, conv_cols=HYB_CONV_COLS,
                        tm=tm_big, tn=512)
                    main3 = main.reshape(bsz, seq, HYB_MAIN)
                    tail3 = tail.reshape(bsz, seq, LANES)
                    c_new = jnp.concatenate([hist[..., a:b] for a, b in HYB_CONV_COLS], axis=-1)
                    o_attn = attention_prompt(main3, bias_tiles, bsz=bsz, seq=seq)
                    keep = min(A_PATTERNS[-1][0], seq)
                    k_new = main3[:, seq - keep:, HYB_K0:HYB_K0 + A_WIDTH].astype(F32)
                    v_new = main3[:, seq - keep:, HYB_V0:HYB_V0 + A_WIDTH].astype(F32)
                    y, s_new = ssd_prompt(main3, tail3, sssm[i], *ssm_args, bsz=bsz, seq=seq)
                    nk.append(k_new.reshape(bsz, -1, A_HEADS, A_HEAD_DIM))
                    nv.append(v_new.reshape(bsz, -1, A_HEADS, A_HEAD_DIM))
                nsc.append(c_new)
                nss.append(s_new)
                x = outproj([o_attn.reshape(m, A_WIDTH), y.reshape(m, SSM_D_INNER)], list(hyb_out[i]),
                            x, norm_mix_post[l], tm=tm)
            else:
                w_main, w_tail = gdn_in[i]
                gdn_args = (gdn_dt_bias[i], gdn_a_log[i], gdn_norm_w[i])
                if step:
                    main, tail = inproj(x, norm_mix_pre[l], w_main, w_tail, tm=tm_big, tn=512)
                    o, c_new, s_new = gdn_step(main.reshape(bsz, seq, GDN_MAIN), tail.reshape(bsz, seq, LANES),
                                               gconv[i], gstate[i], gdn_conv_w[i], *gdn_args)
                else:
                    pad = lambda a: jnp.pad(a, [(0, 0)] * (a.ndim - 1) + [(0, GDN_MAIN - GDN_QKV)])
                    main, tail, hist = inproj_conv(
                        x, norm_mix_pre[l], w_main, w_tail, pad(gdn_conv_w[i]), jnp.zeros((1, GDN_MAIN), F32),
                        pad(gconv[i]), seq=seq, conv_cols=((0, GDN_QKV),), tm=tm_big, tn=512)
                    c_new = hist[..., :GDN_QKV]
                    o, s_new = gdn_prompt(main.reshape(bsz, seq, GDN_MAIN), tail.reshape(bsz, seq, LANES),
                                          gstate[i], *gdn_args, bsz=bsz, seq=seq)
                ngc.append(c_new)
                ngs.append(s_new)
                x = outproj([o.reshape(m, GDN_VW)], [gdn_out[i]], x, norm_mix_post[l], tm=tm)
            wg, wu, wd = ffn_w[l]
            x = ffn(x, norm_ffn_pre[l], wg, wu, wd, norm_ffn_post[l], tm=tm, tf=wg.shape[1] // 2)
        k_out, v_out = [_window_from_lanes(r) for r in rolled] if step else (jnp.stack(nk), jnp.stack(nv))
        return (x.reshape(bsz, seq, d_model), k_out, v_out, jnp.stack(nsc), jnp.stack(nss),
                jnp.stack(ngc), jnp.stack(ngs))

    bsz = x_prompt.shape[0]
    dt_p = x_prompt.dtype
    p_sc0 = jnp.zeros((n_hyb, bsz, SSM_CONV - 1, SSM_XBC), dt_p)
    p_ss0 = jnp.zeros((n_hyb, bsz, SSM_HEADS, SSM_HEAD_DIM, SSM_STATE), F32)
    p_gc0 = jnp.zeros((n_gdn, bsz, GDN_CONV - 1, GDN_QKV), dt_p)
    p_gs0 = jnp.zeros((n_gdn, bsz, GDN_V_HEADS, GDN_DK, GDN_DV), F32)
    y_prompt, pk, pv, psc, pss, pgc, pgs = trunk(x_prompt, None, None, p_sc0, p_ss0, p_gc0, p_gs0)
    y_sample, sk, sv, ssc, sss, sgc, sgs = trunk(
        x_sample, _window_to_lanes(cache_attn_k), _window_to_lanes(cache_attn_v), state_ssm_conv, state_ssm,
        state_gdn_conv, state_gdn)
    return (y_prompt, y_sample, pk, pv, psc, pss, pgc, pgs, sk, sv, ssc, sss, sgc, sgs)
```

```python
import functools
import math

import numpy as np
import jax
import jax.numpy as jnp
from jax import lax
from jax.experimental import pallas as pl
from jax.experimental.pallas import tpu as pltpu

F32 = jnp.float32
BF16 = jnp.bfloat16
EPS = 1e-6
NEG = -1e30
HIGHEST = lax.Precision.HIGHEST

VMEM_LIMIT_BYTES = 56 * 1024 * 1024
LANES = 128

A_HEADS = 8
A_HEAD_DIM = 64
A_WIDTH = A_HEADS * A_HEAD_DIM
A_PATTERNS = ((128, 1), (512, 4), (2048, 16))
A_BAND = 128
ATTN_GROUP = 4
REL_BUCKETS = 32
REL_MAX_DIST = 2048

SSM_D_INNER = 1024
SSM_HEAD_DIM = 64
SSM_HEADS = SSM_D_INNER // SSM_HEAD_DIM
SSM_GROUPS = 2
SSM_STATE = 128
SSM_CONV = 4
SSM_CHUNK = 128
SSM_BC = 2 * SSM_GROUPS * SSM_STATE
SSM_XBC = SSM_D_INNER + SSM_BC

GDN_QK_HEADS = 8
GDN_V_HEADS = 16
GDN_DK = 128
GDN_DV = 128
GDN_CONV = 4
GDN_CHUNK = 64
GDN_HEAD_GROUP = 16
GDN_QK_W = GDN_QK_HEADS * GDN_DK
GDN_VW = GDN_V_HEADS * GDN_DV
GDN_QKV = 2 * GDN_QK_W + GDN_VW

HYB_MAIN = 2 * SSM_D_INNER + 3 * A_WIDTH + SSM_BC
HYB_Q0 = 2 * SSM_D_INNER
HYB_K0 = HYB_Q0 + A_WIDTH
HYB_V0 = HYB_K0 + A_WIDTH
HYB_BC0 = HYB_V0 + A_WIDTH
GDN_MAIN = GDN_QKV + GDN_VW


def _cparams(*sem):
    return pltpu.CompilerParams(dimension_semantics=sem, vmem_limit_bytes=VMEM_LIMIT_BYTES)


def _rms(x):
    return x * lax.rsqrt(jnp.mean(x * x, axis=-1, keepdims=True) + EPS)


def _dot(a, b, **kw):
    return jnp.dot(a, b, preferred_element_type=F32, **kw)


def _dot_nt(a, b, **kw):
    return lax.dot_general(a, b, (((1,), (1,)), ((), ())), preferred_element_type=F32, **kw)


def _dot_tn(a, b, **kw):
    return lax.dot_general(a, b, (((0,), (0,)), ((), ())), preferred_element_type=F32, **kw)


def _iota(shape, dim):
    return lax.broadcasted_iota(jnp.int32, shape, dim)


def _inproj_kernel(x_ref, g_ref, w_ref, wt_ref, o_ref, t_ref, h_ref):
    @pl.when(pl.program_id(1) == 0)
    def _():
        hb = (_rms(x_ref[...]) * g_ref[...]).astype(BF16)
        h_ref[...] = hb
        t_ref[...] = _dot(hb, wt_ref[...])

    o_ref[...] = _dot(h_ref[...], w_ref[...])


def inproj(x, g, w_main, w_tail, *, tm, tn):
    m, d = x.shape
    n = w_main.shape[1]
    return pl.pallas_call(
        _inproj_kernel,
        grid=(m // tm, n // tn),
        in_specs=[
            pl.BlockSpec((tm, d), lambda i, j: (i, 0)),
            pl.BlockSpec((1, d), lambda i, j: (0, 0)),
            pl.BlockSpec((d, tn), lambda i, j: (0, j)),
            pl.BlockSpec((d, LANES), lambda i, j: (0, 0)),
        ],
        out_specs=[
            pl.BlockSpec((tm, tn), lambda i, j: (i, j)),
            pl.BlockSpec((tm, LANES), lambda i, j: (i, 0)),
        ],
        out_shape=[jax.ShapeDtypeStruct((m, n), F32), jax.ShapeDtypeStruct((m, LANES), F32)],
        scratch_shapes=[pltpu.VMEM((tm, d), BF16)],
        compiler_params=_cparams("parallel", "arbitrary"),
        name="inproj",
    )(x, g.reshape(1, d), w_main, w_tail)


CONV_TAPS = 4
CONV_ROWS = 64
CONV_BASE = 8


def _inproj_conv_kernel(x_ref, g_ref, w_ref, wt_ref, cw_ref, cb_ref, c0_ref, o_ref, t_ref, so_ref,
                        h_ref, xp0_ref, xp1_ref, carry_ref, *, n_col, conv_tiles, tiles_per_seq):
    s = pl.program_id(0)
    n_tiles = pl.num_programs(0) - 1
    tm = o_ref.shape[0]
    lo = CONV_BASE - (CONV_TAPS - 1)
    cur = jnp.minimum(s, n_tiles - 1)
    prv = jnp.maximum(s - 1, 0)
    pi, pj = prv // n_col, prv % n_col
    prv_conv = functools.reduce(jnp.logical_or, [jnp.logical_and(pj >= a, pj < b) for a, b in conv_tiles])
    first = pi % tiles_per_seq == 0
    both = lambda a, b: jnp.logical_and(a, b)

    @pl.when(both(cur % n_col == 0, s < n_tiles))
    def _():
        hb = (_rms(x_ref[...]) * g_ref[...]).astype(BF16)
        h_ref[...] = hb
        t_ref[...] = _dot(hb, wt_ref[...])

    for parity, (cur_ref, prv_ref) in enumerate(((xp0_ref, xp1_ref), (xp1_ref, xp0_ref))):
        here = s % 2 == parity

        def project(cur_ref=cur_ref):
            cur_ref[CONV_BASE:CONV_BASE + tm, :] = _dot(h_ref[...], w_ref[...])

        def raw_tail(prv_ref=prv_ref):
            tail = prv_ref[CONV_BASE + tm - (CONV_TAPS - 1):CONV_BASE + tm, :]
            so_ref[0] = tail
            return tail

        if parity == 0:
            @pl.when(s == 0)
            def _():
                project()

        @pl.when(both(here, both(s > 0, jnp.logical_not(prv_conv))))
        def _():
            project()
            raw_tail()
            o_ref[...] = prv_ref[CONV_BASE:CONV_BASE + tm, :].astype(o_ref.dtype)

        @pl.when(both(here, both(both(s > 0, prv_conv), first)))
        def _():
            prv_ref[lo:CONV_BASE, :] = c0_ref[0]

        @pl.when(both(here, both(both(s > 0, prv_conv), jnp.logical_not(first))))
        def _():
            prv_ref[lo:CONV_BASE, :] = carry_ref[pj]

        @pl.when(both(here, both(s > 0, prv_conv)))
        def _():
            project()
            for r0 in range(0, tm, CONV_ROWS):
                ext = prv_ref[r0:r0 + CONV_BASE + CONV_ROWS, :]
                conv = cb_ref[...] + cw_ref[CONV_TAPS - 1:CONV_TAPS, :] * ext[CONV_BASE:]
                for back in range(1, CONV_TAPS):
                    tap = CONV_TAPS - 1 - back
                    conv = conv + cw_ref[tap:tap + 1, :] * pltpu.roll(ext, back, 0)[CONV_BASE:]
                o_ref[r0:r0 + CONV_ROWS, :] = jax.nn.silu(conv).astype(o_ref.dtype)
            carry_ref[pj] = raw_tail()


def inproj_conv(x, g, w_main, w_tail, conv_w, conv_b, conv0, *, seq, conv_cols, tm, tn):
    m, d = x.shape
    n = w_main.shape[1]
    assert seq % tm == 0 and all(a % tn == 0 and b % tn == 0 for a, b in conv_cols)
    tiles_per_seq = seq // tm
    n_row, n_col = m // tm, n // tn
    n_tiles = n_row * n_col
    conv_tiles = tuple((a // tn, b // tn) for a, b in conv_cols)
    hist = CONV_TAPS - 1
    cur = lambda s: jnp.minimum(s, n_tiles - 1)
    prv = lambda s: jnp.maximum(s - 1, 0)
    main, tail, hist_rows = pl.pallas_call(
        functools.partial(_inproj_conv_kernel, n_col=n_col, conv_tiles=conv_tiles, tiles_per_seq=tiles_per_seq),
        grid=(n_tiles + 1,),
        in_specs=[
            pl.BlockSpec((tm, d), lambda s: (cur(s) // n_col, 0)),
            pl.BlockSpec((1, d), lambda s: (0, 0)),
            pl.BlockSpec((d, tn), lambda s: (0, cur(s) % n_col)),
            pl.BlockSpec((d, LANES), lambda s: (0, 0)),
            pl.BlockSpec((CONV_TAPS, tn), lambda s: (0, prv(s) % n_col)),
            pl.BlockSpec((1, tn), lambda s: (0, prv(s) % n_col)),
            pl.BlockSpec((1, hist, tn), lambda s: (prv(s) // n_col // tiles_per_seq, 0, prv(s) % n_col)),
        ],
        out_specs=[
            pl.BlockSpec((tm, tn), lambda s: (prv(s) // n_col, prv(s) % n_col)),
            pl.BlockSpec((tm, LANES), lambda s: (cur(s) // n_col, 0)),
            pl.BlockSpec((1, hist, tn), lambda s: (prv(s) // n_col, 0, prv(s) % n_col)),
        ],
        out_shape=[jax.ShapeDtypeStruct((m, n), BF16), jax.ShapeDtypeStruct((m, LANES), F32),
                   jax.ShapeDtypeStruct((n_row, hist, n), F32)],
        scratch_shapes=[pltpu.VMEM((tm, d), BF16), pltpu.VMEM((CONV_BASE + tm, tn), F32),
                        pltpu.VMEM((CONV_BASE + tm, tn), F32), pltpu.VMEM((n_col, hist, tn), F32)],
        compiler_params=_cparams("arbitrary"),
        name="inproj_conv",
    )(x, g.reshape(1, d), w_main, w_tail, conv_w, conv_b, conv0)
    return main, tail, hist_rows[tiles_per_seq - 1::tiles_per_seq]


def _outproj_kernel(*refs, n_in):
    a_refs, w_refs = refs[:n_in], refs[n_in:2 * n_in]
    x_ref, g_ref, o_ref = refs[2 * n_in:]
    acc = None
    for a_ref, w_ref in zip(a_refs, w_refs):
        t = _dot(a_ref[...].astype(BF16), w_ref[...])
        acc = t if acc is None else acc + t
    o_ref[...] = x_ref[...] + _rms(acc) * g_ref[...]


def outproj(acts, weights, x, g, *, tm):
    m, d = x.shape
    n_in = len(acts)
    in_specs = [pl.BlockSpec((tm, a.shape[1]), lambda i: (i, 0)) for a in acts]
    in_specs += [pl.BlockSpec(w.shape, lambda i: (0, 0)) for w in weights]
    in_specs += [pl.BlockSpec((tm, d), lambda i: (i, 0)), pl.BlockSpec((1, d), lambda i: (0, 0))]
    return pl.pallas_call(
        functools.partial(_outproj_kernel, n_in=n_in),
        grid=(m // tm,),
        in_specs=in_specs,
        out_specs=pl.BlockSpec((tm, d), lambda i: (i, 0)),
        out_shape=jax.ShapeDtypeStruct((m, d), F32),
        compiler_params=_cparams("parallel"),
        name="outproj",
    )(*acts, *weights, x, g.reshape(1, d))


def _ffn_kernel(x_ref, g1_ref, wg_ref, wu_ref, wd_ref, g2_ref, o_ref):
    x = x_ref[...]
    h = (_rms(x) * g1_ref[...]).astype(BF16)
    a = jax.nn.silu(_dot(h, wg_ref[...])) * _dot(h, wu_ref[...])
    o_ref[...] = x + _rms(_dot(a.astype(BF16), wd_ref[...])) * g2_ref[...]


def ffn(x, g1, wg, wu, wd, g2, *, tm):
    m, d = x.shape
    resident = lambda a: pl.BlockSpec(a.shape, lambda i: (0, 0), pipeline_mode=pl.Buffered(1))
    return pl.pallas_call(
        _ffn_kernel,
        grid=(m // tm,),
        in_specs=[
            pl.BlockSpec((tm, d), lambda i: (i, 0)),
            pl.BlockSpec((1, d), lambda i: (0, 0)),
            resident(wg), resident(wu), resident(wd),
            pl.BlockSpec((1, d), lambda i: (0, 0)),
        ],
        out_specs=pl.BlockSpec((tm, d), lambda i: (i, 0)),
        out_shape=jax.ShapeDtypeStruct((m, d), F32),
        compiler_params=_cparams("parallel"),
        name="ffn",
    )(x, g1.reshape(1, d), wg, wu, wd, g2.reshape(1, d))


def _rel_buckets(dist):
    max_exact = REL_BUCKETS // 2
    n = np.maximum(dist, 1).astype(np.float32)
    large = max_exact + (np.log(n / max_exact) / math.log(REL_MAX_DIST / max_exact)
                         * (REL_BUCKETS - max_exact)).astype(np.int32)
    large = np.minimum(large, REL_BUCKETS - 1)
    return np.where(dist < max_exact, dist, large).astype(np.int32)


def _attn_bias_rows(rel_bias):
    u = np.arange(2 * A_BAND)
    valid = u <= A_BAND
    rows = []
    for (_, d) in A_PATTERNS:
        b = rel_bias[_rel_buckets(np.where(valid, A_BAND - u, 0) * d)]
        rows.append(jnp.where(valid[:, None], b.astype(F32), NEG))
    tl = jnp.transpose(jnp.stack(rows), (2, 0, 1))
    tl = tl.reshape(A_HEADS // 2, 2, len(A_PATTERNS), 2 * A_BAND)
    tl = jnp.transpose(tl, (0, 2, 1, 3))[:, :, :, None, :]
    return jnp.broadcast_to(tl, tl.shape[:3] + (8, 2 * A_BAND))


def _attn_kernel(qin_ref, kin_ref, vin_ref, brow_ref, o_ref, m0_ref, m1_ref, l0_ref, l1_ref, acc_ref,
                 q_ref, k_ref, v_ref, b_ref, *, seq):
    n_tiles = seq // A_BAND
    lane = _iota((A_BAND, LANES), 1)
    head0 = lane < A_HEAD_DIM
    m_refs, l_refs = (m0_ref, m1_ref), (l0_ref, l1_ref)
    q_ref[...] = qin_ref[...].astype(F32) * (A_HEAD_DIM ** -0.5)
    k_ref[...] = kin_ref[...].astype(F32)
    v_ref[...] = vin_ref[...].astype(F32)
    for p in range(len(A_PATTERNS)):
        for h in range(2):
            row = jnp.broadcast_to(brow_ref[0, p, h, 0:1, :], (A_BAND, 2 * A_BAND))
            b_ref[0, p, h] = pltpu.roll(row, 0, 1, stride=1, stride_axis=0)
    for h in range(2):
        m_refs[h][...] = jnp.full(m_refs[h].shape, NEG, F32)
        l_refs[h][...] = jnp.zeros_like(l_refs[h])
    acc_ref[...] = jnp.zeros_like(acc_ref)

    for p, (_, d) in enumerate(A_PATTERNS):
        tiles_per_class = n_tiles // d

        def load_tile(idx, d=d, tiles_per_class=tiles_per_class):
            r = idx // tiles_per_class
            t = idx % tiles_per_class
            start = r + t * (d * A_BAND)
            has_prev = t > 0
            prev = jnp.where(has_prev, start - d * A_BAND, start)
            if d > 1:
                rows, prows = pl.ds(start, A_BAND, stride=d), pl.ds(prev, A_BAND, stride=d)
            else:
                rows, prows = pl.ds(pl.multiple_of(start, A_BAND), A_BAND), pl.ds(pl.multiple_of(prev, A_BAND), A_BAND)
            return dict(
                rows=rows, has_prev=has_prev, q=q_ref[rows, :],
                k2=jnp.concatenate([k_ref[prows, :], k_ref[rows, :]], axis=0).astype(BF16),
                v2=jnp.concatenate([v_ref[prows, :], v_ref[rows, :]], axis=0).astype(BF16),
                acc=acc_ref[rows, :], m=[m_refs[h][rows, :] for h in range(2)],
                l=[l_refs[h][rows, :] for h in range(2)])

        def tile_group(idx, carry, load_tile=load_tile, p=p):
            tiles = [load_tile(idx + i * (n_tiles // ATTN_GROUP)) for i in range(ATTN_GROUP)]
            chains = [(tile, h) for tile in tiles for h in range(2)]
            col = _iota((A_BAND, 2 * A_BAND), 1)
            qh = [jnp.where(head0 if h == 0 else jnp.logical_not(head0), tile["q"], 0.0).astype(BF16)
                  for tile, h in chains]
            s = [_dot_nt(qh[c], tile["k2"])
                 + jnp.where(jnp.logical_and(col < A_BAND, jnp.logical_not(tile["has_prev"])), NEG, b_ref[0, p, h])
                 for c, (tile, h) in enumerate(chains)]
            m_new = [jnp.maximum(tile["m"][h], jnp.max(s[c], axis=-1, keepdims=True))
                     for c, (tile, h) in enumerate(chains)]
            alpha = [jnp.exp(tile["m"][h] - m_new[c]) for c, (tile, h) in enumerate(chains)]
            pr = [jnp.exp(s[c] - jnp.concatenate([m_new[c], m_new[c]], axis=1)) for c in range(len(chains))]
            l_new = [alpha[c] * tile["l"][h] + jnp.sum(pr[c], axis=-1, keepdims=True)
                     for c, (tile, h) in enumerate(chains)]
            acc_new = [alpha[c] * tile["acc"] + _dot(pr[c].astype(BF16), tile["v2"])
                       for c, (tile, h) in enumerate(chains)]
            for c, (tile, h) in enumerate(chains):
                m_refs[h][tile["rows"], :] = m_new[c]
                l_refs[h][tile["rows"], :] = l_new[c]
                if h == 1:
                    acc_ref[tile["rows"], :] = jnp.where(head0, acc_new[c - 1], acc_new[c])
            return carry

        lax.fori_loop(0, n_tiles // ATTN_GROUP, tile_group, 0)

    lane_s = _iota((seq, LANES), 1)
    o_ref[...] = (acc_ref[...] / jnp.where(lane_s < A_HEAD_DIM, l0_ref[...], l1_ref[...])).astype(o_ref.dtype)


def attention_prompt(proj, bias_tiles, *, bsz, seq):
    hp = A_HEADS // 2
    qb, kb, vb = HYB_Q0 // LANES, HYB_K0 // LANES, HYB_V0 // LANES
    return pl.pallas_call(
        functools.partial(_attn_kernel, seq=seq),
        grid=(bsz, hp),
        in_specs=[
            pl.BlockSpec((None, seq, LANES), lambda b, h: (b, 0, qb + h)),
            pl.BlockSpec((None, seq, LANES), lambda b, h: (b, 0, kb + h)),
            pl.BlockSpec((None, seq, LANES), lambda b, h: (b, 0, vb + h)),
            pl.BlockSpec((1,) + bias_tiles.shape[1:], lambda b, h: (h, 0, 0, 0, 0)),
        ],
        out_specs=pl.BlockSpec((None, seq, LANES), lambda b, h: (b, 0, h)),
        out_shape=jax.ShapeDtypeStruct((bsz, seq, A_WIDTH), BF16),
        scratch_shapes=[pltpu.VMEM((seq, LANES), F32)] * 8
        + [pltpu.VMEM((1, len(A_PATTERNS), 2, A_BAND, 2 * A_BAND), F32)],
        compiler_params=_cparams("parallel", "parallel"),
        name="attn_prompt",
    )(proj, proj, proj, bias_tiles)


def _attn_logw(rel_bias, past):
    dist = np.arange(past + 1)
    count = np.zeros(past + 1, np.float64)
    for (w, d) in A_PATTERNS:
        count += ((dist % d == 0) & (dist <= w)).astype(np.float64)
    logc = np.where(count > 0, np.log(np.maximum(count, 1.0)), 0.0).astype(np.float32)
    lw = rel_bias[_rel_buckets(dist)].astype(F32).T + logc[None, :]
    return jnp.where((count > 0)[None, :], lw, NEG)


def _step_scores(xk_ref, q_col, kn_col, lw_ref, lw0_ref, s_ref):
    qs = q_col * (A_HEAD_DIM ** -0.5)
    s_new = []
    for h in range(A_HEADS):
        rows = slice(h * A_HEAD_DIM, (h + 1) * A_HEAD_DIM)
        s_ref[h:h + 1, :] = jnp.sum(xk_ref[0, 0, rows, :] * qs[rows], axis=0, keepdims=True)
        s_new.append(jnp.sum(kn_col[rows] * qs[rows], axis=0, keepdims=True))
    s = s_ref[...] + lw_ref[...]
    s_new = jnp.concatenate(s_new, axis=0) + lw0_ref[:, :1]
    m = jnp.maximum(jnp.max(s, axis=-1, keepdims=True), s_new)
    p = jnp.exp(s - m)
    p_new = jnp.exp(s_new - m)
    den = jnp.sum(p, axis=-1, keepdims=True) + p_new
    return p, p_new, den


def _step_output(xv_ref, vn_col, p, p_new, den, o_ref):
    for h in range(A_HEADS):
        rows = slice(h * A_HEAD_DIM, (h + 1) * A_HEAD_DIM)
        pv = jnp.sum(xv_ref[0, 0, rows, :] * p[h:h + 1, :], axis=-1, keepdims=True)
        o_ref[0, rows, :] = (pv + p_new[h:h + 1, :] * vn_col[rows]) / den[h:h + 1, :]


def _attn_step_roll_kernel(q_ref, kn_ref, vn_ref, lw_ref, lw0_ref, xk_ref, xv_ref, o_ref, ko_ref, vo_ref, s_ref,
                           *, layer, past):
    is_layer = pl.program_id(0) == layer
    newest = _iota((A_HEAD_DIM, past), 1) == past - 1
    for x_ref, n_ref, out_ref in ((xk_ref, kn_ref, ko_ref), (xv_ref, vn_ref, vo_ref)):
        for h in range(A_HEADS):
            rows = slice(h * A_HEAD_DIM, (h + 1) * A_HEAD_DIM)
            rolled = pltpu.roll(x_ref[0, 0, rows, :], past - 1, 1)
            out_ref[0, 0, rows, :] = jnp.where(jnp.logical_and(newest, is_layer), n_ref[0, rows, :], rolled)

    @pl.when(is_layer)
    def _():
        p, p_new, den = _step_scores(xk_ref, q_ref[0], kn_ref[0], lw_ref, lw0_ref, s_ref)
        _step_output(xv_ref, vn_ref[0], p, p_new, den, o_ref.at[0])

    @pl.when(jnp.logical_not(is_layer))
    def _():
        o_ref[...] = jnp.zeros_like(o_ref)


def _attn_step_append_kernel(q_ref, kn_ref, vn_ref, lw_ref, lw0_ref, xk_ref, xv_ref, ko_in, vo_in,
                             o_ref, ko_ref, vo_ref, s_ref, *, past):
    del ko_in, vo_in
    p, p_new, den = _step_scores(xk_ref, q_ref[0], kn_ref[0], lw_ref, lw0_ref, s_ref)
    _step_output(xv_ref, vn_ref[0], p, p_new, den, o_ref)
    newest = _iota((A_WIDTH, LANES), 1) == LANES - 1
    for x_ref, n_ref, out_ref in ((xk_ref, kn_ref, ko_ref), (xv_ref, vn_ref, vo_ref)):
        rolled = pltpu.roll(x_ref[0, 0, :, past - LANES:past], LANES - 1, 1)
        out_ref[0, 0] = jnp.where(newest, n_ref[0], rolled)


def attention_step(q_col, kn_col, vn_col, cache_k, cache_v, rolled, logw, *, layer):
    n_layers, bsz, w, past = cache_k.shape
    lw_cache = logw[:, past:0:-1]
    lw_new = jnp.broadcast_to(logw[:, :1], (A_HEADS, LANES))
    out_shape = [jax.ShapeDtypeStruct((bsz, w, 1), F32),
                 jax.ShapeDtypeStruct(cache_k.shape, cache_k.dtype),
                 jax.ShapeDtypeStruct(cache_v.shape, cache_v.dtype)]
    scratch = [pltpu.VMEM((A_HEADS, past), F32)]
    if rolled is None:
        assert layer == 0
        col = pl.BlockSpec((1, w, 1), lambda l, b: (b, 0, 0))
        win = pl.BlockSpec((1, 1, w, past), lambda l, b: (l, b, 0, 0))
        o_all, rolled_k, rolled_v = pl.pallas_call(
            functools.partial(_attn_step_roll_kernel, layer=layer, past=past),
            grid=(n_layers, bsz),
            in_specs=[col, col, col, _small(lw_cache), _small(lw_new), win, win],
            out_specs=[pl.BlockSpec((1, 1, w, 1), lambda l, b: (l, b, 0, 0)), win, win],
            out_shape=[jax.ShapeDtypeStruct((n_layers, bsz, w, 1), F32)] + out_shape[1:],
            scratch_shapes=scratch,
            compiler_params=_cparams("arbitrary", "arbitrary"),
            name="attn_step_roll",
        )(q_col, kn_col, vn_col, lw_cache, lw_new, cache_k, cache_v)
        return o_all[layer], rolled_k, rolled_v
    col = pl.BlockSpec((1, w, 1), lambda b: (b, 0, 0))
    win = pl.BlockSpec((1, 1, w, past), lambda b: (layer, b, 0, 0))
    tail = pl.BlockSpec((1, 1, w, LANES), lambda b: (layer, b, 0, past // LANES - 1))
    return pl.pallas_call(
        functools.partial(_attn_step_append_kernel, past=past),
        grid=(bsz,),
        in_specs=[col, col, col, _small(lw_cache), _small(lw_new), win, win, tail, tail],
        out_specs=[col, tail, tail],
        out_shape=out_shape,
        scratch_shapes=scratch,
        input_output_aliases={7: 1, 8: 2},
        compiler_params=_cparams("arbitrary"),
        name="attn_step_append",
    )(q_col, kn_col, vn_col, lw_cache, lw_new, cache_k, cache_v, *rolled)


def _group_rms(y, w):
    half = SSM_D_INNER // SSM_GROUPS
    return [_rms(y[:, g * half:(g + 1) * half]) * w[:, g * half:(g + 1) * half] for g in range(SSM_GROUPS)]


def _ssd_kernel(z_ref, xs_ref, bc_ref, dt_ref, dtb_ref, alog_ref, d_ref, nw_ref, h0_ref, y_ref, h_ref, ys_ref):
    c = pl.program_id(1)
    ch = SSM_CHUNK

    @pl.when(c == 0)
    def _():
        h_ref[...] = h0_ref[...]

    xs = xs_ref[0].astype(F32)
    bc = bc_ref[0].astype(F32)

    dt = jax.nn.softplus(dt_ref[0] + dtb_ref[...])
    da = dt * (-jnp.exp(alog_ref[...]))
    row = _iota((ch, ch), 0)
    colv = _iota((ch, ch), 1)
    tril = (row >= colv).astype(F32)
    cs = _dot(tril, da, precision=HIGHEST)
    cs_t = cs.T
    causal = row >= colv

    heads = range(SSM_HEADS)
    hpg = SSM_HEADS // SSM_GROUPS
    bm = [bc[:, g * SSM_STATE:(g + 1) * SSM_STATE] for g in range(SSM_GROUPS)]
    cm = [bc[:, (SSM_GROUPS + g) * SSM_STATE:(SSM_GROUPS + g + 1) * SSM_STATE] for g in range(SSM_GROUPS)]
    cb = [_dot_nt(cm[g], bm[g]) for g in range(SSM_GROUPS)]
    dt_t = dt.T
    xs_t = xs.T
    w_t = dt_t * jnp.exp(cs_t[:, ch - 1:ch] - cs_t)
    e_last = jnp.exp(cs[ch - 1:ch, :])
    lane_lo = _iota((ch, LANES), 1) < SSM_HEAD_DIM
    row_lo = _iota((LANES, ch), 0) < SSM_HEAD_DIM
    pairs = range(SSM_HEADS // 2)
    csb = [jnp.broadcast_to(cs[:, h:h + 1], (ch, ch)) for h in heads]
    mix = [cb[h // hpg] * jnp.exp(jnp.where(causal, csb[h] - cs_t[h:h + 1, :], NEG)) * dt_t[h:h + 1, :]
           for h in heads]
    x_pair = [xs[:, j * LANES:(j + 1) * LANES] for j in pairs]
    y_intra = [jnp.where(lane_lo, _dot(mix[2 * j], x_pair[j]), _dot(mix[2 * j + 1], x_pair[j])) for j in pairs]
    h_pair = [h_ref[0, 2 * j:2 * j + 2].reshape(2 * SSM_HEAD_DIM, SSM_STATE) for j in pairs]
    y_inter = [_dot_nt(cm[2 * j // hpg], h_pair[j]) * jnp.exp(jnp.where(lane_lo, csb[2 * j], csb[2 * j + 1]))
               for j in pairs]
    xw_t = [xs_t[j * LANES:(j + 1) * LANES, :] * jnp.where(row_lo, w_t[2 * j:2 * j + 1, :], w_t[2 * j + 1:2 * j + 2, :])
            for j in pairs]
    st = [_dot(xw_t[j], bm[2 * j // hpg]) for j in pairs]
    for j in pairs:
        cols = slice(j * LANES, (j + 1) * LANES)
        ys_ref[:, cols] = y_intra[j] + y_inter[j] + d_ref[:, cols] * x_pair[j]
        decay = jnp.where(row_lo, e_last[:, 2 * j:2 * j + 1], e_last[:, 2 * j + 1:2 * j + 2])
        h_ref[0, 2 * j:2 * j + 2] = (h_pair[j] * decay + st[j]).reshape(2, SSM_HEAD_DIM, SSM_STATE)

    y = ys_ref[...] * jax.nn.silu(z_ref[0].astype(F32))
    half = SSM_D_INNER // SSM_GROUPS
    for g, yg in enumerate(_group_rms(y, nw_ref[...])):
        y_ref[0, :, g * half:(g + 1) * half] = yg.astype(y_ref.dtype)


def _small(a):
    return pl.BlockSpec(a.shape, lambda *_: (0,) * a.ndim)


def ssd_prompt(main, tail, h0, dt_bias, a_log, d_skip, norm_w, *, bsz, seq):
    nc = seq // SSM_CHUNK
    ch = SSM_CHUNK
    small = [_pad_tail(dt_bias.reshape(1, -1)), _pad_tail(a_log.reshape(1, -1)),
             jnp.repeat(d_skip, SSM_HEAD_DIM).reshape(1, -1), norm_w.reshape(1, -1)]
    return pl.pallas_call(
        _ssd_kernel,
        grid=(bsz, nc),
        in_specs=[
            pl.BlockSpec((1, ch, SSM_D_INNER), lambda b, c: (b, c, 0)),
            pl.BlockSpec((1, ch, SSM_D_INNER), lambda b, c: (b, c, 1)),
            pl.BlockSpec((1, ch, SSM_BC), lambda b, c: (b, c, HYB_BC0 // SSM_BC)),
            pl.BlockSpec((1, ch, LANES), lambda b, c: (b, c, 0)),
        ] + [_small(a) for a in small] + [
            pl.BlockSpec((1, SSM_HEADS, SSM_HEAD_DIM, SSM_STATE), lambda b, c: (b, 0, 0, 0)),
        ],
        out_specs=[
            pl.BlockSpec((1, ch, SSM_D_INNER), lambda b, c: (b, c, 0)),
            pl.BlockSpec((1, SSM_HEADS, SSM_HEAD_DIM, SSM_STATE), lambda b, c: (b, 0, 0, 0)),
        ],
        out_shape=[jax.ShapeDtypeStruct((bsz, seq, SSM_D_INNER), BF16),
                   jax.ShapeDtypeStruct((bsz, SSM_HEADS, SSM_HEAD_DIM, SSM_STATE), F32)],
        scratch_shapes=[pltpu.VMEM((ch, SSM_D_INNER), F32)],
        compiler_params=_cparams("parallel", "arbitrary"),
        name="ssd_prompt",
    )(main, main, main, tail, *small, h0)


def _row_to_col(row, eye):
    return jnp.sum(jnp.where(eye, row, 0.0), axis=1, keepdims=True)


def _col_to_row(col, eye):
    return jnp.sum(jnp.where(eye, col, 0.0), axis=0, keepdims=True)


def _conv_step(c0_ref, w_ref, x_row, c0, c1):
    acc = w_ref[SSM_CONV - 1:SSM_CONV, c0:c1] * x_row
    for i in range(SSM_CONV - 1):
        acc = acc + w_ref[i:i + 1, c0:c1] * c0_ref[0, i:i + 1, c0:c1]
    return acc


def _ssd_step_kernel(z_ref, xs_ref, bc_ref, dt_ref, c0_ref, cw_ref, cb_ref, dtb_ref, alog_ref, d_ref, nw_ref,
                     h0_ref, y_ref, co_ref, h_ref, ys_ref):
    xs_raw = xs_ref[0]
    bc_raw = bc_ref[0]
    xs = jax.nn.silu(_conv_step(c0_ref, cw_ref, xs_raw, 0, SSM_D_INNER) + cb_ref[:, 0:SSM_D_INNER])
    bc = jax.nn.silu(_conv_step(c0_ref, cw_ref, bc_raw, SSM_D_INNER, SSM_XBC) + cb_ref[:, SSM_D_INNER:SSM_XBC])
    co_ref[0, 0:SSM_CONV - 2, :] = c0_ref[0, 1:SSM_CONV - 1, :]
    co_ref[0, SSM_CONV - 2:SSM_CONV - 1, 0:SSM_D_INNER] = xs_raw
    co_ref[0, SSM_CONV - 2:SSM_CONV - 1, SSM_D_INNER:SSM_XBC] = bc_raw

    dt = jax.nn.softplus(dt_ref[0] + dtb_ref[...])
    dec = jnp.exp(dt * (-jnp.exp(alog_ref[...])))
    dskip = d_ref[...]
    eye = _iota((LANES, LANES), 0) == _iota((LANES, LANES), 1)
    upper = _iota((LANES, 1), 0) >= SSM_HEAD_DIM
    for j in range(SSM_HEADS // 2):
        h0i, h1i = 2 * j, 2 * j + 1
        g = h0i // (SSM_HEADS // SSM_GROUPS)
        bm = bc[:, g * SSM_STATE:(g + 1) * SSM_STATE]
        cm = bc[:, (SSM_GROUPS + g) * SSM_STATE:(SSM_GROUPS + g + 1) * SSM_STATE]
        x_row = xs[:, j * LANES:(j + 1) * LANES]
        x_col = _row_to_col(x_row, eye)
        pick = lambda v: jnp.where(upper, v[:, h1i:h1i + 1], v[:, h0i:h0i + 1])
        hp = h0_ref[0, h0i:h1i + 1].reshape(2 * SSM_HEAD_DIM, SSM_STATE)
        hn = hp * pick(dec) + (x_col * pick(dt)) * bm
        h_ref[0, h0i:h1i + 1] = hn.reshape(2, SSM_HEAD_DIM, SSM_STATE)
        y_col = jnp.sum(hn * cm, axis=1, keepdims=True) + pick(dskip) * x_col
        ys_ref[:, j * LANES:(j + 1) * LANES] = _col_to_row(y_col, eye)

    y = ys_ref[...] * jax.nn.silu(z_ref[0])
    half = SSM_D_INNER // SSM_GROUPS
    for g, yg in enumerate(_group_rms(y, nw_ref[...])):
        y_ref[0, :, g * half:(g + 1) * half] = yg


def ssd_step(main, tail, conv0, h0, conv_w, conv_b, dt_bias, a_log, d_skip, norm_w):
    bsz = main.shape[0]
    small = [conv_w, conv_b.reshape(1, -1), _pad_tail(dt_bias.reshape(1, -1)), _pad_tail(a_log.reshape(1, -1)),
             _pad_tail(d_skip.reshape(1, -1)), norm_w.reshape(1, -1)]
    hspec = pl.BlockSpec((1, SSM_HEADS, SSM_HEAD_DIM, SSM_STATE), lambda b: (b, 0, 0, 0))
    cspec = pl.BlockSpec((1, SSM_CONV - 1, SSM_XBC), lambda b: (b, 0, 0))
    return pl.pallas_call(
        _ssd_step_kernel,
        grid=(bsz,),
        in_specs=[
            pl.BlockSpec((1, 1, SSM_D_INNER), lambda b: (b, 0, 0)),
            pl.BlockSpec((1, 1, SSM_D_INNER), lambda b: (b, 0, 1)),
            pl.BlockSpec((1, 1, SSM_BC), lambda b: (b, 0, HYB_BC0 // SSM_BC)),
            pl.BlockSpec((1, 1, LANES), lambda b: (b, 0, 0)),
            cspec,
        ] + [_small(a) for a in small] + [hspec],
        out_specs=[pl.BlockSpec((1, 1, SSM_D_INNER), lambda b: (b, 0, 0)), cspec, hspec],
        out_shape=[jax.ShapeDtypeStruct((bsz, 1, SSM_D_INNER), F32),
                   jax.ShapeDtypeStruct(conv0.shape, F32),
                   jax.ShapeDtypeStruct(h0.shape, F32)],
        scratch_shapes=[pltpu.VMEM((1, SSM_D_INNER), F32)],
        compiler_params=_cparams("parallel"),
        name="ssd_step",
    )(main, main, main, tail, conv0, *small, h0)


def _l2norm(x):
    return x * lax.rsqrt(jnp.sum(x * x, axis=-1, keepdims=True) + EPS)


def _unit_lower_inverse(ns, eye):
    size = ns[0].shape[0]
    ps = [eye - n for n in ns]
    ms = [_dot(n, n) for n in ns]
    power = 2
    while 2 * power < size:
        pms = [_dot(jnp.concatenate([p, m], axis=0), m) for p, m in zip(ps, ms)]
        ps = [p + pm[:size] for p, pm in zip(ps, pms)]
        ms = [pm[size:] for pm in pms]
        power *= 2
    return [p + _dot(p, m) for p, m in zip(ps, ms)]


def _gdn_gates(ba, dtb_ref, alog_ref):
    beta = jax.nn.sigmoid(ba)
    g = -jnp.exp(alog_ref[...]) * jax.nn.softplus(ba + dtb_ref[...])
    return beta, g


def _gdn_kernel(q_ref, k_ref, v_ref, z_ref, ba_ref, dtb_ref, alog_ref, nw_ref, s0_ref, o_ref, s_ref):
    c = pl.program_id(1)
    ch = GDN_CHUNK
    nh = GDN_V_HEADS

    @pl.when(c == 0)
    def _():
        s_ref[...] = s0_ref[...]

    beta, g = _gdn_gates(ba_ref[0], dtb_ref, alog_ref)
    row = _iota((ch, ch), 0)
    colv = _iota((ch, ch), 1)
    incl = row >= colv
    strict = row > colv
    eye = (row == colv).astype(F32)
    gcum = _dot(incl.astype(F32), g, precision=HIGHEST)
    gcum_t = jnp.concatenate([gcum, jnp.zeros((LANES - ch, LANES), F32)], axis=0).T

    rep = nh // GDN_QK_HEADS
    for h0 in range(0, nh, GDN_HEAD_GROUP):
        heads = range(h0, h0 + GDN_HEAD_GROUP)
        qk_heads = range(h0 // rep, (h0 + GDN_HEAD_GROUP) // rep)
        qn = {j: _l2norm(q_ref[0, :, j * GDN_DK:(j + 1) * GDN_DK].astype(F32)) * (GDN_DK ** -0.5) for j in qk_heads}
        kn = {j: _l2norm(k_ref[0, :, j * GDN_DK:(j + 1) * GDN_DK].astype(F32)) for j in qk_heads}
        kk = {j: _dot_nt(kn[j], kn[j]) for j in qk_heads}
        qk = {j: _dot_nt(qn[j], kn[j]) for j in qk_heads}
        gc_col = {h: gcum[:, nh + h:nh + h + 1] for h in heads}
        gc_last = {h: gcum[ch - 1:ch, nh + h:nh + h + 1] for h in heads}
        beta_col = {h: beta[:, h:h + 1] for h in heads}
        dec = {h: jnp.exp(jnp.where(incl, gc_col[h] - gcum_t[nh + h:nh + h + 1, 0:ch], NEG)) for h in heads}
        t_inv = dict(zip(heads, _unit_lower_inverse(
            [jnp.where(strict, kk[h // rep] * dec[h], 0.0) * beta_col[h] for h in heads], eye)))
        eg = {h: jnp.exp(gc_col[h]) for h in heads}
        s_prev = {h: s_ref[0, h] for h in heads}
        both = {h: _dot(jnp.concatenate([kn[h // rep] * (beta_col[h] * eg[h]), qn[h // rep] * eg[h]], axis=0),
                        s_prev[h]) for h in heads}
        u = {h: _dot(t_inv[h], v_ref[0, :, h * GDN_DV:(h + 1) * GDN_DV].astype(F32) * beta_col[h] - both[h][:ch])
             for h in heads}
        o = {h: both[h][ch:] + _dot(qk[h // rep] * dec[h], u[h]) for h in heads}
        s_new = {h: s_prev[h] * jnp.exp(gc_last[h])
                 + _dot_tn(kn[h // rep] * jnp.exp(gc_last[h] - gc_col[h]), u[h]) for h in heads}
        for h in heads:
            s_ref[0, h] = s_new[h]
            z_h = z_ref[0, :, h * GDN_DV:(h + 1) * GDN_DV].astype(F32)
            o_ref[0, :, h * GDN_DV:(h + 1) * GDN_DV] = (
                _rms(o[h]) * nw_ref[...] * jax.nn.silu(z_h)).astype(o_ref.dtype)


def _gdn_gate_params(dt_bias, a_log):
    nh = GDN_V_HEADS
    dtb = jnp.zeros((1, LANES), F32).at[0, nh:2 * nh].set(dt_bias)
    alog = jnp.zeros((1, LANES), F32).at[0, nh:2 * nh].set(a_log)
    return dtb, alog


def gdn_prompt(main, tail, s0, dt_bias, a_log, norm_w, *, bsz, seq):
    ch = GDN_CHUNK
    nc = seq // ch
    dtb, alog = _gdn_gate_params(dt_bias, a_log)
    small = [dtb, alog, norm_w.reshape(1, -1)]
    sspec = pl.BlockSpec((1, GDN_V_HEADS, GDN_DK, GDN_DV), lambda b, c: (b, 0, 0, 0))
    return pl.pallas_call(
        _gdn_kernel,
        grid=(bsz, nc),
        in_specs=[
            pl.BlockSpec((1, ch, GDN_QK_W), lambda b, c: (b, c, 0)),
            pl.BlockSpec((1, ch, GDN_QK_W), lambda b, c: (b, c, 1)),
            pl.BlockSpec((1, ch, GDN_VW), lambda b, c: (b, c, 1)),
            pl.BlockSpec((1, ch, GDN_VW), lambda b, c: (b, c, 2)),
            pl.BlockSpec((1, ch, LANES), lambda b, c: (b, c, 0)),
        ] + [_small(a) for a in small] + [sspec],
        out_specs=[pl.BlockSpec((1, ch, GDN_VW), lambda b, c: (b, c, 0)), sspec],
        out_shape=[jax.ShapeDtypeStruct((bsz, seq, GDN_VW), BF16),
                   jax.ShapeDtypeStruct((bsz, GDN_V_HEADS, GDN_DK, GDN_DV), F32)],
        compiler_params=_cparams("parallel", "arbitrary"),
        name="gdn_prompt",
    )(main, main, main, main, tail, *small, s0)


def _gdn_conv_step(c0_ref, w_ref, x_row, c0, c1):
    acc = w_ref[GDN_CONV - 1:GDN_CONV, c0:c1] * x_row
    for i in range(GDN_CONV - 1):
        acc = acc + w_ref[i:i + 1, c0:c1] * c0_ref[0, i:i + 1, c0:c1]
    return acc


def _gdn_step_kernel(q_ref, k_ref, v_ref, z_ref, ba_ref, c0_ref, cw_ref, dtb_ref, alog_ref, nw_ref, s0_ref,
                     o_ref, co_ref, s_ref):
    nh = GDN_V_HEADS
    q_raw, k_raw, v_raw = q_ref[0], k_ref[0], v_ref[0]
    q = jax.nn.silu(_gdn_conv_step(c0_ref, cw_ref, q_raw, 0, GDN_QK_W))
    k = jax.nn.silu(_gdn_conv_step(c0_ref, cw_ref, k_raw, GDN_QK_W, 2 * GDN_QK_W))
    v = jax.nn.silu(_gdn_conv_step(c0_ref, cw_ref, v_raw, 2 * GDN_QK_W, GDN_QKV))
    co_ref[0, 0:GDN_CONV - 2, :] = c0_ref[0, 1:GDN_CONV - 1, :]
    co_ref[0, GDN_CONV - 2:GDN_CONV - 1, 0:GDN_QK_W] = q_raw
    co_ref[0, GDN_CONV - 2:GDN_CONV - 1, GDN_QK_W:2 * GDN_QK_W] = k_raw
    co_ref[0, GDN_CONV - 2:GDN_CONV - 1, 2 * GDN_QK_W:GDN_QKV] = v_raw

    beta, g = _gdn_gates(ba_ref[0], dtb_ref, alog_ref)
    eg_all = jnp.exp(g)
    eye = _iota((LANES, LANES), 0) == _iota((LANES, LANES), 1)
    for j in range(GDN_QK_HEADS):
        qn = _l2norm(q[:, j * GDN_DK:(j + 1) * GDN_DK]) * (GDN_DK ** -0.5)
        kn = _l2norm(k[:, j * GDN_DK:(j + 1) * GDN_DK])
        qk = jnp.sum(qn * kn, axis=-1, keepdims=True)
        q_col = _row_to_col(qn, eye)
        k_col = _row_to_col(kn, eye)
        for h in range(j * (nh // GDN_QK_HEADS), (j + 1) * (nh // GDN_QK_HEADS)):
            b_h = beta[:, h:h + 1]
            eg = eg_all[:, nh + h:nh + h + 1]
            s_prev = s0_ref[0, h]
            ks = jnp.sum(s_prev * k_col, axis=0, keepdims=True)
            qs = jnp.sum(s_prev * q_col, axis=0, keepdims=True)
            v_h = v[:, h * GDN_DV:(h + 1) * GDN_DV]
            u = v_h * b_h - (b_h * eg) * ks
            o = eg * qs + qk * u
            s_ref[0, h] = s_prev * eg + k_col * u
            z_h = z_ref[0, :, h * GDN_DV:(h + 1) * GDN_DV]
            o_ref[0, :, h * GDN_DV:(h + 1) * GDN_DV] = _rms(o) * nw_ref[...] * jax.nn.silu(z_h)


def gdn_step(main, tail, conv0, s0, conv_w, dt_bias, a_log, norm_w):
    bsz = main.shape[0]
    dtb, alog = _gdn_gate_params(dt_bias, a_log)
    small = [conv_w, dtb, alog, norm_w.reshape(1, -1)]
    sspec = pl.BlockSpec((1, GDN_V_HEADS, GDN_DK, GDN_DV), lambda b: (b, 0, 0, 0))
    cspec = pl.BlockSpec((1, GDN_CONV - 1, GDN_QKV), lambda b: (b, 0, 0))
    return pl.pallas_call(
        _gdn_step_kernel,
        grid=(bsz,),
        in_specs=[
            pl.BlockSpec((1, 1, GDN_QK_W), lambda b: (b, 0, 0)),
            pl.BlockSpec((1, 1, GDN_QK_W), lambda b: (b, 0, 1)),
            pl.BlockSpec((1, 1, GDN_VW), lambda b: (b, 0, 1)),
            pl.BlockSpec((1, 1, GDN_VW), lambda b: (b, 0, 2)),
            pl.BlockSpec((1, 1, LANES), lambda b: (b, 0, 0)),
            cspec,
        ] + [_small(a) for a in small] + [sspec],
        out_specs=[pl.BlockSpec((1, 1, GDN_VW), lambda b: (b, 0, 0)), cspec, sspec],
        out_shape=[jax.ShapeDtypeStruct((bsz, 1, GDN_VW), F32),
                   jax.ShapeDtypeStruct(conv0.shape, F32),
                   jax.ShapeDtypeStruct(s0.shape, F32)],
        compiler_params=_cparams("parallel"),
        name="gdn_step",
    )(main, main, main, main, tail, conv0, *small, s0)


def _pad_tail(w):
    return jnp.pad(w, ((0, 0), (0, LANES - w.shape[1])))


def _prep_hyb_in(w):
    a = A_WIDTH
    q, k, v = w[:, 0:a], w[:, a:2 * a], w[:, 2 * a:3 * a]
    z = w[:, 3 * a:3 * a + SSM_D_INNER]
    x0 = 3 * a + SSM_D_INNER
    xs = w[:, x0:x0 + SSM_D_INNER]
    bc = w[:, x0 + SSM_D_INNER:x0 + SSM_XBC]
    dt = w[:, x0 + SSM_XBC:]
    return jnp.concatenate([z, xs, q, k, v, bc], axis=1).astype(BF16), _pad_tail(dt).astype(BF16)


def _prep_gdn_in(w):
    return w[:, :GDN_MAIN].astype(BF16), _pad_tail(w[:, GDN_MAIN:]).astype(BF16)


HYB_CONV_COLS = ((SSM_D_INNER, 2 * SSM_D_INNER), (HYB_BC0, HYB_MAIN))


def _hyb_cols(a):
    out = jnp.zeros(a.shape[:-1] + (HYB_MAIN,), F32)
    (x0, x1), (b0, b1) = HYB_CONV_COLS
    return out.at[..., x0:x1].set(a[..., :SSM_D_INNER]).at[..., b0:b1].set(a[..., SSM_D_INNER:])


def _window_to_lanes(c):
    n, b, past, h, dh = c.shape
    return jnp.transpose(c, (0, 1, 3, 4, 2)).reshape(n, b, h * dh, past)


def _window_from_lanes(c):
    n, b, _, past = c.shape
    return jnp.transpose(c.reshape(n, b, A_HEADS, A_HEAD_DIM, past), (0, 1, 4, 2, 3))


def _row_tile(m, cap):
    return m if m <= cap else cap


def kernel(x_prompt, x_sample, cache_attn_k, cache_attn_v, state_ssm_conv, state_ssm, state_gdn_conv, state_gdn, rel_bias, norm_mix_pre, norm_mix_post, norm_ffn_pre, norm_ffn_post, w_hyb_in, ssm_conv_w, ssm_conv_b, ssm_dt_bias, ssm_a_log, ssm_d, ssm_norm_w, w_hyb_out, w_gdn_in, gdn_conv_w, gdn_dt_bias, gdn_a_log, gdn_norm_w, w_gdn_out, w_ffn_gate, w_ffn_up, w_ffn_down):
    depth = norm_mix_pre.shape[0]
    d_model = x_prompt.shape[-1]
    n_hyb, n_gdn = w_hyb_in.shape[0], w_gdn_in.shape[0]

    hyb_in = [_prep_hyb_in(w_hyb_in[i]) for i in range(n_hyb)]
    hyb_out = [(w_hyb_out[i, :A_WIDTH].astype(BF16), w_hyb_out[i, A_WIDTH:].astype(BF16)) for i in range(n_hyb)]
    gdn_in = [_prep_gdn_in(w_gdn_in[i]) for i in range(n_gdn)]
    gdn_out = [w_gdn_out[i].astype(BF16) for i in range(n_gdn)]
    ffn_w = [(w_ffn_gate[l].astype(BF16), w_ffn_up[l].astype(BF16), w_ffn_down[l].astype(BF16))
             for l in range(depth)]
    bias_tiles = _attn_bias_rows(rel_bias)

    def trunk(x3, k_pre, v_pre, sconv, sssm, gconv, gstate):
        bsz, seq, _ = x3.shape
        m = bsz * seq
        step = seq == 1
        tm_big = _row_tile(m, 1024)
        tm = _row_tile(m, 512)
        x = x3.reshape(m, d_model)
        nk, nv, nsc, nss, ngc, ngs = [], [], [], [], [], []
        rolled = None
        for l in range(depth):
            i = l // 2
            if l % 2 == 0:
                w_main, w_tail = hyb_in[i]
                ssm_args = (ssm_dt_bias[i], ssm_a_log[i], ssm_d[i], ssm_norm_w[i])
                if step:
                    main, tail = inproj(x, norm_mix_pre[l], w_main, w_tail, tm=tm_big, tn=512)
                    main3 = main.reshape(bsz, seq, HYB_MAIN)
                    tail3 = tail.reshape(bsz, seq, LANES)
                    col = lambda c0: main[:, c0:c0 + A_WIDTH].reshape(bsz, A_WIDTH, 1)
                    o_attn, *rolled = attention_step(
                        col(HYB_Q0), col(HYB_K0), col(HYB_V0), k_pre, v_pre, rolled,
                        _attn_logw(rel_bias, k_pre.shape[-1]), layer=i)
                    y, c_new, s_new = ssd_step(main3, tail3, sconv[i], sssm[i], ssm_conv_w[i], ssm_conv_b[i],
                                               *ssm_args)
                else:
                    main, tail, hist = inproj_conv(
                        x, norm_mix_pre[l], w_main, w_tail, _hyb_cols(ssm_conv_w[i]),
                        _hyb_cols(ssm_conv_b[i][None]), _hyb_cols(sconv[i]), seq=seq, conv_cols=HYB_CONV_COLS,
                        tm=tm_big, tn=512)
                    main3 = main.reshape(bsz, seq, HYB_MAIN)
                    tail3 = tail.reshape(bsz, seq, LANES)
                    c_new = jnp.concatenate([hist[..., a:b] for a, b in HYB_CONV_COLS], axis=-1)
                    o_attn = attention_prompt(main3, bias_tiles, bsz=bsz, seq=seq)
                    keep = min(A_PATTERNS[-1][0], seq)
                    k_new = main3[:, seq - keep:, HYB_K0:HYB_K0 + A_WIDTH].astype(F32)
                    v_new = main3[:, seq - keep:, HYB_V0:HYB_V0 + A_WIDTH].astype(F32)
                    y, s_new = ssd_prompt(main3, tail3, sssm[i], *ssm_args, bsz=bsz, seq=seq)
                    nk.append(k_new.reshape(bsz, -1, A_HEADS, A_HEAD_DIM))
                    nv.append(v_new.reshape(bsz, -1, A_HEADS, A_HEAD_DIM))
                nsc.append(c_new)
                nss.append(s_new)
                x = outproj([o_attn.reshape(m, A_WIDTH), y.reshape(m, SSM_D_INNER)], list(hyb_out[i]),
                            x, norm_mix_post[l], tm=tm)
            else:
                w_main, w_tail = gdn_in[i]
                gdn_args = (gdn_dt_bias[i], gdn_a_log[i], gdn_norm_w[i])
                if step:
                    main, tail = inproj(x, norm_mix_pre[l], w_main, w_tail, tm=tm_big, tn=512)
                    o, c_new, s_new = gdn_step(main.reshape(bsz, seq, GDN_MAIN), tail.reshape(bsz, seq, LANES),
                                               gconv[i], gstate[i], gdn_conv_w[i], *gdn_args)
                else:
                    pad = lambda a: jnp.pad(a, [(0, 0)] * (a.ndim - 1) + [(0, GDN_MAIN - GDN_QKV)])
                    main, tail, hist = inproj_conv(
                        x, norm_mix_pre[l], w_main, w_tail, pad(gdn_conv_w[i]), jnp.zeros((1, GDN_MAIN), F32),
                        pad(gconv[i]), seq=seq, conv_cols=((0, GDN_QKV),), tm=tm_big, tn=512)
                    c_new = hist[..., :GDN_QKV]
                    o, s_new = gdn_prompt(main.reshape(bsz, seq, GDN_MAIN), tail.reshape(bsz, seq, LANES),
                                          gstate[i], *gdn_args, bsz=bsz, seq=seq)
                ngc.append(c_new)
                ngs.append(s_new)
                x = outproj([o.reshape(m, GDN_VW)], [gdn_out[i]], x, norm_mix_post[l], tm=tm)
            wg, wu, wd = ffn_w[l]
            x = ffn(x, norm_ffn_pre[l], wg, wu, wd, norm_ffn_post[l], tm=tm)
        k_out, v_out = [_window_from_lanes(r) for r in rolled] if step else (jnp.stack(nk), jnp.stack(nv))
        return (x.reshape(bsz, seq, d_model), k_out, v_out, jnp.stack(nsc), jnp.stack(nss),
                jnp.stack(ngc), jnp.stack(ngs))

    bsz = x_prompt.shape[0]
    dt_p = x_prompt.dtype
    p_sc0 = jnp.zeros((n_hyb, bsz, SSM_CONV - 1, SSM_XBC), dt_p)
    p_ss0 = jnp.zeros((n_hyb, bsz, SSM_HEADS, SSM_HEAD_DIM, SSM_STATE), F32)
    p_gc0 = jnp.zeros((n_gdn, bsz, GDN_CONV - 1, GDN_QKV), dt_p)
    p_gs0 = jnp.zeros((n_gdn, bsz, GDN_V_HEADS, GDN_DK, GDN_DV), F32)
    y_prompt, pk, pv, psc, pss, pgc, pgs = trunk(x_prompt, None, None, p_sc0, p_ss0, p_gc0, p_gs0)
    y_sample, sk, sv, ssc, sss, sgc, sgs = trunk(
        x_sample, _window_to_lanes(cache_attn_k), _window_to_lanes(cache_attn_v), state_ssm_conv, state_ssm,
        state_gdn_conv, state_gdn)
    return (y_prompt, y_sample, pk, pv, psc, pss, pgc, pgs, sk, sv, ssc, sss, sgc, sgs)
```

```python
import functools
import math

import numpy as np
import jax
import jax.numpy as jnp
from jax import lax
from jax.experimental import pallas as pl
from jax.experimental.pallas import tpu as pltpu

F32 = jnp.float32
BF16 = jnp.bfloat16
EPS = 1e-6
NEG = -1e30
HIGHEST = lax.Precision.HIGHEST

VMEM_LIMIT_BYTES = 56 * 1024 * 1024
LANES = 128

A_HEADS = 8
A_HEAD_DIM = 64
A_WIDTH = A_HEADS * A_HEAD_DIM
A_PATTERNS = ((128, 1), (512, 4), (2048, 16))
A_BAND = 128
ATTN_GROUP = 4
REL_BUCKETS = 32
REL_MAX_DIST = 2048

SSM_D_INNER = 1024
SSM_HEAD_DIM = 64
SSM_HEADS = SSM_D_INNER // SSM_HEAD_DIM
SSM_GROUPS = 2
SSM_STATE = 128
SSM_CONV = 4
SSM_CHUNK = 128
SSM_BC = 2 * SSM_GROUPS * SSM_STATE
SSM_XBC = SSM_D_INNER + SSM_BC

GDN_QK_HEADS = 8
GDN_V_HEADS = 16
GDN_DK = 128
GDN_DV = 128
GDN_CONV = 4
GDN_CHUNK = 64
GDN_HEAD_GROUP = 16
GDN_QK_W = GDN_QK_HEADS * GDN_DK
GDN_VW = GDN_V_HEADS * GDN_DV
GDN_QKV = 2 * GDN_QK_W + GDN_VW

HYB_MAIN = 2 * SSM_D_INNER + 3 * A_WIDTH + SSM_BC
HYB_Q0 = 2 * SSM_D_INNER
HYB_K0 = HYB_Q0 + A_WIDTH
HYB_V0 = HYB_K0 + A_WIDTH
HYB_BC0 = HYB_V0 + A_WIDTH
GDN_MAIN = GDN_QKV + GDN_VW


def _cparams(*sem):
    return pltpu.CompilerParams(dimension_semantics=sem, vmem_limit_bytes=VMEM_LIMIT_BYTES)


def _rms(x):
    return x * lax.rsqrt(jnp.mean(x * x, axis=-1, keepdims=True) + EPS)


def _dot(a, b, **kw):
    return jnp.dot(a, b, preferred_element_type=F32, **kw)


def _dot_nt(a, b, **kw):
    return lax.dot_general(a, b, (((1,), (1,)), ((), ())), preferred_element_type=F32, **kw)


def _dot_tn(a, b, **kw):
    return lax.dot_general(a, b, (((0,), (0,)), ((), ())), preferred_element_type=F32, **kw)


def _iota(shape, dim):
    return lax.broadcasted_iota(jnp.int32, shape, dim)


def _inproj_kernel(x_ref, g_ref, w_ref, wt_ref, o_ref, t_ref, h_ref):
    @pl.when(pl.program_id(1) == 0)
    def _():
        hb = (_rms(x_ref[...]) * g_ref[...]).astype(BF16)
        h_ref[...] = hb
        t_ref[...] = _dot(hb, wt_ref[...])

    o_ref[...] = _dot(h_ref[...], w_ref[...])


def inproj(x, g, w_main, w_tail, *, tm, tn):
    m, d = x.shape
    n = w_main.shape[1]
    return pl.pallas_call(
        _inproj_kernel,
        grid=(m // tm, n // tn),
        in_specs=[
            pl.BlockSpec((tm, d), lambda i, j: (i, 0)),
            pl.BlockSpec((1, d), lambda i, j: (0, 0)),
            pl.BlockSpec((d, tn), lambda i, j: (0, j)),
            pl.BlockSpec((d, LANES), lambda i, j: (0, 0)),
        ],
        out_specs=[
            pl.BlockSpec((tm, tn), lambda i, j: (i, j)),
            pl.BlockSpec((tm, LANES), lambda i, j: (i, 0)),
        ],
        out_shape=[jax.ShapeDtypeStruct((m, n), F32), jax.ShapeDtypeStruct((m, LANES), F32)],
        scratch_shapes=[pltpu.VMEM((tm, d), BF16)],
        compiler_params=_cparams("parallel", "arbitrary"),
        name="inproj",
    )(x, g.reshape(1, d), w_main, w_tail)


CONV_TAPS = 4
CONV_ROWS = 64
PROJ_K_SLICES = 4
CONV_BASE = 8


def _inproj_conv_kernel(x_ref, g_ref, w_ref, wt_ref, cw_ref, cb_ref, c0_ref, o_ref, t_ref, so_ref,
                        h_ref, xp0_ref, xp1_ref, carry_ref, *, n_col, conv_tiles, tiles_per_seq):
    s = pl.program_id(0)
    n_tiles = pl.num_programs(0) - 1
    tm = o_ref.shape[0]
    lo = CONV_BASE - (CONV_TAPS - 1)
    cur = jnp.minimum(s, n_tiles - 1)
    prv = jnp.maximum(s - 1, 0)
    pi, pj = prv // n_col, prv % n_col
    prv_conv = functools.reduce(jnp.logical_or, [jnp.logical_and(pj >= a, pj < b) for a, b in conv_tiles])
    first = pi % tiles_per_seq == 0
    both = lambda a, b: jnp.logical_and(a, b)

    @pl.when(both(cur % n_col == 0, s < n_tiles))
    def _():
        hb = (_rms(x_ref[...]) * g_ref[...]).astype(BF16)
        h_ref[...] = hb
        t_ref[...] = _dot(hb, wt_ref[...])

    for parity, (cur_ref, prv_ref) in enumerate(((xp0_ref, xp1_ref), (xp1_ref, xp0_ref))):
        here = s % 2 == parity

        def project(cur_ref=cur_ref):
            cur_ref[CONV_BASE:CONV_BASE + tm, :] = _dot(h_ref[...], w_ref[...])

        def raw_tail(prv_ref=prv_ref):
            tail = prv_ref[CONV_BASE + tm - (CONV_TAPS - 1):CONV_BASE + tm, :]
            so_ref[0] = tail
            return tail

        if parity == 0:
            @pl.when(s == 0)
            def _():
                project()

        @pl.when(both(here, both(s > 0, jnp.logical_not(prv_conv))))
        def _():
            project()
            raw_tail()
            o_ref[...] = prv_ref[CONV_BASE:CONV_BASE + tm, :].astype(o_ref.dtype)

        @pl.when(both(here, both(both(s > 0, prv_conv), first)))
        def _():
            prv_ref[lo:CONV_BASE, :] = c0_ref[0]

        @pl.when(both(here, both(both(s > 0, prv_conv), jnp.logical_not(first))))
        def _():
            prv_ref[lo:CONV_BASE, :] = carry_ref[pj]

        @pl.when(both(here, both(s > 0, prv_conv)))
        def _():
            def conv_chunk(r0):
                ext = prv_ref[r0:r0 + CONV_BASE + CONV_ROWS, :]
                conv = cb_ref[...] + cw_ref[CONV_TAPS - 1:CONV_TAPS, :] * ext[CONV_BASE:]
                for back in range(1, CONV_TAPS):
                    tap = CONV_TAPS - 1 - back
                    conv = conv + cw_ref[tap:tap + 1, :] * pltpu.roll(ext, back, 0)[CONV_BASE:]
                o_ref[r0:r0 + CONV_ROWS, :] = jax.nn.silu(conv).astype(o_ref.dtype)

            d = h_ref.shape[1]
            n_chunks = tm // CONV_ROWS
            acc = None
            for kq in range(PROJ_K_SLICES):
                ks = slice(kq * d // PROJ_K_SLICES, (kq + 1) * d // PROJ_K_SLICES)
                part = _dot(h_ref[:, ks], w_ref[ks, :])
                acc = part if acc is None else acc + part
                for c in range(kq * n_chunks // PROJ_K_SLICES, (kq + 1) * n_chunks // PROJ_K_SLICES):
                    conv_chunk(c * CONV_ROWS)
            cur_ref[CONV_BASE:CONV_BASE + tm, :] = acc
            carry_ref[pj] = raw_tail()


def inproj_conv(x, g, w_main, w_tail, conv_w, conv_b, conv0, *, seq, conv_cols, tm, tn):
    m, d = x.shape
    n = w_main.shape[1]
    assert seq % tm == 0 and all(a % tn == 0 and b % tn == 0 for a, b in conv_cols)
    tiles_per_seq = seq // tm
    n_row, n_col = m // tm, n // tn
    n_tiles = n_row * n_col
    conv_tiles = tuple((a // tn, b // tn) for a, b in conv_cols)
    hist = CONV_TAPS - 1
    cur = lambda s: jnp.minimum(s, n_tiles - 1)
    prv = lambda s: jnp.maximum(s - 1, 0)
    main, tail, hist_rows = pl.pallas_call(
        functools.partial(_inproj_conv_kernel, n_col=n_col, conv_tiles=conv_tiles, tiles_per_seq=tiles_per_seq),
        grid=(n_tiles + 1,),
        in_specs=[
            pl.BlockSpec((tm, d), lambda s: (cur(s) // n_col, 0)),
            pl.BlockSpec((1, d), lambda s: (0, 0)),
            pl.BlockSpec((d, tn), lambda s: (0, cur(s) % n_col)),
            pl.BlockSpec((d, LANES), lambda s: (0, 0)),
            pl.BlockSpec((CONV_TAPS, tn), lambda s: (0, prv(s) % n_col)),
            pl.BlockSpec((1, tn), lambda s: (0, prv(s) % n_col)),
            pl.BlockSpec((1, hist, tn), lambda s: (prv(s) // n_col // tiles_per_seq, 0, prv(s) % n_col)),
        ],
        out_specs=[
            pl.BlockSpec((tm, tn), lambda s: (prv(s) // n_col, prv(s) % n_col)),
            pl.BlockSpec((tm, LANES), lambda s: (cur(s) // n_col, 0)),
            pl.BlockSpec((1, hist, tn), lambda s: (prv(s) // n_col, 0, prv(s) % n_col)),
        ],
        out_shape=[jax.ShapeDtypeStruct((m, n), BF16), jax.ShapeDtypeStruct((m, LANES), F32),
                   jax.ShapeDtypeStruct((n_row, hist, n), F32)],
        scratch_shapes=[pltpu.VMEM((tm, d), BF16), pltpu.VMEM((CONV_BASE + tm, tn), F32),
                        pltpu.VMEM((CONV_BASE + tm, tn), F32), pltpu.VMEM((n_col, hist, tn), F32)],
        compiler_params=_cparams("arbitrary"),
        name="inproj_conv",
    )(x, g.reshape(1, d), w_main, w_tail, conv_w, conv_b, conv0)
    return main, tail, hist_rows[tiles_per_seq - 1::tiles_per_seq]


def _outproj_kernel(*refs, n_in):
    a_refs, w_refs = refs[:n_in], refs[n_in:2 * n_in]
    x_ref, g_ref, o_ref = refs[2 * n_in:]
    acc = None
    for a_ref, w_ref in zip(a_refs, w_refs):
        t = _dot(a_ref[...].astype(BF16), w_ref[...])
        acc = t if acc is None else acc + t
    o_ref[...] = x_ref[...] + _rms(acc) * g_ref[...]


def outproj(acts, weights, x, g, *, tm):
    m, d = x.shape
    n_in = len(acts)
    in_specs = [pl.BlockSpec((tm, a.shape[1]), lambda i: (i, 0)) for a in acts]
    in_specs += [pl.BlockSpec(w.shape, lambda i: (0, 0)) for w in weights]
    in_specs += [pl.BlockSpec((tm, d), lambda i: (i, 0)), pl.BlockSpec((1, d), lambda i: (0, 0))]
    return pl.pallas_call(
        functools.partial(_outproj_kernel, n_in=n_in),
        grid=(m // tm,),
        in_specs=in_specs,
        out_specs=pl.BlockSpec((tm, d), lambda i: (i, 0)),
        out_shape=jax.ShapeDtypeStruct((m, d), F32),
        compiler_params=_cparams("parallel"),
        name="outproj",
    )(*acts, *weights, x, g.reshape(1, d))


def _ffn_kernel(x_ref, g1_ref, wg_ref, wu_ref, wd_ref, g2_ref, o_ref):
    x = x_ref[...]
    h = (_rms(x) * g1_ref[...]).astype(BF16)
    a = jax.nn.silu(_dot(h, wg_ref[...])) * _dot(h, wu_ref[...])
    o_ref[...] = x + _rms(_dot(a.astype(BF16), wd_ref[...])) * g2_ref[...]


def ffn(x, g1, wg, wu, wd, g2, *, tm):
    m, d = x.shape
    resident = lambda a: pl.BlockSpec(a.shape, lambda i: (0, 0), pipeline_mode=pl.Buffered(1))
    return pl.pallas_call(
        _ffn_kernel,
        grid=(m // tm,),
        in_specs=[
            pl.BlockSpec((tm, d), lambda i: (i, 0)),
            pl.BlockSpec((1, d), lambda i: (0, 0)),
            resident(wg), resident(wu), resident(wd),
            pl.BlockSpec((1, d), lambda i: (0, 0)),
        ],
        out_specs=pl.BlockSpec((tm, d), lambda i: (i, 0)),
        out_shape=jax.ShapeDtypeStruct((m, d), F32),
        compiler_params=_cparams("parallel"),
        name="ffn",
    )(x, g1.reshape(1, d), wg, wu, wd, g2.reshape(1, d))


def _rel_buckets(dist):
    max_exact = REL_BUCKETS // 2
    n = np.maximum(dist, 1).astype(np.float32)
    large = max_exact + (np.log(n / max_exact) / math.log(REL_MAX_DIST / max_exact)
                         * (REL_BUCKETS - max_exact)).astype(np.int32)
    large = np.minimum(large, REL_BUCKETS - 1)
    return np.where(dist < max_exact, dist, large).astype(np.int32)


def _attn_bias_rows(rel_bias):
    u = np.arange(2 * A_BAND)
    valid = u <= A_BAND
    rows = []
    for (_, d) in A_PATTERNS:
        b = rel_bias[_rel_buckets(np.where(valid, A_BAND - u, 0) * d)]
        rows.append(jnp.where(valid[:, None], b.astype(F32), NEG))
    tl = jnp.transpose(jnp.stack(rows), (2, 0, 1))
    tl = tl.reshape(A_HEADS // 2, 2, len(A_PATTERNS), 2 * A_BAND)
    tl = jnp.transpose(tl, (0, 2, 1, 3))[:, :, :, None, :]
    return jnp.broadcast_to(tl, tl.shape[:3] + (8, 2 * A_BAND))


def _attn_kernel(qin_ref, kin_ref, vin_ref, brow_ref, o_ref, m0_ref, m1_ref, l0_ref, l1_ref, acc_ref,
                 q_ref, k_ref, v_ref, b_ref, *, seq):
    n_tiles = seq // A_BAND
    lane = _iota((A_BAND, LANES), 1)
    head0 = lane < A_HEAD_DIM
    m_refs, l_refs = (m0_ref, m1_ref), (l0_ref, l1_ref)
    q_ref[...] = qin_ref[...].astype(F32) * (A_HEAD_DIM ** -0.5)
    k_ref[...] = kin_ref[...].astype(F32)
    v_ref[...] = vin_ref[...].astype(F32)
    for p in range(len(A_PATTERNS)):
        for h in range(2):
            row = jnp.broadcast_to(brow_ref[0, p, h, 0:1, :], (A_BAND, 2 * A_BAND))
            b_ref[0, p, h] = pltpu.roll(row, 0, 1, stride=1, stride_axis=0)
    for h in range(2):
        m_refs[h][...] = jnp.full(m_refs[h].shape, NEG, F32)
        l_refs[h][...] = jnp.zeros_like(l_refs[h])
    acc_ref[...] = jnp.zeros_like(acc_ref)

    for p, (_, d) in enumerate(A_PATTERNS):
        tiles_per_class = n_tiles // d

        def load_tile(idx, d=d, tiles_per_class=tiles_per_class):
            r = idx // tiles_per_class
            t = idx % tiles_per_class
            start = r + t * (d * A_BAND)
            has_prev = t > 0
            prev = jnp.where(has_prev, start - d * A_BAND, start)
            if d > 1:
                rows, prows = pl.ds(start, A_BAND, stride=d), pl.ds(prev, A_BAND, stride=d)
            else:
                rows, prows = pl.ds(pl.multiple_of(start, A_BAND), A_BAND), pl.ds(pl.multiple_of(prev, A_BAND), A_BAND)
            return dict(
                rows=rows, has_prev=has_prev, q=q_ref[rows, :],
                k2=jnp.concatenate([k_ref[prows, :], k_ref[rows, :]], axis=0).astype(BF16),
                v2=jnp.concatenate([v_ref[prows, :], v_ref[rows, :]], axis=0).astype(BF16),
                acc=acc_ref[rows, :], m=[m_refs[h][rows, :] for h in range(2)],
                l=[l_refs[h][rows, :] for h in range(2)])

        def tile_group(idx, carry, load_tile=load_tile, p=p):
            tiles = [load_tile(idx + i * (n_tiles // ATTN_GROUP)) for i in range(ATTN_GROUP)]
            chains = [(tile, h) for tile in tiles for h in range(2)]
            col = _iota((A_BAND, 2 * A_BAND), 1)
            qh = [jnp.where(head0 if h == 0 else jnp.logical_not(head0), tile["q"], 0.0).astype(BF16)
                  for tile, h in chains]
            s = [_dot_nt(qh[c], tile["k2"])
                 + jnp.where(jnp.logical_and(col < A_BAND, jnp.logical_not(tile["has_prev"])), NEG, b_ref[0, p, h])
                 for c, (tile, h) in enumerate(chains)]
            m_new = [jnp.maximum(tile["m"][h], jnp.max(s[c], axis=-1, keepdims=True))
                     for c, (tile, h) in enumerate(chains)]
            alpha = [jnp.exp(tile["m"][h] - m_new[c]) for c, (tile, h) in enumerate(chains)]
            pr = [jnp.exp(s[c] - jnp.concatenate([m_new[c], m_new[c]], axis=1)) for c in range(len(chains))]
            l_new = [alpha[c] * tile["l"][h] + jnp.sum(pr[c], axis=-1, keepdims=True)
                     for c, (tile, h) in enumerate(chains)]
            acc_new = [alpha[c] * tile["acc"] + _dot(pr[c].astype(BF16), tile["v2"])
                       for c, (tile, h) in enumerate(chains)]
            for c, (tile, h) in enumerate(chains):
                m_refs[h][tile["rows"], :] = m_new[c]
                l_refs[h][tile["rows"], :] = l_new[c]
                if h == 1:
                    acc_ref[tile["rows"], :] = jnp.where(head0, acc_new[c - 1], acc_new[c])
            return carry

        lax.fori_loop(0, n_tiles // ATTN_GROUP, tile_group, 0)

    lane_s = _iota((seq, LANES), 1)
    o_ref[...] = (acc_ref[...] / jnp.where(lane_s < A_HEAD_DIM, l0_ref[...], l1_ref[...])).astype(o_ref.dtype)


def attention_prompt(proj, bias_tiles, *, bsz, seq):
    hp = A_HEADS // 2
    qb, kb, vb = HYB_Q0 // LANES, HYB_K0 // LANES, HYB_V0 // LANES
    return pl.pallas_call(
        functools.partial(_attn_kernel, seq=seq),
        grid=(bsz, hp),
        in_specs=[
            pl.BlockSpec((None, seq, LANES), lambda b, h: (b, 0, qb + h)),
            pl.BlockSpec((None, seq, LANES), lambda b, h: (b, 0, kb + h)),
            pl.BlockSpec((None, seq, LANES), lambda b, h: (b, 0, vb + h)),
            pl.BlockSpec((1,) + bias_tiles.shape[1:], lambda b, h: (h, 0, 0, 0, 0)),
        ],
        out_specs=pl.BlockSpec((None, seq, LANES), lambda b, h: (b, 0, h)),
        out_shape=jax.ShapeDtypeStruct((bsz, seq, A_WIDTH), BF16),
        scratch_shapes=[pltpu.VMEM((seq, LANES), F32)] * 8
        + [pltpu.VMEM((1, len(A_PATTERNS), 2, A_BAND, 2 * A_BAND), F32)],
        compiler_params=_cparams("parallel", "parallel"),
        name="attn_prompt",
    )(proj, proj, proj, bias_tiles)


def _attn_logw(rel_bias, past):
    dist = np.arange(past + 1)
    count = np.zeros(past + 1, np.float64)
    for (w, d) in A_PATTERNS:
        count += ((dist % d == 0) & (dist <= w)).astype(np.float64)
    logc = np.where(count > 0, np.log(np.maximum(count, 1.0)), 0.0).astype(np.float32)
    lw = rel_bias[_rel_buckets(dist)].astype(F32).T + logc[None, :]
    return jnp.where((count > 0)[None, :], lw, NEG)


def _step_scores(xk_ref, q_col, kn_col, lw_ref, lw0_ref, s_ref):
    qs = q_col * (A_HEAD_DIM ** -0.5)
    s_new = []
    for h in range(A_HEADS):
        rows = slice(h * A_HEAD_DIM, (h + 1) * A_HEAD_DIM)
        s_ref[h:h + 1, :] = jnp.sum(xk_ref[0, 0, rows, :] * qs[rows], axis=0, keepdims=True)
        s_new.append(jnp.sum(kn_col[rows] * qs[rows], axis=0, keepdims=True))
    s = s_ref[...] + lw_ref[...]
    s_new = jnp.concatenate(s_new, axis=0) + lw0_ref[:, :1]
    m = jnp.maximum(jnp.max(s, axis=-1, keepdims=True), s_new)
    p = jnp.exp(s - m)
    p_new = jnp.exp(s_new - m)
    den = jnp.sum(p, axis=-1, keepdims=True) + p_new
    return p, p_new, den


def _step_output(xv_ref, vn_col, p, p_new, den, o_ref):
    for h in range(A_HEADS):
        rows = slice(h * A_HEAD_DIM, (h + 1) * A_HEAD_DIM)
        pv = jnp.sum(xv_ref[0, 0, rows, :] * p[h:h + 1, :], axis=-1, keepdims=True)
        o_ref[0, rows, :] = (pv + p_new[h:h + 1, :] * vn_col[rows]) / den[h:h + 1, :]


def _attn_step_roll_kernel(q_ref, kn_ref, vn_ref, lw_ref, lw0_ref, xk_ref, xv_ref, o_ref, ko_ref, vo_ref, s_ref,
                           *, layer, past):
    is_layer = pl.program_id(0) == layer
    newest = _iota((A_HEAD_DIM, past), 1) == past - 1
    for x_ref, n_ref, out_ref in ((xk_ref, kn_ref, ko_ref), (xv_ref, vn_ref, vo_ref)):
        for h in range(A_HEADS):
            rows = slice(h * A_HEAD_DIM, (h + 1) * A_HEAD_DIM)
            rolled = pltpu.roll(x_ref[0, 0, rows, :], past - 1, 1)
            out_ref[0, 0, rows, :] = jnp.where(jnp.logical_and(newest, is_layer), n_ref[0, rows, :], rolled)

    @pl.when(is_layer)
    def _():
        p, p_new, den = _step_scores(xk_ref, q_ref[0], kn_ref[0], lw_ref, lw0_ref, s_ref)
        _step_output(xv_ref, vn_ref[0], p, p_new, den, o_ref.at[0])

    @pl.when(jnp.logical_not(is_layer))
    def _():
        o_ref[...] = jnp.zeros_like(o_ref)


def _attn_step_append_kernel(q_ref, kn_ref, vn_ref, lw_ref, lw0_ref, xk_ref, xv_ref, ko_in, vo_in,
                             o_ref, ko_ref, vo_ref, s_ref, *, past):
    del ko_in, vo_in
    p, p_new, den = _step_scores(xk_ref, q_ref[0], kn_ref[0], lw_ref, lw0_ref, s_ref)
    _step_output(xv_ref, vn_ref[0], p, p_new, den, o_ref)
    newest = _iota((A_WIDTH, LANES), 1) == LANES - 1
    for x_ref, n_ref, out_ref in ((xk_ref, kn_ref, ko_ref), (xv_ref, vn_ref, vo_ref)):
        rolled = pltpu.roll(x_ref[0, 0, :, past - LANES:past], LANES - 1, 1)
        out_ref[0, 0] = jnp.where(newest, n_ref[0], rolled)


def attention_step(q_col, kn_col, vn_col, cache_k, cache_v, rolled, logw, *, layer):
    n_layers, bsz, w, past = cache_k.shape
    lw_cache = logw[:, past:0:-1]
    lw_new = jnp.broadcast_to(logw[:, :1], (A_HEADS, LANES))
    out_shape = [jax.ShapeDtypeStruct((bsz, w, 1), F32),
                 jax.ShapeDtypeStruct(cache_k.shape, cache_k.dtype),
                 jax.ShapeDtypeStruct(cache_v.shape, cache_v.dtype)]
    scratch = [pltpu.VMEM((A_HEADS, past), F32)]
    if rolled is None:
        assert layer == 0
        col = pl.BlockSpec((1, w, 1), lambda l, b: (b, 0, 0))
        win = pl.BlockSpec((1, 1, w, past), lambda l, b: (l, b, 0, 0))
        o_all, rolled_k, rolled_v = pl.pallas_call(
            functools.partial(_attn_step_roll_kernel, layer=layer, past=past),
            grid=(n_layers, bsz),
            in_specs=[col, col, col, _small(lw_cache), _small(lw_new), win, win],
            out_specs=[pl.BlockSpec((1, 1, w, 1), lambda l, b: (l, b, 0, 0)), win, win],
            out_shape=[jax.ShapeDtypeStruct((n_layers, bsz, w, 1), F32)] + out_shape[1:],
            scratch_shapes=scratch,
            compiler_params=_cparams("arbitrary", "arbitrary"),
            name="attn_step_roll",
        )(q_col, kn_col, vn_col, lw_cache, lw_new, cache_k, cache_v)
        return o_all[layer], rolled_k, rolled_v
    col = pl.BlockSpec((1, w, 1), lambda b: (b, 0, 0))
    win = pl.BlockSpec((1, 1, w, past), lambda b: (layer, b, 0, 0))
    tail = pl.BlockSpec((1, 1, w, LANES), lambda b: (layer, b, 0, past // LANES - 1))
    return pl.pallas_call(
        functools.partial(_attn_step_append_kernel, past=past),
        grid=(bsz,),
        in_specs=[col, col, col, _small(lw_cache), _small(lw_new), win, win, tail, tail],
        out_specs=[col, tail, tail],
        out_shape=out_shape,
        scratch_shapes=scratch,
        input_output_aliases={7: 1, 8: 2},
        compiler_params=_cparams("arbitrary"),
        name="attn_step_append",
    )(q_col, kn_col, vn_col, lw_cache, lw_new, cache_k, cache_v, *rolled)


def _group_rms(y, w):
    half = SSM_D_INNER // SSM_GROUPS
    return [_rms(y[:, g * half:(g + 1) * half]) * w[:, g * half:(g + 1) * half] for g in range(SSM_GROUPS)]


def _ssd_kernel(z_ref, xs_ref, bc_ref, dt_ref, dtb_ref, alog_ref, d_ref, nw_ref, h0_ref, y_ref, h_ref, ys_ref):
    c = pl.program_id(1)
    ch = SSM_CHUNK

    @pl.when(c == 0)
    def _():
        h_ref[...] = h0_ref[...]

    xs = xs_ref[0].astype(F32)
    bc = bc_ref[0].astype(F32)

    dt = jax.nn.softplus(dt_ref[0] + dtb_ref[...])
    da = dt * (-jnp.exp(alog_ref[...]))
    row = _iota((ch, ch), 0)
    colv = _iota((ch, ch), 1)
    tril = (row >= colv).astype(F32)
    cs = _dot(tril, da, precision=HIGHEST)
    cs_t = cs.T
    causal = row >= colv

    heads = range(SSM_HEADS)
    hpg = SSM_HEADS // SSM_GROUPS
    bm = [bc[:, g * SSM_STATE:(g + 1) * SSM_STATE] for g in range(SSM_GROUPS)]
    cm = [bc[:, (SSM_GROUPS + g) * SSM_STATE:(SSM_GROUPS + g + 1) * SSM_STATE] for g in range(SSM_GROUPS)]
    cb = [_dot_nt(cm[g], bm[g]) for g in range(SSM_GROUPS)]
    dt_t = dt.T
    xs_t = xs.T
    w_t = dt_t * jnp.exp(cs_t[:, ch - 1:ch] - cs_t)
    e_last = jnp.exp(cs[ch - 1:ch, :])
    lane_lo = _iota((ch, LANES), 1) < SSM_HEAD_DIM
    row_lo = _iota((LANES, ch), 0) < SSM_HEAD_DIM
    pairs = range(SSM_HEADS // 2)
    csb = [jnp.broadcast_to(cs[:, h:h + 1], (ch, ch)) for h in heads]
    mix = [cb[h // hpg] * jnp.exp(jnp.where(causal, csb[h] - cs_t[h:h + 1, :], NEG)) * dt_t[h:h + 1, :]
           for h in heads]
    x_pair = [xs[:, j * LANES:(j + 1) * LANES] for j in pairs]
    y_intra = [jnp.where(lane_lo, _dot(mix[2 * j], x_pair[j]), _dot(mix[2 * j + 1], x_pair[j])) for j in pairs]
    h_pair = [h_ref[0, 2 * j:2 * j + 2].reshape(2 * SSM_HEAD_DIM, SSM_STATE) for j in pairs]
    y_inter = [_dot_nt(cm[2 * j // hpg], h_pair[j]) * jnp.exp(jnp.where(lane_lo, csb[2 * j], csb[2 * j + 1]))
               for j in pairs]
    xw_t = [xs_t[j * LANES:(j + 1) * LANES, :] * jnp.where(row_lo, w_t[2 * j:2 * j + 1, :], w_t[2 * j + 1:2 * j + 2, :])
            for j in pairs]
    st = [_dot(xw_t[j], bm[2 * j // hpg]) for j in pairs]
    for j in pairs:
        cols = slice(j * LANES, (j + 1) * LANES)
        ys_ref[:, cols] = y_intra[j] + y_inter[j] + d_ref[:, cols] * x_pair[j]
        decay = jnp.where(row_lo, e_last[:, 2 * j:2 * j + 1], e_last[:, 2 * j + 1:2 * j + 2])
        h_ref[0, 2 * j:2 * j + 2] = (h_pair[j] * decay + st[j]).reshape(2, SSM_HEAD_DIM, SSM_STATE)

    y = ys_ref[...] * jax.nn.silu(z_ref[0].astype(F32))
    half = SSM_D_INNER // SSM_GROUPS
    for g, yg in enumerate(_group_rms(y, nw_ref[...])):
        y_ref[0, :, g * half:(g + 1) * half] = yg.astype(y_ref.dtype)


def _small(a):
    return pl.BlockSpec(a.shape, lambda *_: (0,) * a.ndim)


def ssd_prompt(main, tail, h0, dt_bias, a_log, d_skip, norm_w, *, bsz, seq):
    nc = seq // SSM_CHUNK
    ch = SSM_CHUNK
    small = [_pad_tail(dt_bias.reshape(1, -1)), _pad_tail(a_log.reshape(1, -1)),
             jnp.repeat(d_skip, SSM_HEAD_DIM).reshape(1, -1), norm_w.reshape(1, -1)]
    return pl.pallas_call(
        _ssd_kernel,
        grid=(bsz, nc),
        in_specs=[
            pl.BlockSpec((1, ch, SSM_D_INNER), lambda b, c: (b, c, 0)),
            pl.BlockSpec((1, ch, SSM_D_INNER), lambda b, c: (b, c, 1)),
            pl.BlockSpec((1, ch, SSM_BC), lambda b, c: (b, c, HYB_BC0 // SSM_BC)),
            pl.BlockSpec((1, ch, LANES), lambda b, c: (b, c, 0)),
        ] + [_small(a) for a in small] + [
            pl.BlockSpec((1, SSM_HEADS, SSM_HEAD_DIM, SSM_STATE), lambda b, c: (b, 0, 0, 0)),
        ],
        out_specs=[
            pl.BlockSpec((1, ch, SSM_D_INNER), lambda b, c: (b, c, 0)),
            pl.BlockSpec((1, SSM_HEADS, SSM_HEAD_DIM, SSM_STATE), lambda b, c: (b, 0, 0, 0)),
        ],
        out_shape=[jax.ShapeDtypeStruct((bsz, seq, SSM_D_INNER), BF16),
                   jax.ShapeDtypeStruct((bsz, SSM_HEADS, SSM_HEAD_DIM, SSM_STATE), F32)],
        scratch_shapes=[pltpu.VMEM((ch, SSM_D_INNER), F32)],
        compiler_params=_cparams("parallel", "arbitrary"),
        name="ssd_prompt",
    )(main, main, main, tail, *small, h0)


def _row_to_col(row, eye):
    return jnp.sum(jnp.where(eye, row, 0.0), axis=1, keepdims=True)


def _col_to_row(col, eye):
    return jnp.sum(jnp.where(eye, col, 0.0), axis=0, keepdims=True)


def _conv_step(c0_ref, w_ref, x_row, c0, c1):
    acc = w_ref[SSM_CONV - 1:SSM_CONV, c0:c1] * x_row
    for i in range(SSM_CONV - 1):
        acc = acc + w_ref[i:i + 1, c0:c1] * c0_ref[0, i:i + 1, c0:c1]
    return acc


def _ssd_step_kernel(z_ref, xs_ref, bc_ref, dt_ref, c0_ref, cw_ref, cb_ref, dtb_ref, alog_ref, d_ref, nw_ref,
                     h0_ref, y_ref, co_ref, h_ref, ys_ref):
    xs_raw = xs_ref[0]
    bc_raw = bc_ref[0]
    xs = jax.nn.silu(_conv_step(c0_ref, cw_ref, xs_raw, 0, SSM_D_INNER) + cb_ref[:, 0:SSM_D_INNER])
    bc = jax.nn.silu(_conv_step(c0_ref, cw_ref, bc_raw, SSM_D_INNER, SSM_XBC) + cb_ref[:, SSM_D_INNER:SSM_XBC])
    co_ref[0, 0:SSM_CONV - 2, :] = c0_ref[0, 1:SSM_CONV - 1, :]
    co_ref[0, SSM_CONV - 2:SSM_CONV - 1, 0:SSM_D_INNER] = xs_raw
    co_ref[0, SSM_CONV - 2:SSM_CONV - 1, SSM_D_INNER:SSM_XBC] = bc_raw

    dt = jax.nn.softplus(dt_ref[0] + dtb_ref[...])
    dec = jnp.exp(dt * (-jnp.exp(alog_ref[...])))
    dskip = d_ref[...]
    eye = _iota((LANES, LANES), 0) == _iota((LANES, LANES), 1)
    upper = _iota((LANES, 1), 0) >= SSM_HEAD_DIM
    for j in range(SSM_HEADS // 2):
        h0i, h1i = 2 * j, 2 * j + 1
        g = h0i // (SSM_HEADS // SSM_GROUPS)
        bm = bc[:, g * SSM_STATE:(g + 1) * SSM_STATE]
        cm = bc[:, (SSM_GROUPS + g) * SSM_STATE:(SSM_GROUPS + g + 1) * SSM_STATE]
        x_row = xs[:, j * LANES:(j + 1) * LANES]
        x_col = _row_to_col(x_row, eye)
        pick = lambda v: jnp.where(upper, v[:, h1i:h1i + 1], v[:, h0i:h0i + 1])
        hp = h0_ref[0, h0i:h1i + 1].reshape(2 * SSM_HEAD_DIM, SSM_STATE)
        hn = hp * pick(dec) + (x_col * pick(dt)) * bm
        h_ref[0, h0i:h1i + 1] = hn.reshape(2, SSM_HEAD_DIM, SSM_STATE)
        y_col = jnp.sum(hn * cm, axis=1, keepdims=True) + pick(dskip) * x_col
        ys_ref[:, j * LANES:(j + 1) * LANES] = _col_to_row(y_col, eye)

    y = ys_ref[...] * jax.nn.silu(z_ref[0])
    half = SSM_D_INNER // SSM_GROUPS
    for g, yg in enumerate(_group_rms(y, nw_ref[...])):
        y_ref[0, :, g * half:(g + 1) * half] = yg


def ssd_step(main, tail, conv0, h0, conv_w, conv_b, dt_bias, a_log, d_skip, norm_w):
    bsz = main.shape[0]
    small = [conv_w, conv_b.reshape(1, -1), _pad_tail(dt_bias.reshape(1, -1)), _pad_tail(a_log.reshape(1, -1)),
             _pad_tail(d_skip.reshape(1, -1)), norm_w.reshape(1, -1)]
    hspec = pl.BlockSpec((1, SSM_HEADS, SSM_HEAD_DIM, SSM_STATE), lambda b: (b, 0, 0, 0))
    cspec = pl.BlockSpec((1, SSM_CONV - 1, SSM_XBC), lambda b: (b, 0, 0))
    return pl.pallas_call(
        _ssd_step_kernel,
        grid=(bsz,),
        in_specs=[
            pl.BlockSpec((1, 1, SSM_D_INNER), lambda b: (b, 0, 0)),
            pl.BlockSpec((1, 1, SSM_D_INNER), lambda b: (b, 0, 1)),
            pl.BlockSpec((1, 1, SSM_BC), lambda b: (b, 0, HYB_BC0 // SSM_BC)),
            pl.BlockSpec((1, 1, LANES), lambda b: (b, 0, 0)),
            cspec,
        ] + [_small(a) for a in small] + [hspec],
        out_specs=[pl.BlockSpec((1, 1, SSM_D_INNER), lambda b: (b, 0, 0)), cspec, hspec],
        out_shape=[jax.ShapeDtypeStruct((bsz, 1, SSM_D_INNER), F32),
                   jax.ShapeDtypeStruct(conv0.shape, F32),
                   jax.ShapeDtypeStruct(h0.shape, F32)],
        scratch_shapes=[pltpu.VMEM((1, SSM_D_INNER), F32)],
        compiler_params=_cparams("parallel"),
        name="ssd_step",
    )(main, main, main, tail, conv0, *small, h0)


def _l2norm(x):
    return x * lax.rsqrt(jnp.sum(x * x, axis=-1, keepdims=True) + EPS)


def _unit_lower_inverse(ns, eye):
    size = ns[0].shape[0]
    ps = [eye - n for n in ns]
    ms = [_dot(n, n) for n in ns]
    power = 2
    while 2 * power < size:
        pms = [_dot(jnp.concatenate([p, m], axis=0), m) for p, m in zip(ps, ms)]
        ps = [p + pm[:size] for p, pm in zip(ps, pms)]
        ms = [pm[size:] for pm in pms]
        power *= 2
    return [p + _dot(p, m) for p, m in zip(ps, ms)]


def _gdn_gates(ba, dtb_ref, alog_ref):
    beta = jax.nn.sigmoid(ba)
    g = -jnp.exp(alog_ref[...]) * jax.nn.softplus(ba + dtb_ref[...])
    return beta, g


def _gdn_kernel(q_ref, k_ref, v_ref, z_ref, ba_ref, dtb_ref, alog_ref, nw_ref, s0_ref, o_ref, s_ref):
    c = pl.program_id(1)
    ch = GDN_CHUNK
    nh = GDN_V_HEADS

    @pl.when(c == 0)
    def _():
        s_ref[...] = s0_ref[...]

    beta, g = _gdn_gates(ba_ref[0], dtb_ref, alog_ref)
    row = _iota((ch, ch), 0)
    colv = _iota((ch, ch), 1)
    incl = row >= colv
    strict = row > colv
    eye = (row == colv).astype(F32)
    gcum = _dot(incl.astype(F32), g, precision=HIGHEST)
    gcum_t = jnp.concatenate([gcum, jnp.zeros((LANES - ch, LANES), F32)], axis=0).T

    rep = nh // GDN_QK_HEADS
    for h0 in range(0, nh, GDN_HEAD_GROUP):
        heads = range(h0, h0 + GDN_HEAD_GROUP)
        qk_heads = range(h0 // rep, (h0 + GDN_HEAD_GROUP) // rep)
        qn = {j: _l2norm(q_ref[0, :, j * GDN_DK:(j + 1) * GDN_DK].astype(F32)) * (GDN_DK ** -0.5) for j in qk_heads}
        kn = {j: _l2norm(k_ref[0, :, j * GDN_DK:(j + 1) * GDN_DK].astype(F32)) for j in qk_heads}
        kk = {j: _dot_nt(kn[j], kn[j]) for j in qk_heads}
        qk = {j: _dot_nt(qn[j], kn[j]) for j in qk_heads}
        gc_col = {h: gcum[:, nh + h:nh + h + 1] for h in heads}
        gc_last = {h: gcum[ch - 1:ch, nh + h:nh + h + 1] for h in heads}
        beta_col = {h: beta[:, h:h + 1] for h in heads}
        dec = {h: jnp.exp(jnp.where(incl, gc_col[h] - gcum_t[nh + h:nh + h + 1, 0:ch], NEG)) for h in heads}
        t_inv = dict(zip(heads, _unit_lower_inverse(
            [jnp.where(strict, kk[h // rep] * dec[h], 0.0) * beta_col[h] for h in heads], eye)))
        eg = {h: jnp.exp(gc_col[h]) for h in heads}
        s_prev = {h: s_ref[0, h] for h in heads}
        both = {h: _dot(jnp.concatenate([kn[h // rep] * (beta_col[h] * eg[h]), qn[h // rep] * eg[h]], axis=0),
                        s_prev[h]) for h in heads}
        u = {h: _dot(t_inv[h], v_ref[0, :, h * GDN_DV:(h + 1) * GDN_DV].astype(F32) * beta_col[h] - both[h][:ch])
             for h in heads}
        o = {h: both[h][ch:] + _dot(qk[h // rep] * dec[h], u[h]) for h in heads}
        s_new = {h: s_prev[h] * jnp.exp(gc_last[h])
                 + _dot_tn(kn[h // rep] * jnp.exp(gc_last[h] - gc_col[h]), u[h]) for h in heads}
        for h in heads:
            s_ref[0, h] = s_new[h]
            z_h = z_ref[0, :, h * GDN_DV:(h + 1) * GDN_DV].astype(F32)
            o_ref[0, :, h * GDN_DV:(h + 1) * GDN_DV] = (
                _rms(o[h]) * nw_ref[...] * jax.nn.silu(z_h)).astype(o_ref.dtype)


def _gdn_gate_params(dt_bias, a_log):
    nh = GDN_V_HEADS
    dtb = jnp.zeros((1, LANES), F32).at[0, nh:2 * nh].set(dt_bias)
    alog = jnp.zeros((1, LANES), F32).at[0, nh:2 * nh].set(a_log)
    return dtb, alog


def gdn_prompt(main, tail, s0, dt_bias, a_log, norm_w, *, bsz, seq):
    ch = GDN_CHUNK
    nc = seq // ch
    dtb, alog = _gdn_gate_params(dt_bias, a_log)
    small = [dtb, alog, norm_w.reshape(1, -1)]
    sspec = pl.BlockSpec((1, GDN_V_HEADS, GDN_DK, GDN_DV), lambda b, c: (b, 0, 0, 0))
    return pl.pallas_call(
        _gdn_kernel,
        grid=(bsz, nc),
        in_specs=[
            pl.BlockSpec((1, ch, GDN_QK_W), lambda b, c: (b, c, 0)),
            pl.BlockSpec((1, ch, GDN_QK_W), lambda b, c: (b, c, 1)),
            pl.BlockSpec((1, ch, GDN_VW), lambda b, c: (b, c, 1)),
            pl.BlockSpec((1, ch, GDN_VW), lambda b, c: (b, c, 2)),
            pl.BlockSpec((1, ch, LANES), lambda b, c: (b, c, 0)),
        ] + [_small(a) for a in small] + [sspec],
        out_specs=[pl.BlockSpec((1, ch, GDN_VW), lambda b, c: (b, c, 0)), sspec],
        out_shape=[jax.ShapeDtypeStruct((bsz, seq, GDN_VW), BF16),
                   jax.ShapeDtypeStruct((bsz, GDN_V_HEADS, GDN_DK, GDN_DV), F32)],
        compiler_params=_cparams("parallel", "arbitrary"),
        name="gdn_prompt",
    )(main, main, main, main, tail, *small, s0)


def _gdn_conv_step(c0_ref, w_ref, x_row, c0, c1):
    acc = w_ref[GDN_CONV - 1:GDN_CONV, c0:c1] * x_row
    for i in range(GDN_CONV - 1):
        acc = acc + w_ref[i:i + 1, c0:c1] * c0_ref[0, i:i + 1, c0:c1]
    return acc


def _gdn_step_kernel(q_ref, k_ref, v_ref, z_ref, ba_ref, c0_ref, cw_ref, dtb_ref, alog_ref, nw_ref, s0_ref,
                     o_ref, co_ref, s_ref):
    nh = GDN_V_HEADS
    q_raw, k_raw, v_raw = q_ref[0], k_ref[0], v_ref[0]
    q = jax.nn.silu(_gdn_conv_step(c0_ref, cw_ref, q_raw, 0, GDN_QK_W))
    k = jax.nn.silu(_gdn_conv_step(c0_ref, cw_ref, k_raw, GDN_QK_W, 2 * GDN_QK_W))
    v = jax.nn.silu(_gdn_conv_step(c0_ref, cw_ref, v_raw, 2 * GDN_QK_W, GDN_QKV))
    co_ref[0, 0:GDN_CONV - 2, :] = c0_ref[0, 1:GDN_CONV - 1, :]
    co_ref[0, GDN_CONV - 2:GDN_CONV - 1, 0:GDN_QK_W] = q_raw
    co_ref[0, GDN_CONV - 2:GDN_CONV - 1, GDN_QK_W:2 * GDN_QK_W] = k_raw
    co_ref[0, GDN_CONV - 2:GDN_CONV - 1, 2 * GDN_QK_W:GDN_QKV] = v_raw

    beta, g = _gdn_gates(ba_ref[0], dtb_ref, alog_ref)
    eg_all = jnp.exp(g)
    eye = _iota((LANES, LANES), 0) == _iota((LANES, LANES), 1)
    for j in range(GDN_QK_HEADS):
        qn = _l2norm(q[:, j * GDN_DK:(j + 1) * GDN_DK]) * (GDN_DK ** -0.5)
        kn = _l2norm(k[:, j * GDN_DK:(j + 1) * GDN_DK])
        qk = jnp.sum(qn * kn, axis=-1, keepdims=True)
        q_col = _row_to_col(qn, eye)
        k_col = _row_to_col(kn, eye)
        for h in range(j * (nh // GDN_QK_HEADS), (j + 1) * (nh // GDN_QK_HEADS)):
            b_h = beta[:, h:h + 1]
            eg = eg_all[:, nh + h:nh + h + 1]
            s_prev = s0_ref[0, h]
            ks = jnp.sum(s_prev * k_col, axis=0, keepdims=True)
            qs = jnp.sum(s_prev * q_col, axis=0, keepdims=True)
            v_h = v[:, h * GDN_DV:(h + 1) * GDN_DV]
            u = v_h * b_h - (b_h * eg) * ks
            o = eg * qs + qk * u
            s_ref[0, h] = s_prev * eg + k_col * u
            z_h = z_ref[0, :, h * GDN_DV:(h + 1) * GDN_DV]
            o_ref[0, :, h * GDN_DV:(h + 1) * GDN_DV] = _rms(o) * nw_ref[...] * jax.nn.silu(z_h)


def gdn_step(main, tail, conv0, s0, conv_w, dt_bias, a_log, norm_w):
    bsz = main.shape[0]
    dtb, alog = _gdn_gate_params(dt_bias, a_log)
    small = [conv_w, dtb, alog, norm_w.reshape(1, -1)]
    sspec = pl.BlockSpec((1, GDN_V_HEADS, GDN_DK, GDN_DV), lambda b: (b, 0, 0, 0))
    cspec = pl.BlockSpec((1, GDN_CONV - 1, GDN_QKV), lambda b: (b, 0, 0))
    return pl.pallas_call(
        _gdn_step_kernel,
        grid=(bsz,),
        in_specs=[
            pl.BlockSpec((1, 1, GDN_QK_W), lambda b: (b, 0, 0)),
            pl.BlockSpec((1, 1, GDN_QK_W), lambda b: (b, 0, 1)),
            pl.BlockSpec((1, 1, GDN_VW), lambda b: (b, 0, 1)),
            pl.BlockSpec((1, 1, GDN_VW), lambda b: (b, 0, 2)),
            pl.BlockSpec((1, 1, LANES), lambda b: (b, 0, 0)),
            cspec,
        ] + [_small(a) for a in small] + [sspec],
        out_specs=[pl.BlockSpec((1, 1, GDN_VW), lambda b: (b, 0, 0)), cspec, sspec],
        out_shape=[jax.ShapeDtypeStruct((bsz, 1, GDN_VW), F32),
                   jax.ShapeDtypeStruct(conv0.shape, F32),
                   jax.ShapeDtypeStruct(s0.shape, F32)],
        compiler_params=_cparams("parallel"),
        name="gdn_step",
    )(main, main, main, main, tail, conv0, *small, s0)


def _pad_tail(w):
    return jnp.pad(w, ((0, 0), (0, LANES - w.shape[1])))


def _prep_hyb_in(w):
    a = A_WIDTH
    q, k, v = w[:, 0:a], w[:, a:2 * a], w[:, 2 * a:3 * a]
    z = w[:, 3 * a:3 * a + SSM_D_INNER]
    x0 = 3 * a + SSM_D_INNER
    xs = w[:, x0:x0 + SSM_D_INNER]
    bc = w[:, x0 + SSM_D_INNER:x0 + SSM_XBC]
    dt = w[:, x0 + SSM_XBC:]
    return jnp.concatenate([z, xs, q, k, v, bc], axis=1).astype(BF16), _pad_tail(dt).astype(BF16)


def _prep_gdn_in(w):
    return w[:, :GDN_MAIN].astype(BF16), _pad_tail(w[:, GDN_MAIN:]).astype(BF16)


HYB_CONV_COLS = ((SSM_D_INNER, 2 * SSM_D_INNER), (HYB_BC0, HYB_MAIN))


def _hyb_cols(a):
    out = jnp.zeros(a.shape[:-1] + (HYB_MAIN,), F32)
    (x0, x1), (b0, b1) = HYB_CONV_COLS
    return out.at[..., x0:x1].set(a[..., :SSM_D_INNER]).at[..., b0:b1].set(a[..., SSM_D_INNER:])


def _window_to_lanes(c):
    n, b, past, h, dh = c.shape
    return jnp.transpose(c, (0, 1, 3, 4, 2)).reshape(n, b, h * dh, past)


def _window_from_lanes(c):
    n, b, _, past = c.shape
    return jnp.transpose(c.reshape(n, b, A_HEADS, A_HEAD_DIM, past), (0, 1, 4, 2, 3))


def _row_tile(m, cap):
    return m if m <= cap else cap


def kernel(x_prompt, x_sample, cache_attn_k, cache_attn_v, state_ssm_conv, state_ssm, state_gdn_conv, state_gdn, rel_bias, norm_mix_pre, norm_mix_post, norm_ffn_pre, norm_ffn_post, w_hyb_in, ssm_conv_w, ssm_conv_b, ssm_dt_bias, ssm_a_log, ssm_d, ssm_norm_w, w_hyb_out, w_gdn_in, gdn_conv_w, gdn_dt_bias, gdn_a_log, gdn_norm_w, w_gdn_out, w_ffn_gate, w_ffn_up, w_ffn_down):
    depth = norm_mix_pre.shape[0]
    d_model = x_prompt.shape[-1]
    n_hyb, n_gdn = w_hyb_in.shape[0], w_gdn_in.shape[0]

    hyb_in = [_prep_hyb_in(w_hyb_in[i]) for i in range(n_hyb)]
    hyb_out = [(w_hyb_out[i, :A_WIDTH].astype(BF16), w_hyb_out[i, A_WIDTH:].astype(BF16)) for i in range(n_hyb)]
    gdn_in = [_prep_gdn_in(w_gdn_in[i]) for i in range(n_gdn)]
    gdn_out = [w_gdn_out[i].astype(BF16) for i in range(n_gdn)]
    ffn_w = [(w_ffn_gate[l].astype(BF16), w_ffn_up[l].astype(BF16), w_ffn_down[l].astype(BF16))
             for l in range(depth)]
    bias_tiles = _attn_bias_rows(rel_bias)

    def trunk(x3, k_pre, v_pre, sconv, sssm, gconv, gstate):
        bsz, seq, _ = x3.shape
        m = bsz * seq
        step = seq == 1
        tm_big = _row_tile(m, 1024)
        tm = _row_tile(m, 512)
        x = x3.reshape(m, d_model)
        nk, nv, nsc, nss, ngc, ngs = [], [], [], [], [], []
        rolled = None
        for l in range(depth):
            i = l // 2
            if l % 2 == 0:
                w_main, w_tail = hyb_in[i]
                ssm_args = (ssm_dt_bias[i], ssm_a_log[i], ssm_d[i], ssm_norm_w[i])
                if step:
                    main, tail = inproj(x, norm_mix_pre[l], w_main, w_tail, tm=tm_big, tn=512)
                    main3 = main.reshape(bsz, seq, HYB_MAIN)
                    tail3 = tail.reshape(bsz, seq, LANES)
                    col = lambda c0: main[:, c0:c0 + A_WIDTH].reshape(bsz, A_WIDTH, 1)
                    o_attn, *rolled = attention_step(
                        col(HYB_Q0), col(HYB_K0), col(HYB_V0), k_pre, v_pre, rolled,
                        _attn_logw(rel_bias, k_pre.shape[-1]), layer=i)
                    y, c_new, s_new = ssd_step(main3, tail3, sconv[i], sssm[i], ssm_conv_w[i], ssm_conv_b[i],
                                               *ssm_args)
                else:
                    main, tail, hist = inproj_conv(
                        x, norm_mix_pre[l], w_main, w_tail, _hyb_cols(ssm_conv_w[i]),
                        _hyb_cols(ssm_conv_b[i][None]), _hyb_cols(sconv[i]), seq=seq, conv_cols=HYB_CONV_COLS,
                        tm=tm_big, tn=512)
                    main3 = main.reshape(bsz, seq, HYB_MAIN)
                    tail3 = tail.reshape(bsz, seq, LANES)
                    c_new = jnp.concatenate([hist[..., a:b] for a, b in HYB_CONV_COLS], axis=-1)
                    o_attn = attention_prompt(main3, bias_tiles, bsz=bsz, seq=seq)
                    keep = min(A_PATTERNS[-1][0], seq)
                    k_new = main3[:, seq - keep:, HYB_K0:HYB_K0 + A_WIDTH].astype(F32)
                    v_new = main3[:, seq - keep:, HYB_V0:HYB_V0 + A_WIDTH].astype(F32)
                    y, s_new = ssd_prompt(main3, tail3, sssm[i], *ssm_args, bsz=bsz, seq=seq)
                    nk.append(k_new.reshape(bsz, -1, A_HEADS, A_HEAD_DIM))
                    nv.append(v_new.reshape(bsz, -1, A_HEADS, A_HEAD_DIM))
                nsc.append(c_new)
                nss.append(s_new)
                x = outproj([o_attn.reshape(m, A_WIDTH), y.reshape(m, SSM_D_INNER)], list(hyb_out[i]),
                            x, norm_mix_post[l], tm=tm)
            else:
                w_main, w_tail = gdn_in[i]
                gdn_args = (gdn_dt_bias[i], gdn_a_log[i], gdn_norm_w[i])
                if step:
                    main, tail = inproj(x, norm_mix_pre[l], w_main, w_tail, tm=tm_big, tn=512)
                    o, c_new, s_new = gdn_step(main.reshape(bsz, seq, GDN_MAIN), tail.reshape(bsz, seq, LANES),
                                               gconv[i], gstate[i], gdn_conv_w[i], *gdn_args)
                else:
                    pad = lambda a: jnp.pad(a, [(0, 0)] * (a.ndim - 1) + [(0, GDN_MAIN - GDN_QKV)])
                    main, tail, hist = inproj_conv(
                        x, norm_mix_pre[l], w_main, w_tail, pad(gdn_conv_w[i]), jnp.zeros((1, GDN_MAIN), F32),
                        pad(gconv[i]), seq=seq, conv_cols=((0, GDN_QKV),), tm=tm_big, tn=512)
                    c_new = hist[..., :GDN_QKV]
                    o, s_new = gdn_prompt(main.reshape(bsz, seq, GDN_MAIN), tail.reshape(bsz, seq, LANES),
                                          gstate[i], *gdn_args, bsz=bsz, seq=seq)
                ngc.append(c_new)
                ngs.append(s_new)
                x = outproj([o.reshape(m, GDN_VW)], [gdn_out[i]], x, norm_mix_post[l], tm=tm)
            wg, wu, wd = ffn_w[l]
            x = ffn(x, norm_ffn_pre[l], wg, wu, wd, norm_ffn_post[l], tm=tm)
        k_out, v_out = [_window_from_lanes(r) for r in rolled] if step else (jnp.stack(nk), jnp.stack(nv))
        return (x.reshape(bsz, seq, d_model), k_out, v_out, jnp.stack(nsc), jnp.stack(nss),
                jnp.stack(ngc), jnp.stack(ngs))

    bsz = x_prompt.shape[0]
    dt_p = x_prompt.dtype
    p_sc0 = jnp.zeros((n_hyb, bsz, SSM_CONV - 1, SSM_XBC), dt_p)
    p_ss0 = jnp.zeros((n_hyb, bsz, SSM_HEADS, SSM_HEAD_DIM, SSM_STATE), F32)
    p_gc0 = jnp.zeros((n_gdn, bsz, GDN_CONV - 1, GDN_QKV), dt_p)
    p_gs0 = jnp.zeros((n_gdn, bsz, GDN_V_HEADS, GDN_DK, GDN_DV), F32)
    y_prompt, pk, pv, psc, pss, pgc, pgs = trunk(x_prompt, None, None, p_sc0, p_ss0, p_gc0, p_gs0)
    y_sample, sk, sv, ssc, sss, sgc, sgs = trunk(
        x_sample, _window_to_lanes(cache_attn_k), _window_to_lanes(cache_attn_v), state_ssm_conv, state_ssm,
        state_gdn_conv, state_gdn)
    return (y_prompt, y_sample, pk, pv, psc, pss, pgc, pgs, sk, sv, ssc, sss, sgc, sgs)
```

```python
import functools
import math

import numpy as np
import jax
import jax.numpy as jnp
from jax import lax
from jax.experimental import pallas as pl
from jax.experimental.pallas import tpu as pltpu

F32 = jnp.float32
BF16 = jnp.bfloat16
EPS = 1e-6
NEG = -1e30
HIGHEST = lax.Precision.HIGHEST

VMEM_LIMIT_BYTES = 56 * 1024 * 1024
LANES = 128

A_HEADS = 8
A_HEAD_DIM = 64
A_WIDTH = A_HEADS * A_HEAD_DIM
A_PATTERNS = ((128, 1), (512, 4), (2048, 16))
A_BAND = 128
ATTN_GROUP = 4
REL_BUCKETS = 32
REL_MAX_DIST = 2048

SSM_D_INNER = 1024
SSM_HEAD_DIM = 64
SSM_HEADS = SSM_D_INNER // SSM_HEAD_DIM
SSM_GROUPS = 2
SSM_STATE = 128
SSM_CONV = 4
SSM_CHUNK = 128
SSM_BC = 2 * SSM_GROUPS * SSM_STATE
SSM_XBC = SSM_D_INNER + SSM_BC

GDN_QK_HEADS = 8
GDN_V_HEADS = 16
GDN_DK = 128
GDN_DV = 128
GDN_CONV = 4
GDN_CHUNK = 64
GDN_HEAD_GROUP = 16
GDN_QK_W = GDN_QK_HEADS * GDN_DK
GDN_VW = GDN_V_HEADS * GDN_DV
GDN_QKV = 2 * GDN_QK_W + GDN_VW

HYB_MAIN = 2 * SSM_D_INNER + 3 * A_WIDTH + SSM_BC
HYB_Q0 = 2 * SSM_D_INNER
HYB_K0 = HYB_Q0 + A_WIDTH
HYB_V0 = HYB_K0 + A_WIDTH
HYB_BC0 = HYB_V0 + A_WIDTH
GDN_MAIN = GDN_QKV + GDN_VW


def _cparams(*sem):
    return pltpu.CompilerParams(dimension_semantics=sem, vmem_limit_bytes=VMEM_LIMIT_BYTES)


def _rms(x):
    return x * lax.rsqrt(jnp.mean(x * x, axis=-1, keepdims=True) + EPS)


def _dot(a, b, **kw):
    return jnp.dot(a, b, preferred_element_type=F32, **kw)


def _dot_nt(a, b, **kw):
    return lax.dot_general(a, b, (((1,), (1,)), ((), ())), preferred_element_type=F32, **kw)


def _dot_tn(a, b, **kw):
    return lax.dot_general(a, b, (((0,), (0,)), ((), ())), preferred_element_type=F32, **kw)


def _iota(shape, dim):
    return lax.broadcasted_iota(jnp.int32, shape, dim)


def _inproj_kernel(x_ref, g_ref, w_ref, wt_ref, o_ref, t_ref, h_ref):
    @pl.when(pl.program_id(1) == 0)
    def _():
        hb = (_rms(x_ref[...]) * g_ref[...]).astype(BF16)
        h_ref[...] = hb
        t_ref[...] = _dot(hb, wt_ref[...])

    o_ref[...] = _dot(h_ref[...], w_ref[...])


def inproj(x, g, w_main, w_tail, *, tm, tn):
    m, d = x.shape
    n = w_main.shape[1]
    return pl.pallas_call(
        _inproj_kernel,
        grid=(m // tm, n // tn),
        in_specs=[
            pl.BlockSpec((tm, d), lambda i, j: (i, 0)),
            pl.BlockSpec((1, d), lambda i, j: (0, 0)),
            pl.BlockSpec((d, tn), lambda i, j: (0, j)),
            pl.BlockSpec((d, LANES), lambda i, j: (0, 0)),
        ],
        out_specs=[
            pl.BlockSpec((tm, tn), lambda i, j: (i, j)),
            pl.BlockSpec((tm, LANES), lambda i, j: (i, 0)),
        ],
        out_shape=[jax.ShapeDtypeStruct((m, n), F32), jax.ShapeDtypeStruct((m, LANES), F32)],
        scratch_shapes=[pltpu.VMEM((tm, d), BF16)],
        compiler_params=_cparams("parallel", "arbitrary"),
        name="inproj",
    )(x, g.reshape(1, d), w_main, w_tail)


CONV_TAPS = 4
CONV_ROWS = 64
PROJ_K_SLICES = 4
CONV_BASE = 8


def _inproj_conv_kernel(x_ref, g_ref, w_ref, wt_ref, cw_ref, cb_ref, c0_ref, o_ref, t_ref, so_ref,
                        h_ref, xp0_ref, xp1_ref, carry_ref, *, n_col, conv_tiles, tiles_per_seq):
    s = pl.program_id(0)
    n_tiles = pl.num_programs(0) - 1
    tm = o_ref.shape[0]
    lo = CONV_BASE - (CONV_TAPS - 1)
    cur = jnp.minimum(s, n_tiles - 1)
    prv = jnp.maximum(s - 1, 0)
    pi, pj = prv // n_col, prv % n_col
    prv_conv = functools.reduce(jnp.logical_or, [jnp.logical_and(pj >= a, pj < b) for a, b in conv_tiles])
    first = pi % tiles_per_seq == 0
    both = lambda a, b: jnp.logical_and(a, b)

    @pl.when(both(cur % n_col == 0, s < n_tiles))
    def _():
        hb = (_rms(x_ref[...]) * g_ref[...]).astype(BF16)
        h_ref[...] = hb
        t_ref[...] = _dot(hb, wt_ref[...])

    for parity, (cur_ref, prv_ref) in enumerate(((xp0_ref, xp1_ref), (xp1_ref, xp0_ref))):
        here = s % 2 == parity

        def project(cur_ref=cur_ref):
            cur_ref[CONV_BASE:CONV_BASE + tm, :] = _dot(h_ref[...], w_ref[...])

        def raw_tail(prv_ref=prv_ref):
            tail = prv_ref[CONV_BASE + tm - (CONV_TAPS - 1):CONV_BASE + tm, :]
            so_ref[pj, pi] = tail
            return tail

        if parity == 0:
            @pl.when(s == 0)
            def _():
                project()

        @pl.when(both(here, both(s > 0, jnp.logical_not(prv_conv))))
        def _():
            project()
            raw_tail()
            o_ref[...] = prv_ref[CONV_BASE:CONV_BASE + tm, :].astype(o_ref.dtype)

        @pl.when(both(here, both(both(s > 0, prv_conv), first)))
        def _():
            prv_ref[lo:CONV_BASE, :] = c0_ref[pj, pi // tiles_per_seq]

        @pl.when(both(here, both(both(s > 0, prv_conv), jnp.logical_not(first))))
        def _():
            prv_ref[lo:CONV_BASE, :] = carry_ref[pj]

        @pl.when(both(here, both(s > 0, prv_conv)))
        def _():
            def conv_chunk(r0):
                ext = prv_ref[r0:r0 + CONV_BASE + CONV_ROWS, :]
                conv = cb_ref[pj] + cw_ref[pj, CONV_TAPS - 1:CONV_TAPS, :] * ext[CONV_BASE:]
                for back in range(1, CONV_TAPS):
                    tap = CONV_TAPS - 1 - back
                    conv = conv + cw_ref[pj, tap:tap + 1, :] * pltpu.roll(ext, back, 0)[CONV_BASE:]
                o_ref[r0:r0 + CONV_ROWS, :] = jax.nn.silu(conv).astype(o_ref.dtype)

            d = h_ref.shape[1]
            n_chunks = tm // CONV_ROWS
            acc = None
            for kq in range(PROJ_K_SLICES):
                ks = slice(kq * d // PROJ_K_SLICES, (kq + 1) * d // PROJ_K_SLICES)
                part = _dot(h_ref[:, ks], w_ref[ks, :])
                acc = part if acc is None else acc + part
                for c in range(kq * n_chunks // PROJ_K_SLICES, (kq + 1) * n_chunks // PROJ_K_SLICES):
                    conv_chunk(c * CONV_ROWS)
            cur_ref[CONV_BASE:CONV_BASE + tm, :] = acc
            carry_ref[pj] = raw_tail()


def inproj_conv(x, g, w_main, w_tail, conv_w, conv_b, conv0, *, seq, conv_cols, tm, tn):
    m, d = x.shape
    n = w_main.shape[1]
    assert seq % tm == 0 and all(a % tn == 0 and b % tn == 0 for a, b in conv_cols)
    tiles_per_seq = seq // tm
    n_row, n_col = m // tm, n // tn
    n_tiles = n_row * n_col
    conv_tiles = tuple((a // tn, b // tn) for a, b in conv_cols)
    hist = CONV_TAPS - 1
    cur = lambda s: jnp.minimum(s, n_tiles - 1)
    prv = lambda s: jnp.maximum(s - 1, 0)
    by_tile = lambda a: jnp.moveaxis(a.reshape(a.shape[:-1] + (n_col, tn)), -2, 0)
    main, tail, hist_rows = pl.pallas_call(
        functools.partial(_inproj_conv_kernel, n_col=n_col, conv_tiles=conv_tiles, tiles_per_seq=tiles_per_seq),
        grid=(n_tiles + 1,),
        in_specs=[
            pl.BlockSpec((tm, d), lambda s: (cur(s) // n_col, 0)),
            pl.BlockSpec((1, d), lambda s: (0, 0)),
            pl.BlockSpec((None, d, tn), lambda s: (cur(s) % n_col, 0, 0)),
            pl.BlockSpec((d, LANES), lambda s: (0, 0)),
            pl.BlockSpec((n_col, CONV_TAPS, tn), lambda s: (0, 0, 0)),
            pl.BlockSpec((n_col, 1, tn), lambda s: (0, 0, 0)),
            pl.BlockSpec((n_col, m // seq, hist, tn), lambda s: (0, 0, 0, 0)),
        ],
        out_specs=[
            pl.BlockSpec((tm, tn), lambda s: (prv(s) // n_col, prv(s) % n_col)),
            pl.BlockSpec((tm, LANES), lambda s: (cur(s) // n_col, 0)),
            pl.BlockSpec((n_col, n_row, hist, tn), lambda s: (0, 0, 0, 0)),
        ],
        out_shape=[jax.ShapeDtypeStruct((m, n), BF16), jax.ShapeDtypeStruct((m, LANES), F32),
                   jax.ShapeDtypeStruct((n_col, n_row, hist, tn), F32)],
        scratch_shapes=[pltpu.VMEM((tm, d), BF16), pltpu.VMEM((CONV_BASE + tm, tn), F32),
                        pltpu.VMEM((CONV_BASE + tm, tn), F32), pltpu.VMEM((n_col, hist, tn), F32)],
        compiler_params=_cparams("arbitrary"),
        name="inproj_conv",
    )(x, g.reshape(1, d), by_tile(w_main), w_tail, by_tile(conv_w), by_tile(conv_b), by_tile(conv0))
    hist_rows = jnp.moveaxis(hist_rows, 0, -2).reshape(n_row, hist, n)
    return main, tail, hist_rows[tiles_per_seq - 1::tiles_per_seq]


def _outproj_kernel(*refs, n_in):
    a_refs, w_refs = refs[:n_in], refs[n_in:2 * n_in]
    x_ref, g_ref, o_ref = refs[2 * n_in:]
    acc = None
    for a_ref, w_ref in zip(a_refs, w_refs):
        t = _dot(a_ref[...].astype(BF16), w_ref[...])
        acc = t if acc is None else acc + t
    o_ref[...] = x_ref[...] + _rms(acc) * g_ref[...]


def outproj(acts, weights, x, g, *, tm):
    m, d = x.shape
    n_in = len(acts)
    in_specs = [pl.BlockSpec((tm, a.shape[1]), lambda i: (i, 0)) for a in acts]
    in_specs += [pl.BlockSpec(w.shape, lambda i: (0, 0)) for w in weights]
    in_specs += [pl.BlockSpec((tm, d), lambda i: (i, 0)), pl.BlockSpec((1, d), lambda i: (0, 0))]
    return pl.pallas_call(
        functools.partial(_outproj_kernel, n_in=n_in),
        grid=(m // tm,),
        in_specs=in_specs,
        out_specs=pl.BlockSpec((tm, d), lambda i: (i, 0)),
        out_shape=jax.ShapeDtypeStruct((m, d), F32),
        compiler_params=_cparams("parallel"),
        name="outproj",
    )(*acts, *weights, x, g.reshape(1, d))


def _ffn_kernel(x_ref, g1_ref, wg_ref, wu_ref, wd_ref, g2_ref, o_ref):
    x = x_ref[...]
    h = (_rms(x) * g1_ref[...]).astype(BF16)
    a = jax.nn.silu(_dot(h, wg_ref[...])) * _dot(h, wu_ref[...])
    o_ref[...] = x + _rms(_dot(a.astype(BF16), wd_ref[...])) * g2_ref[...]


def ffn(x, g1, wg, wu, wd, g2, *, tm):
    m, d = x.shape
    resident = lambda a: pl.BlockSpec(a.shape, lambda i: (0, 0), pipeline_mode=pl.Buffered(1))
    return pl.pallas_call(
        _ffn_kernel,
        grid=(m // tm,),
        in_specs=[
            pl.BlockSpec((tm, d), lambda i: (i, 0)),
            pl.BlockSpec((1, d), lambda i: (0, 0)),
            resident(wg), resident(wu), resident(wd),
            pl.BlockSpec((1, d), lambda i: (0, 0)),
        ],
        out_specs=pl.BlockSpec((tm, d), lambda i: (i, 0)),
        out_shape=jax.ShapeDtypeStruct((m, d), F32),
        compiler_params=_cparams("parallel"),
        name="ffn",
    )(x, g1.reshape(1, d), wg, wu, wd, g2.reshape(1, d))


def _rel_buckets(dist):
    max_exact = REL_BUCKETS // 2
    n = np.maximum(dist, 1).astype(np.float32)
    large = max_exact + (np.log(n / max_exact) / math.log(REL_MAX_DIST / max_exact)
                         * (REL_BUCKETS - max_exact)).astype(np.int32)
    large = np.minimum(large, REL_BUCKETS - 1)
    return np.where(dist < max_exact, dist, large).astype(np.int32)


def _attn_bias_rows(rel_bias):
    u = np.arange(2 * A_BAND)
    valid = u <= A_BAND
    rows = []
    for (_, d) in A_PATTERNS:
        b = rel_bias[_rel_buckets(np.where(valid, A_BAND - u, 0) * d)]
        rows.append(jnp.where(valid[:, None], b.astype(F32), NEG))
    tl = jnp.transpose(jnp.stack(rows), (2, 0, 1))
    tl = tl.reshape(A_HEADS // 2, 2, len(A_PATTERNS), 2 * A_BAND)
    tl = jnp.transpose(tl, (0, 2, 1, 3))[:, :, :, None, :]
    return jnp.broadcast_to(tl, tl.shape[:3] + (8, 2 * A_BAND))


def _attn_kernel(qin_ref, kin_ref, vin_ref, brow_ref, o_ref, m0_ref, m1_ref, l0_ref, l1_ref, acc_ref,
                 q_ref, k_ref, v_ref, b_ref, *, seq):
    n_tiles = seq // A_BAND
    lane = _iota((A_BAND, LANES), 1)
    head0 = lane < A_HEAD_DIM
    m_refs, l_refs = (m0_ref, m1_ref), (l0_ref, l1_ref)
    q_ref[...] = qin_ref[...].astype(F32) * (A_HEAD_DIM ** -0.5)
    k_ref[...] = kin_ref[...].astype(F32)
    v_ref[...] = vin_ref[...].astype(F32)
    for p in range(len(A_PATTERNS)):
        for h in range(2):
            row = jnp.broadcast_to(brow_ref[0, p, h, 0:1, :], (A_BAND, 2 * A_BAND))
            b_ref[0, p, h] = pltpu.roll(row, 0, 1, stride=1, stride_axis=0)
    for h in range(2):
        m_refs[h][...] = jnp.full(m_refs[h].shape, NEG, F32)
        l_refs[h][...] = jnp.zeros_like(l_refs[h])
    acc_ref[...] = jnp.zeros_like(acc_ref)

    for p, (_, d) in enumerate(A_PATTERNS):
        tiles_per_class = n_tiles // d

        def load_tile(idx, d=d, tiles_per_class=tiles_per_class):
            r = idx // tiles_per_class
            t = idx % tiles_per_class
            start = r + t * (d * A_BAND)
            has_prev = t > 0
            prev = jnp.where(has_prev, start - d * A_BAND, start)
            if d > 1:
                rows, prows = pl.ds(start, A_BAND, stride=d), pl.ds(prev, A_BAND, stride=d)
            else:
                rows, prows = pl.ds(pl.multiple_of(start, A_BAND), A_BAND), pl.ds(pl.multiple_of(prev, A_BAND), A_BAND)
            return dict(
                rows=rows, has_prev=has_prev, q=q_ref[rows, :],
                k2=jnp.concatenate([k_ref[prows, :], k_ref[rows, :]], axis=0).astype(BF16),
                v2=jnp.concatenate([v_ref[prows, :], v_ref[rows, :]], axis=0).astype(BF16),
                acc=acc_ref[rows, :], m=[m_refs[h][rows, :] for h in range(2)],
                l=[l_refs[h][rows, :] for h in range(2)])

        def tile_group(idx, carry, load_tile=load_tile, p=p):
            tiles = [load_tile(idx + i * (n_tiles // ATTN_GROUP)) for i in range(ATTN_GROUP)]
            chains = [(tile, h) for tile in tiles for h in range(2)]
            col = _iota((A_BAND, 2 * A_BAND), 1)
            qh = [jnp.where(head0 if h == 0 else jnp.logical_not(head0), tile["q"], 0.0).astype(BF16)
                  for tile, h in chains]
            s = [_dot_nt(qh[c], tile["k2"])
                 + jnp.where(jnp.logical_and(col < A_BAND, jnp.logical_not(tile["has_prev"])), NEG, b_ref[0, p, h])
                 for c, (tile, h) in enumerate(chains)]
            m_new = [jnp.maximum(tile["m"][h], jnp.max(s[c], axis=-1, keepdims=True))
                     for c, (tile, h) in enumerate(chains)]
            alpha = [jnp.exp(tile["m"][h] - m_new[c]) for c, (tile, h) in enumerate(chains)]
            pr = [jnp.exp(s[c] - jnp.concatenate([m_new[c], m_new[c]], axis=1)) for c in range(len(chains))]
            l_new = [alpha[c] * tile["l"][h] + jnp.sum(pr[c], axis=-1, keepdims=True)
                     for c, (tile, h) in enumerate(chains)]
            acc_new = [alpha[c] * tile["acc"] + _dot(pr[c].astype(BF16), tile["v2"])
                       for c, (tile, h) in enumerate(chains)]
            for c, (tile, h) in enumerate(chains):
                m_refs[h][tile["rows"], :] = m_new[c]
                l_refs[h][tile["rows"], :] = l_new[c]
                if h == 1:
                    acc_ref[tile["rows"], :] = jnp.where(head0, acc_new[c - 1], acc_new[c])
            return carry

        lax.fori_loop(0, n_tiles // ATTN_GROUP, tile_group, 0)

    lane_s = _iota((seq, LANES), 1)
    o_ref[...] = (acc_ref[...] / jnp.where(lane_s < A_HEAD_DIM, l0_ref[...], l1_ref[...])).astype(o_ref.dtype)


def attention_prompt(proj, bias_tiles, *, bsz, seq):
    hp = A_HEADS // 2
    qb, kb, vb = HYB_Q0 // LANES, HYB_K0 // LANES, HYB_V0 // LANES
    return pl.pallas_call(
        functools.partial(_attn_kernel, seq=seq),
        grid=(bsz, hp),
        in_specs=[
            pl.BlockSpec((None, seq, LANES), lambda b, h: (b, 0, qb + h)),
            pl.BlockSpec((None, seq, LANES), lambda b, h: (b, 0, kb + h)),
            pl.BlockSpec((None, seq, LANES), lambda b, h: (b, 0, vb + h)),
            pl.BlockSpec((1,) + bias_tiles.shape[1:], lambda b, h: (h, 0, 0, 0, 0)),
        ],
        out_specs=pl.BlockSpec((None, seq, LANES), lambda b, h: (b, 0, h)),
        out_shape=jax.ShapeDtypeStruct((bsz, seq, A_WIDTH), BF16),
        scratch_shapes=[pltpu.VMEM((seq, LANES), F32)] * 8
        + [pltpu.VMEM((1, len(A_PATTERNS), 2, A_BAND, 2 * A_BAND), F32)],
        compiler_params=_cparams("parallel", "parallel"),
        name="attn_prompt",
    )(proj, proj, proj, bias_tiles)


def _attn_logw(rel_bias, past):
    dist = np.arange(past + 1)
    count = np.zeros(past + 1, np.float64)
    for (w, d) in A_PATTERNS:
        count += ((dist % d == 0) & (dist <= w)).astype(np.float64)
    logc = np.where(count > 0, np.log(np.maximum(count, 1.0)), 0.0).astype(np.float32)
    lw = rel_bias[_rel_buckets(dist)].astype(F32).T + logc[None, :]
    return jnp.where((count > 0)[None, :], lw, NEG)


def _step_scores(xk_ref, q_col, kn_col, lw_ref, lw0_ref, s_ref):
    qs = q_col * (A_HEAD_DIM ** -0.5)
    s_new = []
    for h in range(A_HEADS):
        rows = slice(h * A_HEAD_DIM, (h + 1) * A_HEAD_DIM)
        s_ref[h:h + 1, :] = jnp.sum(xk_ref[0, 0, rows, :] * qs[rows], axis=0, keepdims=True)
        s_new.append(jnp.sum(kn_col[rows] * qs[rows], axis=0, keepdims=True))
    s = s_ref[...] + lw_ref[...]
    s_new = jnp.concatenate(s_new, axis=0) + lw0_ref[:, :1]
    m = jnp.maximum(jnp.max(s, axis=-1, keepdims=True), s_new)
    p = jnp.exp(s - m)
    p_new = jnp.exp(s_new - m)
    den = jnp.sum(p, axis=-1, keepdims=True) + p_new
    return p, p_new, den


def _step_output(xv_ref, vn_col, p, p_new, den, o_ref):
    for h in range(A_HEADS):
        rows = slice(h * A_HEAD_DIM, (h + 1) * A_HEAD_DIM)
        pv = jnp.sum(xv_ref[0, 0, rows, :] * p[h:h + 1, :], axis=-1, keepdims=True)
        o_ref[0, rows, :] = (pv + p_new[h:h + 1, :] * vn_col[rows]) / den[h:h + 1, :]


def _attn_step_roll_kernel(q_ref, kn_ref, vn_ref, lw_ref, lw0_ref, xk_ref, xv_ref, o_ref, ko_ref, vo_ref, s_ref,
                           *, layer, past):
    is_layer = pl.program_id(0) == layer
    newest = _iota((A_HEAD_DIM, past), 1) == past - 1
    for x_ref, n_ref, out_ref in ((xk_ref, kn_ref, ko_ref), (xv_ref, vn_ref, vo_ref)):
        for h in range(A_HEADS):
            rows = slice(h * A_HEAD_DIM, (h + 1) * A_HEAD_DIM)
            rolled = pltpu.roll(x_ref[0, 0, rows, :], past - 1, 1)
            out_ref[0, 0, rows, :] = jnp.where(jnp.logical_and(newest, is_layer), n_ref[0, rows, :], rolled)

    @pl.when(is_layer)
    def _():
        p, p_new, den = _step_scores(xk_ref, q_ref[0], kn_ref[0], lw_ref, lw0_ref, s_ref)
        _step_output(xv_ref, vn_ref[0], p, p_new, den, o_ref.at[0])

    @pl.when(jnp.logical_not(is_layer))
    def _():
        o_ref[...] = jnp.zeros_like(o_ref)


def _attn_step_append_kernel(q_ref, kn_ref, vn_ref, lw_ref, lw0_ref, xk_ref, xv_ref, ko_in, vo_in,
                             o_ref, ko_ref, vo_ref, s_ref, *, past):
    del ko_in, vo_in
    p, p_new, den = _step_scores(xk_ref, q_ref[0], kn_ref[0], lw_ref, lw0_ref, s_ref)
    _step_output(xv_ref, vn_ref[0], p, p_new, den, o_ref)
    newest = _iota((A_WIDTH, LANES), 1) == LANES - 1
    for x_ref, n_ref, out_ref in ((xk_ref, kn_ref, ko_ref), (xv_ref, vn_ref, vo_ref)):
        rolled = pltpu.roll(x_ref[0, 0, :, past - LANES:past], LANES - 1, 1)
        out_ref[0, 0] = jnp.where(newest, n_ref[0], rolled)


def attention_step(q_col, kn_col, vn_col, cache_k, cache_v, rolled, logw, *, layer):
    n_layers, bsz, w, past = cache_k.shape
    lw_cache = logw[:, past:0:-1]
    lw_new = jnp.broadcast_to(logw[:, :1], (A_HEADS, LANES))
    out_shape = [jax.ShapeDtypeStruct((bsz, w, 1), F32),
                 jax.ShapeDtypeStruct(cache_k.shape, cache_k.dtype),
                 jax.ShapeDtypeStruct(cache_v.shape, cache_v.dtype)]
    scratch = [pltpu.VMEM((A_HEADS, past), F32)]
    if rolled is None:
        assert layer == 0
        col = pl.BlockSpec((1, w, 1), lambda l, b: (b, 0, 0))
        win = pl.BlockSpec((1, 1, w, past), lambda l, b: (l, b, 0, 0))
        o_all, rolled_k, rolled_v = pl.pallas_call(
            functools.partial(_attn_step_roll_kernel, layer=layer, past=past),
            grid=(n_layers, bsz),
            in_specs=[col, col, col, _small(lw_cache), _small(lw_new), win, win],
            out_specs=[pl.BlockSpec((1, 1, w, 1), lambda l, b: (l, b, 0, 0)), win, win],
            out_shape=[jax.ShapeDtypeStruct((n_layers, bsz, w, 1), F32)] + out_shape[1:],
            scratch_shapes=scratch,
            compiler_params=_cparams("arbitrary", "arbitrary"),
            name="attn_step_roll",
        )(q_col, kn_col, vn_col, lw_cache, lw_new, cache_k, cache_v)
        return o_all[layer], rolled_k, rolled_v
    col = pl.BlockSpec((1, w, 1), lambda b: (b, 0, 0))
    win = pl.BlockSpec((1, 1, w, past), lambda b: (layer, b, 0, 0))
    tail = pl.BlockSpec((1, 1, w, LANES), lambda b: (layer, b, 0, past // LANES - 1))
    return pl.pallas_call(
        functools.partial(_attn_step_append_kernel, past=past),
        grid=(bsz,),
        in_specs=[col, col, col, _small(lw_cache), _small(lw_new), win, win, tail, tail],
        out_specs=[col, tail, tail],
        out_shape=out_shape,
        scratch_shapes=scratch,
        input_output_aliases={7: 1, 8: 2},
        compiler_params=_cparams("arbitrary"),
        name="attn_step_append",
    )(q_col, kn_col, vn_col, lw_cache, lw_new, cache_k, cache_v, *rolled)


def _group_rms(y, w):
    half = SSM_D_INNER // SSM_GROUPS
    return [_rms(y[:, g * half:(g + 1) * half]) * w[:, g * half:(g + 1) * half] for g in range(SSM_GROUPS)]


def _ssd_kernel(z_ref, xs_ref, bc_ref, dt_ref, dtb_ref, alog_ref, d_ref, nw_ref, h0_ref, y_ref, h_ref, ys_ref):
    c = pl.program_id(1)
    ch = SSM_CHUNK

    @pl.when(c == 0)
    def _():
        h_ref[...] = h0_ref[...]

    xs = xs_ref[0].astype(F32)
    bc = bc_ref[0].astype(F32)

    dt = jax.nn.softplus(dt_ref[0] + dtb_ref[...])
    da = dt * (-jnp.exp(alog_ref[...]))
    row = _iota((ch, ch), 0)
    colv = _iota((ch, ch), 1)
    tril = (row >= colv).astype(F32)
    cs = _dot(tril, da, precision=HIGHEST)
    cs_t = cs.T
    causal = row >= colv

    heads = range(SSM_HEADS)
    hpg = SSM_HEADS // SSM_GROUPS
    bm = [bc[:, g * SSM_STATE:(g + 1) * SSM_STATE] for g in range(SSM_GROUPS)]
    cm = [bc[:, (SSM_GROUPS + g) * SSM_STATE:(SSM_GROUPS + g + 1) * SSM_STATE] for g in range(SSM_GROUPS)]
    cb = [_dot_nt(cm[g], bm[g]) for g in range(SSM_GROUPS)]
    dt_t = dt.T
    xs_t = xs.T
    w_t = dt_t * jnp.exp(cs_t[:, ch - 1:ch] - cs_t)
    e_last = jnp.exp(cs[ch - 1:ch, :])
    lane_lo = _iota((ch, LANES), 1) < SSM_HEAD_DIM
    row_lo = _iota((LANES, ch), 0) < SSM_HEAD_DIM
    pairs = range(SSM_HEADS // 2)
    csb = [jnp.broadcast_to(cs[:, h:h + 1], (ch, ch)) for h in heads]
    mix = [cb[h // hpg] * jnp.exp(jnp.where(causal, csb[h] - cs_t[h:h + 1, :], NEG)) * dt_t[h:h + 1, :]
           for h in heads]
    x_pair = [xs[:, j * LANES:(j + 1) * LANES] for j in pairs]
    y_intra = [jnp.where(lane_lo, _dot(mix[2 * j], x_pair[j]), _dot(mix[2 * j + 1], x_pair[j])) for j in pairs]
    h_pair = [h_ref[0, 2 * j:2 * j + 2].reshape(2 * SSM_HEAD_DIM, SSM_STATE) for j in pairs]
    y_inter = [_dot_nt(cm[2 * j // hpg], h_pair[j]) * jnp.exp(jnp.where(lane_lo, csb[2 * j], csb[2 * j + 1]))
               for j in pairs]
    xw_t = [xs_t[j * LANES:(j + 1) * LANES, :] * jnp.where(row_lo, w_t[2 * j:2 * j + 1, :], w_t[2 * j + 1:2 * j + 2, :])
            for j in pairs]
    st = [_dot(xw_t[j], bm[2 * j // hpg]) for j in pairs]
    for j in pairs:
        cols = slice(j * LANES, (j + 1) * LANES)
        ys_ref[:, cols] = y_intra[j] + y_inter[j] + d_ref[:, cols] * x_pair[j]
        decay = jnp.where(row_lo, e_last[:, 2 * j:2 * j + 1], e_last[:, 2 * j + 1:2 * j + 2])
        h_ref[0, 2 * j:2 * j + 2] = (h_pair[j] * decay + st[j]).reshape(2, SSM_HEAD_DIM, SSM_STATE)

    y = ys_ref[...] * jax.nn.silu(z_ref[0].astype(F32))
    half = SSM_D_INNER // SSM_GROUPS
    for g, yg in enumerate(_group_rms(y, nw_ref[...])):
        y_ref[0, :, g * half:(g + 1) * half] = yg.astype(y_ref.dtype)


def _small(a):
    return pl.BlockSpec(a.shape, lambda *_: (0,) * a.ndim)


def ssd_prompt(main, tail, h0, dt_bias, a_log, d_skip, norm_w, *, bsz, seq):
    nc = seq // SSM_CHUNK
    ch = SSM_CHUNK
    small = [_pad_tail(dt_bias.reshape(1, -1)), _pad_tail(a_log.reshape(1, -1)),
             jnp.repeat(d_skip, SSM_HEAD_DIM).reshape(1, -1), norm_w.reshape(1, -1)]
    return pl.pallas_call(
        _ssd_kernel,
        grid=(bsz, nc),
        in_specs=[
            pl.BlockSpec((1, ch, SSM_D_INNER), lambda b, c: (b, c, 0)),
            pl.BlockSpec((1, ch, SSM_D_INNER), lambda b, c: (b, c, 1)),
            pl.BlockSpec((1, ch, SSM_BC), lambda b, c: (b, c, HYB_BC0 // SSM_BC)),
            pl.BlockSpec((1, ch, LANES), lambda b, c: (b, c, 0)),
        ] + [_small(a) for a in small] + [
            pl.BlockSpec((1, SSM_HEADS, SSM_HEAD_DIM, SSM_STATE), lambda b, c: (b, 0, 0, 0)),
        ],
        out_specs=[
            pl.BlockSpec((1, ch, SSM_D_INNER), lambda b, c: (b, c, 0)),
            pl.BlockSpec((1, SSM_HEADS, SSM_HEAD_DIM, SSM_STATE), lambda b, c: (b, 0, 0, 0)),
        ],
        out_shape=[jax.ShapeDtypeStruct((bsz, seq, SSM_D_INNER), BF16),
                   jax.ShapeDtypeStruct((bsz, SSM_HEADS, SSM_HEAD_DIM, SSM_STATE), F32)],
        scratch_shapes=[pltpu.VMEM((ch, SSM_D_INNER), F32)],
        compiler_params=_cparams("parallel", "arbitrary"),
        name="ssd_prompt",
    )(main, main, main, tail, *small, h0)


def _row_to_col(row, eye):
    return jnp.sum(jnp.where(eye, row, 0.0), axis=1, keepdims=True)


def _col_to_row(col, eye):
    return jnp.sum(jnp.where(eye, col, 0.0), axis=0, keepdims=True)


def _conv_step(c0_ref, w_ref, x_row, c0, c1):
    acc = w_ref[SSM_CONV - 1:SSM_CONV, c0:c1] * x_row
    for i in range(SSM_CONV - 1):
        acc = acc + w_ref[i:i + 1, c0:c1] * c0_ref[0, i:i + 1, c0:c1]
    return acc


def _ssd_step_kernel(z_ref, xs_ref, bc_ref, dt_ref, c0_ref, cw_ref, cb_ref, dtb_ref, alog_ref, d_ref, nw_ref,
                     h0_ref, y_ref, co_ref, h_ref, ys_ref):
    xs_raw = xs_ref[0]
    bc_raw = bc_ref[0]
    xs = jax.nn.silu(_conv_step(c0_ref, cw_ref, xs_raw, 0, SSM_D_INNER) + cb_ref[:, 0:SSM_D_INNER])
    bc = jax.nn.silu(_conv_step(c0_ref, cw_ref, bc_raw, SSM_D_INNER, SSM_XBC) + cb_ref[:, SSM_D_INNER:SSM_XBC])
    co_ref[0, 0:SSM_CONV - 2, :] = c0_ref[0, 1:SSM_CONV - 1, :]
    co_ref[0, SSM_CONV - 2:SSM_CONV - 1, 0:SSM_D_INNER] = xs_raw
    co_ref[0, SSM_CONV - 2:SSM_CONV - 1, SSM_D_INNER:SSM_XBC] = bc_raw

    dt = jax.nn.softplus(dt_ref[0] + dtb_ref[...])
    dec = jnp.exp(dt * (-jnp.exp(alog_ref[...])))
    dskip = d_ref[...]
    eye = _iota((LANES, LANES), 0) == _iota((LANES, LANES), 1)
    upper = _iota((LANES, 1), 0) >= SSM_HEAD_DIM
    for j in range(SSM_HEADS // 2):
        h0i, h1i = 2 * j, 2 * j + 1
        g = h0i // (SSM_HEADS // SSM_GROUPS)
        bm = bc[:, g * SSM_STATE:(g + 1) * SSM_STATE]
        cm = bc[:, (SSM_GROUPS + g) * SSM_STATE:(SSM_GROUPS + g + 1) * SSM_STATE]
        x_row = xs[:, j * LANES:(j + 1) * LANES]
        x_col = _row_to_col(x_row, eye)
        pick = lambda v: jnp.where(upper, v[:, h1i:h1i + 1], v[:, h0i:h0i + 1])
        hp = h0_ref[0, h0i:h1i + 1].reshape(2 * SSM_HEAD_DIM, SSM_STATE)
        hn = hp * pick(dec) + (x_col * pick(dt)) * bm
        h_ref[0, h0i:h1i + 1] = hn.reshape(2, SSM_HEAD_DIM, SSM_STATE)
        y_col = jnp.sum(hn * cm, axis=1, keepdims=True) + pick(dskip) * x_col
        ys_ref[:, j * LANES:(j + 1) * LANES] = _col_to_row(y_col, eye)

    y = ys_ref[...] * jax.nn.silu(z_ref[0])
    half = SSM_D_INNER // SSM_GROUPS
    for g, yg in enumerate(_group_rms(y, nw_ref[...])):
        y_ref[0, :, g * half:(g + 1) * half] = yg


def ssd_step(main, tail, conv0, h0, conv_w, conv_b, dt_bias, a_log, d_skip, norm_w):
    bsz = main.shape[0]
    small = [conv_w, conv_b.reshape(1, -1), _pad_tail(dt_bias.reshape(1, -1)), _pad_tail(a_log.reshape(1, -1)),
             _pad_tail(d_skip.reshape(1, -1)), norm_w.reshape(1, -1)]
    hspec = pl.BlockSpec((1, SSM_HEADS, SSM_HEAD_DIM, SSM_STATE), lambda b: (b, 0, 0, 0))
    cspec = pl.BlockSpec((1, SSM_CONV - 1, SSM_XBC), lambda b: (b, 0, 0))
    return pl.pallas_call(
        _ssd_step_kernel,
        grid=(bsz,),
        in_specs=[
            pl.BlockSpec((1, 1, SSM_D_INNER), lambda b: (b, 0, 0)),
            pl.BlockSpec((1, 1, SSM_D_INNER), lambda b: (b, 0, 1)),
            pl.BlockSpec((1, 1, SSM_BC), lambda b: (b, 0, HYB_BC0 // SSM_BC)),
            pl.BlockSpec((1, 1, LANES), lambda b: (b, 0, 0)),
            cspec,
        ] + [_small(a) for a in small] + [hspec],
        out_specs=[pl.BlockSpec((1, 1, SSM_D_INNER), lambda b: (b, 0, 0)), cspec, hspec],
        out_shape=[jax.ShapeDtypeStruct((bsz, 1, SSM_D_INNER), F32),
                   jax.ShapeDtypeStruct(conv0.shape, F32),
                   jax.ShapeDtypeStruct(h0.shape, F32)],
        scratch_shapes=[pltpu.VMEM((1, SSM_D_INNER), F32)],
        compiler_params=_cparams("parallel"),
        name="ssd_step",
    )(main, main, main, tail, conv0, *small, h0)


def _l2norm(x):
    return x * lax.rsqrt(jnp.sum(x * x, axis=-1, keepdims=True) + EPS)


def _unit_lower_inverse(ns, eye):
    size = ns[0].shape[0]
    ps = [eye - n for n in ns]
    ms = [_dot(n, n) for n in ns]
    power = 2
    while 2 * power < size:
        pms = [_dot(jnp.concatenate([p, m], axis=0), m) for p, m in zip(ps, ms)]
        ps = [p + pm[:size] for p, pm in zip(ps, pms)]
        ms = [pm[size:] for pm in pms]
        power *= 2
    return [p + _dot(p, m) for p, m in zip(ps, ms)]


def _gdn_gates(ba, dtb_ref, alog_ref):
    beta = jax.nn.sigmoid(ba)
    g = -jnp.exp(alog_ref[...]) * jax.nn.softplus(ba + dtb_ref[...])
    return beta, g


def _gdn_kernel(q_ref, k_ref, v_ref, z_ref, ba_ref, dtb_ref, alog_ref, nw_ref, s0_ref, o_ref, s_ref):
    c = pl.program_id(1)
    ch = GDN_CHUNK
    nh = GDN_V_HEADS

    @pl.when(c == 0)
    def _():
        s_ref[...] = s0_ref[...]

    beta, g = _gdn_gates(ba_ref[0], dtb_ref, alog_ref)
    row = _iota((ch, ch), 0)
    colv = _iota((ch, ch), 1)
    incl = row >= colv
    strict = row > colv
    eye = (row == colv).astype(F32)
    gcum = _dot(incl.astype(F32), g, precision=HIGHEST)
    gcum_t = jnp.concatenate([gcum, jnp.zeros((LANES - ch, LANES), F32)], axis=0).T

    rep = nh // GDN_QK_HEADS
    for h0 in range(0, nh, GDN_HEAD_GROUP):
        heads = range(h0, h0 + GDN_HEAD_GROUP)
        qk_heads = range(h0 // rep, (h0 + GDN_HEAD_GROUP) // rep)
        qn = {j: _l2norm(q_ref[0, :, j * GDN_DK:(j + 1) * GDN_DK].astype(F32)) * (GDN_DK ** -0.5) for j in qk_heads}
        kn = {j: _l2norm(k_ref[0, :, j * GDN_DK:(j + 1) * GDN_DK].astype(F32)) for j in qk_heads}
        kk = {j: _dot_nt(kn[j], kn[j]) for j in qk_heads}
        qk = {j: _dot_nt(qn[j], kn[j]) for j in qk_heads}
        gc_col = {h: gcum[:, nh + h:nh + h + 1] for h in heads}
        gc_last = {h: gcum[ch - 1:ch, nh + h:nh + h + 1] for h in heads}
        beta_col = {h: beta[:, h:h + 1] for h in heads}
        dec = {h: jnp.exp(jnp.where(incl, gc_col[h] - gcum_t[nh + h:nh + h + 1, 0:ch], NEG)) for h in heads}
        t_inv = dict(zip(heads, _unit_lower_inverse(
            [jnp.where(strict, kk[h // rep] * dec[h], 0.0) * beta_col[h] for h in heads], eye)))
        eg = {h: jnp.exp(gc_col[h]) for h in heads}
        s_prev = {h: s_ref[0, h] for h in heads}
        both = {h: _dot(jnp.concatenate([kn[h // rep] * (beta_col[h] * eg[h]), qn[h // rep] * eg[h]], axis=0),
                        s_prev[h]) for h in heads}
        u = {h: _dot(t_inv[h], v_ref[0, :, h * GDN_DV:(h + 1) * GDN_DV].astype(F32) * beta_col[h] - both[h][:ch])
             for h in heads}
        o = {h: both[h][ch:] + _dot(qk[h // rep] * dec[h], u[h]) for h in heads}
        s_new = {h: s_prev[h] * jnp.exp(gc_last[h])
                 + _dot_tn(kn[h // rep] * jnp.exp(gc_last[h] - gc_col[h]), u[h]) for h in heads}
        for h in heads:
            s_ref[0, h] = s_new[h]
            z_h = z_ref[0, :, h * GDN_DV:(h + 1) * GDN_DV].astype(F32)
            o_ref[0, :, h * GDN_DV:(h + 1) * GDN_DV] = (
                _rms(o[h]) * nw_ref[...] * jax.nn.silu(z_h)).astype(o_ref.dtype)


def _gdn_gate_params(dt_bias, a_log):
    nh = GDN_V_HEADS
    dtb = jnp.zeros((1, LANES), F32).at[0, nh:2 * nh].set(dt_bias)
    alog = jnp.zeros((1, LANES), F32).at[0, nh:2 * nh].set(a_log)
    return dtb, alog


def gdn_prompt(main, tail, s0, dt_bias, a_log, norm_w, *, bsz, seq):
    ch = GDN_CHUNK
    nc = seq // ch
    dtb, alog = _gdn_gate_params(dt_bias, a_log)
    small = [dtb, alog, norm_w.reshape(1, -1)]
    sspec = pl.BlockSpec((1, GDN_V_HEADS, GDN_DK, GDN_DV), lambda b, c: (b, 0, 0, 0))
    return pl.pallas_call(
        _gdn_kernel,
        grid=(bsz, nc),
        in_specs=[
            pl.BlockSpec((1, ch, GDN_QK_W), lambda b, c: (b, c, 0)),
            pl.BlockSpec((1, ch, GDN_QK_W), lambda b, c: (b, c, 1)),
            pl.BlockSpec((1, ch, GDN_VW), lambda b, c: (b, c, 1)),
            pl.BlockSpec((1, ch, GDN_VW), lambda b, c: (b, c, 2)),
            pl.BlockSpec((1, ch, LANES), lambda b, c: (b, c, 0)),
        ] + [_small(a) for a in small] + [sspec],
        out_specs=[pl.BlockSpec((1, ch, GDN_VW), lambda b, c: (b, c, 0)), sspec],
        out_shape=[jax.ShapeDtypeStruct((bsz, seq, GDN_VW), BF16),
                   jax.ShapeDtypeStruct((bsz, GDN_V_HEADS, GDN_DK, GDN_DV), F32)],
        compiler_params=_cparams("parallel", "arbitrary"),
        name="gdn_prompt",
    )(main, main, main, main, tail, *small, s0)


def _gdn_conv_step(c0_ref, w_ref, x_row, c0, c1):
    acc = w_ref[GDN_CONV - 1:GDN_CONV, c0:c1] * x_row
    for i in range(GDN_CONV - 1):
        acc = acc + w_ref[i:i + 1, c0:c1] * c0_ref[0, i:i + 1, c0:c1]
    return acc


def _gdn_step_kernel(q_ref, k_ref, v_ref, z_ref, ba_ref, c0_ref, cw_ref, dtb_ref, alog_ref, nw_ref, s0_ref,
                     o_ref, co_ref, s_ref):
    nh = GDN_V_HEADS
    q_raw, k_raw, v_raw = q_ref[0], k_ref[0], v_ref[0]
    q = jax.nn.silu(_gdn_conv_step(c0_ref, cw_ref, q_raw, 0, GDN_QK_W))
    k = jax.nn.silu(_gdn_conv_step(c0_ref, cw_ref, k_raw, GDN_QK_W, 2 * GDN_QK_W))
    v = jax.nn.silu(_gdn_conv_step(c0_ref, cw_ref, v_raw, 2 * GDN_QK_W, GDN_QKV))
    co_ref[0, 0:GDN_CONV - 2, :] = c0_ref[0, 1:GDN_CONV - 1, :]
    co_ref[0, GDN_CONV - 2:GDN_CONV - 1, 0:GDN_QK_W] = q_raw
    co_ref[0, GDN_CONV - 2:GDN_CONV - 1, GDN_QK_W:2 * GDN_QK_W] = k_raw
    co_ref[0, GDN_CONV - 2:GDN_CONV - 1, 2 * GDN_QK_W:GDN_QKV] = v_raw

    beta, g = _gdn_gates(ba_ref[0], dtb_ref, alog_ref)
    eg_all = jnp.exp(g)
    eye = _iota((LANES, LANES), 0) == _iota((LANES, LANES), 1)
    for j in range(GDN_QK_HEADS):
        qn = _l2norm(q[:, j * GDN_DK:(j + 1) * GDN_DK]) * (GDN_DK ** -0.5)
        kn = _l2norm(k[:, j * GDN_DK:(j + 1) * GDN_DK])
        qk = jnp.sum(qn * kn, axis=-1, keepdims=True)
        q_col = _row_to_col(qn, eye)
        k_col = _row_to_col(kn, eye)
        for h in range(j * (nh // GDN_QK_HEADS), (j + 1) * (nh // GDN_QK_HEADS)):
            b_h = beta[:, h:h + 1]
            eg = eg_all[:, nh + h:nh + h + 1]
            s_prev = s0_ref[0, h]
            ks = jnp.sum(s_prev * k_col, axis=0, keepdims=True)
            qs = jnp.sum(s_prev * q_col, axis=0, keepdims=True)
            v_h = v[:, h * GDN_DV:(h + 1) * GDN_DV]
            u = v_h * b_h - (b_h * eg) * ks
            o = eg * qs + qk * u
            s_ref[0, h] = s_prev * eg + k_col * u
            z_h = z_ref[0, :, h * GDN_DV:(h + 1) * GDN_DV]
            o_ref[0, :, h * GDN_DV:(h + 1) * GDN_DV] = _rms(o) * nw_ref[...] * jax.nn.silu(z_h)


def gdn_step(main, tail, conv0, s0, conv_w, dt_bias, a_log, norm_w):
    bsz = main.shape[0]
    dtb, alog = _gdn_gate_params(dt_bias, a_log)
    small = [conv_w, dtb, alog, norm_w.reshape(1, -1)]
    sspec = pl.BlockSpec((1, GDN_V_HEADS, GDN_DK, GDN_DV), lambda b: (b, 0, 0, 0))
    cspec = pl.BlockSpec((1, GDN_CONV - 1, GDN_QKV), lambda b: (b, 0, 0))
    return pl.pallas_call(
        _gdn_step_kernel,
        grid=(bsz,),
        in_specs=[
            pl.BlockSpec((1, 1, GDN_QK_W), lambda b: (b, 0, 0)),
            pl.BlockSpec((1, 1, GDN_QK_W), lambda b: (b, 0, 1)),
            pl.BlockSpec((1, 1, GDN_VW), lambda b: (b, 0, 1)),
            pl.BlockSpec((1, 1, GDN_VW), lambda b: (b, 0, 2)),
            pl.BlockSpec((1, 1, LANES), lambda b: (b, 0, 0)),
            cspec,
        ] + [_small(a) for a in small] + [sspec],
        out_specs=[pl.BlockSpec((1, 1, GDN_VW), lambda b: (b, 0, 0)), cspec, sspec],
        out_shape=[jax.ShapeDtypeStruct((bsz, 1, GDN_VW), F32),
                   jax.ShapeDtypeStruct(conv0.shape, F32),
                   jax.ShapeDtypeStruct(s0.shape, F32)],
        compiler_params=_cparams("parallel"),
        name="gdn_step",
    )(main, main, main, main, tail, conv0, *small, s0)


def _pad_tail(w):
    return jnp.pad(w, ((0, 0), (0, LANES - w.shape[1])))


def _prep_hyb_in(w):
    a = A_WIDTH
    q, k, v = w[:, 0:a], w[:, a:2 * a], w[:, 2 * a:3 * a]
    z = w[:, 3 * a:3 * a + SSM_D_INNER]
    x0 = 3 * a + SSM_D_INNER
    xs = w[:, x0:x0 + SSM_D_INNER]
    bc = w[:, x0 + SSM_D_INNER:x0 + SSM_XBC]
    dt = w[:, x0 + SSM_XBC:]
    return jnp.concatenate([z, xs, q, k, v, bc], axis=1).astype(BF16), _pad_tail(dt).astype(BF16)


def _prep_gdn_in(w):
    return w[:, :GDN_MAIN].astype(BF16), _pad_tail(w[:, GDN_MAIN:]).astype(BF16)


HYB_CONV_COLS = ((SSM_D_INNER, 2 * SSM_D_INNER), (HYB_BC0, HYB_MAIN))


def _hyb_cols(a):
    out = jnp.zeros(a.shape[:-1] + (HYB_MAIN,), F32)
    (x0, x1), (b0, b1) = HYB_CONV_COLS
    return out.at[..., x0:x1].set(a[..., :SSM_D_INNER]).at[..., b0:b1].set(a[..., SSM_D_INNER:])


def _window_to_lanes(c):
    n, b, past, h, dh = c.shape
    return jnp.transpose(c, (0, 1, 3, 4, 2)).reshape(n, b, h * dh, past)


def _window_from_lanes(c):
    n, b, _, past = c.shape
    return jnp.transpose(c.reshape(n, b, A_HEADS, A_HEAD_DIM, past), (0, 1, 4, 2, 3))


def _row_tile(m, cap):
    return m if m <= cap else cap


def kernel(x_prompt, x_sample, cache_attn_k, cache_attn_v, state_ssm_conv, state_ssm, state_gdn_conv, state_gdn, rel_bias, norm_mix_pre, norm_mix_post, norm_ffn_pre, norm_ffn_post, w_hyb_in, ssm_conv_w, ssm_conv_b, ssm_dt_bias, ssm_a_log, ssm_d, ssm_norm_w, w_hyb_out, w_gdn_in, gdn_conv_w, gdn_dt_bias, gdn_a_log, gdn_norm_w, w_gdn_out, w_ffn_gate, w_ffn_up, w_ffn_down):
    depth = norm_mix_pre.shape[0]
    d_model = x_prompt.shape[-1]
    n_hyb, n_gdn = w_hyb_in.shape[0], w_gdn_in.shape[0]

    hyb_in = [_prep_hyb_in(w_hyb_in[i]) for i in range(n_hyb)]
    hyb_out = [(w_hyb_out[i, :A_WIDTH].astype(BF16), w_hyb_out[i, A_WIDTH:].astype(BF16)) for i in range(n_hyb)]
    gdn_in = [_prep_gdn_in(w_gdn_in[i]) for i in range(n_gdn)]
    gdn_out = [w_gdn_out[i].astype(BF16) for i in range(n_gdn)]
    ffn_w = [(w_ffn_gate[l].astype(BF16), w_ffn_up[l].astype(BF16), w_ffn_down[l].astype(BF16))
             for l in range(depth)]
    bias_tiles = _attn_bias_rows(rel_bias)

    def trunk(x3, k_pre, v_pre, sconv, sssm, gconv, gstate):
        bsz, seq, _ = x3.shape
        m = bsz * seq
        step = seq == 1
        tm_big = _row_tile(m, 1024)
        tm = _row_tile(m, 512)
        x = x3.reshape(m, d_model)
        nk, nv, nsc, nss, ngc, ngs = [], [], [], [], [], []
        rolled = None
        for l in range(depth):
            i = l // 2
            if l % 2 == 0:
                w_main, w_tail = hyb_in[i]
                ssm_args = (ssm_dt_bias[i], ssm_a_log[i], ssm_d[i], ssm_norm_w[i])
                if step:
                    main, tail = inproj(x, norm_mix_pre[l], w_main, w_tail, tm=tm_big, tn=512)
                    main3 = main.reshape(bsz, seq, HYB_MAIN)
                    tail3 = tail.reshape(bsz, seq, LANES)
                    col = lambda c0: main[:, c0:c0 + A_WIDTH].reshape(bsz, A_WIDTH, 1)
                    o_attn, *rolled = attention_step(
                        col(HYB_Q0), col(HYB_K0), col(HYB_V0), k_pre, v_pre, rolled,
                        _attn_logw(rel_bias, k_pre.shape[-1]), layer=i)
                    y, c_new, s_new = ssd_step(main3, tail3, sconv[i], sssm[i], ssm_conv_w[i], ssm_conv_b[i],
                                               *ssm_args)
                else:
                    main, tail, hist = inproj_conv(
                        x, norm_mix_pre[l], w_main, w_tail, _hyb_cols(ssm_conv_w[i]),
                        _hyb_cols(ssm_conv_b[i][None]), _hyb_cols(sconv[i]), seq=seq, conv_cols=HYB_CONV_COLS,
                        tm=tm_big, tn=512)
                    main3 = main.reshape(bsz, seq, HYB_MAIN)
                    tail3 = tail.reshape(bsz, seq, LANES)
                    c_new = jnp.concatenate([hist[..., a:b] for a, b in HYB_CONV_COLS], axis=-1)
                    o_attn = attention_prompt(main3, bias_tiles, bsz=bsz, seq=seq)
                    keep = min(A_PATTERNS[-1][0], seq)
                    k_new = main3[:, seq - keep:, HYB_K0:HYB_K0 + A_WIDTH].astype(F32)
                    v_new = main3[:, seq - keep:, HYB_V0:HYB_V0 + A_WIDTH].astype(F32)
                    y, s_new = ssd_prompt(main3, tail3, sssm[i], *ssm_args, bsz=bsz, seq=seq)
                    nk.append(k_new.reshape(bsz, -1, A_HEADS, A_HEAD_DIM))
                    nv.append(v_new.reshape(bsz, -1, A_HEADS, A_HEAD_DIM))
                nsc.append(c_new)
                nss.append(s_new)
                x = outproj([o_attn.reshape(m, A_WIDTH), y.reshape(m, SSM_D_INNER)], list(hyb_out[i]),
                            x, norm_mix_post[l], tm=tm)
            else:
                w_main, w_tail = gdn_in[i]
                gdn_args = (gdn_dt_bias[i], gdn_a_log[i], gdn_norm_w[i])
                if step:
                    main, tail = inproj(x, norm_mix_pre[l], w_main, w_tail, tm=tm_big, tn=512)
                    o, c_new, s_new = gdn_step(main.reshape(bsz, seq, GDN_MAIN), tail.reshape(bsz, seq, LANES),
                                               gconv[i], gstate[i], gdn_conv_w[i], *gdn_args)
                else:
                    pad = lambda a: jnp.pad(a, [(0, 0)] * (a.ndim - 1) + [(0, GDN_MAIN - GDN_QKV)])
                    main, tail, hist = inproj_conv(
                        x, norm_mix_pre[l], w_main, w_tail, pad(gdn_conv_w[i]), jnp.zeros((1, GDN_MAIN), F32),
                        pad(gconv[i]), seq=seq, conv_cols=((0, GDN_QKV),), tm=tm_big, tn=512)
                    c_new = hist[..., :GDN_QKV]
                    o, s_new = gdn_prompt(main.reshape(bsz, seq, GDN_MAIN), tail.reshape(bsz, seq, LANES),
                                          gstate[i], *gdn_args, bsz=bsz, seq=seq)
                ngc.append(c_new)
                ngs.append(s_new)
                x = outproj([o.reshape(m, GDN_VW)], [gdn_out[i]], x, norm_mix_post[l], tm=tm)
            wg, wu, wd = ffn_w[l]
            x = ffn(x, norm_ffn_pre[l], wg, wu, wd, norm_ffn_post[l], tm=tm)
        k_out, v_out = [_window_from_lanes(r) for r in rolled] if step else (jnp.stack(nk), jnp.stack(nv))
        return (x.reshape(bsz, seq, d_model), k_out, v_out, jnp.stack(nsc), jnp.stack(nss),
                jnp.stack(ngc), jnp.stack(ngs))

    bsz = x_prompt.shape[0]
    dt_p = x_prompt.dtype
    p_sc0 = jnp.zeros((n_hyb, bsz, SSM_CONV - 1, SSM_XBC), dt_p)
    p_ss0 = jnp.zeros((n_hyb, bsz, SSM_HEADS, SSM_HEAD_DIM, SSM_STATE), F32)
    p_gc0 = jnp.zeros((n_gdn, bsz, GDN_CONV - 1, GDN_QKV), dt_p)
    p_gs0 = jnp.zeros((n_gdn, bsz, GDN_V_HEADS, GDN_DK, GDN_DV), F32)
    y_prompt, pk, pv, psc, pss, pgc, pgs = trunk(x_prompt, None, None, p_sc0, p_ss0, p_gc0, p_gs0)
    y_sample, sk, sv, ssc, sss, sgc, sgs = trunk(
        x_sample, _window_to_lanes(cache_attn_k), _window_to_lanes(cache_attn_v), state_ssm_conv, state_ssm,
        state_gdn_conv, state_gdn)
    return (y_prompt, y_sample, pk, pv, psc, pss, pgc, pgs, sk, sv, ssc, sss, sgc, sgs)
```

```python
import functools
import math

import numpy as np
import jax
import jax.numpy as jnp
from jax import lax
from jax.experimental import pallas as pl
from jax.experimental.pallas import tpu as pltpu

F32 = jnp.float32
BF16 = jnp.bfloat16
EPS = 1e-6
NEG = -1e30
HIGHEST = lax.Precision.HIGHEST

VMEM_LIMIT_BYTES = 56 * 1024 * 1024
LANES = 128

A_HEADS = 8
A_HEAD_DIM = 64
A_WIDTH = A_HEADS * A_HEAD_DIM
A_PATTERNS = ((128, 1), (512, 4), (2048, 16))
A_BAND = 128
ATTN_GROUP = 4
REL_BUCKETS = 32
REL_MAX_DIST = 2048

SSM_D_INNER = 1024
SSM_HEAD_DIM = 64
SSM_HEADS = SSM_D_INNER // SSM_HEAD_DIM
SSM_GROUPS = 2
SSM_STATE = 128
SSM_CONV = 4
SSM_CHUNK = 128
SSM_BC = 2 * SSM_GROUPS * SSM_STATE
SSM_XBC = SSM_D_INNER + SSM_BC

GDN_QK_HEADS = 8
GDN_V_HEADS = 16
GDN_DK = 128
GDN_DV = 128
GDN_CONV = 4
GDN_CHUNK = 64
GDN_STEP_CHUNKS = 1
GDN_QK_W = GDN_QK_HEADS * GDN_DK
GDN_VW = GDN_V_HEADS * GDN_DV
GDN_QKV = 2 * GDN_QK_W + GDN_VW

HYB_MAIN = 2 * SSM_D_INNER + 3 * A_WIDTH + SSM_BC
HYB_Q0 = 2 * SSM_D_INNER
HYB_K0 = HYB_Q0 + A_WIDTH
HYB_V0 = HYB_K0 + A_WIDTH
HYB_BC0 = HYB_V0 + A_WIDTH
GDN_MAIN = GDN_QKV + GDN_VW


def _cparams(*sem):
    return pltpu.CompilerParams(dimension_semantics=sem, vmem_limit_bytes=VMEM_LIMIT_BYTES)


def _rms(x):
    return x * lax.rsqrt(jnp.mean(x * x, axis=-1, keepdims=True) + EPS)


def _dot(a, b, **kw):
    return jnp.dot(a, b, preferred_element_type=F32, **kw)


def _dot_nt(a, b, **kw):
    return lax.dot_general(a, b, (((1,), (1,)), ((), ())), preferred_element_type=F32, **kw)


def _dot_tn(a, b, **kw):
    return lax.dot_general(a, b, (((0,), (0,)), ((), ())), preferred_element_type=F32, **kw)


def _iota(shape, dim):
    return lax.broadcasted_iota(jnp.int32, shape, dim)


def _inproj_kernel(x_ref, g_ref, w_ref, wt_ref, o_ref, t_ref, h_ref):
    @pl.when(pl.program_id(1) == 0)
    def _():
        hb = (_rms(x_ref[...]) * g_ref[...]).astype(BF16)
        h_ref[...] = hb
        t_ref[...] = _dot(hb, wt_ref[...])

    o_ref[...] = _dot(h_ref[...], w_ref[...])


def inproj(x, g, w_main, w_tail, *, tm, tn):
    m, d = x.shape
    n = w_main.shape[1]
    return pl.pallas_call(
        _inproj_kernel,
        grid=(m // tm, n // tn),
        in_specs=[
            pl.BlockSpec((tm, d), lambda i, j: (i, 0)),
            pl.BlockSpec((1, d), lambda i, j: (0, 0)),
            pl.BlockSpec((d, tn), lambda i, j: (0, j)),
            pl.BlockSpec((d, LANES), lambda i, j: (0, 0)),
        ],
        out_specs=[
            pl.BlockSpec((tm, tn), lambda i, j: (i, j)),
            pl.BlockSpec((tm, LANES), lambda i, j: (i, 0)),
        ],
        out_shape=[jax.ShapeDtypeStruct((m, n), F32), jax.ShapeDtypeStruct((m, LANES), F32)],
        scratch_shapes=[pltpu.VMEM((tm, d), BF16)],
        compiler_params=_cparams("parallel", "arbitrary"),
        name="inproj",
    )(x, g.reshape(1, d), w_main, w_tail)


CONV_TAPS = 4
CONV_ROWS = 64
PROJ_K_SLICES = 4
CONV_BASE = 8


def _inproj_conv_kernel(x_ref, g_ref, w_ref, wt_ref, cw_ref, cb_ref, c0_ref, o_ref, t_ref, so_ref,
                        h_ref, xp0_ref, xp1_ref, carry_ref, *, n_col, conv_tiles, tiles_per_seq):
    s = pl.program_id(0)
    n_tiles = pl.num_programs(0) - 1
    tm = o_ref.shape[0]
    lo = CONV_BASE - (CONV_TAPS - 1)
    cur = jnp.minimum(s, n_tiles - 1)
    prv = jnp.maximum(s - 1, 0)
    pi, pj = prv // n_col, prv % n_col
    prv_conv = functools.reduce(jnp.logical_or, [jnp.logical_and(pj >= a, pj < b) for a, b in conv_tiles])
    first = pi % tiles_per_seq == 0
    both = lambda a, b: jnp.logical_and(a, b)

    @pl.when(both(cur % n_col == 0, s < n_tiles))
    def _():
        hb = (_rms(x_ref[...]) * g_ref[...]).astype(BF16)
        h_ref[...] = hb
        t_ref[...] = _dot(hb, wt_ref[...])

    for parity, (cur_ref, prv_ref) in enumerate(((xp0_ref, xp1_ref), (xp1_ref, xp0_ref))):
        here = s % 2 == parity

        def project(cur_ref=cur_ref):
            cur_ref[CONV_BASE:CONV_BASE + tm, :] = _dot(h_ref[...], w_ref[...])

        def raw_tail(prv_ref=prv_ref):
            tail = prv_ref[CONV_BASE + tm - (CONV_TAPS - 1):CONV_BASE + tm, :]
            so_ref[pj, pi] = tail
            return tail

        if parity == 0:
            @pl.when(s == 0)
            def _():
                project()

        @pl.when(both(here, both(s > 0, jnp.logical_not(prv_conv))))
        def _():
            project()
            raw_tail()
            o_ref[...] = prv_ref[CONV_BASE:CONV_BASE + tm, :].astype(o_ref.dtype)

        @pl.when(both(here, both(both(s > 0, prv_conv), first)))
        def _():
            prv_ref[lo:CONV_BASE, :] = c0_ref[pj, pi // tiles_per_seq]

        @pl.when(both(here, both(both(s > 0, prv_conv), jnp.logical_not(first))))
        def _():
            prv_ref[lo:CONV_BASE, :] = carry_ref[pj]

        @pl.when(both(here, both(s > 0, prv_conv)))
        def _():
            def conv_chunk(r0):
                ext = prv_ref[r0:r0 + CONV_BASE + CONV_ROWS, :]
                conv = cb_ref[pj] + cw_ref[pj, CONV_TAPS - 1:CONV_TAPS, :] * ext[CONV_BASE:]
                for back in range(1, CONV_TAPS):
                    tap = CONV_TAPS - 1 - back
                    conv = conv + cw_ref[pj, tap:tap + 1, :] * pltpu.roll(ext, back, 0)[CONV_BASE:]
                o_ref[r0:r0 + CONV_ROWS, :] = jax.nn.silu(conv).astype(o_ref.dtype)

            d = h_ref.shape[1]
            n_chunks = tm // CONV_ROWS
            acc = None
            for kq in range(PROJ_K_SLICES):
                ks = slice(kq * d // PROJ_K_SLICES, (kq + 1) * d // PROJ_K_SLICES)
                part = _dot(h_ref[:, ks], w_ref[ks, :])
                acc = part if acc is None else acc + part
                for c in range(kq * n_chunks // PROJ_K_SLICES, (kq + 1) * n_chunks // PROJ_K_SLICES):
                    conv_chunk(c * CONV_ROWS)
            cur_ref[CONV_BASE:CONV_BASE + tm, :] = acc
            carry_ref[pj] = raw_tail()


def inproj_conv(x, g, w_main, w_tail, conv_w, conv_b, conv0, *, seq, conv_cols, tm, tn):
    m, d = x.shape
    n = w_main.shape[1]
    assert seq % tm == 0 and all(a % tn == 0 and b % tn == 0 for a, b in conv_cols)
    tiles_per_seq = seq // tm
    n_row, n_col = m // tm, n // tn
    n_tiles = n_row * n_col
    conv_tiles = tuple((a // tn, b // tn) for a, b in conv_cols)
    hist = CONV_TAPS - 1
    cur = lambda s: jnp.minimum(s, n_tiles - 1)
    prv = lambda s: jnp.maximum(s - 1, 0)
    by_tile = lambda a: jnp.moveaxis(a.reshape(a.shape[:-1] + (n_col, tn)), -2, 0)
    main, tail, hist_rows = pl.pallas_call(
        functools.partial(_inproj_conv_kernel, n_col=n_col, conv_tiles=conv_tiles, tiles_per_seq=tiles_per_seq),
        grid=(n_tiles + 1,),
        in_specs=[
            pl.BlockSpec((tm, d), lambda s: (cur(s) // n_col, 0)),
            pl.BlockSpec((1, d), lambda s: (0, 0)),
            pl.BlockSpec((None, d, tn), lambda s: (cur(s) % n_col, 0, 0)),
            pl.BlockSpec((d, LANES), lambda s: (0, 0)),
            pl.BlockSpec((n_col, CONV_TAPS, tn), lambda s: (0, 0, 0)),
            pl.BlockSpec((n_col, 1, tn), lambda s: (0, 0, 0)),
            pl.BlockSpec((n_col, m // seq, hist, tn), lambda s: (0, 0, 0, 0)),
        ],
        out_specs=[
            pl.BlockSpec((tm, tn), lambda s: (prv(s) // n_col, prv(s) % n_col)),
            pl.BlockSpec((tm, LANES), lambda s: (cur(s) // n_col, 0)),
            pl.BlockSpec((n_col, n_row, hist, tn), lambda s: (0, 0, 0, 0)),
        ],
        out_shape=[jax.ShapeDtypeStruct((m, n), BF16), jax.ShapeDtypeStruct((m, LANES), F32),
                   jax.ShapeDtypeStruct((n_col, n_row, hist, tn), F32)],
        scratch_shapes=[pltpu.VMEM((tm, d), BF16), pltpu.VMEM((CONV_BASE + tm, tn), F32),
                        pltpu.VMEM((CONV_BASE + tm, tn), F32), pltpu.VMEM((n_col, hist, tn), F32)],
        compiler_params=_cparams("arbitrary"),
        name="inproj_conv",
    )(x, g.reshape(1, d), by_tile(w_main), w_tail, by_tile(conv_w), by_tile(conv_b), by_tile(conv0))
    hist_rows = jnp.moveaxis(hist_rows, 0, -2).reshape(n_row, hist, n)
    return main, tail, hist_rows[tiles_per_seq - 1::tiles_per_seq]


def _outproj_kernel(*refs, n_in):
    a_refs, w_refs = refs[:n_in], refs[n_in:2 * n_in]
    x_ref, g_ref, o_ref = refs[2 * n_in:]
    acc = None
    for a_ref, w_ref in zip(a_refs, w_refs):
        t = _dot(a_ref[...].astype(BF16), w_ref[...])
        acc = t if acc is None else acc + t
    o_ref[...] = x_ref[...] + _rms(acc) * g_ref[...]


def outproj(acts, weights, x, g, *, tm):
    m, d = x.shape
    n_in = len(acts)
    in_specs = [pl.BlockSpec((tm, a.shape[1]), lambda i: (i, 0)) for a in acts]
    in_specs += [pl.BlockSpec(w.shape, lambda i: (0, 0)) for w in weights]
    in_specs += [pl.BlockSpec((tm, d), lambda i: (i, 0)), pl.BlockSpec((1, d), lambda i: (0, 0))]
    return pl.pallas_call(
        functools.partial(_outproj_kernel, n_in=n_in),
        grid=(m // tm,),
        in_specs=in_specs,
        out_specs=pl.BlockSpec((tm, d), lambda i: (i, 0)),
        out_shape=jax.ShapeDtypeStruct((m, d), F32),
        compiler_params=_cparams("parallel"),
        name="outproj",
    )(*acts, *weights, x, g.reshape(1, d))


def _ffn_kernel(x_ref, g1_ref, wg_ref, wu_ref, wd_ref, g2_ref, o_ref):
    x = x_ref[...]
    h = (_rms(x) * g1_ref[...]).astype(BF16)
    a = jax.nn.silu(_dot(h, wg_ref[...])) * _dot(h, wu_ref[...])
    o_ref[...] = x + _rms(_dot(a.astype(BF16), wd_ref[...])) * g2_ref[...]


def ffn(x, g1, wg, wu, wd, g2, *, tm):
    m, d = x.shape
    resident = lambda a: pl.BlockSpec(a.shape, lambda i: (0, 0), pipeline_mode=pl.Buffered(1))
    return pl.pallas_call(
        _ffn_kernel,
        grid=(m // tm,),
        in_specs=[
            pl.BlockSpec((tm, d), lambda i: (i, 0)),
            pl.BlockSpec((1, d), lambda i: (0, 0)),
            resident(wg), resident(wu), resident(wd),
            pl.BlockSpec((1, d), lambda i: (0, 0)),
        ],
        out_specs=pl.BlockSpec((tm, d), lambda i: (i, 0)),
        out_shape=jax.ShapeDtypeStruct((m, d), F32),
        compiler_params=_cparams("parallel"),
        name="ffn",
    )(x, g1.reshape(1, d), wg, wu, wd, g2.reshape(1, d))


def _rel_buckets(dist):
    max_exact = REL_BUCKETS // 2
    n = np.maximum(dist, 1).astype(np.float32)
    large = max_exact + (np.log(n / max_exact) / math.log(REL_MAX_DIST / max_exact)
                         * (REL_BUCKETS - max_exact)).astype(np.int32)
    large = np.minimum(large, REL_BUCKETS - 1)
    return np.where(dist < max_exact, dist, large).astype(np.int32)


def _attn_bias_rows(rel_bias):
    u = np.arange(2 * A_BAND)
    valid = u <= A_BAND
    rows = []
    for (_, d) in A_PATTERNS:
        b = rel_bias[_rel_buckets(np.where(valid, A_BAND - u, 0) * d)]
        rows.append(jnp.where(valid[:, None], b.astype(F32), NEG))
    tl = jnp.transpose(jnp.stack(rows), (2, 0, 1))
    tl = tl.reshape(A_HEADS // 2, 2, len(A_PATTERNS), 2 * A_BAND)
    tl = jnp.transpose(tl, (0, 2, 1, 3))[:, :, :, None, :]
    return jnp.broadcast_to(tl, tl.shape[:3] + (8, 2 * A_BAND))


def _attn_kernel(qin_ref, kin_ref, vin_ref, brow_ref, o_ref, m0_ref, m1_ref, l0_ref, l1_ref, acc_ref,
                 q_ref, k_ref, v_ref, qc_ref, kc_ref, vc_ref, b_ref, *, seq):
    n_tiles = seq // A_BAND
    lane = _iota((A_BAND, LANES), 1)
    head0 = lane < A_HEAD_DIM
    m_refs, l_refs = (m0_ref, m1_ref), (l0_ref, l1_ref)
    q_ref[...] = qin_ref[...].astype(F32) * (A_HEAD_DIM ** -0.5)
    k_ref[...] = kin_ref[...].astype(F32)
    v_ref[...] = vin_ref[...].astype(F32)
    for p in range(len(A_PATTERNS)):
        for h in range(2):
            row = jnp.broadcast_to(brow_ref[0, p, h, 0:1, :], (A_BAND, 2 * A_BAND))
            b_ref[0, p, h] = pltpu.roll(row, 0, 1, stride=1, stride_axis=0)

    first_p = max(range(len(A_PATTERNS)), key=lambda p: A_PATTERNS[p][1])
    d_first = A_PATTERNS[first_p][1]
    class_len = seq // d_first

    def to_class_major(r, carry):
        dst = pl.ds(pl.multiple_of(r * class_len, class_len), class_len)
        for src_ref, dst_ref in ((q_ref, qc_ref), (k_ref, kc_ref), (v_ref, vc_ref)):
            dst_ref[dst, :] = src_ref[pl.ds(r, class_len, stride=d_first), :]
        return carry

    lax.fori_loop(0, d_first, to_class_major, 0)

    for p in [first_p] + [p for p in range(len(A_PATTERNS)) if p != first_p]:
        d = A_PATTERNS[p][1]
        fresh = p == first_p
        tiles_per_class = n_tiles // d

        def load_tile(idx, d=d, tiles_per_class=tiles_per_class, fresh=fresh):
            r = idx // tiles_per_class
            t = idx % tiles_per_class
            start = r + t * (d * A_BAND)
            has_prev = t > 0
            rows = pl.ds(start, A_BAND, stride=d) if d > 1 else pl.ds(pl.multiple_of(start, A_BAND), A_BAND)
            if fresh:
                base = r * class_len + t * A_BAND
                crow = pl.ds(pl.multiple_of(base, A_BAND), A_BAND)
                cprev = pl.ds(pl.multiple_of(jnp.where(has_prev, base - A_BAND, base), A_BAND), A_BAND)
                return dict(
                    rows=rows, has_prev=has_prev, q=qc_ref[crow, :],
                    k2=jnp.concatenate([kc_ref[cprev, :], kc_ref[crow, :]], axis=0).astype(BF16),
                    v2=jnp.concatenate([vc_ref[cprev, :], vc_ref[crow, :]], axis=0).astype(BF16))
            prev = jnp.where(has_prev, start - d * A_BAND, start)
            prows = pl.ds(prev, A_BAND, stride=d) if d > 1 else pl.ds(pl.multiple_of(prev, A_BAND), A_BAND)
            return dict(
                rows=rows, has_prev=has_prev, q=q_ref[rows, :],
                k2=jnp.concatenate([k_ref[prows, :], k_ref[rows, :]], axis=0).astype(BF16),
                v2=jnp.concatenate([v_ref[prows, :], v_ref[rows, :]], axis=0).astype(BF16),
                acc=acc_ref[rows, :], m=[m_refs[h][rows, :] for h in range(2)],
                l=[l_refs[h][rows, :] for h in range(2)])

        def tile_group(idx, carry, load_tile=load_tile, p=p, fresh=fresh):
            tiles = [load_tile(idx + i * (n_tiles // ATTN_GROUP)) for i in range(ATTN_GROUP)]
            chains = [(tile, h) for tile in tiles for h in range(2)]
            col = _iota((A_BAND, 2 * A_BAND), 1)
            qh = [jnp.where(head0 if h == 0 else jnp.logical_not(head0), tile["q"], 0.0).astype(BF16)
                  for tile, h in chains]
            s = [_dot_nt(qh[c], tile["k2"])
                 + jnp.where(jnp.logical_and(col < A_BAND, jnp.logical_not(tile["has_prev"])), NEG, b_ref[0, p, h])
                 for c, (tile, h) in enumerate(chains)]
            if fresh:
                wide = lambda x: jnp.broadcast_to(x, (A_BAND, LANES))
                m_new = [wide(jnp.max(s[c], axis=-1, keepdims=True)) for c in range(len(chains))]
                pr = [jnp.exp(s[c] - jnp.concatenate([m_new[c], m_new[c]], axis=1)) for c in range(len(chains))]
                l_new = [wide(jnp.sum(pr[c], axis=-1, keepdims=True)) for c in range(len(chains))]
                acc_new = [_dot(pr[c].astype(BF16), tile["v2"]) for c, (tile, h) in enumerate(chains)]
            else:
                m_new = [jnp.maximum(tile["m"][h], jnp.max(s[c], axis=-1, keepdims=True))
                         for c, (tile, h) in enumerate(chains)]
                alpha = [jnp.exp(tile["m"][h] - m_new[c]) for c, (tile, h) in enumerate(chains)]
                pr = [jnp.exp(s[c] - jnp.concatenate([m_new[c], m_new[c]], axis=1)) for c in range(len(chains))]
                l_new = [alpha[c] * tile["l"][h] + jnp.sum(pr[c], axis=-1, keepdims=True)
                         for c, (tile, h) in enumerate(chains)]
                acc_new = [alpha[c] * tile["acc"] + _dot(pr[c].astype(BF16), tile["v2"])
                           for c, (tile, h) in enumerate(chains)]
            for c, (tile, h) in enumerate(chains):
                m_refs[h][tile["rows"], :] = m_new[c]
                l_refs[h][tile["rows"], :] = l_new[c]
                if h == 1:
                    acc_ref[tile["rows"], :] = jnp.where(head0, acc_new[c - 1], acc_new[c])
            return carry

        lax.fori_loop(0, n_tiles // ATTN_GROUP, tile_group, 0)

    lane_s = _iota((seq, LANES), 1)
    o_ref[...] = (acc_ref[...] / jnp.where(lane_s < A_HEAD_DIM, l0_ref[...], l1_ref[...])).astype(o_ref.dtype)


def attention_prompt(proj, bias_tiles, *, bsz, seq):
    hp = A_HEADS // 2
    qb, kb, vb = HYB_Q0 // LANES, HYB_K0 // LANES, HYB_V0 // LANES
    return pl.pallas_call(
        functools.partial(_attn_kernel, seq=seq),
        grid=(bsz, hp),
        in_specs=[
            pl.BlockSpec((None, seq, LANES), lambda b, h: (b, 0, qb + h)),
            pl.BlockSpec((None, seq, LANES), lambda b, h: (b, 0, kb + h)),
            pl.BlockSpec((None, seq, LANES), lambda b, h: (b, 0, vb + h)),
            pl.BlockSpec((1,) + bias_tiles.shape[1:], lambda b, h: (h, 0, 0, 0, 0)),
        ],
        out_specs=pl.BlockSpec((None, seq, LANES), lambda b, h: (b, 0, h)),
        out_shape=jax.ShapeDtypeStruct((bsz, seq, A_WIDTH), BF16),
        scratch_shapes=[pltpu.VMEM((seq, LANES), F32)] * 11
        + [pltpu.VMEM((1, len(A_PATTERNS), 2, A_BAND, 2 * A_BAND), F32)],
        compiler_params=_cparams("parallel", "parallel"),
        name="attn_prompt",
    )(proj, proj, proj, bias_tiles)


def _attn_logw(rel_bias, past):
    dist = np.arange(past + 1)
    count = np.zeros(past + 1, np.float64)
    for (w, d) in A_PATTERNS:
        count += ((dist % d == 0) & (dist <= w)).astype(np.float64)
    logc = np.where(count > 0, np.log(np.maximum(count, 1.0)), 0.0).astype(np.float32)
    lw = rel_bias[_rel_buckets(dist)].astype(F32).T + logc[None, :]
    return jnp.where((count > 0)[None, :], lw, NEG)


def _step_scores(xk_ref, q_col, kn_col, lw_ref, lw0_ref, s_ref):
    qs = q_col * (A_HEAD_DIM ** -0.5)
    s_new = []
    for h in range(A_HEADS):
        rows = slice(h * A_HEAD_DIM, (h + 1) * A_HEAD_DIM)
        s_ref[h:h + 1, :] = jnp.sum(xk_ref[0, 0, rows, :] * qs[rows], axis=0, keepdims=True)
        s_new.append(jnp.sum(kn_col[rows] * qs[rows], axis=0, keepdims=True))
    s = s_ref[...] + lw_ref[...]
    s_new = jnp.concatenate(s_new, axis=0) + lw0_ref[:, :1]
    m = jnp.maximum(jnp.max(s, axis=-1, keepdims=True), s_new)
    p = jnp.exp(s - m)
    p_new = jnp.exp(s_new - m)
    den = jnp.sum(p, axis=-1, keepdims=True) + p_new
    return p, p_new, den


def _step_output(xv_ref, vn_col, p, p_new, den, o_ref):
    for h in range(A_HEADS):
        rows = slice(h * A_HEAD_DIM, (h + 1) * A_HEAD_DIM)
        pv = jnp.sum(xv_ref[0, 0, rows, :] * p[h:h + 1, :], axis=-1, keepdims=True)
        o_ref[0, rows, :] = (pv + p_new[h:h + 1, :] * vn_col[rows]) / den[h:h + 1, :]


def _attn_step_roll_kernel(q_ref, kn_ref, vn_ref, lw_ref, lw0_ref, xk_ref, xv_ref, o_ref, ko_ref, vo_ref, s_ref,
                           *, layer, past):
    is_layer = pl.program_id(0) == layer
    newest = _iota((A_HEAD_DIM, past), 1) == past - 1
    for x_ref, n_ref, out_ref in ((xk_ref, kn_ref, ko_ref), (xv_ref, vn_ref, vo_ref)):
        for h in range(A_HEADS):
            rows = slice(h * A_HEAD_DIM, (h + 1) * A_HEAD_DIM)
            rolled = pltpu.roll(x_ref[0, 0, rows, :], past - 1, 1)
            out_ref[0, 0, rows, :] = jnp.where(jnp.logical_and(newest, is_layer), n_ref[0, rows, :], rolled)

    @pl.when(is_layer)
    def _():
        p, p_new, den = _step_scores(xk_ref, q_ref[0], kn_ref[0], lw_ref, lw0_ref, s_ref)
        _step_output(xv_ref, vn_ref[0], p, p_new, den, o_ref.at[0])

    @pl.when(jnp.logical_not(is_layer))
    def _():
        o_ref[...] = jnp.zeros_like(o_ref)


def _attn_step_append_kernel(q_ref, kn_ref, vn_ref, lw_ref, lw0_ref, xk_ref, xv_ref, ko_in, vo_in,
                             o_ref, ko_ref, vo_ref, s_ref, *, past):
    del ko_in, vo_in
    p, p_new, den = _step_scores(xk_ref, q_ref[0], kn_ref[0], lw_ref, lw0_ref, s_ref)
    _step_output(xv_ref, vn_ref[0], p, p_new, den, o_ref)
    newest = _iota((A_WIDTH, LANES), 1) == LANES - 1
    for x_ref, n_ref, out_ref in ((xk_ref, kn_ref, ko_ref), (xv_ref, vn_ref, vo_ref)):
        rolled = pltpu.roll(x_ref[0, 0, :, past - LANES:past], LANES - 1, 1)
        out_ref[0, 0] = jnp.where(newest, n_ref[0], rolled)


def attention_step(q_col, kn_col, vn_col, cache_k, cache_v, rolled, logw, *, layer):
    n_layers, bsz, w, past = cache_k.shape
    lw_cache = logw[:, past:0:-1]
    lw_new = jnp.broadcast_to(logw[:, :1], (A_HEADS, LANES))
    out_shape = [jax.ShapeDtypeStruct((bsz, w, 1), F32),
                 jax.ShapeDtypeStruct(cache_k.shape, cache_k.dtype),
                 jax.ShapeDtypeStruct(cache_v.shape, cache_v.dtype)]
    scratch = [pltpu.VMEM((A_HEADS, past), F32)]
    if rolled is None:
        assert layer == 0
        col = pl.BlockSpec((1, w, 1), lambda l, b: (b, 0, 0))
        win = pl.BlockSpec((1, 1, w, past), lambda l, b: (l, b, 0, 0))
        o_all, rolled_k, rolled_v = pl.pallas_call(
            functools.partial(_attn_step_roll_kernel, layer=layer, past=past),
            grid=(n_layers, bsz),
            in_specs=[col, col, col, _small(lw_cache), _small(lw_new), win, win],
            out_specs=[pl.BlockSpec((1, 1, w, 1), lambda l, b: (l, b, 0, 0)), win, win],
            out_shape=[jax.ShapeDtypeStruct((n_layers, bsz, w, 1), F32)] + out_shape[1:],
            scratch_shapes=scratch,
            compiler_params=_cparams("arbitrary", "arbitrary"),
            name="attn_step_roll",
        )(q_col, kn_col, vn_col, lw_cache, lw_new, cache_k, cache_v)
        return o_all[layer], rolled_k, rolled_v
    col = pl.BlockSpec((1, w, 1), lambda b: (b, 0, 0))
    win = pl.BlockSpec((1, 1, w, past), lambda b: (layer, b, 0, 0))
    tail = pl.BlockSpec((1, 1, w, LANES), lambda b: (layer, b, 0, past // LANES - 1))
    return pl.pallas_call(
        functools.partial(_attn_step_append_kernel, past=past),
        grid=(bsz,),
        in_specs=[col, col, col, _small(lw_cache), _small(lw_new), win, win, tail, tail],
        out_specs=[col, tail, tail],
        out_shape=out_shape,
        scratch_shapes=scratch,
        input_output_aliases={7: 1, 8: 2},
        compiler_params=_cparams("arbitrary"),
        name="attn_step_append",
    )(q_col, kn_col, vn_col, lw_cache, lw_new, cache_k, cache_v, *rolled)


def _group_rms(y, w):
    half = SSM_D_INNER // SSM_GROUPS
    return [_rms(y[:, g * half:(g + 1) * half]) * w[:, g * half:(g + 1) * half] for g in range(SSM_GROUPS)]


def _ssd_kernel(z_ref, xs_ref, bc_ref, dt_ref, dtb_ref, alog_ref, d_ref, nw_ref, h0_ref, y_ref, h_ref, ys_ref):
    c = pl.program_id(1)
    ch = SSM_CHUNK

    @pl.when(c == 0)
    def _():
        h_ref[...] = h0_ref[...]

    xs = xs_ref[0].astype(F32)
    bc = bc_ref[0].astype(F32)

    dt = jax.nn.softplus(dt_ref[0] + dtb_ref[...])
    da = dt * (-jnp.exp(alog_ref[...]))
    row = _iota((ch, ch), 0)
    colv = _iota((ch, ch), 1)
    tril = (row >= colv).astype(F32)
    cs = _dot(tril, da, precision=HIGHEST)
    cs_t = cs.T
    causal = row >= colv

    heads = range(SSM_HEADS)
    hpg = SSM_HEADS // SSM_GROUPS
    bm = [bc[:, g * SSM_STATE:(g + 1) * SSM_STATE] for g in range(SSM_GROUPS)]
    cm = [bc[:, (SSM_GROUPS + g) * SSM_STATE:(SSM_GROUPS + g + 1) * SSM_STATE] for g in range(SSM_GROUPS)]
    cb = [_dot_nt(cm[g], bm[g]) for g in range(SSM_GROUPS)]
    dt_t = dt.T
    xs_t = xs.T
    w_t = dt_t * jnp.exp(cs_t[:, ch - 1:ch] - cs_t)
    e_last = jnp.exp(cs[ch - 1:ch, :])
    lane_lo = _iota((ch, LANES), 1) < SSM_HEAD_DIM
    row_lo = _iota((LANES, ch), 0) < SSM_HEAD_DIM
    pairs = range(SSM_HEADS // 2)
    csb = [jnp.broadcast_to(cs[:, h:h + 1], (ch, ch)) for h in heads]
    mix = [cb[h // hpg] * jnp.exp(jnp.where(causal, csb[h] - cs_t[h:h + 1, :], NEG)) * dt_t[h:h + 1, :]
           for h in heads]
    x_pair = [xs[:, j * LANES:(j + 1) * LANES] for j in pairs]
    y_intra = [jnp.where(lane_lo, _dot(mix[2 * j], x_pair[j]), _dot(mix[2 * j + 1], x_pair[j])) for j in pairs]
    h_pair = [h_ref[0, 2 * j:2 * j + 2].reshape(2 * SSM_HEAD_DIM, SSM_STATE) for j in pairs]
    y_inter = [_dot_nt(cm[2 * j // hpg], h_pair[j]) * jnp.exp(jnp.where(lane_lo, csb[2 * j], csb[2 * j + 1]))
               for j in pairs]
    xw_t = [xs_t[j * LANES:(j + 1) * LANES, :] * jnp.where(row_lo, w_t[2 * j:2 * j + 1, :], w_t[2 * j + 1:2 * j + 2, :])
            for j in pairs]
    st = [_dot(xw_t[j], bm[2 * j // hpg]) for j in pairs]
    for j in pairs:
        cols = slice(j * LANES, (j + 1) * LANES)
        ys_ref[:, cols] = y_intra[j] + y_inter[j] + d_ref[:, cols] * x_pair[j]
        decay = jnp.where(row_lo, e_last[:, 2 * j:2 * j + 1], e_last[:, 2 * j + 1:2 * j + 2])
        h_ref[0, 2 * j:2 * j + 2] = (h_pair[j] * decay + st[j]).reshape(2, SSM_HEAD_DIM, SSM_STATE)

    y = ys_ref[...] * jax.nn.silu(z_ref[0].astype(F32))
    half = SSM_D_INNER // SSM_GROUPS
    for g, yg in enumerate(_group_rms(y, nw_ref[...])):
        y_ref[0, :, g * half:(g + 1) * half] = yg.astype(y_ref.dtype)


def _small(a):
    return pl.BlockSpec(a.shape, lambda *_: (0,) * a.ndim)


def ssd_prompt(main, tail, h0, dt_bias, a_log, d_skip, norm_w, *, bsz, seq):
    nc = seq // SSM_CHUNK
    ch = SSM_CHUNK
    small = [_pad_tail(dt_bias.reshape(1, -1)), _pad_tail(a_log.reshape(1, -1)),
             jnp.repeat(d_skip, SSM_HEAD_DIM).reshape(1, -1), norm_w.reshape(1, -1)]
    return pl.pallas_call(
        _ssd_kernel,
        grid=(bsz, nc),
        in_specs=[
            pl.BlockSpec((1, ch, SSM_D_INNER), lambda b, c: (b, c, 0)),
            pl.BlockSpec((1, ch, SSM_D_INNER), lambda b, c: (b, c, 1)),
            pl.BlockSpec((1, ch, SSM_BC), lambda b, c: (b, c, HYB_BC0 // SSM_BC)),
            pl.BlockSpec((1, ch, LANES), lambda b, c: (b, c, 0)),
        ] + [_small(a) for a in small] + [
            pl.BlockSpec((1, SSM_HEADS, SSM_HEAD_DIM, SSM_STATE), lambda b, c: (b, 0, 0, 0)),
        ],
        out_specs=[
            pl.BlockSpec((1, ch, SSM_D_INNER), lambda b, c: (b, c, 0)),
            pl.BlockSpec((1, SSM_HEADS, SSM_HEAD_DIM, SSM_STATE), lambda b, c: (b, 0, 0, 0)),
        ],
        out_shape=[jax.ShapeDtypeStruct((bsz, seq, SSM_D_INNER), BF16),
                   jax.ShapeDtypeStruct((bsz, SSM_HEADS, SSM_HEAD_DIM, SSM_STATE), F32)],
        scratch_shapes=[pltpu.VMEM((ch, SSM_D_INNER), F32)],
        compiler_params=_cparams("parallel", "arbitrary"),
        name="ssd_prompt",
    )(main, main, main, tail, *small, h0)


def _row_to_col(row, eye):
    return jnp.sum(jnp.where(eye, row, 0.0), axis=1, keepdims=True)


def _col_to_row(col, eye):
    return jnp.sum(jnp.where(eye, col, 0.0), axis=0, keepdims=True)


def _conv_step(c0_ref, w_ref, x_row, c0, c1):
    acc = w_ref[SSM_CONV - 1:SSM_CONV, c0:c1] * x_row
    for i in range(SSM_CONV - 1):
        acc = acc + w_ref[i:i + 1, c0:c1] * c0_ref[0, i:i + 1, c0:c1]
    return acc


def _ssd_step_kernel(z_ref, xs_ref, bc_ref, dt_ref, c0_ref, cw_ref, cb_ref, dtb_ref, alog_ref, d_ref, nw_ref,
                     h0_ref, y_ref, co_ref, h_ref, ys_ref):
    xs_raw = xs_ref[0]
    bc_raw = bc_ref[0]
    xs = jax.nn.silu(_conv_step(c0_ref, cw_ref, xs_raw, 0, SSM_D_INNER) + cb_ref[:, 0:SSM_D_INNER])
    bc = jax.nn.silu(_conv_step(c0_ref, cw_ref, bc_raw, SSM_D_INNER, SSM_XBC) + cb_ref[:, SSM_D_INNER:SSM_XBC])
    co_ref[0, 0:SSM_CONV - 2, :] = c0_ref[0, 1:SSM_CONV - 1, :]
    co_ref[0, SSM_CONV - 2:SSM_CONV - 1, 0:SSM_D_INNER] = xs_raw
    co_ref[0, SSM_CONV - 2:SSM_CONV - 1, SSM_D_INNER:SSM_XBC] = bc_raw

    dt = jax.nn.softplus(dt_ref[0] + dtb_ref[...])
    dec = jnp.exp(dt * (-jnp.exp(alog_ref[...])))
    dskip = d_ref[...]
    eye = _iota((LANES, LANES), 0) == _iota((LANES, LANES), 1)
    upper = _iota((LANES, 1), 0) >= SSM_HEAD_DIM
    for j in range(SSM_HEADS // 2):
        h0i, h1i = 2 * j, 2 * j + 1
        g = h0i // (SSM_HEADS // SSM_GROUPS)
        bm = bc[:, g * SSM_STATE:(g + 1) * SSM_STATE]
        cm = bc[:, (SSM_GROUPS + g) * SSM_STATE:(SSM_GROUPS + g + 1) * SSM_STATE]
        x_row = xs[:, j * LANES:(j + 1) * LANES]
        x_col = _row_to_col(x_row, eye)
        pick = lambda v: jnp.where(upper, v[:, h1i:h1i + 1], v[:, h0i:h0i + 1])
        hp = h0_ref[0, h0i:h1i + 1].reshape(2 * SSM_HEAD_DIM, SSM_STATE)
        hn = hp * pick(dec) + (x_col * pick(dt)) * bm
        h_ref[0, h0i:h1i + 1] = hn.reshape(2, SSM_HEAD_DIM, SSM_STATE)
        y_col = jnp.sum(hn * cm, axis=1, keepdims=True) + pick(dskip) * x_col
        ys_ref[:, j * LANES:(j + 1) * LANES] = _col_to_row(y_col, eye)

    y = ys_ref[...] * jax.nn.silu(z_ref[0])
    half = SSM_D_INNER // SSM_GROUPS
    for g, yg in enumerate(_group_rms(y, nw_ref[...])):
        y_ref[0, :, g * half:(g + 1) * half] = yg


def ssd_step(main, tail, conv0, h0, conv_w, conv_b, dt_bias, a_log, d_skip, norm_w):
    bsz = main.shape[0]
    small = [conv_w, conv_b.reshape(1, -1), _pad_tail(dt_bias.reshape(1, -1)), _pad_tail(a_log.reshape(1, -1)),
             _pad_tail(d_skip.reshape(1, -1)), norm_w.reshape(1, -1)]
    hspec = pl.BlockSpec((1, SSM_HEADS, SSM_HEAD_DIM, SSM_STATE), lambda b: (b, 0, 0, 0))
    cspec = pl.BlockSpec((1, SSM_CONV - 1, SSM_XBC), lambda b: (b, 0, 0))
    return pl.pallas_call(
        _ssd_step_kernel,
        grid=(bsz,),
        in_specs=[
            pl.BlockSpec((1, 1, SSM_D_INNER), lambda b: (b, 0, 0)),
            pl.BlockSpec((1, 1, SSM_D_INNER), lambda b: (b, 0, 1)),
            pl.BlockSpec((1, 1, SSM_BC), lambda b: (b, 0, HYB_BC0 // SSM_BC)),
            pl.BlockSpec((1, 1, LANES), lambda b: (b, 0, 0)),
            cspec,
        ] + [_small(a) for a in small] + [hspec],
        out_specs=[pl.BlockSpec((1, 1, SSM_D_INNER), lambda b: (b, 0, 0)), cspec, hspec],
        out_shape=[jax.ShapeDtypeStruct((bsz, 1, SSM_D_INNER), F32),
                   jax.ShapeDtypeStruct(conv0.shape, F32),
                   jax.ShapeDtypeStruct(h0.shape, F32)],
        scratch_shapes=[pltpu.VMEM((1, SSM_D_INNER), F32)],
        compiler_params=_cparams("parallel"),
        name="ssd_step",
    )(main, main, main, tail, conv0, *small, h0)


def _l2norm(x):
    return x * lax.rsqrt(jnp.sum(x * x, axis=-1, keepdims=True) + EPS)


def _unit_lower_inverse(ns, eye):
    size = ns[0].shape[0]
    ps = [eye - n for n in ns]
    ms = [_dot(n, n) for n in ns]
    power = 2
    while 2 * power < size:
        pms = [_dot(jnp.concatenate([p, m], axis=0), m) for p, m in zip(ps, ms)]
        ps = [p + pm[:size] for p, pm in zip(ps, pms)]
        ms = [pm[size:] for pm in pms]
        power *= 2
    return [p + _dot(p, m) for p, m in zip(ps, ms)]


def _gdn_gates(ba, dtb_ref, alog_ref):
    beta = jax.nn.sigmoid(ba)
    g = -jnp.exp(alog_ref[...]) * jax.nn.softplus(ba + dtb_ref[...])
    return beta, g


def _gdn_kernel(q_ref, k_ref, v_ref, z_ref, ba_ref, dtb_ref, alog_ref, nw_ref, s0_ref, o_ref, s_ref):
    c = pl.program_id(1)
    ch = GDN_CHUNK
    nh = GDN_V_HEADS

    @pl.when(c == 0)
    def _():
        s_ref[...] = s0_ref[...]

    row = _iota((ch, ch), 0)
    colv = _iota((ch, ch), 1)
    incl = row >= colv
    strict = row > colv
    eye = (row == colv).astype(F32)
    rep = nh // GDN_QK_HEADS
    heads = range(nh)
    qk_heads = range(GDN_QK_HEADS)
    subs = range(GDN_STEP_CHUNKS)
    rows = {sc: slice(sc * ch, (sc + 1) * ch) for sc in subs}
    sh = [(sc, h) for sc in subs for h in heads]
    sj = [(sc, j) for sc in subs for j in qk_heads]

    gates = {sc: _gdn_gates(ba_ref[0, rows[sc], :], dtb_ref, alog_ref) for sc in subs}
    gcum = {sc: _dot(incl.astype(F32), gates[sc][1], precision=HIGHEST) for sc in subs}
    gcum_t = {sc: jnp.concatenate([gcum[sc], jnp.zeros((LANES - ch, LANES), F32)], axis=0).T for sc in subs}
    qn = {(sc, j): _l2norm(q_ref[0, rows[sc], j * GDN_DK:(j + 1) * GDN_DK].astype(F32)) * (GDN_DK ** -0.5)
          for sc, j in sj}
    kn = {(sc, j): _l2norm(k_ref[0, rows[sc], j * GDN_DK:(j + 1) * GDN_DK].astype(F32)) for sc, j in sj}
    kk = {k: _dot_nt(kn[k], kn[k]) for k in sj}
    qk = {k: _dot_nt(qn[k], kn[k]) for k in sj}
    gc_col = {(sc, h): gcum[sc][:, nh + h:nh + h + 1] for sc, h in sh}
    gc_last = {(sc, h): gcum[sc][ch - 1:ch, nh + h:nh + h + 1] for sc, h in sh}
    beta_col = {(sc, h): gates[sc][0][:, h:h + 1] for sc, h in sh}
    dec = {(sc, h): jnp.exp(jnp.where(incl, gc_col[sc, h] - gcum_t[sc][nh + h:nh + h + 1, 0:ch], NEG))
           for sc, h in sh}
    t_inv = dict(zip(sh, _unit_lower_inverse(
        [jnp.where(strict, kk[sc, h // rep] * dec[sc, h], 0.0) * beta_col[sc, h] for sc, h in sh], eye)))
    eg = {k: jnp.exp(gc_col[k]) for k in sh}
    lhs = {(sc, h): jnp.concatenate([kn[sc, h // rep] * (beta_col[sc, h] * eg[sc, h]), qn[sc, h // rep] * eg[sc, h]],
                                    axis=0) for sc, h in sh}
    vb = {(sc, h): v_ref[0, rows[sc], h * GDN_DV:(h + 1) * GDN_DV].astype(F32) * beta_col[sc, h] for sc, h in sh}
    attn = {(sc, h): qk[sc, h // rep] * dec[sc, h] for sc, h in sh}
    kdec = {(sc, h): kn[sc, h // rep] * jnp.exp(gc_last[sc, h] - gc_col[sc, h]) for sc, h in sh}

    state = {h: s_ref[0, h] for h in heads}
    for sc in subs:
        both = {h: _dot(lhs[sc, h], state[h]) for h in heads}
        u = {h: _dot(t_inv[sc, h], vb[sc, h] - both[h][:ch]) for h in heads}
        o = {h: both[h][ch:] + _dot(attn[sc, h], u[h]) for h in heads}
        state = {h: state[h] * jnp.exp(gc_last[sc, h]) + _dot_tn(kdec[sc, h], u[h]) for h in heads}
        for h in heads:
            z_h = z_ref[0, rows[sc], h * GDN_DV:(h + 1) * GDN_DV].astype(F32)
            o_ref[0, rows[sc], h * GDN_DV:(h + 1) * GDN_DV] = (
                _rms(o[h]) * nw_ref[...] * jax.nn.silu(z_h)).astype(o_ref.dtype)
    for h in heads:
        s_ref[0, h] = state[h]


def _gdn_gate_params(dt_bias, a_log):
    nh = GDN_V_HEADS
    dtb = jnp.zeros((1, LANES), F32).at[0, nh:2 * nh].set(dt_bias)
    alog = jnp.zeros((1, LANES), F32).at[0, nh:2 * nh].set(a_log)
    return dtb, alog


def gdn_prompt(main, tail, s0, dt_bias, a_log, norm_w, *, bsz, seq):
    ch = GDN_CHUNK * GDN_STEP_CHUNKS
    assert seq % ch == 0
    nc = seq // ch
    dtb, alog = _gdn_gate_params(dt_bias, a_log)
    small = [dtb, alog, norm_w.reshape(1, -1)]
    sspec = pl.BlockSpec((1, GDN_V_HEADS, GDN_DK, GDN_DV), lambda b, c: (b, 0, 0, 0))
    return pl.pallas_call(
        _gdn_kernel,
        grid=(bsz, nc),
        in_specs=[
            pl.BlockSpec((1, ch, GDN_QK_W), lambda b, c: (b, c, 0)),
            pl.BlockSpec((1, ch, GDN_QK_W), lambda b, c: (b, c, 1)),
            pl.BlockSpec((1, ch, GDN_VW), lambda b, c: (b, c, 1)),
            pl.BlockSpec((1, ch, GDN_VW), lambda b, c: (b, c, 2)),
            pl.BlockSpec((1, ch, LANES), lambda b, c: (b, c, 0)),
        ] + [_small(a) for a in small] + [sspec],
        out_specs=[pl.BlockSpec((1, ch, GDN_VW), lambda b, c: (b, c, 0)), sspec],
        out_shape=[jax.ShapeDtypeStruct((bsz, seq, GDN_VW), BF16),
                   jax.ShapeDtypeStruct((bsz, GDN_V_HEADS, GDN_DK, GDN_DV), F32)],
        compiler_params=_cparams("parallel", "arbitrary"),
        name="gdn_prompt",
    )(main, main, main, main, tail, *small, s0)


def _gdn_conv_step(c0_ref, w_ref, x_row, c0, c1):
    acc = w_ref[GDN_CONV - 1:GDN_CONV, c0:c1] * x_row
    for i in range(GDN_CONV - 1):
        acc = acc + w_ref[i:i + 1, c0:c1] * c0_ref[0, i:i + 1, c0:c1]
    return acc


def _gdn_step_kernel(q_ref, k_ref, v_ref, z_ref, ba_ref, c0_ref, cw_ref, dtb_ref, alog_ref, nw_ref, s0_ref,
                     o_ref, co_ref, s_ref):
    nh = GDN_V_HEADS
    q_raw, k_raw, v_raw = q_ref[0], k_ref[0], v_ref[0]
    q = jax.nn.silu(_gdn_conv_step(c0_ref, cw_ref, q_raw, 0, GDN_QK_W))
    k = jax.nn.silu(_gdn_conv_step(c0_ref, cw_ref, k_raw, GDN_QK_W, 2 * GDN_QK_W))
    v = jax.nn.silu(_gdn_conv_step(c0_ref, cw_ref, v_raw, 2 * GDN_QK_W, GDN_QKV))
    co_ref[0, 0:GDN_CONV - 2, :] = c0_ref[0, 1:GDN_CONV - 1, :]
    co_ref[0, GDN_CONV - 2:GDN_CONV - 1, 0:GDN_QK_W] = q_raw
    co_ref[0, GDN_CONV - 2:GDN_CONV - 1, GDN_QK_W:2 * GDN_QK_W] = k_raw
    co_ref[0, GDN_CONV - 2:GDN_CONV - 1, 2 * GDN_QK_W:GDN_QKV] = v_raw

    beta, g = _gdn_gates(ba_ref[0], dtb_ref, alog_ref)
    eg_all = jnp.exp(g)
    eye = _iota((LANES, LANES), 0) == _iota((LANES, LANES), 1)
    for j in range(GDN_QK_HEADS):
        qn = _l2norm(q[:, j * GDN_DK:(j + 1) * GDN_DK]) * (GDN_DK ** -0.5)
        kn = _l2norm(k[:, j * GDN_DK:(j + 1) * GDN_DK])
        qk = jnp.sum(qn * kn, axis=-1, keepdims=True)
        q_col = _row_to_col(qn, eye)
        k_col = _row_to_col(kn, eye)
        for h in range(j * (nh // GDN_QK_HEADS), (j + 1) * (nh // GDN_QK_HEADS)):
            b_h = beta[:, h:h + 1]
            eg = eg_all[:, nh + h:nh + h + 1]
            s_prev = s0_ref[0, h]
            ks = jnp.sum(s_prev * k_col, axis=0, keepdims=True)
            qs = jnp.sum(s_prev * q_col, axis=0, keepdims=True)
            v_h = v[:, h * GDN_DV:(h + 1) * GDN_DV]
            u = v_h * b_h - (b_h * eg) * ks
            o = eg * qs + qk * u
            s_ref[0, h] = s_prev * eg + k_col * u
            z_h = z_ref[0, :, h * GDN_DV:(h + 1) * GDN_DV]
            o_ref[0, :, h * GDN_DV:(h + 1) * GDN_DV] = _rms(o) * nw_ref[...] * jax.nn.silu(z_h)


def gdn_step(main, tail, conv0, s0, conv_w, dt_bias, a_log, norm_w):
    bsz = main.shape[0]
    dtb, alog = _gdn_gate_params(dt_bias, a_log)
    small = [conv_w, dtb, alog, norm_w.reshape(1, -1)]
    sspec = pl.BlockSpec((1, GDN_V_HEADS, GDN_DK, GDN_DV), lambda b: (b, 0, 0, 0))
    cspec = pl.BlockSpec((1, GDN_CONV - 1, GDN_QKV), lambda b: (b, 0, 0))
    return pl.pallas_call(
        _gdn_step_kernel,
        grid=(bsz,),
        in_specs=[
            pl.BlockSpec((1, 1, GDN_QK_W), lambda b: (b, 0, 0)),
            pl.BlockSpec((1, 1, GDN_QK_W), lambda b: (b, 0, 1)),
            pl.BlockSpec((1, 1, GDN_VW), lambda b: (b, 0, 1)),
            pl.BlockSpec((1, 1, GDN_VW), lambda b: (b, 0, 2)),
            pl.BlockSpec((1, 1, LANES), lambda b: (b, 0, 0)),
            cspec,
        ] + [_small(a) for a in small] + [sspec],
        out_specs=[pl.BlockSpec((1, 1, GDN_VW), lambda b: (b, 0, 0)), cspec, sspec],
        out_shape=[jax.ShapeDtypeStruct((bsz, 1, GDN_VW), F32),
                   jax.ShapeDtypeStruct(conv0.shape, F32),
                   jax.ShapeDtypeStruct(s0.shape, F32)],
        compiler_params=_cparams("parallel"),
        name="gdn_step",
    )(main, main, main, main, tail, conv0, *small, s0)


def _pad_tail(w):
    return jnp.pad(w, ((0, 0), (0, LANES - w.shape[1])))


def _prep_hyb_in(w):
    a = A_WIDTH
    q, k, v = w[:, 0:a], w[:, a:2 * a], w[:, 2 * a:3 * a]
    z = w[:, 3 * a:3 * a + SSM_D_INNER]
    x0 = 3 * a + SSM_D_INNER
    xs = w[:, x0:x0 + SSM_D_INNER]
    bc = w[:, x0 + SSM_D_INNER:x0 + SSM_XBC]
    dt = w[:, x0 + SSM_XBC:]
    return jnp.concatenate([z, xs, q, k, v, bc], axis=1).astype(BF16), _pad_tail(dt).astype(BF16)


def _prep_gdn_in(w):
    return w[:, :GDN_MAIN].astype(BF16), _pad_tail(w[:, GDN_MAIN:]).astype(BF16)


HYB_CONV_COLS = ((SSM_D_INNER, 2 * SSM_D_INNER), (HYB_BC0, HYB_MAIN))


def _hyb_cols(a):
    out = jnp.zeros(a.shape[:-1] + (HYB_MAIN,), F32)
    (x0, x1), (b0, b1) = HYB_CONV_COLS
    return out.at[..., x0:x1].set(a[..., :SSM_D_INNER]).at[..., b0:b1].set(a[..., SSM_D_INNER:])


def _window_to_lanes(c):
    n, b, past, h, dh = c.shape
    return jnp.transpose(c, (0, 1, 3, 4, 2)).reshape(n, b, h * dh, past)


def _window_from_lanes(c):
    n, b, _, past = c.shape
    return jnp.transpose(c.reshape(n, b, A_HEADS, A_HEAD_DIM, past), (0, 1, 4, 2, 3))


def _row_tile(m, cap):
    return m if m <= cap else cap


def kernel(x_prompt, x_sample, cache_attn_k, cache_attn_v, state_ssm_conv, state_ssm, state_gdn_conv, state_gdn, rel_bias, norm_mix_pre, norm_mix_post, norm_ffn_pre, norm_ffn_post, w_hyb_in, ssm_conv_w, ssm_conv_b, ssm_dt_bias, ssm_a_log, ssm_d, ssm_norm_w, w_hyb_out, w_gdn_in, gdn_conv_w, gdn_dt_bias, gdn_a_log, gdn_norm_w, w_gdn_out, w_ffn_gate, w_ffn_up, w_ffn_down):
    depth = norm_mix_pre.shape[0]
    d_model = x_prompt.shape[-1]
    n_hyb, n_gdn = w_hyb_in.shape[0], w_gdn_in.shape[0]

    hyb_in = [_prep_hyb_in(w_hyb_in[i]) for i in range(n_hyb)]
    hyb_out = [(w_hyb_out[i, :A_WIDTH].astype(BF16), w_hyb_out[i, A_WIDTH:].astype(BF16)) for i in range(n_hyb)]
    gdn_in = [_prep_gdn_in(w_gdn_in[i]) for i in range(n_gdn)]
    gdn_out = [w_gdn_out[i].astype(BF16) for i in range(n_gdn)]
    ffn_w = [(w_ffn_gate[l].astype(BF16), w_ffn_up[l].astype(BF16), w_ffn_down[l].astype(BF16))
             for l in range(depth)]
    bias_tiles = _attn_bias_rows(rel_bias)

    def trunk(x3, k_pre, v_pre, sconv, sssm, gconv, gstate):
        bsz, seq, _ = x3.shape
        m = bsz * seq
        step = seq == 1
        tm_big = _row_tile(m, 2048)
        tm = _row_tile(m, 512)
        x = x3.reshape(m, d_model)
        nk, nv, nsc, nss, ngc, ngs = [], [], [], [], [], []
        rolled = None
        for l in range(depth):
            i = l // 2
            if l % 2 == 0:
                w_main, w_tail = hyb_in[i]
                ssm_args = (ssm_dt_bias[i], ssm_a_log[i], ssm_d[i], ssm_norm_w[i])
                if step:
                    main, tail = inproj(x, norm_mix_pre[l], w_main, w_tail, tm=tm_big, tn=512)
                    main3 = main.reshape(bsz, seq, HYB_MAIN)
                    tail3 = tail.reshape(bsz, seq, LANES)
                    col = lambda c0: main[:, c0:c0 + A_WIDTH].reshape(bsz, A_WIDTH, 1)
                    o_attn, *rolled = attention_step(
                        col(HYB_Q0), col(HYB_K0), col(HYB_V0), k_pre, v_pre, rolled,
                        _attn_logw(rel_bias, k_pre.shape[-1]), layer=i)
                    y, c_new, s_new = ssd_step(main3, tail3, sconv[i], sssm[i], ssm_conv_w[i], ssm_conv_b[i],
                                               *ssm_args)
                else:
                    main, tail, hist = inproj_conv(
                        x, norm_mix_pre[l], w_main, w_tail, _hyb_cols(ssm_conv_w[i]),
                        _hyb_cols(ssm_conv_b[i][None]), _hyb_cols(sconv[i]), seq=seq, conv_cols=HYB_CONV_COLS,
                        tm=tm_big, tn=512)
                    main3 = main.reshape(bsz, seq, HYB_MAIN)
                    tail3 = tail.reshape(bsz, seq, LANES)
                    c_new = jnp.concatenate([hist[..., a:b] for a, b in HYB_CONV_COLS], axis=-1)
                    o_attn = attention_prompt(main3, bias_tiles, bsz=bsz, seq=seq)
                    keep = min(A_PATTERNS[-1][0], seq)
                    k_new = main3[:, seq - keep:, HYB_K0:HYB_K0 + A_WIDTH].astype(F32)
                    v_new = main3[:, seq - keep:, HYB_V0:HYB_V0 + A_WIDTH].astype(F32)
                    y, s_new = ssd_prompt(main3, tail3, sssm[i], *ssm_args, bsz=bsz, seq=seq)
                    nk.append(k_new.reshape(bsz, -1, A_HEADS, A_HEAD_DIM))
                    nv.append(v_new.reshape(bsz, -1, A_HEADS, A_HEAD_DIM))
                nsc.append(c_new)
                nss.append(s_new)
                x = outproj([o_attn.reshape(m, A_WIDTH), y.reshape(m, SSM_D_INNER)], list(hyb_out[i]),
                            x, norm_mix_post[l], tm=tm)
            else:
                w_main, w_tail = gdn_in[i]
                gdn_args = (gdn_dt_bias[i], gdn_a_log[i], gdn_norm_w[i])
                if step:
                    main, tail = inproj(x, norm_mix_pre[l], w_main, w_tail, tm=tm_big, tn=512)
                    o, c_new, s_new = gdn_step(main.reshape(bsz, seq, GDN_MAIN), tail.reshape(bsz, seq, LANES),
                                               gconv[i], gstate[i], gdn_conv_w[i], *gdn_args)
                else:
                    pad = lambda a: jnp.pad(a, [(0, 0)] * (a.ndim - 1) + [(0, GDN_MAIN - GDN_QKV)])
                    main, tail, hist = inproj_conv(
                        x, norm_mix_pre[l], w_main, w_tail, pad(gdn_conv_w[i]), jnp.zeros((1, GDN_MAIN), F32),
                        pad(gconv[i]), seq=seq, conv_cols=((0, GDN_QKV),), tm=tm_big, tn=512)
                    c_new = hist[..., :GDN_QKV]
                    o, s_new = gdn_prompt(main.reshape(bsz, seq, GDN_MAIN), tail.reshape(bsz, seq, LANES),
                                          gstate[i], *gdn_args, bsz=bsz, seq=seq)
                ngc.append(c_new)
                ngs.append(s_new)
                x = outproj([o.reshape(m, GDN_VW)], [gdn_out[i]], x, norm_mix_post[l], tm=tm)
            wg, wu, wd = ffn_w[l]
            x = ffn(x, norm_ffn_pre[l], wg, wu, wd, norm_ffn_post[l], tm=tm)
        k_out, v_out = [_window_from_lanes(r) for r in rolled] if step else (jnp.stack(nk), jnp.stack(nv))
        return (x.reshape(bsz, seq, d_model), k_out, v_out, jnp.stack(nsc), jnp.stack(nss),
                jnp.stack(ngc), jnp.stack(ngs))

    bsz = x_prompt.shape[0]
    dt_p = x_prompt.dtype
    p_sc0 = jnp.zeros((n_hyb, bsz, SSM_CONV - 1, SSM_XBC), dt_p)
    p_ss0 = jnp.zeros((n_hyb, bsz, SSM_HEADS, SSM_HEAD_DIM, SSM_STATE), F32)
    p_gc0 = jnp.zeros((n_gdn, bsz, GDN_CONV - 1, GDN_QKV), dt_p)
    p_gs0 = jnp.zeros((n_gdn, bsz, GDN_V_HEADS, GDN_DK, GDN_DV), F32)
    y_prompt, pk, pv, psc, pss, pgc, pgs = trunk(x_prompt, None, None, p_sc0, p_ss0, p_gc0, p_gs0)
    y_sample, sk, sv, ssc, sss, sgc, sgs = trunk(
        x_sample, _window_to_lanes(cache_attn_k), _window_to_lanes(cache_attn_v), state_ssm_conv, state_ssm,
        state_gdn_conv, state_gdn)
    return (y_prompt, y_sample, pk, pv, psc, pss, pgc, pgs, sk, sv, ssc, sss, sgc, sgs)
```

```python
import functools
import math

import numpy as np
import jax
import jax.numpy as jnp
from jax import lax
from jax.experimental import pallas as pl
from jax.experimental.pallas import tpu as pltpu

F32 = jnp.float32
BF16 = jnp.bfloat16
EPS = 1e-6
NEG = -1e30
HIGHEST = lax.Precision.HIGHEST

VMEM_LIMIT_BYTES = 56 * 1024 * 1024
LANES = 128

A_HEADS = 8
A_HEAD_DIM = 64
A_WIDTH = A_HEADS * A_HEAD_DIM
A_PATTERNS = ((128, 1), (512, 4), (2048, 16))
A_BAND = 128
ATTN_GROUP = 4
REL_BUCKETS = 32
REL_MAX_DIST = 2048

SSM_D_INNER = 1024
SSM_HEAD_DIM = 64
SSM_HEADS = SSM_D_INNER // SSM_HEAD_DIM
SSM_GROUPS = 2
SSM_STATE = 128
SSM_CONV = 4
SSM_CHUNK = 128
SSM_BC = 2 * SSM_GROUPS * SSM_STATE
SSM_XBC = SSM_D_INNER + SSM_BC

GDN_QK_HEADS = 8
GDN_V_HEADS = 16
GDN_DK = 128
GDN_DV = 128
GDN_CONV = 4
GDN_CHUNK = 64
GDN_STEP_CHUNKS = 1
GDN_QK_W = GDN_QK_HEADS * GDN_DK
GDN_VW = GDN_V_HEADS * GDN_DV
GDN_QKV = 2 * GDN_QK_W + GDN_VW

HYB_MAIN = 2 * SSM_D_INNER + 3 * A_WIDTH + SSM_BC
HYB_Q0 = 2 * SSM_D_INNER
HYB_K0 = HYB_Q0 + A_WIDTH
HYB_V0 = HYB_K0 + A_WIDTH
HYB_BC0 = HYB_V0 + A_WIDTH
GDN_MAIN = GDN_QKV + GDN_VW


def _cparams(*sem):
    return pltpu.CompilerParams(dimension_semantics=sem, vmem_limit_bytes=VMEM_LIMIT_BYTES)


def _rms(x):
    return x * lax.rsqrt(jnp.mean(x * x, axis=-1, keepdims=True) + EPS)


def _dot(a, b, **kw):
    return jnp.dot(a, b, preferred_element_type=F32, **kw)


def _dot_nt(a, b, **kw):
    return lax.dot_general(a, b, (((1,), (1,)), ((), ())), preferred_element_type=F32, **kw)


def _dot_tn(a, b, **kw):
    return lax.dot_general(a, b, (((0,), (0,)), ((), ())), preferred_element_type=F32, **kw)


def _iota(shape, dim):
    return lax.broadcasted_iota(jnp.int32, shape, dim)


def _inproj_kernel(x_ref, g_ref, w_ref, wt_ref, o_ref, t_ref, h_ref):
    @pl.when(pl.program_id(1) == 0)
    def _():
        hb = (_rms(x_ref[...]) * g_ref[...]).astype(BF16)
        h_ref[...] = hb
        t_ref[...] = _dot(hb, wt_ref[...])

    o_ref[...] = _dot(h_ref[...], w_ref[...])


def inproj(x, g, w_main, w_tail, *, tm, tn):
    m, d = x.shape
    n = w_main.shape[1]
    return pl.pallas_call(
        _inproj_kernel,
        grid=(m // tm, n // tn),
        in_specs=[
            pl.BlockSpec((tm, d), lambda i, j: (i, 0)),
            pl.BlockSpec((1, d), lambda i, j: (0, 0)),
            pl.BlockSpec((d, tn), lambda i, j: (0, j)),
            pl.BlockSpec((d, LANES), lambda i, j: (0, 0)),
        ],
        out_specs=[
            pl.BlockSpec((tm, tn), lambda i, j: (i, j)),
            pl.BlockSpec((tm, LANES), lambda i, j: (i, 0)),
        ],
        out_shape=[jax.ShapeDtypeStruct((m, n), F32), jax.ShapeDtypeStruct((m, LANES), F32)],
        scratch_shapes=[pltpu.VMEM((tm, d), BF16)],
        compiler_params=_cparams("parallel", "arbitrary"),
        name="inproj",
    )(x, g.reshape(1, d), w_main, w_tail)


CONV_TAPS = 4
CONV_ROWS = 64
PROJ_K_SLICES = 4
CONV_BASE = 8


def _inproj_conv_kernel(x_ref, g_ref, w_ref, wt_ref, cw_ref, cb_ref, c0_ref, o_ref, t_ref, so_ref,
                        h_ref, xp0_ref, xp1_ref, carry_ref, *, n_col, conv_tiles, tiles_per_seq):
    s = pl.program_id(0)
    n_tiles = pl.num_programs(0) - 1
    tm = o_ref.shape[0]
    lo = CONV_BASE - (CONV_TAPS - 1)
    cur = jnp.minimum(s, n_tiles - 1)
    prv = jnp.maximum(s - 1, 0)
    pi, pj = prv // n_col, prv % n_col
    prv_conv = functools.reduce(jnp.logical_or, [jnp.logical_and(pj >= a, pj < b) for a, b in conv_tiles])
    first = pi % tiles_per_seq == 0
    both = lambda a, b: jnp.logical_and(a, b)

    @pl.when(both(cur % n_col == 0, s < n_tiles))
    def _():
        hb = (_rms(x_ref[...]) * g_ref[...]).astype(BF16)
        h_ref[...] = hb
        t_ref[...] = _dot(hb, wt_ref[...])

    for parity, (cur_ref, prv_ref) in enumerate(((xp0_ref, xp1_ref), (xp1_ref, xp0_ref))):
        here = s % 2 == parity

        def project(cur_ref=cur_ref):
            cur_ref[CONV_BASE:CONV_BASE + tm, :] = _dot(h_ref[...], w_ref[...])

        def raw_tail(prv_ref=prv_ref):
            tail = prv_ref[CONV_BASE + tm - (CONV_TAPS - 1):CONV_BASE + tm, :]
            so_ref[pj, pi] = tail
            return tail

        if parity == 0:
            @pl.when(s == 0)
            def _():
                project()

        @pl.when(both(here, both(s > 0, jnp.logical_not(prv_conv))))
        def _():
            project()
            raw_tail()
            o_ref[...] = prv_ref[CONV_BASE:CONV_BASE + tm, :].astype(o_ref.dtype)

        @pl.when(both(here, both(both(s > 0, prv_conv), first)))
        def _():
            prv_ref[lo:CONV_BASE, :] = c0_ref[pj, pi // tiles_per_seq]

        @pl.when(both(here, both(both(s > 0, prv_conv), jnp.logical_not(first))))
        def _():
            prv_ref[lo:CONV_BASE, :] = carry_ref[pj]

        @pl.when(both(here, both(s > 0, prv_conv)))
        def _():
            def conv_chunk(r0):
                ext = prv_ref[r0:r0 + CONV_BASE + CONV_ROWS, :]
                conv = cb_ref[pj] + cw_ref[pj, CONV_TAPS - 1:CONV_TAPS, :] * ext[CONV_BASE:]
                for back in range(1, CONV_TAPS):
                    tap = CONV_TAPS - 1 - back
                    conv = conv + cw_ref[pj, tap:tap + 1, :] * pltpu.roll(ext, back, 0)[CONV_BASE:]
                o_ref[r0:r0 + CONV_ROWS, :] = jax.nn.silu(conv).astype(o_ref.dtype)

            d = h_ref.shape[1]
            n_chunks = tm // CONV_ROWS
            acc = None
            for kq in range(PROJ_K_SLICES):
                ks = slice(kq * d // PROJ_K_SLICES, (kq + 1) * d // PROJ_K_SLICES)
                part = _dot(h_ref[:, ks], w_ref[ks, :])
                acc = part if acc is None else acc + part
                for c in range(kq * n_chunks // PROJ_K_SLICES, (kq + 1) * n_chunks // PROJ_K_SLICES):
                    conv_chunk(c * CONV_ROWS)
            cur_ref[CONV_BASE:CONV_BASE + tm, :] = acc
            carry_ref[pj] = raw_tail()


def inproj_conv(x, g, w_main, w_tail, conv_w, conv_b, conv0, *, seq, conv_cols, tm, tn):
    m, d = x.shape
    n = w_main.shape[1]
    assert seq % tm == 0 and all(a % tn == 0 and b % tn == 0 for a, b in conv_cols)
    tiles_per_seq = seq // tm
    n_row, n_col = m // tm, n // tn
    n_tiles = n_row * n_col
    conv_tiles = tuple((a // tn, b // tn) for a, b in conv_cols)
    hist = CONV_TAPS - 1
    cur = lambda s: jnp.minimum(s, n_tiles - 1)
    prv = lambda s: jnp.maximum(s - 1, 0)
    by_tile = lambda a: jnp.moveaxis(a.reshape(a.shape[:-1] + (n_col, tn)), -2, 0)
    main, tail, hist_rows = pl.pallas_call(
        functools.partial(_inproj_conv_kernel, n_col=n_col, conv_tiles=conv_tiles, tiles_per_seq=tiles_per_seq),
        grid=(n_tiles + 1,),
        in_specs=[
            pl.BlockSpec((tm, d), lambda s: (cur(s) // n_col, 0)),
            pl.BlockSpec((1, d), lambda s: (0, 0)),
            pl.BlockSpec((None, d, tn), lambda s: (cur(s) % n_col, 0, 0)),
            pl.BlockSpec((d, LANES), lambda s: (0, 0)),
            pl.BlockSpec((n_col, CONV_TAPS, tn), lambda s: (0, 0, 0)),
            pl.BlockSpec((n_col, 1, tn), lambda s: (0, 0, 0)),
            pl.BlockSpec((n_col, m // seq, hist, tn), lambda s: (0, 0, 0, 0)),
        ],
        out_specs=[
            pl.BlockSpec((tm, tn), lambda s: (prv(s) // n_col, prv(s) % n_col)),
            pl.BlockSpec((tm, LANES), lambda s: (cur(s) // n_col, 0)),
            pl.BlockSpec((n_col, n_row, hist, tn), lambda s: (0, 0, 0, 0)),
        ],
        out_shape=[jax.ShapeDtypeStruct((m, n), BF16), jax.ShapeDtypeStruct((m, LANES), F32),
                   jax.ShapeDtypeStruct((n_col, n_row, hist, tn), F32)],
        scratch_shapes=[pltpu.VMEM((tm, d), BF16), pltpu.VMEM((CONV_BASE + tm, tn), F32),
                        pltpu.VMEM((CONV_BASE + tm, tn), F32), pltpu.VMEM((n_col, hist, tn), F32)],
        compiler_params=_cparams("arbitrary"),
        name="inproj_conv",
    )(x, g.reshape(1, d), by_tile(w_main), w_tail, by_tile(conv_w), by_tile(conv_b), by_tile(conv0))
    hist_rows = jnp.moveaxis(hist_rows, 0, -2).reshape(n_row, hist, n)
    return main, tail, hist_rows[tiles_per_seq - 1::tiles_per_seq]


def _outproj_kernel(*refs, n_in):
    a_refs, w_refs = refs[:n_in], refs[n_in:2 * n_in]
    x_ref, g_ref, o_ref = refs[2 * n_in:]
    acc = None
    for a_ref, w_ref in zip(a_refs, w_refs):
        t = _dot(a_ref[...].astype(BF16), w_ref[...])
        acc = t if acc is None else acc + t
    o_ref[...] = x_ref[...] + _rms(acc) * g_ref[...]


def outproj(acts, weights, x, g, *, tm):
    m, d = x.shape
    n_in = len(acts)
    in_specs = [pl.BlockSpec((tm, a.shape[1]), lambda i: (i, 0)) for a in acts]
    in_specs += [pl.BlockSpec(w.shape, lambda i: (0, 0)) for w in weights]
    in_specs += [pl.BlockSpec((tm, d), lambda i: (i, 0)), pl.BlockSpec((1, d), lambda i: (0, 0))]
    return pl.pallas_call(
        functools.partial(_outproj_kernel, n_in=n_in),
        grid=(m // tm,),
        in_specs=in_specs,
        out_specs=pl.BlockSpec((tm, d), lambda i: (i, 0)),
        out_shape=jax.ShapeDtypeStruct((m, d), F32),
        compiler_params=_cparams("parallel"),
        name="outproj",
    )(*acts, *weights, x, g.reshape(1, d))


def _ffn_kernel(x_ref, g1_ref, wg_ref, wu_ref, wd_ref, g2_ref, o_ref):
    x = x_ref[...]
    h = (_rms(x) * g1_ref[...]).astype(BF16)
    a = jax.nn.silu(_dot(h, wg_ref[...])) * _dot(h, wu_ref[...])
    o_ref[...] = x + _rms(_dot(a.astype(BF16), wd_ref[...])) * g2_ref[...]


def ffn(x, g1, wg, wu, wd, g2, *, tm):
    m, d = x.shape
    resident = lambda a: pl.BlockSpec(a.shape, lambda i: (0, 0), pipeline_mode=pl.Buffered(1))
    return pl.pallas_call(
        _ffn_kernel,
        grid=(m // tm,),
        in_specs=[
            pl.BlockSpec((tm, d), lambda i: (i, 0)),
            pl.BlockSpec((1, d), lambda i: (0, 0)),
            resident(wg), resident(wu), resident(wd),
            pl.BlockSpec((1, d), lambda i: (0, 0)),
        ],
        out_specs=pl.BlockSpec((tm, d), lambda i: (i, 0)),
        out_shape=jax.ShapeDtypeStruct((m, d), F32),
        compiler_params=_cparams("parallel"),
        name="ffn",
    )(x, g1.reshape(1, d), wg, wu, wd, g2.reshape(1, d))


def _rel_buckets(dist):
    max_exact = REL_BUCKETS // 2
    n = np.maximum(dist, 1).astype(np.float32)
    large = max_exact + (np.log(n / max_exact) / math.log(REL_MAX_DIST / max_exact)
                         * (REL_BUCKETS - max_exact)).astype(np.int32)
    large = np.minimum(large, REL_BUCKETS - 1)
    return np.where(dist < max_exact, dist, large).astype(np.int32)


def _attn_bias_rows(rel_bias):
    u = np.arange(2 * A_BAND)
    valid = u <= A_BAND
    rows = []
    for (_, d) in A_PATTERNS:
        b = rel_bias[_rel_buckets(np.where(valid, A_BAND - u, 0) * d)]
        rows.append(jnp.where(valid[:, None], b.astype(F32), NEG))
    tl = jnp.transpose(jnp.stack(rows), (2, 0, 1))
    tl = tl.reshape(A_HEADS // 2, 2, len(A_PATTERNS), 2 * A_BAND)
    tl = jnp.transpose(tl, (0, 2, 1, 3))[:, :, :, None, :]
    return jnp.broadcast_to(tl, tl.shape[:3] + (8, 2 * A_BAND))


def _attn_kernel(qin_ref, kin_ref, vin_ref, brow_ref, o_ref, m0_ref, m1_ref, l0_ref, l1_ref, acc_ref,
                 q_ref, k_ref, v_ref, qc_ref, kc_ref, vc_ref, b_ref, *, seq):
    n_tiles = seq // A_BAND
    lane = _iota((A_BAND, LANES), 1)
    head0 = lane < A_HEAD_DIM
    m_refs, l_refs = (m0_ref, m1_ref), (l0_ref, l1_ref)
    q_ref[...] = qin_ref[...].astype(F32) * (A_HEAD_DIM ** -0.5)
    k_ref[...] = kin_ref[...].astype(F32)
    v_ref[...] = vin_ref[...].astype(F32)
    for p in range(len(A_PATTERNS)):
        for h in range(2):
            row = jnp.broadcast_to(brow_ref[0, p, h, 0:1, :], (A_BAND, 2 * A_BAND))
            b_ref[0, p, h] = pltpu.roll(row, 0, 1, stride=1, stride_axis=0)

    first_p = max(range(len(A_PATTERNS)), key=lambda p: A_PATTERNS[p][1])
    d_first = A_PATTERNS[first_p][1]
    class_len = seq // d_first

    def to_class_major(r, carry):
        dst = pl.ds(pl.multiple_of(r * class_len, class_len), class_len)
        for src_ref, dst_ref in ((q_ref, qc_ref), (k_ref, kc_ref), (v_ref, vc_ref)):
            dst_ref[dst, :] = src_ref[pl.ds(r, class_len, stride=d_first), :]
        return carry

    lax.fori_loop(0, d_first, to_class_major, 0)

    for p in [first_p] + [p for p in range(len(A_PATTERNS)) if p != first_p]:
        d = A_PATTERNS[p][1]
        fresh = p == first_p
        tiles_per_class = n_tiles // d

        def load_tile(idx, d=d, tiles_per_class=tiles_per_class, fresh=fresh):
            r = idx // tiles_per_class
            t = idx % tiles_per_class
            start = r + t * (d * A_BAND)
            has_prev = t > 0
            rows = pl.ds(start, A_BAND, stride=d) if d > 1 else pl.ds(pl.multiple_of(start, A_BAND), A_BAND)
            if fresh:
                base = r * class_len + t * A_BAND
                crow = pl.ds(pl.multiple_of(base, A_BAND), A_BAND)
                cprev = pl.ds(pl.multiple_of(jnp.where(has_prev, base - A_BAND, base), A_BAND), A_BAND)
                return dict(
                    rows=rows, has_prev=has_prev, q=qc_ref[crow, :],
                    k2=jnp.concatenate([kc_ref[cprev, :], kc_ref[crow, :]], axis=0).astype(BF16),
                    v2=jnp.concatenate([vc_ref[cprev, :], vc_ref[crow, :]], axis=0).astype(BF16))
            prev = jnp.where(has_prev, start - d * A_BAND, start)
            prows = pl.ds(prev, A_BAND, stride=d) if d > 1 else pl.ds(pl.multiple_of(prev, A_BAND), A_BAND)
            return dict(
                rows=rows, has_prev=has_prev, q=q_ref[rows, :],
                k2=jnp.concatenate([k_ref[prows, :], k_ref[rows, :]], axis=0).astype(BF16),
                v2=jnp.concatenate([v_ref[prows, :], v_ref[rows, :]], axis=0).astype(BF16),
                acc=acc_ref[rows, :], m=[m_refs[h][rows, :] for h in range(2)],
                l=[l_refs[h][rows, :] for h in range(2)])

        def tile_group(idx, carry, load_tile=load_tile, p=p, fresh=fresh):
            tiles = [load_tile(idx + i * (n_tiles // ATTN_GROUP)) for i in range(ATTN_GROUP)]
            chains = [(tile, h) for tile in tiles for h in range(2)]
            col = _iota((A_BAND, 2 * A_BAND), 1)
            qh = [jnp.where(head0 if h == 0 else jnp.logical_not(head0), tile["q"], 0.0).astype(BF16)
                  for tile, h in chains]
            s = [_dot_nt(qh[c], tile["k2"])
                 + jnp.where(jnp.logical_and(col < A_BAND, jnp.logical_not(tile["has_prev"])), NEG, b_ref[0, p, h])
                 for c, (tile, h) in enumerate(chains)]
            if fresh:
                wide = lambda x: jnp.broadcast_to(x, (A_BAND, LANES))
                m_new = [wide(jnp.max(s[c], axis=-1, keepdims=True)) for c in range(len(chains))]
                pr = [jnp.exp(s[c] - jnp.concatenate([m_new[c], m_new[c]], axis=1)) for c in range(len(chains))]
                l_new = [wide(jnp.sum(pr[c], axis=-1, keepdims=True)) for c in range(len(chains))]
                acc_new = [_dot(pr[c].astype(BF16), tile["v2"]) for c, (tile, h) in enumerate(chains)]
            else:
                m_new = [jnp.maximum(tile["m"][h], jnp.max(s[c], axis=-1, keepdims=True))
                         for c, (tile, h) in enumerate(chains)]
                alpha = [jnp.exp(tile["m"][h] - m_new[c]) for c, (tile, h) in enumerate(chains)]
                pr = [jnp.exp(s[c] - jnp.concatenate([m_new[c], m_new[c]], axis=1)) for c in range(len(chains))]
                l_new = [alpha[c] * tile["l"][h] + jnp.sum(pr[c], axis=-1, keepdims=True)
                         for c, (tile, h) in enumerate(chains)]
                acc_new = [alpha[c] * tile["acc"] + _dot(pr[c].astype(BF16), tile["v2"])
                           for c, (tile, h) in enumerate(chains)]
            for c, (tile, h) in enumerate(chains):
                m_refs[h][tile["rows"], :] = m_new[c]
                l_refs[h][tile["rows"], :] = l_new[c]
                if h == 1:
                    acc_ref[tile["rows"], :] = jnp.where(head0, acc_new[c - 1], acc_new[c])
            return carry

        lax.fori_loop(0, n_tiles // ATTN_GROUP, tile_group, 0)

    lane_s = _iota((seq, LANES), 1)
    o_ref[...] = (acc_ref[...] / jnp.where(lane_s < A_HEAD_DIM, l0_ref[...], l1_ref[...])).astype(o_ref.dtype)


def attention_prompt(proj, bias_tiles, *, bsz, seq):
    hp = A_HEADS // 2
    qb, kb, vb = HYB_Q0 // LANES, HYB_K0 // LANES, HYB_V0 // LANES
    return pl.pallas_call(
        functools.partial(_attn_kernel, seq=seq),
        grid=(bsz, hp),
        in_specs=[
            pl.BlockSpec((None, seq, LANES), lambda b, h: (b, 0, qb + h)),
            pl.BlockSpec((None, seq, LANES), lambda b, h: (b, 0, kb + h)),
            pl.BlockSpec((None, seq, LANES), lambda b, h: (b, 0, vb + h)),
            pl.BlockSpec((1,) + bias_tiles.shape[1:], lambda b, h: (h, 0, 0, 0, 0)),
        ],
        out_specs=pl.BlockSpec((None, seq, LANES), lambda b, h: (b, 0, h)),
        out_shape=jax.ShapeDtypeStruct((bsz, seq, A_WIDTH), BF16),
        scratch_shapes=[pltpu.VMEM((seq, LANES), F32)] * 11
        + [pltpu.VMEM((1, len(A_PATTERNS), 2, A_BAND, 2 * A_BAND), F32)],
        compiler_params=_cparams("parallel", "parallel"),
        name="attn_prompt",
    )(proj, proj, proj, bias_tiles)


def _attn_logw(rel_bias, past):
    dist = np.arange(past + 1)
    count = np.zeros(past + 1, np.float64)
    for (w, d) in A_PATTERNS:
        count += ((dist % d == 0) & (dist <= w)).astype(np.float64)
    logc = np.where(count > 0, np.log(np.maximum(count, 1.0)), 0.0).astype(np.float32)
    lw = rel_bias[_rel_buckets(dist)].astype(F32).T + logc[None, :]
    return jnp.where((count > 0)[None, :], lw, NEG)


def _step_scores(xk_ref, q_col, kn_col, lw_ref, lw0_ref, s_ref):
    qs = q_col * (A_HEAD_DIM ** -0.5)
    s_new = []
    for h in range(A_HEADS):
        rows = slice(h * A_HEAD_DIM, (h + 1) * A_HEAD_DIM)
        s_ref[h:h + 1, :] = jnp.sum(xk_ref[0, 0, rows, :] * qs[rows], axis=0, keepdims=True)
        s_new.append(jnp.sum(kn_col[rows] * qs[rows], axis=0, keepdims=True))
    s = s_ref[...] + lw_ref[...]
    s_new = jnp.concatenate(s_new, axis=0) + lw0_ref[:, :1]
    m = jnp.maximum(jnp.max(s, axis=-1, keepdims=True), s_new)
    p = jnp.exp(s - m)
    p_new = jnp.exp(s_new - m)
    den = jnp.sum(p, axis=-1, keepdims=True) + p_new
    return p, p_new, den


def _step_output(xv_ref, vn_col, p, p_new, den, o_ref):
    for h in range(A_HEADS):
        rows = slice(h * A_HEAD_DIM, (h + 1) * A_HEAD_DIM)
        pv = jnp.sum(xv_ref[0, 0, rows, :] * p[h:h + 1, :], axis=-1, keepdims=True)
        o_ref[0, rows, :] = (pv + p_new[h:h + 1, :] * vn_col[rows]) / den[h:h + 1, :]


def _attn_step_roll_kernel(q_ref, kn_ref, vn_ref, lw_ref, lw0_ref, xk_ref, xv_ref, o_ref, ko_ref, vo_ref, s_ref,
                           *, layer, past):
    is_layer = pl.program_id(0) == layer
    newest = _iota((A_HEAD_DIM, past), 1) == past - 1
    for x_ref, n_ref, out_ref in ((xk_ref, kn_ref, ko_ref), (xv_ref, vn_ref, vo_ref)):
        for h in range(A_HEADS):
            rows = slice(h * A_HEAD_DIM, (h + 1) * A_HEAD_DIM)
            rolled = pltpu.roll(x_ref[0, 0, rows, :], past - 1, 1)
            out_ref[0, 0, rows, :] = jnp.where(jnp.logical_and(newest, is_layer), n_ref[0, rows, :], rolled)

    @pl.when(is_layer)
    def _():
        p, p_new, den = _step_scores(xk_ref, q_ref[0], kn_ref[0], lw_ref, lw0_ref, s_ref)
        _step_output(xv_ref, vn_ref[0], p, p_new, den, o_ref.at[0])

    @pl.when(jnp.logical_not(is_layer))
    def _():
        o_ref[...] = jnp.zeros_like(o_ref)


def _attn_step_append_kernel(q_ref, kn_ref, vn_ref, lw_ref, lw0_ref, xk_ref, xv_ref, ko_in, vo_in,
                             o_ref, ko_ref, vo_ref, s_ref, *, past):
    del ko_in, vo_in
    p, p_new, den = _step_scores(xk_ref, q_ref[0], kn_ref[0], lw_ref, lw0_ref, s_ref)
    _step_output(xv_ref, vn_ref[0], p, p_new, den, o_ref)
    newest = _iota((A_WIDTH, LANES), 1) == LANES - 1
    for x_ref, n_ref, out_ref in ((xk_ref, kn_ref, ko_ref), (xv_ref, vn_ref, vo_ref)):
        rolled = pltpu.roll(x_ref[0, 0, :, past - LANES:past], LANES - 1, 1)
        out_ref[0, 0] = jnp.where(newest, n_ref[0], rolled)


def attention_step(q_col, kn_col, vn_col, cache_k, cache_v, rolled, logw, *, layer):
    n_layers, bsz, w, past = cache_k.shape
    lw_cache = logw[:, past:0:-1]
    lw_new = jnp.broadcast_to(logw[:, :1], (A_HEADS, LANES))
    out_shape = [jax.ShapeDtypeStruct((bsz, w, 1), F32),
                 jax.ShapeDtypeStruct(cache_k.shape, cache_k.dtype),
                 jax.ShapeDtypeStruct(cache_v.shape, cache_v.dtype)]
    scratch = [pltpu.VMEM((A_HEADS, past), F32)]
    if rolled is None:
        assert layer == 0
        col = pl.BlockSpec((1, w, 1), lambda l, b: (b, 0, 0))
        win = pl.BlockSpec((1, 1, w, past), lambda l, b: (l, b, 0, 0))
        o_all, rolled_k, rolled_v = pl.pallas_call(
            functools.partial(_attn_step_roll_kernel, layer=layer, past=past),
            grid=(n_layers, bsz),
            in_specs=[col, col, col, _small(lw_cache), _small(lw_new), win, win],
            out_specs=[pl.BlockSpec((1, 1, w, 1), lambda l, b: (l, b, 0, 0)), win, win],
            out_shape=[jax.ShapeDtypeStruct((n_layers, bsz, w, 1), F32)] + out_shape[1:],
            scratch_shapes=scratch,
            compiler_params=_cparams("arbitrary", "arbitrary"),
            name="attn_step_roll",
        )(q_col, kn_col, vn_col, lw_cache, lw_new, cache_k, cache_v)
        return o_all[layer], rolled_k, rolled_v
    col = pl.BlockSpec((1, w, 1), lambda b: (b, 0, 0))
    win = pl.BlockSpec((1, 1, w, past), lambda b: (layer, b, 0, 0))
    tail = pl.BlockSpec((1, 1, w, LANES), lambda b: (layer, b, 0, past // LANES - 1))
    return pl.pallas_call(
        functools.partial(_attn_step_append_kernel, past=past),
        grid=(bsz,),
        in_specs=[col, col, col, _small(lw_cache), _small(lw_new), win, win, tail, tail],
        out_specs=[col, tail, tail],
        out_shape=out_shape,
        scratch_shapes=scratch,
        input_output_aliases={7: 1, 8: 2},
        compiler_params=_cparams("arbitrary"),
        name="attn_step_append",
    )(q_col, kn_col, vn_col, lw_cache, lw_new, cache_k, cache_v, *rolled)


def _group_rms(y, w):
    half = SSM_D_INNER // SSM_GROUPS
    return [_rms(y[:, g * half:(g + 1) * half]) * w[:, g * half:(g + 1) * half] for g in range(SSM_GROUPS)]


def _ssd_kernel(z_ref, xs_ref, bc_ref, dt_ref, dtb_ref, alog_ref, d_ref, nw_ref, h0_ref, y_ref, h_ref, ys_ref):
    c = pl.program_id(1)
    ch = SSM_CHUNK

    @pl.when(c == 0)
    def _():
        h_ref[...] = h0_ref[...]

    xs = xs_ref[0].astype(F32)
    bc = bc_ref[0].astype(F32)

    dt = jax.nn.softplus(dt_ref[0] + dtb_ref[...])
    da = dt * (-jnp.exp(alog_ref[...]))
    row = _iota((ch, ch), 0)
    colv = _iota((ch, ch), 1)
    tril = (row >= colv).astype(F32)
    cs = _dot(tril, da, precision=HIGHEST)
    cs_t = cs.T
    causal = row >= colv

    heads = range(SSM_HEADS)
    hpg = SSM_HEADS // SSM_GROUPS
    bm = [bc[:, g * SSM_STATE:(g + 1) * SSM_STATE] for g in range(SSM_GROUPS)]
    cm = [bc[:, (SSM_GROUPS + g) * SSM_STATE:(SSM_GROUPS + g + 1) * SSM_STATE] for g in range(SSM_GROUPS)]
    cb = [_dot_nt(cm[g], bm[g]) for g in range(SSM_GROUPS)]
    dt_t = dt.T
    xs_t = xs.T
    w_t = dt_t * jnp.exp(cs_t[:, ch - 1:ch] - cs_t)
    e_last = jnp.exp(cs[ch - 1:ch, :])
    lane_lo = _iota((ch, LANES), 1) < SSM_HEAD_DIM
    row_lo = _iota((LANES, ch), 0) < SSM_HEAD_DIM
    pairs = range(SSM_HEADS // 2)
    csb = [jnp.broadcast_to(cs[:, h:h + 1], (ch, ch)) for h in heads]
    mix = [cb[h // hpg] * jnp.exp(jnp.where(causal, csb[h] - cs_t[h:h + 1, :], NEG)) * dt_t[h:h + 1, :]
           for h in heads]
    x_pair = [xs[:, j * LANES:(j + 1) * LANES] for j in pairs]
    y_intra = [jnp.where(lane_lo, _dot(mix[2 * j], x_pair[j]), _dot(mix[2 * j + 1], x_pair[j])) for j in pairs]
    h_pair = [h_ref[0, 2 * j:2 * j + 2].reshape(2 * SSM_HEAD_DIM, SSM_STATE) for j in pairs]
    y_inter = [_dot_nt(cm[2 * j // hpg], h_pair[j]) * jnp.exp(jnp.where(lane_lo, csb[2 * j], csb[2 * j + 1]))
               for j in pairs]
    xw_t = [xs_t[j * LANES:(j + 1) * LANES, :] * jnp.where(row_lo, w_t[2 * j:2 * j + 1, :], w_t[2 * j + 1:2 * j + 2, :])
            for j in pairs]
    st = [_dot(xw_t[j], bm[2 * j // hpg]) for j in pairs]
    for j in pairs:
        cols = slice(j * LANES, (j + 1) * LANES)
        ys_ref[:, cols] = y_intra[j] + y_inter[j] + d_ref[:, cols] * x_pair[j]
        decay = jnp.where(row_lo, e_last[:, 2 * j:2 * j + 1], e_last[:, 2 * j + 1:2 * j + 2])
        h_ref[0, 2 * j:2 * j + 2] = (h_pair[j] * decay + st[j]).reshape(2, SSM_HEAD_DIM, SSM_STATE)

    y = ys_ref[...] * jax.nn.silu(z_ref[0].astype(F32))
    half = SSM_D_INNER // SSM_GROUPS
    for g, yg in enumerate(_group_rms(y, nw_ref[...])):
        y_ref[0, :, g * half:(g + 1) * half] = yg.astype(y_ref.dtype)


def _small(a):
    return pl.BlockSpec(a.shape, lambda *_: (0,) * a.ndim)


def ssd_prompt(main, tail, h0, dt_bias, a_log, d_skip, norm_w, *, bsz, seq):
    nc = seq // SSM_CHUNK
    ch = SSM_CHUNK
    small = [_pad_tail(dt_bias.reshape(1, -1)), _pad_tail(a_log.reshape(1, -1)),
             jnp.repeat(d_skip, SSM_HEAD_DIM).reshape(1, -1), norm_w.reshape(1, -1)]
    return pl.pallas_call(
        _ssd_kernel,
        grid=(bsz, nc),
        in_specs=[
            pl.BlockSpec((1, ch, SSM_D_INNER), lambda b, c: (b, c, 0)),
            pl.BlockSpec((1, ch, SSM_D_INNER), lambda b, c: (b, c, 1)),
            pl.BlockSpec((1, ch, SSM_BC), lambda b, c: (b, c, HYB_BC0 // SSM_BC)),
            pl.BlockSpec((1, ch, LANES), lambda b, c: (b, c, 0)),
        ] + [_small(a) for a in small] + [
            pl.BlockSpec((1, SSM_HEADS, SSM_HEAD_DIM, SSM_STATE), lambda b, c: (b, 0, 0, 0)),
        ],
        out_specs=[
            pl.BlockSpec((1, ch, SSM_D_INNER), lambda b, c: (b, c, 0)),
            pl.BlockSpec((1, SSM_HEADS, SSM_HEAD_DIM, SSM_STATE), lambda b, c: (b, 0, 0, 0)),
        ],
        out_shape=[jax.ShapeDtypeStruct((bsz, seq, SSM_D_INNER), BF16),
                   jax.ShapeDtypeStruct((bsz, SSM_HEADS, SSM_HEAD_DIM, SSM_STATE), F32)],
        scratch_shapes=[pltpu.VMEM((ch, SSM_D_INNER), F32)],
        compiler_params=_cparams("parallel", "arbitrary"),
        name="ssd_prompt",
    )(main, main, main, tail, *small, h0)


def _row_to_col(row, eye):
    return jnp.sum(jnp.where(eye, row, 0.0), axis=1, keepdims=True)


def _col_to_row(col, eye):
    return jnp.sum(jnp.where(eye, col, 0.0), axis=0, keepdims=True)


def _conv_step(c0_ref, w_ref, x_row, c0, c1):
    acc = w_ref[SSM_CONV - 1:SSM_CONV, c0:c1] * x_row
    for i in range(SSM_CONV - 1):
        acc = acc + w_ref[i:i + 1, c0:c1] * c0_ref[0, i:i + 1, c0:c1]
    return acc


def _ssd_step_kernel(z_ref, xs_ref, bc_ref, dt_ref, c0_ref, cw_ref, cb_ref, dtb_ref, alog_ref, d_ref, nw_ref,
                     h0_ref, y_ref, co_ref, h_ref, ys_ref):
    xs_raw = xs_ref[0]
    bc_raw = bc_ref[0]
    xs = jax.nn.silu(_conv_step(c0_ref, cw_ref, xs_raw, 0, SSM_D_INNER) + cb_ref[:, 0:SSM_D_INNER])
    bc = jax.nn.silu(_conv_step(c0_ref, cw_ref, bc_raw, SSM_D_INNER, SSM_XBC) + cb_ref[:, SSM_D_INNER:SSM_XBC])
    co_ref[0, 0:SSM_CONV - 2, :] = c0_ref[0, 1:SSM_CONV - 1, :]
    co_ref[0, SSM_CONV - 2:SSM_CONV - 1, 0:SSM_D_INNER] = xs_raw
    co_ref[0, SSM_CONV - 2:SSM_CONV - 1, SSM_D_INNER:SSM_XBC] = bc_raw

    dt = jax.nn.softplus(dt_ref[0] + dtb_ref[...])
    dec = jnp.exp(dt * (-jnp.exp(alog_ref[...])))
    dskip = d_ref[...]
    eye = _iota((LANES, LANES), 0) == _iota((LANES, LANES), 1)
    upper = _iota((LANES, 1), 0) >= SSM_HEAD_DIM
    pairs = range(SSM_HEADS // 2)
    grp = [2 * j // (SSM_HEADS // SSM_GROUPS) for j in pairs]
    bm = [bc[:, g * SSM_STATE:(g + 1) * SSM_STATE] for g in grp]
    cm = [bc[:, (SSM_GROUPS + g) * SSM_STATE:(SSM_GROUPS + g + 1) * SSM_STATE] for g in grp]
    pick = lambda v, j: jnp.where(upper, v[:, 2 * j + 1:2 * j + 2], v[:, 2 * j:2 * j + 1])
    x_col = [_row_to_col(xs[:, j * LANES:(j + 1) * LANES], eye) for j in pairs]
    hp = [h0_ref[0, 2 * j:2 * j + 2].reshape(2 * SSM_HEAD_DIM, SSM_STATE) for j in pairs]
    hn = [hp[j] * pick(dec, j) + (x_col[j] * pick(dt, j)) * bm[j] for j in pairs]
    y_col = [jnp.sum(hn[j] * cm[j], axis=1, keepdims=True) + pick(dskip, j) * x_col[j] for j in pairs]
    y_row = [_col_to_row(y_col[j], eye) for j in pairs]
    for j in pairs:
        h_ref[0, 2 * j:2 * j + 2] = hn[j].reshape(2, SSM_HEAD_DIM, SSM_STATE)
        ys_ref[:, j * LANES:(j + 1) * LANES] = y_row[j]

    y = ys_ref[...] * jax.nn.silu(z_ref[0])
    half = SSM_D_INNER // SSM_GROUPS
    for g, yg in enumerate(_group_rms(y, nw_ref[...])):
        y_ref[0, :, g * half:(g + 1) * half] = yg


def ssd_step(main, tail, conv0, h0, conv_w, conv_b, dt_bias, a_log, d_skip, norm_w):
    bsz = main.shape[0]
    small = [conv_w, conv_b.reshape(1, -1), _pad_tail(dt_bias.reshape(1, -1)), _pad_tail(a_log.reshape(1, -1)),
             _pad_tail(d_skip.reshape(1, -1)), norm_w.reshape(1, -1)]
    hspec = pl.BlockSpec((1, SSM_HEADS, SSM_HEAD_DIM, SSM_STATE), lambda b: (b, 0, 0, 0))
    cspec = pl.BlockSpec((1, SSM_CONV - 1, SSM_XBC), lambda b: (b, 0, 0))
    return pl.pallas_call(
        _ssd_step_kernel,
        grid=(bsz,),
        in_specs=[
            pl.BlockSpec((1, 1, SSM_D_INNER), lambda b: (b, 0, 0)),
            pl.BlockSpec((1, 1, SSM_D_INNER), lambda b: (b, 0, 1)),
            pl.BlockSpec((1, 1, SSM_BC), lambda b: (b, 0, HYB_BC0 // SSM_BC)),
            pl.BlockSpec((1, 1, LANES), lambda b: (b, 0, 0)),
            cspec,
        ] + [_small(a) for a in small] + [hspec],
        out_specs=[pl.BlockSpec((1, 1, SSM_D_INNER), lambda b: (b, 0, 0)), cspec, hspec],
        out_shape=[jax.ShapeDtypeStruct((bsz, 1, SSM_D_INNER), F32),
                   jax.ShapeDtypeStruct(conv0.shape, F32),
                   jax.ShapeDtypeStruct(h0.shape, F32)],
        scratch_shapes=[pltpu.VMEM((1, SSM_D_INNER), F32)],
        compiler_params=_cparams("parallel"),
        name="ssd_step",
    )(main, main, main, tail, conv0, *small, h0)


def _l2norm(x):
    return x * lax.rsqrt(jnp.sum(x * x, axis=-1, keepdims=True) + EPS)


def _unit_lower_inverse(ns, eye):
    size = ns[0].shape[0]
    ps = [eye - n for n in ns]
    ms = [_dot(n, n) for n in ns]
    power = 2
    while 2 * power < size:
        pms = [_dot(jnp.concatenate([p, m], axis=0), m) for p, m in zip(ps, ms)]
        ps = [p + pm[:size] for p, pm in zip(ps, pms)]
        ms = [pm[size:] for pm in pms]
        power *= 2
    return [p + _dot(p, m) for p, m in zip(ps, ms)]


def _gdn_gates(ba, dtb_ref, alog_ref):
    beta = jax.nn.sigmoid(ba)
    g = -jnp.exp(alog_ref[...]) * jax.nn.softplus(ba + dtb_ref[...])
    return beta, g


def _gdn_kernel(q_ref, k_ref, v_ref, z_ref, ba_ref, dtb_ref, alog_ref, nw_ref, s0_ref, o_ref, s_ref):
    c = pl.program_id(1)
    ch = GDN_CHUNK
    nh = GDN_V_HEADS

    @pl.when(c == 0)
    def _():
        s_ref[...] = s0_ref[...]

    row = _iota((ch, ch), 0)
    colv = _iota((ch, ch), 1)
    incl = row >= colv
    strict = row > colv
    eye = (row == colv).astype(F32)
    rep = nh // GDN_QK_HEADS
    heads = range(nh)
    qk_heads = range(GDN_QK_HEADS)
    subs = range(GDN_STEP_CHUNKS)
    rows = {sc: slice(sc * ch, (sc + 1) * ch) for sc in subs}
    sh = [(sc, h) for sc in subs for h in heads]
    sj = [(sc, j) for sc in subs for j in qk_heads]

    gates = {sc: _gdn_gates(ba_ref[0, rows[sc], :], dtb_ref, alog_ref) for sc in subs}
    gcum = {sc: _dot(incl.astype(F32), gates[sc][1], precision=HIGHEST) for sc in subs}
    gcum_t = {sc: jnp.concatenate([gcum[sc], jnp.zeros((LANES - ch, LANES), F32)], axis=0).T for sc in subs}
    qn = {(sc, j): _l2norm(q_ref[0, rows[sc], j * GDN_DK:(j + 1) * GDN_DK].astype(F32)) * (GDN_DK ** -0.5)
          for sc, j in sj}
    kn = {(sc, j): _l2norm(k_ref[0, rows[sc], j * GDN_DK:(j + 1) * GDN_DK].astype(F32)) for sc, j in sj}
    kk = {k: _dot_nt(kn[k], kn[k]) for k in sj}
    qk = {k: _dot_nt(qn[k], kn[k]) for k in sj}
    gc_col = {(sc, h): gcum[sc][:, nh + h:nh + h + 1] for sc, h in sh}
    gc_last = {(sc, h): gcum[sc][ch - 1:ch, nh + h:nh + h + 1] for sc, h in sh}
    beta_col = {(sc, h): gates[sc][0][:, h:h + 1] for sc, h in sh}
    dec = {(sc, h): jnp.exp(jnp.where(incl, gc_col[sc, h] - gcum_t[sc][nh + h:nh + h + 1, 0:ch], NEG))
           for sc, h in sh}
    t_inv = dict(zip(sh, _unit_lower_inverse(
        [jnp.where(strict, kk[sc, h // rep] * dec[sc, h], 0.0) * beta_col[sc, h] for sc, h in sh], eye)))
    eg = {k: jnp.exp(gc_col[k]) for k in sh}
    lhs = {(sc, h): jnp.concatenate([kn[sc, h // rep] * (beta_col[sc, h] * eg[sc, h]), qn[sc, h // rep] * eg[sc, h]],
                                    axis=0) for sc, h in sh}
    vb = {(sc, h): v_ref[0, rows[sc], h * GDN_DV:(h + 1) * GDN_DV].astype(F32) * beta_col[sc, h] for sc, h in sh}
    attn = {(sc, h): qk[sc, h // rep] * dec[sc, h] for sc, h in sh}
    kdec = {(sc, h): kn[sc, h // rep] * jnp.exp(gc_last[sc, h] - gc_col[sc, h]) for sc, h in sh}

    state = {h: s_ref[0, h] for h in heads}
    for sc in subs:
        both = {h: _dot(lhs[sc, h], state[h]) for h in heads}
        u = {h: _dot(t_inv[sc, h], vb[sc, h] - both[h][:ch]) for h in heads}
        o = {h: both[h][ch:] + _dot(attn[sc, h], u[h]) for h in heads}
        state = {h: state[h] * jnp.exp(gc_last[sc, h]) + _dot_tn(kdec[sc, h], u[h]) for h in heads}
        for h in heads:
            z_h = z_ref[0, rows[sc], h * GDN_DV:(h + 1) * GDN_DV].astype(F32)
            o_ref[0, rows[sc], h * GDN_DV:(h + 1) * GDN_DV] = (
                _rms(o[h]) * nw_ref[...] * jax.nn.silu(z_h)).astype(o_ref.dtype)
    for h in heads:
        s_ref[0, h] = state[h]


def _gdn_gate_params(dt_bias, a_log):
    nh = GDN_V_HEADS
    dtb = jnp.zeros((1, LANES), F32).at[0, nh:2 * nh].set(dt_bias)
    alog = jnp.zeros((1, LANES), F32).at[0, nh:2 * nh].set(a_log)
    return dtb, alog


def gdn_prompt(main, tail, s0, dt_bias, a_log, norm_w, *, bsz, seq):
    ch = GDN_CHUNK * GDN_STEP_CHUNKS
    assert seq % ch == 0
    nc = seq // ch
    dtb, alog = _gdn_gate_params(dt_bias, a_log)
    small = [dtb, alog, norm_w.reshape(1, -1)]
    sspec = pl.BlockSpec((1, GDN_V_HEADS, GDN_DK, GDN_DV), lambda b, c: (b, 0, 0, 0))
    return pl.pallas_call(
        _gdn_kernel,
        grid=(bsz, nc),
        in_specs=[
            pl.BlockSpec((1, ch, GDN_QK_W), lambda b, c: (b, c, 0)),
            pl.BlockSpec((1, ch, GDN_QK_W), lambda b, c: (b, c, 1)),
            pl.BlockSpec((1, ch, GDN_VW), lambda b, c: (b, c, 1)),
            pl.BlockSpec((1, ch, GDN_VW), lambda b, c: (b, c, 2)),
            pl.BlockSpec((1, ch, LANES), lambda b, c: (b, c, 0)),
        ] + [_small(a) for a in small] + [sspec],
        out_specs=[pl.BlockSpec((1, ch, GDN_VW), lambda b, c: (b, c, 0)), sspec],
        out_shape=[jax.ShapeDtypeStruct((bsz, seq, GDN_VW), BF16),
                   jax.ShapeDtypeStruct((bsz, GDN_V_HEADS, GDN_DK, GDN_DV), F32)],
        compiler_params=_cparams("parallel", "arbitrary"),
        name="gdn_prompt",
    )(main, main, main, main, tail, *small, s0)


def _gdn_conv_step(c0_ref, w_ref, x_row, c0, c1):
    acc = w_ref[GDN_CONV - 1:GDN_CONV, c0:c1] * x_row
    for i in range(GDN_CONV - 1):
        acc = acc + w_ref[i:i + 1, c0:c1] * c0_ref[0, i:i + 1, c0:c1]
    return acc


def _gdn_step_kernel(q_ref, k_ref, v_ref, z_ref, ba_ref, c0_ref, cw_ref, dtb_ref, alog_ref, nw_ref, s0_ref,
                     o_ref, co_ref, s_ref):
    nh = GDN_V_HEADS
    q_raw, k_raw, v_raw = q_ref[0], k_ref[0], v_ref[0]
    q = jax.nn.silu(_gdn_conv_step(c0_ref, cw_ref, q_raw, 0, GDN_QK_W))
    k = jax.nn.silu(_gdn_conv_step(c0_ref, cw_ref, k_raw, GDN_QK_W, 2 * GDN_QK_W))
    v = jax.nn.silu(_gdn_conv_step(c0_ref, cw_ref, v_raw, 2 * GDN_QK_W, GDN_QKV))
    co_ref[0, 0:GDN_CONV - 2, :] = c0_ref[0, 1:GDN_CONV - 1, :]
    co_ref[0, GDN_CONV - 2:GDN_CONV - 1, 0:GDN_QK_W] = q_raw
    co_ref[0, GDN_CONV - 2:GDN_CONV - 1, GDN_QK_W:2 * GDN_QK_W] = k_raw
    co_ref[0, GDN_CONV - 2:GDN_CONV - 1, 2 * GDN_QK_W:GDN_QKV] = v_raw

    beta, g = _gdn_gates(ba_ref[0], dtb_ref, alog_ref)
    eg_all = jnp.exp(g)
    eye = _iota((LANES, LANES), 0) == _iota((LANES, LANES), 1)
    rep = nh // GDN_QK_HEADS
    heads = range(nh)
    qn = [_l2norm(q[:, j * GDN_DK:(j + 1) * GDN_DK]) * (GDN_DK ** -0.5) for j in range(GDN_QK_HEADS)]
    kn = [_l2norm(k[:, j * GDN_DK:(j + 1) * GDN_DK]) for j in range(GDN_QK_HEADS)]
    qk = [jnp.sum(a * b, axis=-1, keepdims=True) for a, b in zip(qn, kn)]
    q_col = [_row_to_col(a, eye) for a in qn]
    k_col = [_row_to_col(a, eye) for a in kn]
    b_h = [beta[:, h:h + 1] for h in heads]
    eg = [eg_all[:, nh + h:nh + h + 1] for h in heads]
    s_prev = [s0_ref[0, h] for h in heads]
    ks = [jnp.sum(s_prev[h] * k_col[h // rep], axis=0, keepdims=True) for h in heads]
    qs = [jnp.sum(s_prev[h] * q_col[h // rep], axis=0, keepdims=True) for h in heads]
    u = [v[:, h * GDN_DV:(h + 1) * GDN_DV] * b_h[h] - (b_h[h] * eg[h]) * ks[h] for h in heads]
    o = [eg[h] * qs[h] + qk[h // rep] * u[h] for h in heads]
    for h in heads:
        s_ref[0, h] = s_prev[h] * eg[h] + k_col[h // rep] * u[h]
        z_h = z_ref[0, :, h * GDN_DV:(h + 1) * GDN_DV]
        o_ref[0, :, h * GDN_DV:(h + 1) * GDN_DV] = _rms(o[h]) * nw_ref[...] * jax.nn.silu(z_h)


def gdn_step(main, tail, conv0, s0, conv_w, dt_bias, a_log, norm_w):
    bsz = main.shape[0]
    dtb, alog = _gdn_gate_params(dt_bias, a_log)
    small = [conv_w, dtb, alog, norm_w.reshape(1, -1)]
    sspec = pl.BlockSpec((1, GDN_V_HEADS, GDN_DK, GDN_DV), lambda b: (b, 0, 0, 0))
    cspec = pl.BlockSpec((1, GDN_CONV - 1, GDN_QKV), lambda b: (b, 0, 0))
    return pl.pallas_call(
        _gdn_step_kernel,
        grid=(bsz,),
        in_specs=[
            pl.BlockSpec((1, 1, GDN_QK_W), lambda b: (b, 0, 0)),
            pl.BlockSpec((1, 1, GDN_QK_W), lambda b: (b, 0, 1)),
            pl.BlockSpec((1, 1, GDN_VW), lambda b: (b, 0, 1)),
            pl.BlockSpec((1, 1, GDN_VW), lambda b: (b, 0, 2)),
            pl.BlockSpec((1, 1, LANES), lambda b: (b, 0, 0)),
            cspec,
        ] + [_small(a) for a in small] + [sspec],
        out_specs=[pl.BlockSpec((1, 1, GDN_VW), lambda b: (b, 0, 0)), cspec, sspec],
        out_shape=[jax.ShapeDtypeStruct((bsz, 1, GDN_VW), F32),
                   jax.ShapeDtypeStruct(conv0.shape, F32),
                   jax.ShapeDtypeStruct(s0.shape, F32)],
        compiler_params=_cparams("parallel"),
        name="gdn_step",
    )(main, main, main, main, tail, conv0, *small, s0)


def _pad_tail(w):
    return jnp.pad(w, ((0, 0), (0, LANES - w.shape[1])))


def _prep_hyb_in(w):
    a = A_WIDTH
    q, k, v = w[:, 0:a], w[:, a:2 * a], w[:, 2 * a:3 * a]
    z = w[:, 3 * a:3 * a + SSM_D_INNER]
    x0 = 3 * a + SSM_D_INNER
    xs = w[:, x0:x0 + SSM_D_INNER]
    bc = w[:, x0 + SSM_D_INNER:x0 + SSM_XBC]
    dt = w[:, x0 + SSM_XBC:]
    return jnp.concatenate([z, xs, q, k, v, bc], axis=1).astype(BF16), _pad_tail(dt).astype(BF16)


def _prep_gdn_in(w):
    return w[:, :GDN_MAIN].astype(BF16), _pad_tail(w[:, GDN_MAIN:]).astype(BF16)


HYB_CONV_COLS = ((SSM_D_INNER, 2 * SSM_D_INNER), (HYB_BC0, HYB_MAIN))


def _hyb_cols(a):
    out = jnp.zeros(a.shape[:-1] + (HYB_MAIN,), F32)
    (x0, x1), (b0, b1) = HYB_CONV_COLS
    return out.at[..., x0:x1].set(a[..., :SSM_D_INNER]).at[..., b0:b1].set(a[..., SSM_D_INNER:])


def _window_to_lanes(c):
    n, b, past, h, dh = c.shape
    return jnp.transpose(c, (0, 1, 3, 4, 2)).reshape(n, b, h * dh, past)


def _window_from_lanes(c):
    n, b, _, past = c.shape
    return jnp.transpose(c.reshape(n, b, A_HEADS, A_HEAD_DIM, past), (0, 1, 4, 2, 3))


def _row_tile(m, cap):
    return m if m <= cap else cap


def kernel(x_prompt, x_sample, cache_attn_k, cache_attn_v, state_ssm_conv, state_ssm, state_gdn_conv, state_gdn, rel_bias, norm_mix_pre, norm_mix_post, norm_ffn_pre, norm_ffn_post, w_hyb_in, ssm_conv_w, ssm_conv_b, ssm_dt_bias, ssm_a_log, ssm_d, ssm_norm_w, w_hyb_out, w_gdn_in, gdn_conv_w, gdn_dt_bias, gdn_a_log, gdn_norm_w, w_gdn_out, w_ffn_gate, w_ffn_up, w_ffn_down):
    depth = norm_mix_pre.shape[0]
    d_model = x_prompt.shape[-1]
    n_hyb, n_gdn = w_hyb_in.shape[0], w_gdn_in.shape[0]

    hyb_in = [_prep_hyb_in(w_hyb_in[i]) for i in range(n_hyb)]
    hyb_out = [(w_hyb_out[i, :A_WIDTH].astype(BF16), w_hyb_out[i, A_WIDTH:].astype(BF16)) for i in range(n_hyb)]
    gdn_in = [_prep_gdn_in(w_gdn_in[i]) for i in range(n_gdn)]
    gdn_out = [w_gdn_out[i].astype(BF16) for i in range(n_gdn)]
    ffn_w = [(w_ffn_gate[l].astype(BF16), w_ffn_up[l].astype(BF16), w_ffn_down[l].astype(BF16))
             for l in range(depth)]
    bias_tiles = _attn_bias_rows(rel_bias)

    def trunk(x3, k_pre, v_pre, sconv, sssm, gconv, gstate):
        bsz, seq, _ = x3.shape
        m = bsz * seq
        step = seq == 1
        tm_big = _row_tile(m, 2048)
        tm = _row_tile(m, 512)
        x = x3.reshape(m, d_model)
        nk, nv, nsc, nss, ngc, ngs = [], [], [], [], [], []
        rolled = None
        for l in range(depth):
            i = l // 2
            if l % 2 == 0:
                w_main, w_tail = hyb_in[i]
                ssm_args = (ssm_dt_bias[i], ssm_a_log[i], ssm_d[i], ssm_norm_w[i])
                if step:
                    main, tail = inproj(x, norm_mix_pre[l], w_main, w_tail, tm=tm_big, tn=512)
                    main3 = main.reshape(bsz, seq, HYB_MAIN)
                    tail3 = tail.reshape(bsz, seq, LANES)
                    col = lambda c0: main[:, c0:c0 + A_WIDTH].reshape(bsz, A_WIDTH, 1)
                    o_attn, *rolled = attention_step(
                        col(HYB_Q0), col(HYB_K0), col(HYB_V0), k_pre, v_pre, rolled,
                        _attn_logw(rel_bias, k_pre.shape[-1]), layer=i)
                    y, c_new, s_new = ssd_step(main3, tail3, sconv[i], sssm[i], ssm_conv_w[i], ssm_conv_b[i],
                                               *ssm_args)
                else:
                    main, tail, hist = inproj_conv(
                        x, norm_mix_pre[l], w_main, w_tail, _hyb_cols(ssm_conv_w[i]),
                        _hyb_cols(ssm_conv_b[i][None]), _hyb_cols(sconv[i]), seq=seq, conv_cols=HYB_CONV_COLS,
                        tm=tm_big, tn=512)
                    main3 = main.reshape(bsz, seq, HYB_MAIN)
                    tail3 = tail.reshape(bsz, seq, LANES)
                    c_new = jnp.concatenate([hist[..., a:b] for a, b in HYB_CONV_COLS], axis=-1)
                    o_attn = attention_prompt(main3, bias_tiles, bsz=bsz, seq=seq)
                    keep = min(A_PATTERNS[-1][0], seq)
                    k_new = main3[:, seq - keep:, HYB_K0:HYB_K0 + A_WIDTH].astype(F32)
                    v_new = main3[:, seq - keep:, HYB_V0:HYB_V0 + A_WIDTH].astype(F32)
                    y, s_new = ssd_prompt(main3, tail3, sssm[i], *ssm_args, bsz=bsz, seq=seq)
                    nk.append(k_new.reshape(bsz, -1, A_HEADS, A_HEAD_DIM))
                    nv.append(v_new.reshape(bsz, -1, A_HEADS, A_HEAD_DIM))
                nsc.append(c_new)
                nss.append(s_new)
                x = outproj([o_attn.reshape(m, A_WIDTH), y.reshape(m, SSM_D_INNER)], list(hyb_out[i]),
                            x, norm_mix_post[l], tm=tm)
            else:
                w_main, w_tail = gdn_in[i]
                gdn_args = (gdn_dt_bias[i], gdn_a_log[i], gdn_norm_w[i])
                if step:
                    main, tail = inproj(x, norm_mix_pre[l], w_main, w_tail, tm=tm_big, tn=512)
                    o, c_new, s_new = gdn_step(main.reshape(bsz, seq, GDN_MAIN), tail.reshape(bsz, seq, LANES),
                                               gconv[i], gstate[i], gdn_conv_w[i], *gdn_args)
                else:
                    pad = lambda a: jnp.pad(a, [(0, 0)] * (a.ndim - 1) + [(0, GDN_MAIN - GDN_QKV)])
                    main, tail, hist = inproj_conv(
                        x, norm_mix_pre[l], w_main, w_tail, pad(gdn_conv_w[i]), jnp.zeros((1, GDN_MAIN), F32),
                        pad(gconv[i]), seq=seq, conv_cols=((0, GDN_QKV),), tm=tm_big, tn=512)
                    c_new = hist[..., :GDN_QKV]
                    o, s_new = gdn_prompt(main.reshape(bsz, seq, GDN_MAIN), tail.reshape(bsz, seq, LANES),
                                          gstate[i], *gdn_args, bsz=bsz, seq=seq)
                ngc.append(c_new)
                ngs.append(s_new)
                x = outproj([o.reshape(m, GDN_VW)], [gdn_out[i]], x, norm_mix_post[l], tm=tm)
            wg, wu, wd = ffn_w[l]
            x = ffn(x, norm_ffn_pre[l], wg, wu, wd, norm_ffn_post[l], tm=tm)
        k_out, v_out = [_window_from_lanes(r) for r in rolled] if step else (jnp.stack(nk), jnp.stack(nv))
        return (x.reshape(bsz, seq, d_model), k_out, v_out, jnp.stack(nsc), jnp.stack(nss),
                jnp.stack(ngc), jnp.stack(ngs))

    bsz = x_prompt.shape[0]
    dt_p = x_prompt.dtype
    p_sc0 = jnp.zeros((n_hyb, bsz, SSM_CONV - 1, SSM_XBC), dt_p)
    p_ss0 = jnp.zeros((n_hyb, bsz, SSM_HEADS, SSM_HEAD_DIM, SSM_STATE), F32)
    p_gc0 = jnp.zeros((n_gdn, bsz, GDN_CONV - 1, GDN_QKV), dt_p)
    p_gs0 = jnp.zeros((n_gdn, bsz, GDN_V_HEADS, GDN_DK, GDN_DV), F32)
    y_prompt, pk, pv, psc, pss, pgc, pgs = trunk(x_prompt, None, None, p_sc0, p_ss0, p_gc0, p_gs0)
    y_sample, sk, sv, ssc, sss, sgc, sgs = trunk(
        x_sample, _window_to_lanes(cache_attn_k), _window_to_lanes(cache_attn_v), state_ssm_conv, state_ssm,
        state_gdn_conv, state_gdn)
    return (y_prompt, y_sample, pk, pv, psc, pss, pgc, pgs, sk, sv, ssc, sss, sgc, sgs)
```

```python
import functools
import math

import numpy as np
import jax
import jax.numpy as jnp
from jax import lax
from jax.experimental import pallas as pl
from jax.experimental.pallas import tpu as pltpu

F32 = jnp.float32
BF16 = jnp.bfloat16
EPS = 1e-6
NEG = -1e30
HIGHEST = lax.Precision.HIGHEST

VMEM_LIMIT_BYTES = 56 * 1024 * 1024
LANES = 128

A_HEADS = 8
A_HEAD_DIM = 64
A_WIDTH = A_HEADS * A_HEAD_DIM
A_PATTERNS = ((128, 1), (512, 4), (2048, 16))
A_BAND = 128
ATTN_GROUP = 4
REL_BUCKETS = 32
REL_MAX_DIST = 2048

SSM_D_INNER = 1024
SSM_HEAD_DIM = 64
SSM_HEADS = SSM_D_INNER // SSM_HEAD_DIM
SSM_GROUPS = 2
SSM_STATE = 128
SSM_CONV = 4
SSM_CHUNK = 128
SSM_BC = 2 * SSM_GROUPS * SSM_STATE
SSM_XBC = SSM_D_INNER + SSM_BC

GDN_QK_HEADS = 8
GDN_V_HEADS = 16
GDN_DK = 128
GDN_DV = 128
GDN_CONV = 4
GDN_CHUNK = 64
GDN_STEP_CHUNKS = 1
GDN_QK_W = GDN_QK_HEADS * GDN_DK
GDN_VW = GDN_V_HEADS * GDN_DV
GDN_QKV = 2 * GDN_QK_W + GDN_VW

HYB_MAIN = 2 * SSM_D_INNER + 3 * A_WIDTH + SSM_BC
HYB_Q0 = 2 * SSM_D_INNER
HYB_K0 = HYB_Q0 + A_WIDTH
HYB_V0 = HYB_K0 + A_WIDTH
HYB_BC0 = HYB_V0 + A_WIDTH
GDN_MAIN = GDN_QKV + GDN_VW


def _cparams(*sem):
    return pltpu.CompilerParams(dimension_semantics=sem, vmem_limit_bytes=VMEM_LIMIT_BYTES)


def _rms(x):
    return x * lax.rsqrt(jnp.mean(x * x, axis=-1, keepdims=True) + EPS)


def _dot(a, b, **kw):
    return jnp.dot(a, b, preferred_element_type=F32, **kw)


def _dot_nt(a, b, **kw):
    return lax.dot_general(a, b, (((1,), (1,)), ((), ())), preferred_element_type=F32, **kw)


def _dot_tn(a, b, **kw):
    return lax.dot_general(a, b, (((0,), (0,)), ((), ())), preferred_element_type=F32, **kw)


def _iota(shape, dim):
    return lax.broadcasted_iota(jnp.int32, shape, dim)


def _inproj_kernel(x_ref, g_ref, w_ref, wt_ref, o_ref, t_ref, h_ref):
    @pl.when(pl.program_id(1) == 0)
    def _():
        hb = (_rms(x_ref[...]) * g_ref[...]).astype(BF16)
        h_ref[...] = hb
        t_ref[...] = _dot(hb, wt_ref[...])

    o_ref[...] = _dot(h_ref[...], w_ref[...])


def inproj(x, g, w_main, w_tail, *, tm, tn):
    m, d = x.shape
    n = w_main.shape[1]
    return pl.pallas_call(
        _inproj_kernel,
        grid=(m // tm, n // tn),
        in_specs=[
            pl.BlockSpec((tm, d), lambda i, j: (i, 0)),
            pl.BlockSpec((1, d), lambda i, j: (0, 0)),
            pl.BlockSpec((d, tn), lambda i, j: (0, j)),
            pl.BlockSpec((d, LANES), lambda i, j: (0, 0)),
        ],
        out_specs=[
            pl.BlockSpec((tm, tn), lambda i, j: (i, j)),
            pl.BlockSpec((tm, LANES), lambda i, j: (i, 0)),
        ],
        out_shape=[jax.ShapeDtypeStruct((m, n), F32), jax.ShapeDtypeStruct((m, LANES), F32)],
        scratch_shapes=[pltpu.VMEM((tm, d), BF16)],
        compiler_params=_cparams("parallel", "arbitrary"),
        name="inproj",
    )(x, g.reshape(1, d), w_main, w_tail)


CONV_TAPS = 4
CONV_ROWS = 64
PROJ_K_SLICES = 4
CONV_BASE = 8


def _inproj_conv_kernel(x_ref, g_ref, w_ref, wt_ref, cw_ref, cb_ref, c0_ref, o_ref, t_ref, so_ref,
                        h_ref, xp0_ref, xp1_ref, carry_ref, *, n_col, conv_tiles, tiles_per_seq):
    s = pl.program_id(0)
    n_tiles = pl.num_programs(0) - 1
    tm = o_ref.shape[0]
    lo = CONV_BASE - (CONV_TAPS - 1)
    cur = jnp.minimum(s, n_tiles - 1)
    prv = jnp.maximum(s - 1, 0)
    pi, pj = prv // n_col, prv % n_col
    prv_conv = functools.reduce(jnp.logical_or, [jnp.logical_and(pj >= a, pj < b) for a, b in conv_tiles])
    first = pi % tiles_per_seq == 0
    both = lambda a, b: jnp.logical_and(a, b)

    @pl.when(both(cur % n_col == 0, s < n_tiles))
    def _():
        hb = (_rms(x_ref[...]) * g_ref[...]).astype(BF16)
        h_ref[...] = hb
        t_ref[...] = _dot(hb, wt_ref[...])

    for parity, (cur_ref, prv_ref) in enumerate(((xp0_ref, xp1_ref), (xp1_ref, xp0_ref))):
        here = s % 2 == parity

        def project(cur_ref=cur_ref):
            cur_ref[CONV_BASE:CONV_BASE + tm, :] = _dot(h_ref[...], w_ref[...])

        def raw_tail(prv_ref=prv_ref):
            tail = prv_ref[CONV_BASE + tm - (CONV_TAPS - 1):CONV_BASE + tm, :]
            so_ref[pj, pi] = tail
            return tail

        if parity == 0:
            @pl.when(s == 0)
            def _():
                project()

        @pl.when(both(here, both(s > 0, jnp.logical_not(prv_conv))))
        def _():
            project()
            raw_tail()
            o_ref[...] = prv_ref[CONV_BASE:CONV_BASE + tm, :].astype(o_ref.dtype)

        @pl.when(both(here, both(both(s > 0, prv_conv), first)))
        def _():
            prv_ref[lo:CONV_BASE, :] = c0_ref[pj, pi // tiles_per_seq]

        @pl.when(both(here, both(both(s > 0, prv_conv), jnp.logical_not(first))))
        def _():
            prv_ref[lo:CONV_BASE, :] = carry_ref[pj]

        @pl.when(both(here, both(s > 0, prv_conv)))
        def _():
            def conv_chunk(r0):
                ext = prv_ref[r0:r0 + CONV_BASE + CONV_ROWS, :]
                conv = cb_ref[pj] + cw_ref[pj, CONV_TAPS - 1:CONV_TAPS, :] * ext[CONV_BASE:]
                for back in range(1, CONV_TAPS):
                    tap = CONV_TAPS - 1 - back
                    conv = conv + cw_ref[pj, tap:tap + 1, :] * pltpu.roll(ext, back, 0)[CONV_BASE:]
                o_ref[r0:r0 + CONV_ROWS, :] = jax.nn.silu(conv).astype(o_ref.dtype)

            d = h_ref.shape[1]
            n_chunks = tm // CONV_ROWS
            acc = None
            for kq in range(PROJ_K_SLICES):
                ks = slice(kq * d // PROJ_K_SLICES, (kq + 1) * d // PROJ_K_SLICES)
                part = _dot(h_ref[:, ks], w_ref[ks, :])
                acc = part if acc is None else acc + part
                for c in range(kq * n_chunks // PROJ_K_SLICES, (kq + 1) * n_chunks // PROJ_K_SLICES):
                    conv_chunk(c * CONV_ROWS)
            cur_ref[CONV_BASE:CONV_BASE + tm, :] = acc
            carry_ref[pj] = raw_tail()


def inproj_conv(x, g, w_main, w_tail, conv_w, conv_b, conv0, *, seq, conv_cols, tm, tn):
    m, d = x.shape
    n = w_main.shape[1]
    assert seq % tm == 0 and all(a % tn == 0 and b % tn == 0 for a, b in conv_cols)
    tiles_per_seq = seq // tm
    n_row, n_col = m // tm, n // tn
    n_tiles = n_row * n_col
    conv_tiles = tuple((a // tn, b // tn) for a, b in conv_cols)
    hist = CONV_TAPS - 1
    cur = lambda s: jnp.minimum(s, n_tiles - 1)
    prv = lambda s: jnp.maximum(s - 1, 0)
    by_tile = lambda a: jnp.moveaxis(a.reshape(a.shape[:-1] + (n_col, tn)), -2, 0)
    main, tail, hist_rows = pl.pallas_call(
        functools.partial(_inproj_conv_kernel, n_col=n_col, conv_tiles=conv_tiles, tiles_per_seq=tiles_per_seq),
        grid=(n_tiles + 1,),
        in_specs=[
            pl.BlockSpec((tm, d), lambda s: (cur(s) // n_col, 0)),
            pl.BlockSpec((1, d), lambda s: (0, 0)),
            pl.BlockSpec((None, d, tn), lambda s: (cur(s) % n_col, 0, 0)),
            pl.BlockSpec((d, LANES), lambda s: (0, 0)),
            pl.BlockSpec((n_col, CONV_TAPS, tn), lambda s: (0, 0, 0)),
            pl.BlockSpec((n_col, 1, tn), lambda s: (0, 0, 0)),
            pl.BlockSpec((n_col, m // seq, hist, tn), lambda s: (0, 0, 0, 0)),
        ],
        out_specs=[
            pl.BlockSpec((tm, tn), lambda s: (prv(s) // n_col, prv(s) % n_col)),
            pl.BlockSpec((tm, LANES), lambda s: (cur(s) // n_col, 0)),
            pl.BlockSpec((n_col, n_row, hist, tn), lambda s: (0, 0, 0, 0)),
        ],
        out_shape=[jax.ShapeDtypeStruct((m, n), BF16), jax.ShapeDtypeStruct((m, LANES), F32),
                   jax.ShapeDtypeStruct((n_col, n_row, hist, tn), F32)],
        scratch_shapes=[pltpu.VMEM((tm, d), BF16), pltpu.VMEM((CONV_BASE + tm, tn), F32),
                        pltpu.VMEM((CONV_BASE + tm, tn), F32), pltpu.VMEM((n_col, hist, tn), F32)],
        compiler_params=_cparams("arbitrary"),
        name="inproj_conv",
    )(x, g.reshape(1, d), by_tile(w_main), w_tail, by_tile(conv_w), by_tile(conv_b), by_tile(conv0))
    hist_rows = jnp.moveaxis(hist_rows, 0, -2).reshape(n_row, hist, n)
    return main, tail, hist_rows[tiles_per_seq - 1::tiles_per_seq]


def _mix_ffn_kernel(*refs, n_in):
    a_refs, w_refs = refs[:n_in], refs[n_in:2 * n_in]
    x_ref, gm_ref, g1_ref, wg_ref, wu_ref, wd_ref, g2_ref, o_ref = refs[2 * n_in:]
    mixed = None
    for a_ref, w_ref in zip(a_refs, w_refs):
        t = _dot(a_ref[...].astype(BF16), w_ref[...])
        mixed = t if mixed is None else mixed + t
    x = x_ref[...] + _rms(mixed) * gm_ref[...]
    h = (_rms(x) * g1_ref[...]).astype(BF16)
    a = jax.nn.silu(_dot(h, wg_ref[...])) * _dot(h, wu_ref[...])
    o_ref[...] = x + _rms(_dot(a.astype(BF16), wd_ref[...])) * g2_ref[...]


def mix_ffn(acts, w_outs, x, g_mix, g1, wg, wu, wd, g2, *, tm):
    m, d = x.shape
    n_in = len(acts)
    resident = lambda a: pl.BlockSpec(a.shape, lambda i: (0, 0), pipeline_mode=pl.Buffered(1))
    vec = pl.BlockSpec((1, d), lambda i: (0, 0))
    return pl.pallas_call(
        functools.partial(_mix_ffn_kernel, n_in=n_in),
        grid=(m // tm,),
        in_specs=[pl.BlockSpec((tm, a.shape[1]), lambda i: (i, 0)) for a in acts]
        + [resident(w) for w in w_outs]
        + [pl.BlockSpec((tm, d), lambda i: (i, 0)), vec, vec, resident(wg), resident(wu), resident(wd), vec],
        out_specs=pl.BlockSpec((tm, d), lambda i: (i, 0)),
        out_shape=jax.ShapeDtypeStruct((m, d), F32),
        compiler_params=_cparams("parallel"),
        name="mix_ffn",
    )(*acts, *w_outs, x, g_mix.reshape(1, d), g1.reshape(1, d), wg, wu, wd, g2.reshape(1, d))


def _rel_buckets(dist):
    max_exact = REL_BUCKETS // 2
    n = np.maximum(dist, 1).astype(np.float32)
    large = max_exact + (np.log(n / max_exact) / math.log(REL_MAX_DIST / max_exact)
                         * (REL_BUCKETS - max_exact)).astype(np.int32)
    large = np.minimum(large, REL_BUCKETS - 1)
    return np.where(dist < max_exact, dist, large).astype(np.int32)


def _attn_bias_rows(rel_bias):
    u = np.arange(2 * A_BAND)
    valid = u <= A_BAND
    rows = []
    for (_, d) in A_PATTERNS:
        b = rel_bias[_rel_buckets(np.where(valid, A_BAND - u, 0) * d)]
        rows.append(jnp.where(valid[:, None], b.astype(F32), NEG))
    tl = jnp.transpose(jnp.stack(rows), (2, 0, 1))
    tl = tl.reshape(A_HEADS // 2, 2, len(A_PATTERNS), 2 * A_BAND)
    tl = jnp.transpose(tl, (0, 2, 1, 3))[:, :, :, None, :]
    return jnp.broadcast_to(tl, tl.shape[:3] + (8, 2 * A_BAND))


def _attn_kernel(qin_ref, kin_ref, vin_ref, brow_ref, o_ref, m0_ref, m1_ref, l0_ref, l1_ref, acc_ref,
                 q_ref, k_ref, v_ref, qc_ref, kc_ref, vc_ref, b_ref, *, seq):
    n_tiles = seq // A_BAND
    lane = _iota((A_BAND, LANES), 1)
    head0 = lane < A_HEAD_DIM
    m_refs, l_refs = (m0_ref, m1_ref), (l0_ref, l1_ref)
    q_ref[...] = qin_ref[...].astype(F32) * (A_HEAD_DIM ** -0.5)
    k_ref[...] = kin_ref[...].astype(F32)
    v_ref[...] = vin_ref[...].astype(F32)
    for p in range(len(A_PATTERNS)):
        for h in range(2):
            row = jnp.broadcast_to(brow_ref[0, p, h, 0:1, :], (A_BAND, 2 * A_BAND))
            b_ref[0, p, h] = pltpu.roll(row, 0, 1, stride=1, stride_axis=0)

    first_p = max(range(len(A_PATTERNS)), key=lambda p: A_PATTERNS[p][1])
    d_first = A_PATTERNS[first_p][1]
    class_len = seq // d_first

    def to_class_major(r, carry):
        dst = pl.ds(pl.multiple_of(r * class_len, class_len), class_len)
        for src_ref, dst_ref in ((q_ref, qc_ref), (k_ref, kc_ref), (v_ref, vc_ref)):
            dst_ref[dst, :] = src_ref[pl.ds(r, class_len, stride=d_first), :]
        return carry

    lax.fori_loop(0, d_first, to_class_major, 0)

    for p in [first_p] + [p for p in range(len(A_PATTERNS)) if p != first_p]:
        d = A_PATTERNS[p][1]
        fresh = p == first_p
        tiles_per_class = n_tiles // d

        def load_tile(idx, d=d, tiles_per_class=tiles_per_class, fresh=fresh):
            r = idx // tiles_per_class
            t = idx % tiles_per_class
            start = r + t * (d * A_BAND)
            has_prev = t > 0
            rows = pl.ds(start, A_BAND, stride=d) if d > 1 else pl.ds(pl.multiple_of(start, A_BAND), A_BAND)
            if fresh:
                base = r * class_len + t * A_BAND
                crow = pl.ds(pl.multiple_of(base, A_BAND), A_BAND)
                cprev = pl.ds(pl.multiple_of(jnp.where(has_prev, base - A_BAND, base), A_BAND), A_BAND)
                return dict(
                    rows=rows, has_prev=has_prev, q=qc_ref[crow, :],
                    k2=jnp.concatenate([kc_ref[cprev, :], kc_ref[crow, :]], axis=0).astype(BF16),
                    v2=jnp.concatenate([vc_ref[cprev, :], vc_ref[crow, :]], axis=0).astype(BF16))
            prev = jnp.where(has_prev, start - d * A_BAND, start)
            prows = pl.ds(prev, A_BAND, stride=d) if d > 1 else pl.ds(pl.multiple_of(prev, A_BAND), A_BAND)
            return dict(
                rows=rows, has_prev=has_prev, q=q_ref[rows, :],
                k2=jnp.concatenate([k_ref[prows, :], k_ref[rows, :]], axis=0).astype(BF16),
                v2=jnp.concatenate([v_ref[prows, :], v_ref[rows, :]], axis=0).astype(BF16),
                acc=acc_ref[rows, :], m=[m_refs[h][rows, :] for h in range(2)],
                l=[l_refs[h][rows, :] for h in range(2)])

        def tile_group(idx, carry, load_tile=load_tile, p=p, fresh=fresh):
            tiles = [load_tile(idx + i * (n_tiles // ATTN_GROUP)) for i in range(ATTN_GROUP)]
            chains = [(tile, h) for tile in tiles for h in range(2)]
            col = _iota((A_BAND, 2 * A_BAND), 1)
            qh = [jnp.where(head0 if h == 0 else jnp.logical_not(head0), tile["q"], 0.0).astype(BF16)
                  for tile, h in chains]
            s = [_dot_nt(qh[c], tile["k2"])
                 + jnp.where(jnp.logical_and(col < A_BAND, jnp.logical_not(tile["has_prev"])), NEG, b_ref[0, p, h])
                 for c, (tile, h) in enumerate(chains)]
            if fresh:
                wide = lambda x: jnp.broadcast_to(x, (A_BAND, LANES))
                m_new = [wide(jnp.max(s[c], axis=-1, keepdims=True)) for c in range(len(chains))]
                pr = [jnp.exp(s[c] - jnp.concatenate([m_new[c], m_new[c]], axis=1)) for c in range(len(chains))]
                l_new = [wide(jnp.sum(pr[c], axis=-1, keepdims=True)) for c in range(len(chains))]
                acc_new = [_dot(pr[c].astype(BF16), tile["v2"]) for c, (tile, h) in enumerate(chains)]
            else:
                m_new = [jnp.maximum(tile["m"][h], jnp.max(s[c], axis=-1, keepdims=True))
                         for c, (tile, h) in enumerate(chains)]
                alpha = [jnp.exp(tile["m"][h] - m_new[c]) for c, (tile, h) in enumerate(chains)]
                pr = [jnp.exp(s[c] - jnp.concatenate([m_new[c], m_new[c]], axis=1)) for c in range(len(chains))]
                l_new = [alpha[c] * tile["l"][h] + jnp.sum(pr[c], axis=-1, keepdims=True)
                         for c, (tile, h) in enumerate(chains)]
                acc_new = [alpha[c] * tile["acc"] + _dot(pr[c].astype(BF16), tile["v2"])
                           for c, (tile, h) in enumerate(chains)]
            for c, (tile, h) in enumerate(chains):
                m_refs[h][tile["rows"], :] = m_new[c]
                l_refs[h][tile["rows"], :] = l_new[c]
                if h == 1:
                    acc_ref[tile["rows"], :] = jnp.where(head0, acc_new[c - 1], acc_new[c])
            return carry

        lax.fori_loop(0, n_tiles // ATTN_GROUP, tile_group, 0)

    lane_s = _iota((seq, LANES), 1)
    o_ref[...] = (acc_ref[...] / jnp.where(lane_s < A_HEAD_DIM, l0_ref[...], l1_ref[...])).astype(o_ref.dtype)


def attention_prompt(proj, bias_tiles, *, bsz, seq):
    hp = A_HEADS // 2
    qb, kb, vb = HYB_Q0 // LANES, HYB_K0 // LANES, HYB_V0 // LANES
    return pl.pallas_call(
        functools.partial(_attn_kernel, seq=seq),
        grid=(bsz, hp),
        in_specs=[
            pl.BlockSpec((None, seq, LANES), lambda b, h: (b, 0, qb + h)),
            pl.BlockSpec((None, seq, LANES), lambda b, h: (b, 0, kb + h)),
            pl.BlockSpec((None, seq, LANES), lambda b, h: (b, 0, vb + h)),
            pl.BlockSpec((1,) + bias_tiles.shape[1:], lambda b, h: (h, 0, 0, 0, 0)),
        ],
        out_specs=pl.BlockSpec((None, seq, LANES), lambda b, h: (b, 0, h)),
        out_shape=jax.ShapeDtypeStruct((bsz, seq, A_WIDTH), BF16),
        scratch_shapes=[pltpu.VMEM((seq, LANES), F32)] * 11
        + [pltpu.VMEM((1, len(A_PATTERNS), 2, A_BAND, 2 * A_BAND), F32)],
        compiler_params=_cparams("parallel", "parallel"),
        name="attn_prompt",
    )(proj, proj, proj, bias_tiles)


def _attn_logw(rel_bias, past):
    dist = np.arange(past + 1)
    count = np.zeros(past + 1, np.float64)
    for (w, d) in A_PATTERNS:
        count += ((dist % d == 0) & (dist <= w)).astype(np.float64)
    logc = np.where(count > 0, np.log(np.maximum(count, 1.0)), 0.0).astype(np.float32)
    lw = rel_bias[_rel_buckets(dist)].astype(F32).T + logc[None, :]
    return jnp.where((count > 0)[None, :], lw, NEG)


def _step_scores(xk_ref, q_col, kn_col, lw_ref, lw0_ref, s_ref):
    qs = q_col * (A_HEAD_DIM ** -0.5)
    s_new = []
    for h in range(A_HEADS):
        rows = slice(h * A_HEAD_DIM, (h + 1) * A_HEAD_DIM)
        s_ref[h:h + 1, :] = jnp.sum(xk_ref[0, 0, rows, :] * qs[rows], axis=0, keepdims=True)
        s_new.append(jnp.sum(kn_col[rows] * qs[rows], axis=0, keepdims=True))
    s = s_ref[...] + lw_ref[...]
    s_new = jnp.concatenate(s_new, axis=0) + lw0_ref[:, :1]
    m = jnp.maximum(jnp.max(s, axis=-1, keepdims=True), s_new)
    p = jnp.exp(s - m)
    p_new = jnp.exp(s_new - m)
    den = jnp.sum(p, axis=-1, keepdims=True) + p_new
    return p, p_new, den


def _step_output(xv_ref, vn_col, p, p_new, den, o_ref):
    for h in range(A_HEADS):
        rows = slice(h * A_HEAD_DIM, (h + 1) * A_HEAD_DIM)
        pv = jnp.sum(xv_ref[0, 0, rows, :] * p[h:h + 1, :], axis=-1, keepdims=True)
        o_ref[0, rows, :] = (pv + p_new[h:h + 1, :] * vn_col[rows]) / den[h:h + 1, :]


def _attn_step_roll_kernel(q_ref, kn_ref, vn_ref, lw_ref, lw0_ref, xk_ref, xv_ref, o_ref, ko_ref, vo_ref, s_ref,
                           *, layer, past):
    is_layer = pl.program_id(0) == layer
    newest = _iota((A_HEAD_DIM, past), 1) == past - 1
    for x_ref, n_ref, out_ref in ((xk_ref, kn_ref, ko_ref), (xv_ref, vn_ref, vo_ref)):
        for h in range(A_HEADS):
            rows = slice(h * A_HEAD_DIM, (h + 1) * A_HEAD_DIM)
            rolled = pltpu.roll(x_ref[0, 0, rows, :], past - 1, 1)
            out_ref[0, 0, rows, :] = jnp.where(jnp.logical_and(newest, is_layer), n_ref[0, rows, :], rolled)

    @pl.when(is_layer)
    def _():
        p, p_new, den = _step_scores(xk_ref, q_ref[0], kn_ref[0], lw_ref, lw0_ref, s_ref)
        _step_output(xv_ref, vn_ref[0], p, p_new, den, o_ref.at[0])

    @pl.when(jnp.logical_not(is_layer))
    def _():
        o_ref[...] = jnp.zeros_like(o_ref)


def _attn_step_append_kernel(q_ref, kn_ref, vn_ref, lw_ref, lw0_ref, xk_ref, xv_ref, ko_in, vo_in,
                             o_ref, ko_ref, vo_ref, s_ref, *, past):
    del ko_in, vo_in
    p, p_new, den = _step_scores(xk_ref, q_ref[0], kn_ref[0], lw_ref, lw0_ref, s_ref)
    _step_output(xv_ref, vn_ref[0], p, p_new, den, o_ref)
    newest = _iota((A_WIDTH, LANES), 1) == LANES - 1
    for x_ref, n_ref, out_ref in ((xk_ref, kn_ref, ko_ref), (xv_ref, vn_ref, vo_ref)):
        rolled = pltpu.roll(x_ref[0, 0, :, past - LANES:past], LANES - 1, 1)
        out_ref[0, 0] = jnp.where(newest, n_ref[0], rolled)


def attention_step(q_col, kn_col, vn_col, cache_k, cache_v, rolled, logw, *, layer):
    n_layers, bsz, w, past = cache_k.shape
    lw_cache = logw[:, past:0:-1]
    lw_new = jnp.broadcast_to(logw[:, :1], (A_HEADS, LANES))
    out_shape = [jax.ShapeDtypeStruct((bsz, w, 1), F32),
                 jax.ShapeDtypeStruct(cache_k.shape, cache_k.dtype),
                 jax.ShapeDtypeStruct(cache_v.shape, cache_v.dtype)]
    scratch = [pltpu.VMEM((A_HEADS, past), F32)]
    if rolled is None:
        assert layer == 0
        col = pl.BlockSpec((1, w, 1), lambda l, b: (b, 0, 0))
        win = pl.BlockSpec((1, 1, w, past), lambda l, b: (l, b, 0, 0))
        o_all, rolled_k, rolled_v = pl.pallas_call(
            functools.partial(_attn_step_roll_kernel, layer=layer, past=past),
            grid=(n_layers, bsz),
            in_specs=[col, col, col, _small(lw_cache), _small(lw_new), win, win],
            out_specs=[pl.BlockSpec((1, 1, w, 1), lambda l, b: (l, b, 0, 0)), win, win],
            out_shape=[jax.ShapeDtypeStruct((n_layers, bsz, w, 1), F32)] + out_shape[1:],
            scratch_shapes=scratch,
            compiler_params=_cparams("arbitrary", "arbitrary"),
            name="attn_step_roll",
        )(q_col, kn_col, vn_col, lw_cache, lw_new, cache_k, cache_v)
        return o_all[layer], rolled_k, rolled_v
    col = pl.BlockSpec((1, w, 1), lambda b: (b, 0, 0))
    win = pl.BlockSpec((1, 1, w, past), lambda b: (layer, b, 0, 0))
    tail = pl.BlockSpec((1, 1, w, LANES), lambda b: (layer, b, 0, past // LANES - 1))
    return pl.pallas_call(
        functools.partial(_attn_step_append_kernel, past=past),
        grid=(bsz,),
        in_specs=[col, col, col, _small(lw_cache), _small(lw_new), win, win, tail, tail],
        out_specs=[col, tail, tail],
        out_shape=out_shape,
        scratch_shapes=scratch,
        input_output_aliases={7: 1, 8: 2},
        compiler_params=_cparams("arbitrary"),
        name="attn_step_append",
    )(q_col, kn_col, vn_col, lw_cache, lw_new, cache_k, cache_v, *rolled)


def _group_rms(y, w):
    half = SSM_D_INNER // SSM_GROUPS
    return [_rms(y[:, g * half:(g + 1) * half]) * w[:, g * half:(g + 1) * half] for g in range(SSM_GROUPS)]


def _ssd_kernel(z_ref, xs_ref, bc_ref, dt_ref, dtb_ref, alog_ref, d_ref, nw_ref, h0_ref, y_ref, h_ref, ys_ref):
    c = pl.program_id(1)
    ch = SSM_CHUNK

    @pl.when(c == 0)
    def _():
        h_ref[...] = h0_ref[...]

    xs = xs_ref[0].astype(F32)
    bc = bc_ref[0].astype(F32)

    dt = jax.nn.softplus(dt_ref[0] + dtb_ref[...])
    da = dt * (-jnp.exp(alog_ref[...]))
    row = _iota((ch, ch), 0)
    colv = _iota((ch, ch), 1)
    tril = (row >= colv).astype(F32)
    cs = _dot(tril, da, precision=HIGHEST)
    cs_t = cs.T
    causal = row >= colv

    heads = range(SSM_HEADS)
    hpg = SSM_HEADS // SSM_GROUPS
    bm = [bc[:, g * SSM_STATE:(g + 1) * SSM_STATE] for g in range(SSM_GROUPS)]
    cm = [bc[:, (SSM_GROUPS + g) * SSM_STATE:(SSM_GROUPS + g + 1) * SSM_STATE] for g in range(SSM_GROUPS)]
    cb = [_dot_nt(cm[g], bm[g]) for g in range(SSM_GROUPS)]
    dt_t = dt.T
    xs_t = xs.T
    w_t = dt_t * jnp.exp(cs_t[:, ch - 1:ch] - cs_t)
    e_last = jnp.exp(cs[ch - 1:ch, :])
    lane_lo = _iota((ch, LANES), 1) < SSM_HEAD_DIM
    row_lo = _iota((LANES, ch), 0) < SSM_HEAD_DIM
    pairs = range(SSM_HEADS // 2)
    csb = [jnp.broadcast_to(cs[:, h:h + 1], (ch, ch)) for h in heads]
    mix = [cb[h // hpg] * jnp.exp(jnp.where(causal, csb[h] - cs_t[h:h + 1, :], NEG)) * dt_t[h:h + 1, :]
           for h in heads]
    x_pair = [xs[:, j * LANES:(j + 1) * LANES] for j in pairs]
    y_intra = [jnp.where(lane_lo, _dot(mix[2 * j], x_pair[j]), _dot(mix[2 * j + 1], x_pair[j])) for j in pairs]
    h_pair = [h_ref[0, 2 * j:2 * j + 2].reshape(2 * SSM_HEAD_DIM, SSM_STATE) for j in pairs]
    y_inter = [_dot_nt(cm[2 * j // hpg], h_pair[j]) * jnp.exp(jnp.where(lane_lo, csb[2 * j], csb[2 * j + 1]))
               for j in pairs]
    xw_t = [xs_t[j * LANES:(j + 1) * LANES, :] * jnp.where(row_lo, w_t[2 * j:2 * j + 1, :], w_t[2 * j + 1:2 * j + 2, :])
            for j in pairs]
    st = [_dot(xw_t[j], bm[2 * j // hpg]) for j in pairs]
    for j in pairs:
        cols = slice(j * LANES, (j + 1) * LANES)
        ys_ref[:, cols] = y_intra[j] + y_inter[j] + d_ref[:, cols] * x_pair[j]
        decay = jnp.where(row_lo, e_last[:, 2 * j:2 * j + 1], e_last[:, 2 * j + 1:2 * j + 2])
        h_ref[0, 2 * j:2 * j + 2] = (h_pair[j] * decay + st[j]).reshape(2, SSM_HEAD_DIM, SSM_STATE)

    y = ys_ref[...] * jax.nn.silu(z_ref[0].astype(F32))
    half = SSM_D_INNER // SSM_GROUPS
    for g, yg in enumerate(_group_rms(y, nw_ref[...])):
        y_ref[0, :, g * half:(g + 1) * half] = yg.astype(y_ref.dtype)


def _small(a):
    return pl.BlockSpec(a.shape, lambda *_: (0,) * a.ndim)


def ssd_prompt(main, tail, h0, dt_bias, a_log, d_skip, norm_w, *, bsz, seq):
    nc = seq // SSM_CHUNK
    ch = SSM_CHUNK
    small = [_pad_tail(dt_bias.reshape(1, -1)), _pad_tail(a_log.reshape(1, -1)),
             jnp.repeat(d_skip, SSM_HEAD_DIM).reshape(1, -1), norm_w.reshape(1, -1)]
    return pl.pallas_call(
        _ssd_kernel,
        grid=(bsz, nc),
        in_specs=[
            pl.BlockSpec((1, ch, SSM_D_INNER), lambda b, c: (b, c, 0)),
            pl.BlockSpec((1, ch, SSM_D_INNER), lambda b, c: (b, c, 1)),
            pl.BlockSpec((1, ch, SSM_BC), lambda b, c: (b, c, HYB_BC0 // SSM_BC)),
            pl.BlockSpec((1, ch, LANES), lambda b, c: (b, c, 0)),
        ] + [_small(a) for a in small] + [
            pl.BlockSpec((1, SSM_HEADS, SSM_HEAD_DIM, SSM_STATE), lambda b, c: (b, 0, 0, 0)),
        ],
        out_specs=[
            pl.BlockSpec((1, ch, SSM_D_INNER), lambda b, c: (b, c, 0)),
            pl.BlockSpec((1, SSM_HEADS, SSM_HEAD_DIM, SSM_STATE), lambda b, c: (b, 0, 0, 0)),
        ],
        out_shape=[jax.ShapeDtypeStruct((bsz, seq, SSM_D_INNER), BF16),
                   jax.ShapeDtypeStruct((bsz, SSM_HEADS, SSM_HEAD_DIM, SSM_STATE), F32)],
        scratch_shapes=[pltpu.VMEM((ch, SSM_D_INNER), F32)],
        compiler_params=_cparams("parallel", "arbitrary"),
        name="ssd_prompt",
    )(main, main, main, tail, *small, h0)


def _row_to_col(row, eye):
    return jnp.sum(jnp.where(eye, row, 0.0), axis=1, keepdims=True)


def _col_to_row(col, eye):
    return jnp.sum(jnp.where(eye, col, 0.0), axis=0, keepdims=True)


def _conv_step(c0_ref, w_ref, x_row, c0, c1):
    acc = w_ref[SSM_CONV - 1:SSM_CONV, c0:c1] * x_row
    for i in range(SSM_CONV - 1):
        acc = acc + w_ref[i:i + 1, c0:c1] * c0_ref[0, i:i + 1, c0:c1]
    return acc


def _ssd_step_kernel(z_ref, xs_ref, bc_ref, dt_ref, c0_ref, cw_ref, cb_ref, dtb_ref, alog_ref, d_ref, nw_ref,
                     h0_ref, y_ref, co_ref, h_ref, ys_ref):
    xs_raw = xs_ref[0]
    bc_raw = bc_ref[0]
    xs = jax.nn.silu(_conv_step(c0_ref, cw_ref, xs_raw, 0, SSM_D_INNER) + cb_ref[:, 0:SSM_D_INNER])
    bc = jax.nn.silu(_conv_step(c0_ref, cw_ref, bc_raw, SSM_D_INNER, SSM_XBC) + cb_ref[:, SSM_D_INNER:SSM_XBC])
    co_ref[0, 0:SSM_CONV - 2, :] = c0_ref[0, 1:SSM_CONV - 1, :]
    co_ref[0, SSM_CONV - 2:SSM_CONV - 1, 0:SSM_D_INNER] = xs_raw
    co_ref[0, SSM_CONV - 2:SSM_CONV - 1, SSM_D_INNER:SSM_XBC] = bc_raw

    dt = jax.nn.softplus(dt_ref[0] + dtb_ref[...])
    dec = jnp.exp(dt * (-jnp.exp(alog_ref[...])))
    dskip = d_ref[...]
    eye = _iota((LANES, LANES), 0) == _iota((LANES, LANES), 1)
    upper = _iota((LANES, 1), 0) >= SSM_HEAD_DIM
    pairs = range(SSM_HEADS // 2)
    grp = [2 * j // (SSM_HEADS // SSM_GROUPS) for j in pairs]
    bm = [bc[:, g * SSM_STATE:(g + 1) * SSM_STATE] for g in grp]
    cm = [bc[:, (SSM_GROUPS + g) * SSM_STATE:(SSM_GROUPS + g + 1) * SSM_STATE] for g in grp]
    pick = lambda v, j: jnp.where(upper, v[:, 2 * j + 1:2 * j + 2], v[:, 2 * j:2 * j + 1])
    x_col = [_row_to_col(xs[:, j * LANES:(j + 1) * LANES], eye) for j in pairs]
    hp = [h0_ref[0, 2 * j:2 * j + 2].reshape(2 * SSM_HEAD_DIM, SSM_STATE) for j in pairs]
    hn = [hp[j] * pick(dec, j) + (x_col[j] * pick(dt, j)) * bm[j] for j in pairs]
    y_col = [jnp.sum(hn[j] * cm[j], axis=1, keepdims=True) + pick(dskip, j) * x_col[j] for j in pairs]
    y_row = [_col_to_row(y_col[j], eye) for j in pairs]
    for j in pairs:
        h_ref[0, 2 * j:2 * j + 2] = hn[j].reshape(2, SSM_HEAD_DIM, SSM_STATE)
        ys_ref[:, j * LANES:(j + 1) * LANES] = y_row[j]

    y = ys_ref[...] * jax.nn.silu(z_ref[0])
    half = SSM_D_INNER // SSM_GROUPS
    for g, yg in enumerate(_group_rms(y, nw_ref[...])):
        y_ref[0, :, g * half:(g + 1) * half] = yg


def ssd_step(main, tail, conv0, h0, conv_w, conv_b, dt_bias, a_log, d_skip, norm_w):
    bsz = main.shape[0]
    small = [conv_w, conv_b.reshape(1, -1), _pad_tail(dt_bias.reshape(1, -1)), _pad_tail(a_log.reshape(1, -1)),
             _pad_tail(d_skip.reshape(1, -1)), norm_w.reshape(1, -1)]
    hspec = pl.BlockSpec((1, SSM_HEADS, SSM_HEAD_DIM, SSM_STATE), lambda b: (b, 0, 0, 0))
    cspec = pl.BlockSpec((1, SSM_CONV - 1, SSM_XBC), lambda b: (b, 0, 0))
    return pl.pallas_call(
        _ssd_step_kernel,
        grid=(bsz,),
        in_specs=[
            pl.BlockSpec((1, 1, SSM_D_INNER), lambda b: (b, 0, 0)),
            pl.BlockSpec((1, 1, SSM_D_INNER), lambda b: (b, 0, 1)),
            pl.BlockSpec((1, 1, SSM_BC), lambda b: (b, 0, HYB_BC0 // SSM_BC)),
            pl.BlockSpec((1, 1, LANES), lambda b: (b, 0, 0)),
            cspec,
        ] + [_small(a) for a in small] + [hspec],
        out_specs=[pl.BlockSpec((1, 1, SSM_D_INNER), lambda b: (b, 0, 0)), cspec, hspec],
        out_shape=[jax.ShapeDtypeStruct((bsz, 1, SSM_D_INNER), F32),
                   jax.ShapeDtypeStruct(conv0.shape, F32),
                   jax.ShapeDtypeStruct(h0.shape, F32)],
        scratch_shapes=[pltpu.VMEM((1, SSM_D_INNER), F32)],
        compiler_params=_cparams("parallel"),
        name="ssd_step",
    )(main, main, main, tail, conv0, *small, h0)


def _l2norm(x):
    return x * lax.rsqrt(jnp.sum(x * x, axis=-1, keepdims=True) + EPS)


def _unit_lower_inverse(ns, eye):
    size = ns[0].shape[0]
    ps = [eye - n for n in ns]
    ms = [_dot(n, n) for n in ns]
    power = 2
    while 2 * power < size:
        pms = [_dot(jnp.concatenate([p, m], axis=0), m) for p, m in zip(ps, ms)]
        ps = [p + pm[:size] for p, pm in zip(ps, pms)]
        ms = [pm[size:] for pm in pms]
        power *= 2
    return [p + _dot(p, m) for p, m in zip(ps, ms)]


def _gdn_gates(ba, dtb_ref, alog_ref):
    beta = jax.nn.sigmoid(ba)
    g = -jnp.exp(alog_ref[...]) * jax.nn.softplus(ba + dtb_ref[...])
    return beta, g


def _gdn_kernel(q_ref, k_ref, v_ref, z_ref, ba_ref, dtb_ref, alog_ref, nw_ref, s0_ref, o_ref, s_ref):
    c = pl.program_id(1)
    ch = GDN_CHUNK
    nh = GDN_V_HEADS

    @pl.when(c == 0)
    def _():
        s_ref[...] = s0_ref[...]

    row = _iota((ch, ch), 0)
    colv = _iota((ch, ch), 1)
    incl = row >= colv
    strict = row > colv
    eye = (row == colv).astype(F32)
    rep = nh // GDN_QK_HEADS
    heads = range(nh)
    qk_heads = range(GDN_QK_HEADS)
    subs = range(GDN_STEP_CHUNKS)
    rows = {sc: slice(sc * ch, (sc + 1) * ch) for sc in subs}
    sh = [(sc, h) for sc in subs for h in heads]
    sj = [(sc, j) for sc in subs for j in qk_heads]

    gates = {sc: _gdn_gates(ba_ref[0, rows[sc], :], dtb_ref, alog_ref) for sc in subs}
    gcum = {sc: _dot(incl.astype(F32), gates[sc][1], precision=HIGHEST) for sc in subs}
    gcum_t = {sc: jnp.concatenate([gcum[sc], jnp.zeros((LANES - ch, LANES), F32)], axis=0).T for sc in subs}
    qn = {(sc, j): _l2norm(q_ref[0, rows[sc], j * GDN_DK:(j + 1) * GDN_DK].astype(F32)) * (GDN_DK ** -0.5)
          for sc, j in sj}
    kn = {(sc, j): _l2norm(k_ref[0, rows[sc], j * GDN_DK:(j + 1) * GDN_DK].astype(F32)) for sc, j in sj}
    kk = {k: _dot_nt(kn[k], kn[k]) for k in sj}
    qk = {k: _dot_nt(qn[k], kn[k]) for k in sj}
    gc_col = {(sc, h): gcum[sc][:, nh + h:nh + h + 1] for sc, h in sh}
    gc_last = {(sc, h): gcum[sc][ch - 1:ch, nh + h:nh + h + 1] for sc, h in sh}
    beta_col = {(sc, h): gates[sc][0][:, h:h + 1] for sc, h in sh}
    dec = {(sc, h): jnp.exp(jnp.where(incl, gc_col[sc, h] - gcum_t[sc][nh + h:nh + h + 1, 0:ch], NEG))
           for sc, h in sh}
    t_inv = dict(zip(sh, _unit_lower_inverse(
        [jnp.where(strict, kk[sc, h // rep] * dec[sc, h], 0.0) * beta_col[sc, h] for sc, h in sh], eye)))
    eg = {k: jnp.exp(gc_col[k]) for k in sh}
    lhs = {(sc, h): jnp.concatenate([kn[sc, h // rep] * (beta_col[sc, h] * eg[sc, h]), qn[sc, h // rep] * eg[sc, h]],
                                    axis=0) for sc, h in sh}
    vb = {(sc, h): v_ref[0, rows[sc], h * GDN_DV:(h + 1) * GDN_DV].astype(F32) * beta_col[sc, h] for sc, h in sh}
    attn = {(sc, h): qk[sc, h // rep] * dec[sc, h] for sc, h in sh}
    kdec = {(sc, h): kn[sc, h // rep] * jnp.exp(gc_last[sc, h] - gc_col[sc, h]) for sc, h in sh}

    state = {h: s_ref[0, h] for h in heads}
    for sc in subs:
        both = {h: _dot(lhs[sc, h], state[h]) for h in heads}
        u = {h: _dot(t_inv[sc, h], vb[sc, h] - both[h][:ch]) for h in heads}
        o = {h: both[h][ch:] + _dot(attn[sc, h], u[h]) for h in heads}
        state = {h: state[h] * jnp.exp(gc_last[sc, h]) + _dot_tn(kdec[sc, h], u[h]) for h in heads}
        for h in heads:
            z_h = z_ref[0, rows[sc], h * GDN_DV:(h + 1) * GDN_DV].astype(F32)
            o_ref[0, rows[sc], h * GDN_DV:(h + 1) * GDN_DV] = (
                _rms(o[h]) * nw_ref[...] * jax.nn.silu(z_h)).astype(o_ref.dtype)
    for h in heads:
        s_ref[0, h] = state[h]


def _gdn_gate_params(dt_bias, a_log):
    nh = GDN_V_HEADS
    dtb = jnp.zeros((1, LANES), F32).at[0, nh:2 * nh].set(dt_bias)
    alog = jnp.zeros((1, LANES), F32).at[0, nh:2 * nh].set(a_log)
    return dtb, alog


def gdn_prompt(main, tail, s0, dt_bias, a_log, norm_w, *, bsz, seq):
    ch = GDN_CHUNK * GDN_STEP_CHUNKS
    assert seq % ch == 0
    nc = seq // ch
    dtb, alog = _gdn_gate_params(dt_bias, a_log)
    small = [dtb, alog, norm_w.reshape(1, -1)]
    sspec = pl.BlockSpec((1, GDN_V_HEADS, GDN_DK, GDN_DV), lambda b, c: (b, 0, 0, 0))
    return pl.pallas_call(
        _gdn_kernel,
        grid=(bsz, nc),
        in_specs=[
            pl.BlockSpec((1, ch, GDN_QK_W), lambda b, c: (b, c, 0)),
            pl.BlockSpec((1, ch, GDN_QK_W), lambda b, c: (b, c, 1)),
            pl.BlockSpec((1, ch, GDN_VW), lambda b, c: (b, c, 1)),
            pl.BlockSpec((1, ch, GDN_VW), lambda b, c: (b, c, 2)),
            pl.BlockSpec((1, ch, LANES), lambda b, c: (b, c, 0)),
        ] + [_small(a) for a in small] + [sspec],
        out_specs=[pl.BlockSpec((1, ch, GDN_VW), lambda b, c: (b, c, 0)), sspec],
        out_shape=[jax.ShapeDtypeStruct((bsz, seq, GDN_VW), BF16),
                   jax.ShapeDtypeStruct((bsz, GDN_V_HEADS, GDN_DK, GDN_DV), F32)],
        compiler_params=_cparams("parallel", "arbitrary"),
        name="gdn_prompt",
    )(main, main, main, main, tail, *small, s0)


def _gdn_conv_step(c0_ref, w_ref, x_row, c0, c1):
    acc = w_ref[GDN_CONV - 1:GDN_CONV, c0:c1] * x_row
    for i in range(GDN_CONV - 1):
        acc = acc + w_ref[i:i + 1, c0:c1] * c0_ref[0, i:i + 1, c0:c1]
    return acc


def _gdn_step_kernel(q_ref, k_ref, v_ref, z_ref, ba_ref, c0_ref, cw_ref, dtb_ref, alog_ref, nw_ref, s0_ref,
                     o_ref, co_ref, s_ref):
    nh = GDN_V_HEADS
    q_raw, k_raw, v_raw = q_ref[0], k_ref[0], v_ref[0]
    q = jax.nn.silu(_gdn_conv_step(c0_ref, cw_ref, q_raw, 0, GDN_QK_W))
    k = jax.nn.silu(_gdn_conv_step(c0_ref, cw_ref, k_raw, GDN_QK_W, 2 * GDN_QK_W))
    v = jax.nn.silu(_gdn_conv_step(c0_ref, cw_ref, v_raw, 2 * GDN_QK_W, GDN_QKV))
    co_ref[0, 0:GDN_CONV - 2, :] = c0_ref[0, 1:GDN_CONV - 1, :]
    co_ref[0, GDN_CONV - 2:GDN_CONV - 1, 0:GDN_QK_W] = q_raw
    co_ref[0, GDN_CONV - 2:GDN_CONV - 1, GDN_QK_W:2 * GDN_QK_W] = k_raw
    co_ref[0, GDN_CONV - 2:GDN_CONV - 1, 2 * GDN_QK_W:GDN_QKV] = v_raw

    beta, g = _gdn_gates(ba_ref[0], dtb_ref, alog_ref)
    eg_all = jnp.exp(g)
    eye = _iota((LANES, LANES), 0) == _iota((LANES, LANES), 1)
    rep = nh // GDN_QK_HEADS
    heads = range(nh)
    qn = [_l2norm(q[:, j * GDN_DK:(j + 1) * GDN_DK]) * (GDN_DK ** -0.5) for j in range(GDN_QK_HEADS)]
    kn = [_l2norm(k[:, j * GDN_DK:(j + 1) * GDN_DK]) for j in range(GDN_QK_HEADS)]
    qk = [jnp.sum(a * b, axis=-1, keepdims=True) for a, b in zip(qn, kn)]
    q_col = [_row_to_col(a, eye) for a in qn]
    k_col = [_row_to_col(a, eye) for a in kn]
    b_h = [beta[:, h:h + 1] for h in heads]
    eg = [eg_all[:, nh + h:nh + h + 1] for h in heads]
    s_prev = [s0_ref[0, h] for h in heads]
    ks = [jnp.sum(s_prev[h] * k_col[h // rep], axis=0, keepdims=True) for h in heads]
    qs = [jnp.sum(s_prev[h] * q_col[h // rep], axis=0, keepdims=True) for h in heads]
    u = [v[:, h * GDN_DV:(h + 1) * GDN_DV] * b_h[h] - (b_h[h] * eg[h]) * ks[h] for h in heads]
    o = [eg[h] * qs[h] + qk[h // rep] * u[h] for h in heads]
    for h in heads:
        s_ref[0, h] = s_prev[h] * eg[h] + k_col[h // rep] * u[h]
        z_h = z_ref[0, :, h * GDN_DV:(h + 1) * GDN_DV]
        o_ref[0, :, h * GDN_DV:(h + 1) * GDN_DV] = _rms(o[h]) * nw_ref[...] * jax.nn.silu(z_h)


def gdn_step(main, tail, conv0, s0, conv_w, dt_bias, a_log, norm_w):
    bsz = main.shape[0]
    dtb, alog = _gdn_gate_params(dt_bias, a_log)
    small = [conv_w, dtb, alog, norm_w.reshape(1, -1)]
    sspec = pl.BlockSpec((1, GDN_V_HEADS, GDN_DK, GDN_DV), lambda b: (b, 0, 0, 0))
    cspec = pl.BlockSpec((1, GDN_CONV - 1, GDN_QKV), lambda b: (b, 0, 0))
    return pl.pallas_call(
        _gdn_step_kernel,
        grid=(bsz,),
        in_specs=[
            pl.BlockSpec((1, 1, GDN_QK_W), lambda b: (b, 0, 0)),
            pl.BlockSpec((1, 1, GDN_QK_W), lambda b: (b, 0, 1)),
            pl.BlockSpec((1, 1, GDN_VW), lambda b: (b, 0, 1)),
            pl.BlockSpec((1, 1, GDN_VW), lambda b: (b, 0, 2)),
            pl.BlockSpec((1, 1, LANES), lambda b: (b, 0, 0)),
            cspec,
        ] + [_small(a) for a in small] + [sspec],
        out_specs=[pl.BlockSpec((1, 1, GDN_VW), lambda b: (b, 0, 0)), cspec, sspec],
        out_shape=[jax.ShapeDtypeStruct((bsz, 1, GDN_VW), F32),
                   jax.ShapeDtypeStruct(conv0.shape, F32),
                   jax.ShapeDtypeStruct(s0.shape, F32)],
        compiler_params=_cparams("parallel"),
        name="gdn_step",
    )(main, main, main, main, tail, conv0, *small, s0)


def _pad_tail(w):
    return jnp.pad(w, ((0, 0), (0, LANES - w.shape[1])))


def _prep_hyb_in(w):
    a = A_WIDTH
    q, k, v = w[:, 0:a], w[:, a:2 * a], w[:, 2 * a:3 * a]
    z = w[:, 3 * a:3 * a + SSM_D_INNER]
    x0 = 3 * a + SSM_D_INNER
    xs = w[:, x0:x0 + SSM_D_INNER]
    bc = w[:, x0 + SSM_D_INNER:x0 + SSM_XBC]
    dt = w[:, x0 + SSM_XBC:]
    return jnp.concatenate([z, xs, q, k, v, bc], axis=1).astype(BF16), _pad_tail(dt).astype(BF16)


def _prep_gdn_in(w):
    return w[:, :GDN_MAIN].astype(BF16), _pad_tail(w[:, GDN_MAIN:]).astype(BF16)


HYB_CONV_COLS = ((SSM_D_INNER, 2 * SSM_D_INNER), (HYB_BC0, HYB_MAIN))


def _hyb_cols(a):
    out = jnp.zeros(a.shape[:-1] + (HYB_MAIN,), F32)
    (x0, x1), (b0, b1) = HYB_CONV_COLS
    return out.at[..., x0:x1].set(a[..., :SSM_D_INNER]).at[..., b0:b1].set(a[..., SSM_D_INNER:])


def _window_to_lanes(c):
    n, b, past, h, dh = c.shape
    return jnp.transpose(c, (0, 1, 3, 4, 2)).reshape(n, b, h * dh, past)


def _window_from_lanes(c):
    n, b, _, past = c.shape
    return jnp.transpose(c.reshape(n, b, A_HEADS, A_HEAD_DIM, past), (0, 1, 4, 2, 3))


def _row_tile(m, cap):
    return m if m <= cap else cap


def kernel(x_prompt, x_sample, cache_attn_k, cache_attn_v, state_ssm_conv, state_ssm, state_gdn_conv, state_gdn, rel_bias, norm_mix_pre, norm_mix_post, norm_ffn_pre, norm_ffn_post, w_hyb_in, ssm_conv_w, ssm_conv_b, ssm_dt_bias, ssm_a_log, ssm_d, ssm_norm_w, w_hyb_out, w_gdn_in, gdn_conv_w, gdn_dt_bias, gdn_a_log, gdn_norm_w, w_gdn_out, w_ffn_gate, w_ffn_up, w_ffn_down):
    depth = norm_mix_pre.shape[0]
    d_model = x_prompt.shape[-1]
    n_hyb, n_gdn = w_hyb_in.shape[0], w_gdn_in.shape[0]

    hyb_in = [_prep_hyb_in(w_hyb_in[i]) for i in range(n_hyb)]
    hyb_out = [(w_hyb_out[i, :A_WIDTH].astype(BF16), w_hyb_out[i, A_WIDTH:].astype(BF16)) for i in range(n_hyb)]
    gdn_in = [_prep_gdn_in(w_gdn_in[i]) for i in range(n_gdn)]
    gdn_out = [w_gdn_out[i].astype(BF16) for i in range(n_gdn)]
    ffn_w = [(w_ffn_gate[l].astype(BF16), w_ffn_up[l].astype(BF16), w_ffn_down[l].astype(BF16))
             for l in range(depth)]
    bias_tiles = _attn_bias_rows(rel_bias)

    def trunk(x3, k_pre, v_pre, sconv, sssm, gconv, gstate):
        bsz, seq, _ = x3.shape
        m = bsz * seq
        step = seq == 1
        tm_big = _row_tile(m, 2048)
        tm = _row_tile(m, 512)
        x = x3.reshape(m, d_model)
        nk, nv, nsc, nss, ngc, ngs = [], [], [], [], [], []
        rolled = None
        for l in range(depth):
            i = l // 2
            if l % 2 == 0:
                w_main, w_tail = hyb_in[i]
                ssm_args = (ssm_dt_bias[i], ssm_a_log[i], ssm_d[i], ssm_norm_w[i])
                if step:
                    main, tail = inproj(x, norm_mix_pre[l], w_main, w_tail, tm=tm_big, tn=512)
                    main3 = main.reshape(bsz, seq, HYB_MAIN)
                    tail3 = tail.reshape(bsz, seq, LANES)
                    col = lambda c0: main[:, c0:c0 + A_WIDTH].reshape(bsz, A_WIDTH, 1)
                    o_attn, *rolled = attention_step(
                        col(HYB_Q0), col(HYB_K0), col(HYB_V0), k_pre, v_pre, rolled,
                        _attn_logw(rel_bias, k_pre.shape[-1]), layer=i)
                    y, c_new, s_new = ssd_step(main3, tail3, sconv[i], sssm[i], ssm_conv_w[i], ssm_conv_b[i],
                                               *ssm_args)
                else:
                    main, tail, hist = inproj_conv(
                        x, norm_mix_pre[l], w_main, w_tail, _hyb_cols(ssm_conv_w[i]),
                        _hyb_cols(ssm_conv_b[i][None]), _hyb_cols(sconv[i]), seq=seq, conv_cols=HYB_CONV_COLS,
                        tm=tm_big, tn=512)
                    main3 = main.reshape(bsz, seq, HYB_MAIN)
                    tail3 = tail.reshape(bsz, seq, LANES)
                    c_new = jnp.concatenate([hist[..., a:b] for a, b in HYB_CONV_COLS], axis=-1)
                    o_attn = attention_prompt(main3, bias_tiles, bsz=bsz, seq=seq)
                    keep = min(A_PATTERNS[-1][0], seq)
                    k_new = main3[:, seq - keep:, HYB_K0:HYB_K0 + A_WIDTH].astype(F32)
                    v_new = main3[:, seq - keep:, HYB_V0:HYB_V0 + A_WIDTH].astype(F32)
                    y, s_new = ssd_prompt(main3, tail3, sssm[i], *ssm_args, bsz=bsz, seq=seq)
                    nk.append(k_new.reshape(bsz, -1, A_HEADS, A_HEAD_DIM))
                    nv.append(v_new.reshape(bsz, -1, A_HEADS, A_HEAD_DIM))
                nsc.append(c_new)
                nss.append(s_new)
                acts, w_outs = [o_attn.reshape(m, A_WIDTH), y.reshape(m, SSM_D_INNER)], list(hyb_out[i])
            else:
                w_main, w_tail = gdn_in[i]
                gdn_args = (gdn_dt_bias[i], gdn_a_log[i], gdn_norm_w[i])
                if step:
                    main, tail = inproj(x, norm_mix_pre[l], w_main, w_tail, tm=tm_big, tn=512)
                    o, c_new, s_new = gdn_step(main.reshape(bsz, seq, GDN_MAIN), tail.reshape(bsz, seq, LANES),
                                               gconv[i], gstate[i], gdn_conv_w[i], *gdn_args)
                else:
                    pad = lambda a: jnp.pad(a, [(0, 0)] * (a.ndim - 1) + [(0, GDN_MAIN - GDN_QKV)])
                    main, tail, hist = inproj_conv(
                        x, norm_mix_pre[l], w_main, w_tail, pad(gdn_conv_w[i]), jnp.zeros((1, GDN_MAIN), F32),
                        pad(gconv[i]), seq=seq, conv_cols=((0, GDN_QKV),), tm=tm_big, tn=512)
                    c_new = hist[..., :GDN_QKV]
                    o, s_new = gdn_prompt(main.reshape(bsz, seq, GDN_MAIN), tail.reshape(bsz, seq, LANES),
                                          gstate[i], *gdn_args, bsz=bsz, seq=seq)
                ngc.append(c_new)
                ngs.append(s_new)
                acts, w_outs = [o.reshape(m, GDN_VW)], [gdn_out[i]]
            wg, wu, wd = ffn_w[l]
            x = mix_ffn(acts, w_outs, x, norm_mix_post[l], norm_ffn_pre[l], wg, wu, wd, norm_ffn_post[l], tm=tm)
        k_out, v_out = [_window_from_lanes(r) for r in rolled] if step else (jnp.stack(nk), jnp.stack(nv))
        return (x.reshape(bsz, seq, d_model), k_out, v_out, jnp.stack(nsc), jnp.stack(nss),
                jnp.stack(ngc), jnp.stack(ngs))

    bsz = x_prompt.shape[0]
    dt_p = x_prompt.dtype
    p_sc0 = jnp.zeros((n_hyb, bsz, SSM_CONV - 1, SSM_XBC), dt_p)
    p_ss0 = jnp.zeros((n_hyb, bsz, SSM_HEADS, SSM_HEAD_DIM, SSM_STATE), F32)
    p_gc0 = jnp.zeros((n_gdn, bsz, GDN_CONV - 1, GDN_QKV), dt_p)
    p_gs0 = jnp.zeros((n_gdn, bsz, GDN_V_HEADS, GDN_DK, GDN_DV), F32)
    y_prompt, pk, pv, psc, pss, pgc, pgs = trunk(x_prompt, None, None, p_sc0, p_ss0, p_gc0, p_gs0)
    y_sample, sk, sv, ssc, sss, sgc, sgs = trunk(
        x_sample, _window_to_lanes(cache_attn_k), _window_to_lanes(cache_attn_v), state_ssm_conv, state_ssm,
        state_gdn_conv, state_gdn)
    return (y_prompt, y_sample, pk, pv, psc, pss, pgc, pgs, sk, sv, ssc, sss, sgc, sgs)
```

```python
import functools
import math

import numpy as np
import jax
import jax.numpy as jnp
from jax import lax
from jax.experimental import pallas as pl
from jax.experimental.pallas import tpu as pltpu

F32 = jnp.float32
BF16 = jnp.bfloat16
EPS = 1e-6
NEG = -1e30
HIGHEST = lax.Precision.HIGHEST

VMEM_LIMIT_BYTES = 56 * 1024 * 1024
LANES = 128

A_HEADS = 8
A_HEAD_DIM = 64
A_WIDTH = A_HEADS * A_HEAD_DIM
A_PATTERNS = ((128, 1), (512, 4), (2048, 16))
A_BAND = 128
ATTN_GROUP = 4
REL_BUCKETS = 32
REL_MAX_DIST = 2048

SSM_D_INNER = 1024
SSM_HEAD_DIM = 64
SSM_HEADS = SSM_D_INNER // SSM_HEAD_DIM
SSM_GROUPS = 2
SSM_STATE = 128
SSM_CONV = 4
SSM_CHUNK = 128
SSM_BC = 2 * SSM_GROUPS * SSM_STATE
SSM_XBC = SSM_D_INNER + SSM_BC

GDN_QK_HEADS = 8
GDN_V_HEADS = 16
GDN_DK = 128
GDN_DV = 128
GDN_CONV = 4
GDN_CHUNK = 64
GDN_STEP_CHUNKS = 1
GDN_QK_W = GDN_QK_HEADS * GDN_DK
GDN_VW = GDN_V_HEADS * GDN_DV
GDN_QKV = 2 * GDN_QK_W + GDN_VW

HYB_MAIN = 2 * SSM_D_INNER + 3 * A_WIDTH + SSM_BC
HYB_Q0 = 2 * SSM_D_INNER
HYB_K0 = HYB_Q0 + A_WIDTH
HYB_V0 = HYB_K0 + A_WIDTH
HYB_BC0 = HYB_V0 + A_WIDTH
GDN_MAIN = GDN_QKV + GDN_VW


def _cparams(*sem):
    return pltpu.CompilerParams(dimension_semantics=sem, vmem_limit_bytes=VMEM_LIMIT_BYTES)


def _rms(x):
    return x * lax.rsqrt(jnp.mean(x * x, axis=-1, keepdims=True) + EPS)


def _dot(a, b, **kw):
    return jnp.dot(a, b, preferred_element_type=F32, **kw)


def _dot_nt(a, b, **kw):
    return lax.dot_general(a, b, (((1,), (1,)), ((), ())), preferred_element_type=F32, **kw)


def _dot_tn(a, b, **kw):
    return lax.dot_general(a, b, (((0,), (0,)), ((), ())), preferred_element_type=F32, **kw)


def _iota(shape, dim):
    return lax.broadcasted_iota(jnp.int32, shape, dim)


def _inproj_kernel(x_ref, g_ref, w_ref, wt_ref, o_ref, t_ref, h_ref):
    @pl.when(pl.program_id(1) == 0)
    def _():
        hb = (_rms(x_ref[...]) * g_ref[...]).astype(BF16)
        h_ref[...] = hb
        t_ref[...] = _dot(hb, wt_ref[...])

    o_ref[...] = _dot(h_ref[...], w_ref[...])


def inproj(x, g, w_main, w_tail, *, tm, tn):
    m, d = x.shape
    n = w_main.shape[1]
    return pl.pallas_call(
        _inproj_kernel,
        grid=(m // tm, n // tn),
        in_specs=[
            pl.BlockSpec((tm, d), lambda i, j: (i, 0)),
            pl.BlockSpec((1, d), lambda i, j: (0, 0)),
            pl.BlockSpec((d, tn), lambda i, j: (0, j)),
            pl.BlockSpec((d, LANES), lambda i, j: (0, 0)),
        ],
        out_specs=[
            pl.BlockSpec((tm, tn), lambda i, j: (i, j)),
            pl.BlockSpec((tm, LANES), lambda i, j: (i, 0)),
        ],
        out_shape=[jax.ShapeDtypeStruct((m, n), F32), jax.ShapeDtypeStruct((m, LANES), F32)],
        scratch_shapes=[pltpu.VMEM((tm, d), BF16)],
        compiler_params=_cparams("parallel", "arbitrary"),
        name="inproj",
    )(x, g.reshape(1, d), w_main, w_tail)


CONV_TAPS = 4
CONV_ROWS = 64
PROJ_K_SLICES = 4
CONV_BASE = 8


def _inproj_conv_kernel(x_ref, g_ref, w_ref, wt_ref, cw_ref, cb_ref, c0_ref, o_ref, t_ref, so_ref,
                        h_ref, xp0_ref, xp1_ref, carry_ref, *, n_col, conv_tiles, tiles_per_seq):
    s = pl.program_id(0)
    n_tiles = pl.num_programs(0) - 1
    tm = o_ref.shape[0]
    lo = CONV_BASE - (CONV_TAPS - 1)
    cur = jnp.minimum(s, n_tiles - 1)
    prv = jnp.maximum(s - 1, 0)
    pi, pj = prv // n_col, prv % n_col
    prv_conv = functools.reduce(jnp.logical_or, [jnp.logical_and(pj >= a, pj < b) for a, b in conv_tiles])
    first = pi % tiles_per_seq == 0
    both = lambda a, b: jnp.logical_and(a, b)

    @pl.when(both(cur % n_col == 0, s < n_tiles))
    def _():
        hb = (_rms(x_ref[...]) * g_ref[...]).astype(BF16)
        h_ref[...] = hb
        t_ref[...] = _dot(hb, wt_ref[...])

    for parity, (cur_ref, prv_ref) in enumerate(((xp0_ref, xp1_ref), (xp1_ref, xp0_ref))):
        here = s % 2 == parity

        def project(cur_ref=cur_ref):
            cur_ref[CONV_BASE:CONV_BASE + tm, :] = _dot(h_ref[...], w_ref[...])

        def raw_tail(prv_ref=prv_ref):
            tail = prv_ref[CONV_BASE + tm - (CONV_TAPS - 1):CONV_BASE + tm, :]
            so_ref[pj, pi] = tail
            return tail

        if parity == 0:
            @pl.when(s == 0)
            def _():
                project()

        @pl.when(both(here, both(s > 0, jnp.logical_not(prv_conv))))
        def _():
            project()
            raw_tail()
            o_ref[...] = prv_ref[CONV_BASE:CONV_BASE + tm, :].astype(o_ref.dtype)

        @pl.when(both(here, both(both(s > 0, prv_conv), first)))
        def _():
            prv_ref[lo:CONV_BASE, :] = c0_ref[pj, pi // tiles_per_seq]

        @pl.when(both(here, both(both(s > 0, prv_conv), jnp.logical_not(first))))
        def _():
            prv_ref[lo:CONV_BASE, :] = carry_ref[pj]

        @pl.when(both(here, both(s > 0, prv_conv)))
        def _():
            def conv_chunk(r0):
                ext = prv_ref[r0:r0 + CONV_BASE + CONV_ROWS, :]
                conv = cb_ref[pj] + cw_ref[pj, CONV_TAPS - 1:CONV_TAPS, :] * ext[CONV_BASE:]
                for back in range(1, CONV_TAPS):
                    tap = CONV_TAPS - 1 - back
                    conv = conv + cw_ref[pj, tap:tap + 1, :] * pltpu.roll(ext, back, 0)[CONV_BASE:]
                o_ref[r0:r0 + CONV_ROWS, :] = jax.nn.silu(conv).astype(o_ref.dtype)

            d = h_ref.shape[1]
            n_chunks = tm // CONV_ROWS
            acc = None
            for kq in range(PROJ_K_SLICES):
                ks = slice(kq * d // PROJ_K_SLICES, (kq + 1) * d // PROJ_K_SLICES)
                part = _dot(h_ref[:, ks], w_ref[ks, :])
                acc = part if acc is None else acc + part
                for c in range(kq * n_chunks // PROJ_K_SLICES, (kq + 1) * n_chunks // PROJ_K_SLICES):
                    conv_chunk(c * CONV_ROWS)
            cur_ref[CONV_BASE:CONV_BASE + tm, :] = acc
            carry_ref[pj] = raw_tail()


def inproj_conv(x, g, w_main, w_tail, conv_w, conv_b, conv0, *, seq, conv_cols, tm, tn):
    m, d = x.shape
    n = w_main.shape[1]
    assert seq % tm == 0 and all(a % tn == 0 and b % tn == 0 for a, b in conv_cols)
    tiles_per_seq = seq // tm
    n_row, n_col = m // tm, n // tn
    n_tiles = n_row * n_col
    conv_tiles = tuple((a // tn, b // tn) for a, b in conv_cols)
    hist = CONV_TAPS - 1
    cur = lambda s: jnp.minimum(s, n_tiles - 1)
    prv = lambda s: jnp.maximum(s - 1, 0)
    by_tile = lambda a: jnp.moveaxis(a.reshape(a.shape[:-1] + (n_col, tn)), -2, 0)
    main, tail, hist_rows = pl.pallas_call(
        functools.partial(_inproj_conv_kernel, n_col=n_col, conv_tiles=conv_tiles, tiles_per_seq=tiles_per_seq),
        grid=(n_tiles + 1,),
        in_specs=[
            pl.BlockSpec((tm, d), lambda s: (cur(s) // n_col, 0)),
            pl.BlockSpec((1, d), lambda s: (0, 0)),
            pl.BlockSpec((None, d, tn), lambda s: (cur(s) % n_col, 0, 0)),
            pl.BlockSpec((d, LANES), lambda s: (0, 0)),
            pl.BlockSpec((n_col, CONV_TAPS, tn), lambda s: (0, 0, 0)),
            pl.BlockSpec((n_col, 1, tn), lambda s: (0, 0, 0)),
            pl.BlockSpec((n_col, m // seq, hist, tn), lambda s: (0, 0, 0, 0)),
        ],
        out_specs=[
            pl.BlockSpec((tm, tn), lambda s: (prv(s) // n_col, prv(s) % n_col)),
            pl.BlockSpec((tm, LANES), lambda s: (cur(s) // n_col, 0)),
            pl.BlockSpec((n_col, n_row, hist, tn), lambda s: (0, 0, 0, 0)),
        ],
        out_shape=[jax.ShapeDtypeStruct((m, n), BF16), jax.ShapeDtypeStruct((m, LANES), F32),
                   jax.ShapeDtypeStruct((n_col, n_row, hist, tn), F32)],
        scratch_shapes=[pltpu.VMEM((tm, d), BF16), pltpu.VMEM((CONV_BASE + tm, tn), F32),
                        pltpu.VMEM((CONV_BASE + tm, tn), F32), pltpu.VMEM((n_col, hist, tn), F32)],
        compiler_params=_cparams("arbitrary"),
        name="inproj_conv",
    )(x, g.reshape(1, d), by_tile(w_main), w_tail, by_tile(conv_w), by_tile(conv_b), by_tile(conv0))
    hist_rows = jnp.moveaxis(hist_rows, 0, -2).reshape(n_row, hist, n)
    return main, tail, hist_rows[tiles_per_seq - 1::tiles_per_seq]


def _mix_ffn_kernel(*refs, n_in):
    a_refs, w_refs = refs[:n_in], refs[n_in:2 * n_in]
    x_ref, gm_ref, g1_ref, wg_ref, wu_ref, wd_ref, g2_ref, o_ref = refs[2 * n_in:]
    mixed = None
    for a_ref, w_ref in zip(a_refs, w_refs):
        t = _dot(a_ref[...].astype(BF16), w_ref[...])
        mixed = t if mixed is None else mixed + t
    x = x_ref[...] + _rms(mixed) * gm_ref[...]
    h = (_rms(x) * g1_ref[...]).astype(BF16)
    a = jax.nn.silu(_dot(h, wg_ref[...])) * _dot(h, wu_ref[...])
    o_ref[...] = x + _rms(_dot(a.astype(BF16), wd_ref[...])) * g2_ref[...]


def mix_ffn(acts, w_outs, x, g_mix, g1, wg, wu, wd, g2, *, tm):
    m, d = x.shape
    n_in = len(acts)
    resident = lambda a: pl.BlockSpec(a.shape, lambda i: (0, 0), pipeline_mode=pl.Buffered(1))
    vec = pl.BlockSpec((1, d), lambda i: (0, 0))
    return pl.pallas_call(
        functools.partial(_mix_ffn_kernel, n_in=n_in),
        grid=(m // tm,),
        in_specs=[pl.BlockSpec((tm, a.shape[1]), lambda i: (i, 0)) for a in acts]
        + [resident(w) for w in w_outs]
        + [pl.BlockSpec((tm, d), lambda i: (i, 0)), vec, vec, resident(wg), resident(wu), resident(wd), vec],
        out_specs=pl.BlockSpec((tm, d), lambda i: (i, 0)),
        out_shape=jax.ShapeDtypeStruct((m, d), F32),
        compiler_params=_cparams("parallel"),
        name="mix_ffn",
    )(*acts, *w_outs, x, g_mix.reshape(1, d), g1.reshape(1, d), wg, wu, wd, g2.reshape(1, d))


def _rel_buckets(dist):
    max_exact = REL_BUCKETS // 2
    n = np.maximum(dist, 1).astype(np.float32)
    large = max_exact + (np.log(n / max_exact) / math.log(REL_MAX_DIST / max_exact)
                         * (REL_BUCKETS - max_exact)).astype(np.int32)
    large = np.minimum(large, REL_BUCKETS - 1)
    return np.where(dist < max_exact, dist, large).astype(np.int32)


def _attn_bias_rows(rel_bias):
    u = np.arange(2 * A_BAND)
    valid = u <= A_BAND
    rows = []
    for (_, d) in A_PATTERNS:
        b = rel_bias[_rel_buckets(np.where(valid, A_BAND - u, 0) * d)]
        rows.append(jnp.where(valid[:, None], b.astype(F32), NEG))
    tl = jnp.transpose(jnp.stack(rows), (2, 0, 1))
    tl = tl.reshape(A_HEADS // 2, 2, len(A_PATTERNS), 2 * A_BAND)
    tl = jnp.transpose(tl, (0, 2, 1, 3))[:, :, :, None, :]
    return jnp.broadcast_to(tl, tl.shape[:3] + (8, 2 * A_BAND))


def _attn_kernel(qin_ref, kin_ref, vin_ref, brow_ref, o_ref, *scratch, seq):
    n_pat = len(A_PATTERNS)
    m_refs, l_refs, acc_refs = scratch[0:n_pat], scratch[n_pat:2 * n_pat], scratch[2 * n_pat:3 * n_pat]
    q_ref, k_ref, v_ref, qc_ref, kc_ref, vc_ref, b_ref = scratch[3 * n_pat:]
    n_tiles = seq // A_BAND
    lane = _iota((A_BAND, LANES), 1)
    head0 = lane < A_HEAD_DIM
    q_ref[...] = qin_ref[...].astype(F32) * (A_HEAD_DIM ** -0.5)
    k_ref[...] = kin_ref[...].astype(F32)
    v_ref[...] = vin_ref[...].astype(F32)
    for p in range(len(A_PATTERNS)):
        for h in range(2):
            row = jnp.broadcast_to(brow_ref[0, p, h, 0:1, :], (A_BAND, 2 * A_BAND))
            b_ref[0, p, h] = pltpu.roll(row, 0, 1, stride=1, stride_axis=0)

    first_p = max(range(len(A_PATTERNS)), key=lambda p: A_PATTERNS[p][1])
    d_first = A_PATTERNS[first_p][1]
    class_len = seq // d_first

    def to_class_major(r, carry):
        dst = pl.ds(pl.multiple_of(r * class_len, class_len), class_len)
        for src_ref, dst_ref in ((q_ref, qc_ref), (k_ref, kc_ref), (v_ref, vc_ref)):
            dst_ref[dst, :] = src_ref[pl.ds(r, class_len, stride=d_first), :]
        return carry

    lax.fori_loop(0, d_first, to_class_major, 0)

    for p, (_, d) in enumerate(A_PATTERNS):
        tiles_per_class = n_tiles // d
        class_major = p == first_p

        def load_tile(idx, d=d, tiles_per_class=tiles_per_class, class_major=class_major):
            r = idx // tiles_per_class
            t = idx % tiles_per_class
            start = r + t * (d * A_BAND)
            has_prev = t > 0
            rows = pl.ds(start, A_BAND, stride=d) if d > 1 else pl.ds(pl.multiple_of(start, A_BAND), A_BAND)
            if class_major:
                base = r * class_len + t * A_BAND
                crow = pl.ds(pl.multiple_of(base, A_BAND), A_BAND)
                cprev = pl.ds(pl.multiple_of(jnp.where(has_prev, base - A_BAND, base), A_BAND), A_BAND)
                return dict(
                    rows=rows, has_prev=has_prev, q=qc_ref[crow, :],
                    k2=jnp.concatenate([kc_ref[cprev, :], kc_ref[crow, :]], axis=0).astype(BF16),
                    v2=jnp.concatenate([vc_ref[cprev, :], vc_ref[crow, :]], axis=0).astype(BF16))
            prev = jnp.where(has_prev, start - d * A_BAND, start)
            prows = pl.ds(prev, A_BAND, stride=d) if d > 1 else pl.ds(pl.multiple_of(prev, A_BAND), A_BAND)
            return dict(
                rows=rows, has_prev=has_prev, q=q_ref[rows, :],
                k2=jnp.concatenate([k_ref[prows, :], k_ref[rows, :]], axis=0).astype(BF16),
                v2=jnp.concatenate([v_ref[prows, :], v_ref[rows, :]], axis=0).astype(BF16))

        def tile_group(idx, carry, load_tile=load_tile, p=p):
            tiles = [load_tile(idx + i * (n_tiles // ATTN_GROUP)) for i in range(ATTN_GROUP)]
            chains = [(tile, h) for tile in tiles for h in range(2)]
            n_chain = range(len(chains))
            col = _iota((A_BAND, 2 * A_BAND), 1)
            wide = lambda x: jnp.broadcast_to(x, (A_BAND, LANES))
            qh = [jnp.where(head0 if h == 0 else jnp.logical_not(head0), tile["q"], 0.0).astype(BF16)
                  for tile, h in chains]
            s = [_dot_nt(qh[c], tile["k2"])
                 + jnp.where(jnp.logical_and(col < A_BAND, jnp.logical_not(tile["has_prev"])), NEG, b_ref[0, p, h])
                 for c, (tile, h) in enumerate(chains)]
            m_new = [wide(jnp.max(s[c], axis=-1, keepdims=True)) for c in n_chain]
            pr = [jnp.exp(s[c] - jnp.concatenate([m_new[c], m_new[c]], axis=1)) for c in n_chain]
            l_new = [wide(jnp.sum(pr[c], axis=-1, keepdims=True)) for c in n_chain]
            acc_new = [_dot(pr[c].astype(BF16), tile["v2"]) for c, (tile, h) in enumerate(chains)]
            for c, (tile, h) in enumerate(chains):
                if h == 1:
                    m_refs[p][tile["rows"], :] = jnp.where(head0, m_new[c - 1], m_new[c])
                    l_refs[p][tile["rows"], :] = jnp.where(head0, l_new[c - 1], l_new[c])
                    acc_refs[p][tile["rows"], :] = jnp.where(head0, acc_new[c - 1], acc_new[c])
            return carry

        lax.fori_loop(0, n_tiles // ATTN_GROUP, tile_group, 0)

    m_all = functools.reduce(jnp.maximum, [m_ref[...] for m_ref in m_refs])
    wts = [jnp.exp(m_ref[...] - m_all) for m_ref in m_refs]
    num = functools.reduce(jnp.add, [w * acc_ref[...] for w, acc_ref in zip(wts, acc_refs)])
    den = functools.reduce(jnp.add, [w * l_ref[...] for w, l_ref in zip(wts, l_refs)])
    o_ref[...] = (num / den).astype(o_ref.dtype)


def attention_prompt(proj, bias_tiles, *, bsz, seq):
    hp = A_HEADS // 2
    qb, kb, vb = HYB_Q0 // LANES, HYB_K0 // LANES, HYB_V0 // LANES
    return pl.pallas_call(
        functools.partial(_attn_kernel, seq=seq),
        grid=(bsz, hp),
        in_specs=[
            pl.BlockSpec((None, seq, LANES), lambda b, h: (b, 0, qb + h)),
            pl.BlockSpec((None, seq, LANES), lambda b, h: (b, 0, kb + h)),
            pl.BlockSpec((None, seq, LANES), lambda b, h: (b, 0, vb + h)),
            pl.BlockSpec((1,) + bias_tiles.shape[1:], lambda b, h: (h, 0, 0, 0, 0)),
        ],
        out_specs=pl.BlockSpec((None, seq, LANES), lambda b, h: (b, 0, h)),
        out_shape=jax.ShapeDtypeStruct((bsz, seq, A_WIDTH), BF16),
        scratch_shapes=[pltpu.VMEM((seq, LANES), F32)] * (3 * len(A_PATTERNS) + 6)
        + [pltpu.VMEM((1, len(A_PATTERNS), 2, A_BAND, 2 * A_BAND), F32)],
        compiler_params=_cparams("parallel", "parallel"),
        name="attn_prompt",
    )(proj, proj, proj, bias_tiles)


def _attn_logw(rel_bias, past):
    dist = np.arange(past + 1)
    count = np.zeros(past + 1, np.float64)
    for (w, d) in A_PATTERNS:
        count += ((dist % d == 0) & (dist <= w)).astype(np.float64)
    logc = np.where(count > 0, np.log(np.maximum(count, 1.0)), 0.0).astype(np.float32)
    lw = rel_bias[_rel_buckets(dist)].astype(F32).T + logc[None, :]
    return jnp.where((count > 0)[None, :], lw, NEG)


def _step_scores(xk_ref, q_col, kn_col, lw_ref, lw0_ref, s_ref):
    qs = q_col * (A_HEAD_DIM ** -0.5)
    s_new = []
    for h in range(A_HEADS):
        rows = slice(h * A_HEAD_DIM, (h + 1) * A_HEAD_DIM)
        s_ref[h:h + 1, :] = jnp.sum(xk_ref[0, 0, rows, :] * qs[rows], axis=0, keepdims=True)
        s_new.append(jnp.sum(kn_col[rows] * qs[rows], axis=0, keepdims=True))
    s = s_ref[...] + lw_ref[...]
    s_new = jnp.concatenate(s_new, axis=0) + lw0_ref[:, :1]
    m = jnp.maximum(jnp.max(s, axis=-1, keepdims=True), s_new)
    p = jnp.exp(s - m)
    p_new = jnp.exp(s_new - m)
    den = jnp.sum(p, axis=-1, keepdims=True) + p_new
    return p, p_new, den


def _step_output(xv_ref, vn_col, p, p_new, den, o_ref):
    for h in range(A_HEADS):
        rows = slice(h * A_HEAD_DIM, (h + 1) * A_HEAD_DIM)
        pv = jnp.sum(xv_ref[0, 0, rows, :] * p[h:h + 1, :], axis=-1, keepdims=True)
        o_ref[0, rows, :] = (pv + p_new[h:h + 1, :] * vn_col[rows]) / den[h:h + 1, :]


def _attn_step_roll_kernel(q_ref, kn_ref, vn_ref, lw_ref, lw0_ref, xk_ref, xv_ref, o_ref, ko_ref, vo_ref, s_ref,
                           *, layer, past):
    is_layer = pl.program_id(0) == layer
    newest = _iota((A_HEAD_DIM, past), 1) == past - 1
    for x_ref, n_ref, out_ref in ((xk_ref, kn_ref, ko_ref), (xv_ref, vn_ref, vo_ref)):
        for h in range(A_HEADS):
            rows = slice(h * A_HEAD_DIM, (h + 1) * A_HEAD_DIM)
            rolled = pltpu.roll(x_ref[0, 0, rows, :], past - 1, 1)
            out_ref[0, 0, rows, :] = jnp.where(jnp.logical_and(newest, is_layer), n_ref[0, rows, :], rolled)

    @pl.when(is_layer)
    def _():
        p, p_new, den = _step_scores(xk_ref, q_ref[0], kn_ref[0], lw_ref, lw0_ref, s_ref)
        _step_output(xv_ref, vn_ref[0], p, p_new, den, o_ref.at[0])

    @pl.when(jnp.logical_not(is_layer))
    def _():
        o_ref[...] = jnp.zeros_like(o_ref)


def _attn_step_append_kernel(q_ref, kn_ref, vn_ref, lw_ref, lw0_ref, xk_ref, xv_ref, ko_in, vo_in,
                             o_ref, ko_ref, vo_ref, s_ref, *, past):
    del ko_in, vo_in
    p, p_new, den = _step_scores(xk_ref, q_ref[0], kn_ref[0], lw_ref, lw0_ref, s_ref)
    _step_output(xv_ref, vn_ref[0], p, p_new, den, o_ref)
    newest = _iota((A_WIDTH, LANES), 1) == LANES - 1
    for x_ref, n_ref, out_ref in ((xk_ref, kn_ref, ko_ref), (xv_ref, vn_ref, vo_ref)):
        rolled = pltpu.roll(x_ref[0, 0, :, past - LANES:past], LANES - 1, 1)
        out_ref[0, 0] = jnp.where(newest, n_ref[0], rolled)


def attention_step(q_col, kn_col, vn_col, cache_k, cache_v, rolled, logw, *, layer):
    n_layers, bsz, w, past = cache_k.shape
    lw_cache = logw[:, past:0:-1]
    lw_new = jnp.broadcast_to(logw[:, :1], (A_HEADS, LANES))
    out_shape = [jax.ShapeDtypeStruct((bsz, w, 1), F32),
                 jax.ShapeDtypeStruct(cache_k.shape, cache_k.dtype),
                 jax.ShapeDtypeStruct(cache_v.shape, cache_v.dtype)]
    scratch = [pltpu.VMEM((A_HEADS, past), F32)]
    if rolled is None:
        assert layer == 0
        col = pl.BlockSpec((1, w, 1), lambda l, b: (b, 0, 0))
        win = pl.BlockSpec((1, 1, w, past), lambda l, b: (l, b, 0, 0))
        o_all, rolled_k, rolled_v = pl.pallas_call(
            functools.partial(_attn_step_roll_kernel, layer=layer, past=past),
            grid=(n_layers, bsz),
            in_specs=[col, col, col, _small(lw_cache), _small(lw_new), win, win],
            out_specs=[pl.BlockSpec((1, 1, w, 1), lambda l, b: (l, b, 0, 0)), win, win],
            out_shape=[jax.ShapeDtypeStruct((n_layers, bsz, w, 1), F32)] + out_shape[1:],
            scratch_shapes=scratch,
            compiler_params=_cparams("arbitrary", "arbitrary"),
            name="attn_step_roll",
        )(q_col, kn_col, vn_col, lw_cache, lw_new, cache_k, cache_v)
        return o_all[layer], rolled_k, rolled_v
    col = pl.BlockSpec((1, w, 1), lambda b: (b, 0, 0))
    win = pl.BlockSpec((1, 1, w, past), lambda b: (layer, b, 0, 0))
    tail = pl.BlockSpec((1, 1, w, LANES), lambda b: (layer, b, 0, past // LANES - 1))
    return pl.pallas_call(
        functools.partial(_attn_step_append_kernel, past=past),
        grid=(bsz,),
        in_specs=[col, col, col, _small(lw_cache), _small(lw_new), win, win, tail, tail],
        out_specs=[col, tail, tail],
        out_shape=out_shape,
        scratch_shapes=scratch,
        input_output_aliases={7: 1, 8: 2},
        compiler_params=_cparams("arbitrary"),
        name="attn_step_append",
    )(q_col, kn_col, vn_col, lw_cache, lw_new, cache_k, cache_v, *rolled)


def _group_rms(y, w):
    half = SSM_D_INNER // SSM_GROUPS
    return [_rms(y[:, g * half:(g + 1) * half]) * w[:, g * half:(g + 1) * half] for g in range(SSM_GROUPS)]


def _ssd_kernel(z_ref, xs_ref, bc_ref, dt_ref, dtb_ref, alog_ref, d_ref, nw_ref, h0_ref, y_ref, h_ref, ys_ref):
    c = pl.program_id(1)
    ch = SSM_CHUNK

    @pl.when(c == 0)
    def _():
        h_ref[...] = h0_ref[...]

    xs = xs_ref[0].astype(F32)
    bc = bc_ref[0].astype(F32)

    dt = jax.nn.softplus(dt_ref[0] + dtb_ref[...])
    da = dt * (-jnp.exp(alog_ref[...]))
    row = _iota((ch, ch), 0)
    colv = _iota((ch, ch), 1)
    tril = (row >= colv).astype(F32)
    cs = _dot(tril, da, precision=HIGHEST)
    cs_t = cs.T
    causal = row >= colv

    heads = range(SSM_HEADS)
    hpg = SSM_HEADS // SSM_GROUPS
    bm = [bc[:, g * SSM_STATE:(g + 1) * SSM_STATE] for g in range(SSM_GROUPS)]
    cm = [bc[:, (SSM_GROUPS + g) * SSM_STATE:(SSM_GROUPS + g + 1) * SSM_STATE] for g in range(SSM_GROUPS)]
    cb = [_dot_nt(cm[g], bm[g]) for g in range(SSM_GROUPS)]
    dt_t = dt.T
    xs_t = xs.T
    w_t = dt_t * jnp.exp(cs_t[:, ch - 1:ch] - cs_t)
    e_last = jnp.exp(cs[ch - 1:ch, :])
    lane_lo = _iota((ch, LANES), 1) < SSM_HEAD_DIM
    row_lo = _iota((LANES, ch), 0) < SSM_HEAD_DIM
    pairs = range(SSM_HEADS // 2)
    csb = [jnp.broadcast_to(cs[:, h:h + 1], (ch, ch)) for h in heads]
    mix = [cb[h // hpg] * jnp.exp(jnp.where(causal, csb[h] - cs_t[h:h + 1, :], NEG)) * dt_t[h:h + 1, :]
           for h in heads]
    x_pair = [xs[:, j * LANES:(j + 1) * LANES] for j in pairs]
    y_intra = [jnp.where(lane_lo, _dot(mix[2 * j], x_pair[j]), _dot(mix[2 * j + 1], x_pair[j])) for j in pairs]
    h_pair = [h_ref[0, 2 * j:2 * j + 2].reshape(2 * SSM_HEAD_DIM, SSM_STATE) for j in pairs]
    y_inter = [_dot_nt(cm[2 * j // hpg], h_pair[j]) * jnp.exp(jnp.where(lane_lo, csb[2 * j], csb[2 * j + 1]))
               for j in pairs]
    xw_t = [xs_t[j * LANES:(j + 1) * LANES, :] * jnp.where(row_lo, w_t[2 * j:2 * j + 1, :], w_t[2 * j + 1:2 * j + 2, :])
            for j in pairs]
    st = [_dot(xw_t[j], bm[2 * j // hpg]) for j in pairs]
    for j in pairs:
        cols = slice(j * LANES, (j + 1) * LANES)
        ys_ref[:, cols] = y_intra[j] + y_inter[j] + d_ref[:, cols] * x_pair[j]
        decay = jnp.where(row_lo, e_last[:, 2 * j:2 * j + 1], e_last[:, 2 * j + 1:2 * j + 2])
        h_ref[0, 2 * j:2 * j + 2] = (h_pair[j] * decay + st[j]).reshape(2, SSM_HEAD_DIM, SSM_STATE)

    y = ys_ref[...] * jax.nn.silu(z_ref[0].astype(F32))
    half = SSM_D_INNER // SSM_GROUPS
    for g, yg in enumerate(_group_rms(y, nw_ref[...])):
        y_ref[0, :, g * half:(g + 1) * half] = yg.astype(y_ref.dtype)


def _small(a):
    return pl.BlockSpec(a.shape, lambda *_: (0,) * a.ndim)


def ssd_prompt(main, tail, h0, dt_bias, a_log, d_skip, norm_w, *, bsz, seq):
    nc = seq // SSM_CHUNK
    ch = SSM_CHUNK
    small = [_pad_tail(dt_bias.reshape(1, -1)), _pad_tail(a_log.reshape(1, -1)),
             jnp.repeat(d_skip, SSM_HEAD_DIM).reshape(1, -1), norm_w.reshape(1, -1)]
    return pl.pallas_call(
        _ssd_kernel,
        grid=(bsz, nc),
        in_specs=[
            pl.BlockSpec((1, ch, SSM_D_INNER), lambda b, c: (b, c, 0)),
            pl.BlockSpec((1, ch, SSM_D_INNER), lambda b, c: (b, c, 1)),
            pl.BlockSpec((1, ch, SSM_BC), lambda b, c: (b, c, HYB_BC0 // SSM_BC)),
            pl.BlockSpec((1, ch, LANES), lambda b, c: (b, c, 0)),
        ] + [_small(a) for a in small] + [
            pl.BlockSpec((1, SSM_HEADS, SSM_HEAD_DIM, SSM_STATE), lambda b, c: (b, 0, 0, 0)),
        ],
        out_specs=[
            pl.BlockSpec((1, ch, SSM_D_INNER), lambda b, c: (b, c, 0)),
            pl.BlockSpec((1, SSM_HEADS, SSM_HEAD_DIM, SSM_STATE), lambda b, c: (b, 0, 0, 0)),
        ],
        out_shape=[jax.ShapeDtypeStruct((bsz, seq, SSM_D_INNER), BF16),
                   jax.ShapeDtypeStruct((bsz, SSM_HEADS, SSM_HEAD_DIM, SSM_STATE), F32)],
        scratch_shapes=[pltpu.VMEM((ch, SSM_D_INNER), F32)],
        compiler_params=_cparams("parallel", "arbitrary"),
        name="ssd_prompt",
    )(main, main, main, tail, *small, h0)


def _row_to_col(row, eye):
    return jnp.sum(jnp.where(eye, row, 0.0), axis=1, keepdims=True)


def _col_to_row(col, eye):
    return jnp.sum(jnp.where(eye, col, 0.0), axis=0, keepdims=True)


def _conv_step(c0_ref, w_ref, x_row, c0, c1):
    acc = w_ref[SSM_CONV - 1:SSM_CONV, c0:c1] * x_row
    for i in range(SSM_CONV - 1):
        acc = acc + w_ref[i:i + 1, c0:c1] * c0_ref[0, i:i + 1, c0:c1]
    return acc


def _ssd_step_kernel(z_ref, xs_ref, bc_ref, dt_ref, c0_ref, cw_ref, cb_ref, dtb_ref, alog_ref, d_ref, nw_ref,
                     h0_ref, y_ref, co_ref, h_ref, ys_ref):
    xs_raw = xs_ref[0]
    bc_raw = bc_ref[0]
    xs = jax.nn.silu(_conv_step(c0_ref, cw_ref, xs_raw, 0, SSM_D_INNER) + cb_ref[:, 0:SSM_D_INNER])
    bc = jax.nn.silu(_conv_step(c0_ref, cw_ref, bc_raw, SSM_D_INNER, SSM_XBC) + cb_ref[:, SSM_D_INNER:SSM_XBC])
    co_ref[0, 0:SSM_CONV - 2, :] = c0_ref[0, 1:SSM_CONV - 1, :]
    co_ref[0, SSM_CONV - 2:SSM_CONV - 1, 0:SSM_D_INNER] = xs_raw
    co_ref[0, SSM_CONV - 2:SSM_CONV - 1, SSM_D_INNER:SSM_XBC] = bc_raw

    dt = jax.nn.softplus(dt_ref[0] + dtb_ref[...])
    dec = jnp.exp(dt * (-jnp.exp(alog_ref[...])))
    dskip = d_ref[...]
    eye = _iota((LANES, LANES), 0) == _iota((LANES, LANES), 1)
    upper = _iota((LANES, 1), 0) >= SSM_HEAD_DIM
    pairs = range(SSM_HEADS // 2)
    grp = [2 * j // (SSM_HEADS // SSM_GROUPS) for j in pairs]
    bm = [bc[:, g * SSM_STATE:(g + 1) * SSM_STATE] for g in grp]
    cm = [bc[:, (SSM_GROUPS + g) * SSM_STATE:(SSM_GROUPS + g + 1) * SSM_STATE] for g in grp]
    pick = lambda v, j: jnp.where(upper, v[:, 2 * j + 1:2 * j + 2], v[:, 2 * j:2 * j + 1])
    x_col = [_row_to_col(xs[:, j * LANES:(j + 1) * LANES], eye) for j in pairs]
    hp = [h0_ref[0, 2 * j:2 * j + 2].reshape(2 * SSM_HEAD_DIM, SSM_STATE) for j in pairs]
    hn = [hp[j] * pick(dec, j) + (x_col[j] * pick(dt, j)) * bm[j] for j in pairs]
    y_col = [jnp.sum(hn[j] * cm[j], axis=1, keepdims=True) + pick(dskip, j) * x_col[j] for j in pairs]
    y_row = [_col_to_row(y_col[j], eye) for j in pairs]
    for j in pairs:
        h_ref[0, 2 * j:2 * j + 2] = hn[j].reshape(2, SSM_HEAD_DIM, SSM_STATE)
        ys_ref[:, j * LANES:(j + 1) * LANES] = y_row[j]

    y = ys_ref[...] * jax.nn.silu(z_ref[0])
    half = SSM_D_INNER // SSM_GROUPS
    for g, yg in enumerate(_group_rms(y, nw_ref[...])):
        y_ref[0, :, g * half:(g + 1) * half] = yg


def ssd_step(main, tail, conv0, h0, conv_w, conv_b, dt_bias, a_log, d_skip, norm_w):
    bsz = main.shape[0]
    small = [conv_w, conv_b.reshape(1, -1), _pad_tail(dt_bias.reshape(1, -1)), _pad_tail(a_log.reshape(1, -1)),
             _pad_tail(d_skip.reshape(1, -1)), norm_w.reshape(1, -1)]
    hspec = pl.BlockSpec((1, SSM_HEADS, SSM_HEAD_DIM, SSM_STATE), lambda b: (b, 0, 0, 0))
    cspec = pl.BlockSpec((1, SSM_CONV - 1, SSM_XBC), lambda b: (b, 0, 0))
    return pl.pallas_call(
        _ssd_step_kernel,
        grid=(bsz,),
        in_specs=[
            pl.BlockSpec((1, 1, SSM_D_INNER), lambda b: (b, 0, 0)),
            pl.BlockSpec((1, 1, SSM_D_INNER), lambda b: (b, 0, 1)),
            pl.BlockSpec((1, 1, SSM_BC), lambda b: (b, 0, HYB_BC0 // SSM_BC)),
            pl.BlockSpec((1, 1, LANES), lambda b: (b, 0, 0)),
            cspec,
        ] + [_small(a) for a in small] + [hspec],
        out_specs=[pl.BlockSpec((1, 1, SSM_D_INNER), lambda b: (b, 0, 0)), cspec, hspec],
        out_shape=[jax.ShapeDtypeStruct((bsz, 1, SSM_D_INNER), F32),
                   jax.ShapeDtypeStruct(conv0.shape, F32),
                   jax.ShapeDtypeStruct(h0.shape, F32)],
        scratch_shapes=[pltpu.VMEM((1, SSM_D_INNER), F32)],
        compiler_params=_cparams("parallel"),
        name="ssd_step",
    )(main, main, main, tail, conv0, *small, h0)


def _l2norm(x):
    return x * lax.rsqrt(jnp.sum(x * x, axis=-1, keepdims=True) + EPS)


def _unit_lower_inverse(ns, eye):
    size = ns[0].shape[0]
    ps = [eye - n for n in ns]
    ms = [_dot(n, n) for n in ns]
    power = 2
    while 2 * power < size:
        pms = [_dot(jnp.concatenate([p, m], axis=0), m) for p, m in zip(ps, ms)]
        ps = [p + pm[:size] for p, pm in zip(ps, pms)]
        ms = [pm[size:] for pm in pms]
        power *= 2
    return [p + _dot(p, m) for p, m in zip(ps, ms)]


def _gdn_gates(ba, dtb_ref, alog_ref):
    beta = jax.nn.sigmoid(ba)
    g = -jnp.exp(alog_ref[...]) * jax.nn.softplus(ba + dtb_ref[...])
    return beta, g


def _gdn_kernel(q_ref, k_ref, v_ref, z_ref, ba_ref, dtb_ref, alog_ref, nw_ref, s0_ref, o_ref, s_ref):
    c = pl.program_id(1)
    ch = GDN_CHUNK
    nh = GDN_V_HEADS

    @pl.when(c == 0)
    def _():
        s_ref[...] = s0_ref[...]

    row = _iota((ch, ch), 0)
    colv = _iota((ch, ch), 1)
    incl = row >= colv
    strict = row > colv
    eye = (row == colv).astype(F32)
    rep = nh // GDN_QK_HEADS
    heads = range(nh)
    qk_heads = range(GDN_QK_HEADS)
    subs = range(GDN_STEP_CHUNKS)
    rows = {sc: slice(sc * ch, (sc + 1) * ch) for sc in subs}
    sh = [(sc, h) for sc in subs for h in heads]
    sj = [(sc, j) for sc in subs for j in qk_heads]

    gates = {sc: _gdn_gates(ba_ref[0, rows[sc], :], dtb_ref, alog_ref) for sc in subs}
    gcum = {sc: _dot(incl.astype(F32), gates[sc][1], precision=HIGHEST) for sc in subs}
    gcum_t = {sc: jnp.concatenate([gcum[sc], jnp.zeros((LANES - ch, LANES), F32)], axis=0).T for sc in subs}
    qn = {(sc, j): _l2norm(q_ref[0, rows[sc], j * GDN_DK:(j + 1) * GDN_DK].astype(F32)) * (GDN_DK ** -0.5)
          for sc, j in sj}
    kn = {(sc, j): _l2norm(k_ref[0, rows[sc], j * GDN_DK:(j + 1) * GDN_DK].astype(F32)) for sc, j in sj}
    kk = {k: _dot_nt(kn[k], kn[k]) for k in sj}
    qk = {k: _dot_nt(qn[k], kn[k]) for k in sj}
    gc_col = {(sc, h): gcum[sc][:, nh + h:nh + h + 1] for sc, h in sh}
    gc_last = {(sc, h): gcum[sc][ch - 1:ch, nh + h:nh + h + 1] for sc, h in sh}
    beta_col = {(sc, h): gates[sc][0][:, h:h + 1] for sc, h in sh}
    dec = {(sc, h): jnp.exp(jnp.where(incl, gc_col[sc, h] - gcum_t[sc][nh + h:nh + h + 1, 0:ch], NEG))
           for sc, h in sh}
    t_inv = dict(zip(sh, _unit_lower_inverse(
        [jnp.where(strict, kk[sc, h // rep] * dec[sc, h], 0.0) * beta_col[sc, h] for sc, h in sh], eye)))
    eg = {k: jnp.exp(gc_col[k]) for k in sh}
    lhs = {(sc, h): jnp.concatenate([kn[sc, h // rep] * (beta_col[sc, h] * eg[sc, h]), qn[sc, h // rep] * eg[sc, h]],
                                    axis=0) for sc, h in sh}
    vb = {(sc, h): v_ref[0, rows[sc], h * GDN_DV:(h + 1) * GDN_DV].astype(F32) * beta_col[sc, h] for sc, h in sh}
    attn = {(sc, h): qk[sc, h // rep] * dec[sc, h] for sc, h in sh}
    kdec = {(sc, h): kn[sc, h // rep] * jnp.exp(gc_last[sc, h] - gc_col[sc, h]) for sc, h in sh}

    state = {h: s_ref[0, h] for h in heads}
    for sc in subs:
        both = {h: _dot(lhs[sc, h], state[h]) for h in heads}
        u = {h: _dot(t_inv[sc, h], vb[sc, h] - both[h][:ch]) for h in heads}
        o = {h: both[h][ch:] + _dot(attn[sc, h], u[h]) for h in heads}
        state = {h: state[h] * jnp.exp(gc_last[sc, h]) + _dot_tn(kdec[sc, h], u[h]) for h in heads}
        for h in heads:
            z_h = z_ref[0, rows[sc], h * GDN_DV:(h + 1) * GDN_DV].astype(F32)
            o_ref[0, rows[sc], h * GDN_DV:(h + 1) * GDN_DV] = (
                _rms(o[h]) * nw_ref[...] * jax.nn.silu(z_h)).astype(o_ref.dtype)
    for h in heads:
        s_ref[0, h] = state[h]


def _gdn_gate_params(dt_bias, a_log):
    nh = GDN_V_HEADS
    dtb = jnp.zeros((1, LANES), F32).at[0, nh:2 * nh].set(dt_bias)
    alog = jnp.zeros((1, LANES), F32).at[0, nh:2 * nh].set(a_log)
    return dtb, alog


def gdn_prompt(main, tail, s0, dt_bias, a_log, norm_w, *, bsz, seq):
    ch = GDN_CHUNK * GDN_STEP_CHUNKS
    assert seq % ch == 0
    nc = seq // ch
    dtb, alog = _gdn_gate_params(dt_bias, a_log)
    small = [dtb, alog, norm_w.reshape(1, -1)]
    sspec = pl.BlockSpec((1, GDN_V_HEADS, GDN_DK, GDN_DV), lambda b, c: (b, 0, 0, 0))
    return pl.pallas_call(
        _gdn_kernel,
        grid=(bsz, nc),
        in_specs=[
            pl.BlockSpec((1, ch, GDN_QK_W), lambda b, c: (b, c, 0)),
            pl.BlockSpec((1, ch, GDN_QK_W), lambda b, c: (b, c, 1)),
            pl.BlockSpec((1, ch, GDN_VW), lambda b, c: (b, c, 1)),
            pl.BlockSpec((1, ch, GDN_VW), lambda b, c: (b, c, 2)),
            pl.BlockSpec((1, ch, LANES), lambda b, c: (b, c, 0)),
        ] + [_small(a) for a in small] + [sspec],
        out_specs=[pl.BlockSpec((1, ch, GDN_VW), lambda b, c: (b, c, 0)), sspec],
        out_shape=[jax.ShapeDtypeStruct((bsz, seq, GDN_VW), BF16),
                   jax.ShapeDtypeStruct((bsz, GDN_V_HEADS, GDN_DK, GDN_DV), F32)],
        compiler_params=_cparams("parallel", "arbitrary"),
        name="gdn_prompt",
    )(main, main, main, main, tail, *small, s0)


def _gdn_conv_step(c0_ref, w_ref, x_row, c0, c1):
    acc = w_ref[GDN_CONV - 1:GDN_CONV, c0:c1] * x_row
    for i in range(GDN_CONV - 1):
        acc = acc + w_ref[i:i + 1, c0:c1] * c0_ref[0, i:i + 1, c0:c1]
    return acc


def _gdn_step_kernel(q_ref, k_ref, v_ref, z_ref, ba_ref, c0_ref, cw_ref, dtb_ref, alog_ref, nw_ref, s0_ref,
                     o_ref, co_ref, s_ref):
    nh = GDN_V_HEADS
    q_raw, k_raw, v_raw = q_ref[0], k_ref[0], v_ref[0]
    q = jax.nn.silu(_gdn_conv_step(c0_ref, cw_ref, q_raw, 0, GDN_QK_W))
    k = jax.nn.silu(_gdn_conv_step(c0_ref, cw_ref, k_raw, GDN_QK_W, 2 * GDN_QK_W))
    v = jax.nn.silu(_gdn_conv_step(c0_ref, cw_ref, v_raw, 2 * GDN_QK_W, GDN_QKV))
    co_ref[0, 0:GDN_CONV - 2, :] = c0_ref[0, 1:GDN_CONV - 1, :]
    co_ref[0, GDN_CONV - 2:GDN_CONV - 1, 0:GDN_QK_W] = q_raw
    co_ref[0, GDN_CONV - 2:GDN_CONV - 1, GDN_QK_W:2 * GDN_QK_W] = k_raw
    co_ref[0, GDN_CONV - 2:GDN_CONV - 1, 2 * GDN_QK_W:GDN_QKV] = v_raw

    beta, g = _gdn_gates(ba_ref[0], dtb_ref, alog_ref)
    eg_all = jnp.exp(g)
    eye = _iota((LANES, LANES), 0) == _iota((LANES, LANES), 1)
    rep = nh // GDN_QK_HEADS
    heads = range(nh)
    qn = [_l2norm(q[:, j * GDN_DK:(j + 1) * GDN_DK]) * (GDN_DK ** -0.5) for j in range(GDN_QK_HEADS)]
    kn = [_l2norm(k[:, j * GDN_DK:(j + 1) * GDN_DK]) for j in range(GDN_QK_HEADS)]
    qk = [jnp.sum(a * b, axis=-1, keepdims=True) for a, b in zip(qn, kn)]
    q_col = [_row_to_col(a, eye) for a in qn]
    k_col = [_row_to_col(a, eye) for a in kn]
    b_h = [beta[:, h:h + 1] for h in heads]
    eg = [eg_all[:, nh + h:nh + h + 1] for h in heads]
    s_prev = [s0_ref[0, h] for h in heads]
    ks = [jnp.sum(s_prev[h] * k_col[h // rep], axis=0, keepdims=True) for h in heads]
    qs = [jnp.sum(s_prev[h] * q_col[h // rep], axis=0, keepdims=True) for h in heads]
    u = [v[:, h * GDN_DV:(h + 1) * GDN_DV] * b_h[h] - (b_h[h] * eg[h]) * ks[h] for h in heads]
    o = [eg[h] * qs[h] + qk[h // rep] * u[h] for h in heads]
    for h in heads:
        s_ref[0, h] = s_prev[h] * eg[h] + k_col[h // rep] * u[h]
        z_h = z_ref[0, :, h * GDN_DV:(h + 1) * GDN_DV]
        o_ref[0, :, h * GDN_DV:(h + 1) * GDN_DV] = _rms(o[h]) * nw_ref[...] * jax.nn.silu(z_h)


def gdn_step(main, tail, conv0, s0, conv_w, dt_bias, a_log, norm_w):
    bsz = main.shape[0]
    dtb, alog = _gdn_gate_params(dt_bias, a_log)
    small = [conv_w, dtb, alog, norm_w.reshape(1, -1)]
    sspec = pl.BlockSpec((1, GDN_V_HEADS, GDN_DK, GDN_DV), lambda b: (b, 0, 0, 0))
    cspec = pl.BlockSpec((1, GDN_CONV - 1, GDN_QKV), lambda b: (b, 0, 0))
    return pl.pallas_call(
        _gdn_step_kernel,
        grid=(bsz,),
        in_specs=[
            pl.BlockSpec((1, 1, GDN_QK_W), lambda b: (b, 0, 0)),
            pl.BlockSpec((1, 1, GDN_QK_W), lambda b: (b, 0, 1)),
            pl.BlockSpec((1, 1, GDN_VW), lambda b: (b, 0, 1)),
            pl.BlockSpec((1, 1, GDN_VW), lambda b: (b, 0, 2)),
            pl.BlockSpec((1, 1, LANES), lambda b: (b, 0, 0)),
            cspec,
        ] + [_small(a) for a in small] + [sspec],
        out_specs=[pl.BlockSpec((1, 1, GDN_VW), lambda b: (b, 0, 0)), cspec, sspec],
        out_shape=[jax.ShapeDtypeStruct((bsz, 1, GDN_VW), F32),
                   jax.ShapeDtypeStruct(conv0.shape, F32),
                   jax.ShapeDtypeStruct(s0.shape, F32)],
        compiler_params=_cparams("parallel"),
        name="gdn_step",
    )(main, main, main, main, tail, conv0, *small, s0)


def _pad_tail(w):
    return jnp.pad(w, ((0, 0), (0, LANES - w.shape[1])))


def _prep_hyb_in(w):
    a = A_WIDTH
    q, k, v = w[:, 0:a], w[:, a:2 * a], w[:, 2 * a:3 * a]
    z = w[:, 3 * a:3 * a + SSM_D_INNER]
    x0 = 3 * a + SSM_D_INNER
    xs = w[:, x0:x0 + SSM_D_INNER]
    bc = w[:, x0 + SSM_D_INNER:x0 + SSM_XBC]
    dt = w[:, x0 + SSM_XBC:]
    return jnp.concatenate([z, xs, q, k, v, bc], axis=1).astype(BF16), _pad_tail(dt).astype(BF16)


def _prep_gdn_in(w):
    return w[:, :GDN_MAIN].astype(BF16), _pad_tail(w[:, GDN_MAIN:]).astype(BF16)


HYB_CONV_COLS = ((SSM_D_INNER, 2 * SSM_D_INNER), (HYB_BC0, HYB_MAIN))


def _hyb_cols(a):
    out = jnp.zeros(a.shape[:-1] + (HYB_MAIN,), F32)
    (x0, x1), (b0, b1) = HYB_CONV_COLS
    return out.at[..., x0:x1].set(a[..., :SSM_D_INNER]).at[..., b0:b1].set(a[..., SSM_D_INNER:])


def _window_to_lanes(c):
    n, b, past, h, dh = c.shape
    return jnp.transpose(c, (0, 1, 3, 4, 2)).reshape(n, b, h * dh, past)


def _window_from_lanes(c):
    n, b, _, past = c.shape
    return jnp.transpose(c.reshape(n, b, A_HEADS, A_HEAD_DIM, past), (0, 1, 4, 2, 3))


def _row_tile(m, cap):
    return m if m <= cap else cap


def kernel(x_prompt, x_sample, cache_attn_k, cache_attn_v, state_ssm_conv, state_ssm, state_gdn_conv, state_gdn, rel_bias, norm_mix_pre, norm_mix_post, norm_ffn_pre, norm_ffn_post, w_hyb_in, ssm_conv_w, ssm_conv_b, ssm_dt_bias, ssm_a_log, ssm_d, ssm_norm_w, w_hyb_out, w_gdn_in, gdn_conv_w, gdn_dt_bias, gdn_a_log, gdn_norm_w, w_gdn_out, w_ffn_gate, w_ffn_up, w_ffn_down):
    depth = norm_mix_pre.shape[0]
    d_model = x_prompt.shape[-1]
    n_hyb, n_gdn = w_hyb_in.shape[0], w_gdn_in.shape[0]

    hyb_in = [_prep_hyb_in(w_hyb_in[i]) for i in range(n_hyb)]
    hyb_out = [(w_hyb_out[i, :A_WIDTH].astype(BF16), w_hyb_out[i, A_WIDTH:].astype(BF16)) for i in range(n_hyb)]
    gdn_in = [_prep_gdn_in(w_gdn_in[i]) for i in range(n_gdn)]
    gdn_out = [w_gdn_out[i].astype(BF16) for i in range(n_gdn)]
    ffn_w = [(w_ffn_gate[l].astype(BF16), w_ffn_up[l].astype(BF16), w_ffn_down[l].astype(BF16))
             for l in range(depth)]
    bias_tiles = _attn_bias_rows(rel_bias)

    def trunk(x3, k_pre, v_pre, sconv, sssm, gconv, gstate):
        bsz, seq, _ = x3.shape
        m = bsz * seq
        step = seq == 1
        tm_big = _row_tile(m, 2048)
        tm = _row_tile(m, 512)
        x = x3.reshape(m, d_model)
        nk, nv, nsc, nss, ngc, ngs = [], [], [], [], [], []
        rolled = None
        for l in range(depth):
            i = l // 2
            if l % 2 == 0:
                w_main, w_tail = hyb_in[i]
                ssm_args = (ssm_dt_bias[i], ssm_a_log[i], ssm_d[i], ssm_norm_w[i])
                if step:
                    main, tail = inproj(x, norm_mix_pre[l], w_main, w_tail, tm=tm_big, tn=512)
                    main3 = main.reshape(bsz, seq, HYB_MAIN)
                    tail3 = tail.reshape(bsz, seq, LANES)
                    col = lambda c0: main[:, c0:c0 + A_WIDTH].reshape(bsz, A_WIDTH, 1)
                    o_attn, *rolled = attention_step(
                        col(HYB_Q0), col(HYB_K0), col(HYB_V0), k_pre, v_pre, rolled,
                        _attn_logw(rel_bias, k_pre.shape[-1]), layer=i)
                    y, c_new, s_new = ssd_step(main3, tail3, sconv[i], sssm[i], ssm_conv_w[i], ssm_conv_b[i],
                                               *ssm_args)
                else:
                    main, tail, hist = inproj_conv(
                        x, norm_mix_pre[l], w_main, w_tail, _hyb_cols(ssm_conv_w[i]),
                        _hyb_cols(ssm_conv_b[i][None]), _hyb_cols(sconv[i]), seq=seq, conv_cols=HYB_CONV_COLS,
                        tm=tm_big, tn=512)
                    main3 = main.reshape(bsz, seq, HYB_MAIN)
                    tail3 = tail.reshape(bsz, seq, LANES)
                    c_new = jnp.concatenate([hist[..., a:b] for a, b in HYB_CONV_COLS], axis=-1)
                    o_attn = attention_prompt(main3, bias_tiles, bsz=bsz, seq=seq)
                    keep = min(A_PATTERNS[-1][0], seq)
                    k_new = main3[:, seq - keep:, HYB_K0:HYB_K0 + A_WIDTH].astype(F32)
                    v_new = main3[:, seq - keep:, HYB_V0:HYB_V0 + A_WIDTH].astype(F32)
                    y, s_new = ssd_prompt(main3, tail3, sssm[i], *ssm_args, bsz=bsz, seq=seq)
                    nk.append(k_new.reshape(bsz, -1, A_HEADS, A_HEAD_DIM))
                    nv.append(v_new.reshape(bsz, -1, A_HEADS, A_HEAD_DIM))
                nsc.append(c_new)
                nss.append(s_new)
                acts, w_outs = [o_attn.reshape(m, A_WIDTH), y.reshape(m, SSM_D_INNER)], list(hyb_out[i])
            else:
                w_main, w_tail = gdn_in[i]
                gdn_args = (gdn_dt_bias[i], gdn_a_log[i], gdn_norm_w[i])
                if step:
                    main, tail = inproj(x, norm_mix_pre[l], w_main, w_tail, tm=tm_big, tn=512)
                    o, c_new, s_new = gdn_step(main.reshape(bsz, seq, GDN_MAIN), tail.reshape(bsz, seq, LANES),
                                               gconv[i], gstate[i], gdn_conv_w[i], *gdn_args)
                else:
                    pad = lambda a: jnp.pad(a, [(0, 0)] * (a.ndim - 1) + [(0, GDN_MAIN - GDN_QKV)])
                    main, tail, hist = inproj_conv(
                        x, norm_mix_pre[l], w_main, w_tail, pad(gdn_conv_w[i]), jnp.zeros((1, GDN_MAIN), F32),
                        pad(gconv[i]), seq=seq, conv_cols=((0, GDN_QKV),), tm=tm_big, tn=512)
                    c_new = hist[..., :GDN_QKV]
                    o, s_new = gdn_prompt(main.reshape(bsz, seq, GDN_MAIN), tail.reshape(bsz, seq, LANES),
                                          gstate[i], *gdn_args, bsz=bsz, seq=seq)
                ngc.append(c_new)
                ngs.append(s_new)
                acts, w_outs = [o.reshape(m, GDN_VW)], [gdn_out[i]]
            wg, wu, wd = ffn_w[l]
            x = mix_ffn(acts, w_outs, x, norm_mix_post[l], norm_ffn_pre[l], wg, wu, wd, norm_ffn_post[l], tm=tm)
        k_out, v_out = [_window_from_lanes(r) for r in rolled] if step else (jnp.stack(nk), jnp.stack(nv))
        return (x.reshape(bsz, seq, d_model), k_out, v_out, jnp.stack(nsc), jnp.stack(nss),
                jnp.stack(ngc), jnp.stack(ngs))

    bsz = x_prompt.shape[0]
    dt_p = x_prompt.dtype
    p_sc0 = jnp.zeros((n_hyb, bsz, SSM_CONV - 1, SSM_XBC), dt_p)
    p_ss0 = jnp.zeros((n_hyb, bsz, SSM_HEADS, SSM_HEAD_DIM, SSM_STATE), F32)
    p_gc0 = jnp.zeros((n_gdn, bsz, GDN_CONV - 1, GDN_QKV), dt_p)
    p_gs0 = jnp.zeros((n_gdn, bsz, GDN_V_HEADS, GDN_DK, GDN_DV), F32)
    y_prompt, pk, pv, psc, pss, pgc, pgs = trunk(x_prompt, None, None, p_sc0, p_ss0, p_gc0, p_gs0)
    y_sample, sk, sv, ssc, sss, sgc, sgs = trunk(
        x_sample, _window_to_lanes(cache_attn_k), _window_to_lanes(cache_attn_v), state_ssm_conv, state_ssm,
        state_gdn_conv, state_gdn)
    return (y_prompt, y_sample, pk, pv, psc, pss, pgc, pgs, sk, sv, ssc, sss, sgc, sgs)
```

```python
import functools
import math

import numpy as np
import jax
import jax.numpy as jnp
from jax import lax
from jax.experimental import pallas as pl
from jax.experimental.pallas import tpu as pltpu

F32 = jnp.float32
BF16 = jnp.bfloat16
EPS = 1e-6
NEG = -1e30
HIGHEST = lax.Precision.HIGHEST

VMEM_LIMIT_BYTES = 56 * 1024 * 1024
LANES = 128

A_HEADS = 8
A_HEAD_DIM = 64
A_WIDTH = A_HEADS * A_HEAD_DIM
A_PATTERNS = ((128, 1), (512, 4), (2048, 16))
A_BAND = 128
ATTN_GROUP = 8
REL_BUCKETS = 32
REL_MAX_DIST = 2048

SSM_D_INNER = 1024
SSM_HEAD_DIM = 64
SSM_HEADS = SSM_D_INNER // SSM_HEAD_DIM
SSM_GROUPS = 2
SSM_STATE = 128
SSM_CONV = 4
SSM_CHUNK = 128
SSM_BC = 2 * SSM_GROUPS * SSM_STATE
SSM_XBC = SSM_D_INNER + SSM_BC

GDN_QK_HEADS = 8
GDN_V_HEADS = 16
GDN_DK = 128
GDN_DV = 128
GDN_CONV = 4
GDN_CHUNK = 64
GDN_STEP_CHUNKS = 1
GDN_QK_W = GDN_QK_HEADS * GDN_DK
GDN_VW = GDN_V_HEADS * GDN_DV
GDN_QKV = 2 * GDN_QK_W + GDN_VW

HYB_MAIN = 2 * SSM_D_INNER + 3 * A_WIDTH + SSM_BC
HYB_Q0 = 2 * SSM_D_INNER
HYB_K0 = HYB_Q0 + A_WIDTH
HYB_V0 = HYB_K0 + A_WIDTH
HYB_BC0 = HYB_V0 + A_WIDTH
GDN_MAIN = GDN_QKV + GDN_VW


def _cparams(*sem):
    return pltpu.CompilerParams(dimension_semantics=sem, vmem_limit_bytes=VMEM_LIMIT_BYTES)


def _rms(x):
    return x * lax.rsqrt(jnp.mean(x * x, axis=-1, keepdims=True) + EPS)


def _dot(a, b, **kw):
    return jnp.dot(a, b, preferred_element_type=F32, **kw)


def _dot_nt(a, b, **kw):
    return lax.dot_general(a, b, (((1,), (1,)), ((), ())), preferred_element_type=F32, **kw)


def _dot_tn(a, b, **kw):
    return lax.dot_general(a, b, (((0,), (0,)), ((), ())), preferred_element_type=F32, **kw)


def _iota(shape, dim):
    return lax.broadcasted_iota(jnp.int32, shape, dim)


def _inproj_kernel(x_ref, g_ref, w_ref, wt_ref, o_ref, t_ref, h_ref):
    @pl.when(pl.program_id(1) == 0)
    def _():
        hb = (_rms(x_ref[...]) * g_ref[...]).astype(BF16)
        h_ref[...] = hb
        t_ref[...] = _dot(hb, wt_ref[...])

    o_ref[...] = _dot(h_ref[...], w_ref[...])


def inproj(x, g, w_main, w_tail, *, tm, tn):
    m, d = x.shape
    n = w_main.shape[1]
    return pl.pallas_call(
        _inproj_kernel,
        grid=(m // tm, n // tn),
        in_specs=[
            pl.BlockSpec((tm, d), lambda i, j: (i, 0)),
            pl.BlockSpec((1, d), lambda i, j: (0, 0)),
            pl.BlockSpec((d, tn), lambda i, j: (0, j)),
            pl.BlockSpec((d, LANES), lambda i, j: (0, 0)),
        ],
        out_specs=[
            pl.BlockSpec((tm, tn), lambda i, j: (i, j)),
            pl.BlockSpec((tm, LANES), lambda i, j: (i, 0)),
        ],
        out_shape=[jax.ShapeDtypeStruct((m, n), F32), jax.ShapeDtypeStruct((m, LANES), F32)],
        scratch_shapes=[pltpu.VMEM((tm, d), BF16)],
        compiler_params=_cparams("parallel", "arbitrary"),
        name="inproj",
    )(x, g.reshape(1, d), w_main, w_tail)


CONV_TAPS = 4
CONV_ROWS = 64
PROJ_K_SLICES = 4
CONV_BASE = 8


def _inproj_conv_kernel(x_ref, g_ref, w_ref, wt_ref, cw_ref, cb_ref, c0_ref, o_ref, t_ref, so_ref,
                        h_ref, xp0_ref, xp1_ref, carry_ref, *, n_col, conv_tiles, tiles_per_seq):
    s = pl.program_id(0)
    n_tiles = pl.num_programs(0) - 1
    tm = o_ref.shape[0]
    lo = CONV_BASE - (CONV_TAPS - 1)
    cur = jnp.minimum(s, n_tiles - 1)
    prv = jnp.maximum(s - 1, 0)
    pi, pj = prv // n_col, prv % n_col
    prv_conv = functools.reduce(jnp.logical_or, [jnp.logical_and(pj >= a, pj < b) for a, b in conv_tiles])
    first = pi % tiles_per_seq == 0
    both = lambda a, b: jnp.logical_and(a, b)

    @pl.when(both(cur % n_col == 0, s < n_tiles))
    def _():
        hb = (_rms(x_ref[...]) * g_ref[...]).astype(BF16)
        h_ref[...] = hb
        t_ref[...] = _dot(hb, wt_ref[...])

    for parity, (cur_ref, prv_ref) in enumerate(((xp0_ref, xp1_ref), (xp1_ref, xp0_ref))):
        here = s % 2 == parity

        def project(cur_ref=cur_ref):
            cur_ref[CONV_BASE:CONV_BASE + tm, :] = _dot(h_ref[...], w_ref[...])

        def raw_tail(prv_ref=prv_ref):
            tail = prv_ref[CONV_BASE + tm - (CONV_TAPS - 1):CONV_BASE + tm, :]
            so_ref[pj, pi] = tail
            return tail

        if parity == 0:
            @pl.when(s == 0)
            def _():
                project()

        @pl.when(both(here, both(s > 0, jnp.logical_not(prv_conv))))
        def _():
            project()
            raw_tail()
            o_ref[...] = prv_ref[CONV_BASE:CONV_BASE + tm, :].astype(o_ref.dtype)

        @pl.when(both(here, both(both(s > 0, prv_conv), first)))
        def _():
            prv_ref[lo:CONV_BASE, :] = c0_ref[pj, pi // tiles_per_seq]

        @pl.when(both(here, both(both(s > 0, prv_conv), jnp.logical_not(first))))
        def _():
            prv_ref[lo:CONV_BASE, :] = carry_ref[pj]

        @pl.when(both(here, both(s > 0, prv_conv)))
        def _():
            def conv_chunk(r0):
                ext = prv_ref[r0:r0 + CONV_BASE + CONV_ROWS, :]
                conv = cb_ref[pj] + cw_ref[pj, CONV_TAPS - 1:CONV_TAPS, :] * ext[CONV_BASE:]
                for back in range(1, CONV_TAPS):
                    tap = CONV_TAPS - 1 - back
                    conv = conv + cw_ref[pj, tap:tap + 1, :] * pltpu.roll(ext, back, 0)[CONV_BASE:]
                o_ref[r0:r0 + CONV_ROWS, :] = jax.nn.silu(conv).astype(o_ref.dtype)

            d = h_ref.shape[1]
            n_chunks = tm // CONV_ROWS
            acc = None
            for kq in range(PROJ_K_SLICES):
                ks = slice(kq * d // PROJ_K_SLICES, (kq + 1) * d // PROJ_K_SLICES)
                part = _dot(h_ref[:, ks], w_ref[ks, :])
                acc = part if acc is None else acc + part
                for c in range(kq * n_chunks // PROJ_K_SLICES, (kq + 1) * n_chunks // PROJ_K_SLICES):
                    conv_chunk(c * CONV_ROWS)
            cur_ref[CONV_BASE:CONV_BASE + tm, :] = acc
            carry_ref[pj] = raw_tail()


def inproj_conv(x, g, w_main, w_tail, conv_w, conv_b, conv0, *, seq, conv_cols, tm, tn):
    m, d = x.shape
    n = w_main.shape[1]
    assert seq % tm == 0 and all(a % tn == 0 and b % tn == 0 for a, b in conv_cols)
    tiles_per_seq = seq // tm
    n_row, n_col = m // tm, n // tn
    n_tiles = n_row * n_col
    conv_tiles = tuple((a // tn, b // tn) for a, b in conv_cols)
    hist = CONV_TAPS - 1
    cur = lambda s: jnp.minimum(s, n_tiles - 1)
    prv = lambda s: jnp.maximum(s - 1, 0)
    by_tile = lambda a: jnp.moveaxis(a.reshape(a.shape[:-1] + (n_col, tn)), -2, 0)
    main, tail, hist_rows = pl.pallas_call(
        functools.partial(_inproj_conv_kernel, n_col=n_col, conv_tiles=conv_tiles, tiles_per_seq=tiles_per_seq),
        grid=(n_tiles + 1,),
        in_specs=[
            pl.BlockSpec((tm, d), lambda s: (cur(s) // n_col, 0)),
            pl.BlockSpec((1, d), lambda s: (0, 0)),
            pl.BlockSpec((None, d, tn), lambda s: (cur(s) % n_col, 0, 0)),
            pl.BlockSpec((d, LANES), lambda s: (0, 0)),
            pl.BlockSpec((n_col, CONV_TAPS, tn), lambda s: (0, 0, 0)),
            pl.BlockSpec((n_col, 1, tn), lambda s: (0, 0, 0)),
            pl.BlockSpec((n_col, m // seq, hist, tn), lambda s: (0, 0, 0, 0)),
        ],
        out_specs=[
            pl.BlockSpec((tm, tn), lambda s: (prv(s) // n_col, prv(s) % n_col)),
            pl.BlockSpec((tm, LANES), lambda s: (cur(s) // n_col, 0)),
            pl.BlockSpec((n_col, n_row, hist, tn), lambda s: (0, 0, 0, 0)),
        ],
        out_shape=[jax.ShapeDtypeStruct((m, n), BF16), jax.ShapeDtypeStruct((m, LANES), F32),
                   jax.ShapeDtypeStruct((n_col, n_row, hist, tn), F32)],
        scratch_shapes=[pltpu.VMEM((tm, d), BF16), pltpu.VMEM((CONV_BASE + tm, tn), F32),
                        pltpu.VMEM((CONV_BASE + tm, tn), F32), pltpu.VMEM((n_col, hist, tn), F32)],
        compiler_params=_cparams("arbitrary"),
        name="inproj_conv",
    )(x, g.reshape(1, d), by_tile(w_main), w_tail, by_tile(conv_w), by_tile(conv_b), by_tile(conv0))
    hist_rows = jnp.moveaxis(hist_rows, 0, -2).reshape(n_row, hist, n)
    return main, tail, hist_rows[tiles_per_seq - 1::tiles_per_seq]


def _mix_ffn_kernel(*refs, n_in):
    a_refs, w_refs = refs[:n_in], refs[n_in:2 * n_in]
    x_ref, gm_ref, g1_ref, wg_ref, wu_ref, wd_ref, g2_ref, o_ref = refs[2 * n_in:]
    mixed = None
    for a_ref, w_ref in zip(a_refs, w_refs):
        t = _dot(a_ref[...].astype(BF16), w_ref[...])
        mixed = t if mixed is None else mixed + t
    x = x_ref[...] + _rms(mixed) * gm_ref[...]
    h = (_rms(x) * g1_ref[...]).astype(BF16)
    a = jax.nn.silu(_dot(h, wg_ref[...])) * _dot(h, wu_ref[...])
    o_ref[...] = x + _rms(_dot(a.astype(BF16), wd_ref[...])) * g2_ref[...]


def mix_ffn(acts, w_outs, x, g_mix, g1, wg, wu, wd, g2, *, tm):
    m, d = x.shape
    n_in = len(acts)
    resident = lambda a: pl.BlockSpec(a.shape, lambda i: (0, 0), pipeline_mode=pl.Buffered(1))
    vec = pl.BlockSpec((1, d), lambda i: (0, 0))
    return pl.pallas_call(
        functools.partial(_mix_ffn_kernel, n_in=n_in),
        grid=(m // tm,),
        in_specs=[pl.BlockSpec((tm, a.shape[1]), lambda i: (i, 0)) for a in acts]
        + [resident(w) for w in w_outs]
        + [pl.BlockSpec((tm, d), lambda i: (i, 0)), vec, vec, resident(wg), resident(wu), resident(wd), vec],
        out_specs=pl.BlockSpec((tm, d), lambda i: (i, 0)),
        out_shape=jax.ShapeDtypeStruct((m, d), F32),
        compiler_params=_cparams("parallel"),
        name="mix_ffn",
    )(*acts, *w_outs, x, g_mix.reshape(1, d), g1.reshape(1, d), wg, wu, wd, g2.reshape(1, d))


def _rel_buckets(dist):
    max_exact = REL_BUCKETS // 2
    n = np.maximum(dist, 1).astype(np.float32)
    large = max_exact + (np.log(n / max_exact) / math.log(REL_MAX_DIST / max_exact)
                         * (REL_BUCKETS - max_exact)).astype(np.int32)
    large = np.minimum(large, REL_BUCKETS - 1)
    return np.where(dist < max_exact, dist, large).astype(np.int32)


def _attn_bias_rows(rel_bias):
    u = np.arange(2 * A_BAND)
    valid = u <= A_BAND
    rows = []
    for (_, d) in A_PATTERNS:
        b = rel_bias[_rel_buckets(np.where(valid, A_BAND - u, 0) * d)]
        rows.append(jnp.where(valid[:, None], b.astype(F32), NEG))
    tl = jnp.transpose(jnp.stack(rows), (2, 0, 1))
    tl = tl.reshape(A_HEADS // 2, 2, len(A_PATTERNS), 2 * A_BAND)
    tl = jnp.transpose(tl, (0, 2, 1, 3))[:, :, :, None, :]
    return jnp.broadcast_to(tl, tl.shape[:3] + (8, 2 * A_BAND))


def _attn_kernel(qin_ref, kin_ref, vin_ref, brow_ref, o_ref, *scratch, seq):
    n_pat = len(A_PATTERNS)
    m_refs, l_refs, acc_refs = scratch[0:n_pat], scratch[n_pat:2 * n_pat], scratch[2 * n_pat:3 * n_pat]
    q_ref, k_ref, v_ref, qc_ref, kc_ref, vc_ref, b_ref = scratch[3 * n_pat:]
    n_tiles = seq // A_BAND
    lane = _iota((A_BAND, LANES), 1)
    head0 = lane < A_HEAD_DIM
    q_ref[...] = qin_ref[...].astype(F32) * (A_HEAD_DIM ** -0.5)
    k_ref[...] = kin_ref[...].astype(F32)
    v_ref[...] = vin_ref[...].astype(F32)
    for p in range(len(A_PATTERNS)):
        for h in range(2):
            row = jnp.broadcast_to(brow_ref[0, p, h, 0:1, :], (A_BAND, 2 * A_BAND))
            b_ref[0, p, h] = pltpu.roll(row, 0, 1, stride=1, stride_axis=0)

    first_p = max(range(len(A_PATTERNS)), key=lambda p: A_PATTERNS[p][1])
    d_first = A_PATTERNS[first_p][1]
    class_len = seq // d_first

    def to_class_major(r, carry):
        dst = pl.ds(pl.multiple_of(r * class_len, class_len), class_len)
        for src_ref, dst_ref in ((q_ref, qc_ref), (k_ref, kc_ref), (v_ref, vc_ref)):
            dst_ref[dst, :] = src_ref[pl.ds(r, class_len, stride=d_first), :]
        return carry

    lax.fori_loop(0, d_first, to_class_major, 0)

    for p, (_, d) in enumerate(A_PATTERNS):
        tiles_per_class = n_tiles // d
        class_major = p == first_p

        def load_tile(idx, d=d, tiles_per_class=tiles_per_class, class_major=class_major):
            r = idx // tiles_per_class
            t = idx % tiles_per_class
            start = r + t * (d * A_BAND)
            has_prev = t > 0
            rows = pl.ds(start, A_BAND, stride=d) if d > 1 else pl.ds(pl.multiple_of(start, A_BAND), A_BAND)
            if class_major:
                base = r * class_len + t * A_BAND
                crow = pl.ds(pl.multiple_of(base, A_BAND), A_BAND)
                cprev = pl.ds(pl.multiple_of(jnp.where(has_prev, base - A_BAND, base), A_BAND), A_BAND)
                return dict(
                    rows=rows, has_prev=has_prev, q=qc_ref[crow, :],
                    k2=jnp.concatenate([kc_ref[cprev, :], kc_ref[crow, :]], axis=0).astype(BF16),
                    v2=jnp.concatenate([vc_ref[cprev, :], vc_ref[crow, :]], axis=0).astype(BF16))
            prev = jnp.where(has_prev, start - d * A_BAND, start)
            prows = pl.ds(prev, A_BAND, stride=d) if d > 1 else pl.ds(pl.multiple_of(prev, A_BAND), A_BAND)
            return dict(
                rows=rows, has_prev=has_prev, q=q_ref[rows, :],
                k2=jnp.concatenate([k_ref[prows, :], k_ref[rows, :]], axis=0).astype(BF16),
                v2=jnp.concatenate([v_ref[prows, :], v_ref[rows, :]], axis=0).astype(BF16))

        def tile_group(idx, carry, load_tile=load_tile, p=p):
            tiles = [load_tile(idx + i * (n_tiles // ATTN_GROUP)) for i in range(ATTN_GROUP)]
            chains = [(tile, h) for tile in tiles for h in range(2)]
            n_chain = range(len(chains))
            col = _iota((A_BAND, 2 * A_BAND), 1)
            wide = lambda x: jnp.broadcast_to(x, (A_BAND, LANES))
            qh = [jnp.where(head0 if h == 0 else jnp.logical_not(head0), tile["q"], 0.0).astype(BF16)
                  for tile, h in chains]
            s = [_dot_nt(qh[c], tile["k2"])
                 + jnp.where(jnp.logical_and(col < A_BAND, jnp.logical_not(tile["has_prev"])), NEG, b_ref[0, p, h])
                 for c, (tile, h) in enumerate(chains)]
            m_new = [wide(jnp.max(s[c], axis=-1, keepdims=True)) for c in n_chain]
            pr = [jnp.exp(s[c] - jnp.concatenate([m_new[c], m_new[c]], axis=1)) for c in n_chain]
            l_new = [wide(jnp.sum(pr[c], axis=-1, keepdims=True)) for c in n_chain]
            acc_new = [_dot(pr[c].astype(BF16), tile["v2"]) for c, (tile, h) in enumerate(chains)]
            for c, (tile, h) in enumerate(chains):
                if h == 1:
                    m_refs[p][tile["rows"], :] = jnp.where(head0, m_new[c - 1], m_new[c])
                    l_refs[p][tile["rows"], :] = jnp.where(head0, l_new[c - 1], l_new[c])
                    acc_refs[p][tile["rows"], :] = jnp.where(head0, acc_new[c - 1], acc_new[c])
            return carry

        lax.fori_loop(0, n_tiles // ATTN_GROUP, tile_group, 0)

    m_all = functools.reduce(jnp.maximum, [m_ref[...] for m_ref in m_refs])
    wts = [jnp.exp(m_ref[...] - m_all) for m_ref in m_refs]
    num = functools.reduce(jnp.add, [w * acc_ref[...] for w, acc_ref in zip(wts, acc_refs)])
    den = functools.reduce(jnp.add, [w * l_ref[...] for w, l_ref in zip(wts, l_refs)])
    o_ref[...] = (num / den).astype(o_ref.dtype)


def attention_prompt(proj, bias_tiles, *, bsz, seq):
    hp = A_HEADS // 2
    qb, kb, vb = HYB_Q0 // LANES, HYB_K0 // LANES, HYB_V0 // LANES
    return pl.pallas_call(
        functools.partial(_attn_kernel, seq=seq),
        grid=(bsz, hp),
        in_specs=[
            pl.BlockSpec((None, seq, LANES), lambda b, h: (b, 0, qb + h)),
            pl.BlockSpec((None, seq, LANES), lambda b, h: (b, 0, kb + h)),
            pl.BlockSpec((None, seq, LANES), lambda b, h: (b, 0, vb + h)),
            pl.BlockSpec((1,) + bias_tiles.shape[1:], lambda b, h: (h, 0, 0, 0, 0)),
        ],
        out_specs=pl.BlockSpec((None, seq, LANES), lambda b, h: (b, 0, h)),
        out_shape=jax.ShapeDtypeStruct((bsz, seq, A_WIDTH), BF16),
        scratch_shapes=[pltpu.VMEM((seq, LANES), F32)] * (3 * len(A_PATTERNS) + 6)
        + [pltpu.VMEM((1, len(A_PATTERNS), 2, A_BAND, 2 * A_BAND), F32)],
        compiler_params=_cparams("parallel", "parallel"),
        name="attn_prompt",
    )(proj, proj, proj, bias_tiles)


def _attn_logw(rel_bias, past):
    dist = np.arange(past + 1)
    count = np.zeros(past + 1, np.float64)
    for (w, d) in A_PATTERNS:
        count += ((dist % d == 0) & (dist <= w)).astype(np.float64)
    logc = np.where(count > 0, np.log(np.maximum(count, 1.0)), 0.0).astype(np.float32)
    lw = rel_bias[_rel_buckets(dist)].astype(F32).T + logc[None, :]
    return jnp.where((count > 0)[None, :], lw, NEG)


def _step_scores(xk_ref, q_col, kn_col, lw_ref, lw0_ref, s_ref):
    qs = q_col * (A_HEAD_DIM ** -0.5)
    s_new = []
    for h in range(A_HEADS):
        rows = slice(h * A_HEAD_DIM, (h + 1) * A_HEAD_DIM)
        s_ref[h:h + 1, :] = jnp.sum(xk_ref[0, 0, rows, :] * qs[rows], axis=0, keepdims=True)
        s_new.append(jnp.sum(kn_col[rows] * qs[rows], axis=0, keepdims=True))
    s = s_ref[...] + lw_ref[...]
    s_new = jnp.concatenate(s_new, axis=0) + lw0_ref[:, :1]
    m = jnp.maximum(jnp.max(s, axis=-1, keepdims=True), s_new)
    p = jnp.exp(s - m)
    p_new = jnp.exp(s_new - m)
    den = jnp.sum(p, axis=-1, keepdims=True) + p_new
    return p, p_new, den


def _step_output(xv_ref, vn_col, p, p_new, den, o_ref):
    for h in range(A_HEADS):
        rows = slice(h * A_HEAD_DIM, (h + 1) * A_HEAD_DIM)
        pv = jnp.sum(xv_ref[0, 0, rows, :] * p[h:h + 1, :], axis=-1, keepdims=True)
        o_ref[0, rows, :] = (pv + p_new[h:h + 1, :] * vn_col[rows]) / den[h:h + 1, :]


def _attn_step_roll_kernel(q_ref, kn_ref, vn_ref, lw_ref, lw0_ref, xk_ref, xv_ref, o_ref, ko_ref, vo_ref, s_ref,
                           *, layer, past):
    is_layer = pl.program_id(0) == layer
    newest = _iota((A_HEAD_DIM, past), 1) == past - 1
    for x_ref, n_ref, out_ref in ((xk_ref, kn_ref, ko_ref), (xv_ref, vn_ref, vo_ref)):
        for h in range(A_HEADS):
            rows = slice(h * A_HEAD_DIM, (h + 1) * A_HEAD_DIM)
            rolled = pltpu.roll(x_ref[0, 0, rows, :], past - 1, 1)
            out_ref[0, 0, rows, :] = jnp.where(jnp.logical_and(newest, is_layer), n_ref[0, rows, :], rolled)

    @pl.when(is_layer)
    def _():
        p, p_new, den = _step_scores(xk_ref, q_ref[0], kn_ref[0], lw_ref, lw0_ref, s_ref)
        _step_output(xv_ref, vn_ref[0], p, p_new, den, o_ref.at[0])

    @pl.when(jnp.logical_not(is_layer))
    def _():
        o_ref[...] = jnp.zeros_like(o_ref)


def _attn_step_append_kernel(q_ref, kn_ref, vn_ref, lw_ref, lw0_ref, xk_ref, xv_ref, ko_in, vo_in,
                             o_ref, ko_ref, vo_ref, s_ref, *, past):
    del ko_in, vo_in
    p, p_new, den = _step_scores(xk_ref, q_ref[0], kn_ref[0], lw_ref, lw0_ref, s_ref)
    _step_output(xv_ref, vn_ref[0], p, p_new, den, o_ref)
    newest = _iota((A_WIDTH, LANES), 1) == LANES - 1
    for x_ref, n_ref, out_ref in ((xk_ref, kn_ref, ko_ref), (xv_ref, vn_ref, vo_ref)):
        rolled = pltpu.roll(x_ref[0, 0, :, past - LANES:past], LANES - 1, 1)
        out_ref[0, 0] = jnp.where(newest, n_ref[0], rolled)


def attention_step(q_col, kn_col, vn_col, cache_k, cache_v, rolled, logw, *, layer):
    n_layers, bsz, w, past = cache_k.shape
    lw_cache = logw[:, past:0:-1]
    lw_new = jnp.broadcast_to(logw[:, :1], (A_HEADS, LANES))
    out_shape = [jax.ShapeDtypeStruct((bsz, w, 1), F32),
                 jax.ShapeDtypeStruct(cache_k.shape, cache_k.dtype),
                 jax.ShapeDtypeStruct(cache_v.shape, cache_v.dtype)]
    scratch = [pltpu.VMEM((A_HEADS, past), F32)]
    if rolled is None:
        assert layer == 0
        col = pl.BlockSpec((1, w, 1), lambda l, b: (b, 0, 0))
        win = pl.BlockSpec((1, 1, w, past), lambda l, b: (l, b, 0, 0))
        o_all, rolled_k, rolled_v = pl.pallas_call(
            functools.partial(_attn_step_roll_kernel, layer=layer, past=past),
            grid=(n_layers, bsz),
            in_specs=[col, col, col, _small(lw_cache), _small(lw_new), win, win],
            out_specs=[pl.BlockSpec((1, 1, w, 1), lambda l, b: (l, b, 0, 0)), win, win],
            out_shape=[jax.ShapeDtypeStruct((n_layers, bsz, w, 1), F32)] + out_shape[1:],
            scratch_shapes=scratch,
            compiler_params=_cparams("arbitrary", "arbitrary"),
            name="attn_step_roll",
        )(q_col, kn_col, vn_col, lw_cache, lw_new, cache_k, cache_v)
        return o_all[layer], rolled_k, rolled_v
    col = pl.BlockSpec((1, w, 1), lambda b: (b, 0, 0))
    win = pl.BlockSpec((1, 1, w, past), lambda b: (layer, b, 0, 0))
    tail = pl.BlockSpec((1, 1, w, LANES), lambda b: (layer, b, 0, past // LANES - 1))
    return pl.pallas_call(
        functools.partial(_attn_step_append_kernel, past=past),
        grid=(bsz,),
        in_specs=[col, col, col, _small(lw_cache), _small(lw_new), win, win, tail, tail],
        out_specs=[col, tail, tail],
        out_shape=out_shape,
        scratch_shapes=scratch,
        input_output_aliases={7: 1, 8: 2},
        compiler_params=_cparams("arbitrary"),
        name="attn_step_append",
    )(q_col, kn_col, vn_col, lw_cache, lw_new, cache_k, cache_v, *rolled)


def _group_rms(y, w):
    half = SSM_D_INNER // SSM_GROUPS
    return [_rms(y[:, g * half:(g + 1) * half]) * w[:, g * half:(g + 1) * half] for g in range(SSM_GROUPS)]


def _ssd_kernel(z_ref, xs_ref, bc_ref, dt_ref, dtb_ref, alog_ref, d_ref, nw_ref, h0_ref, y_ref, h_ref, ys_ref):
    c = pl.program_id(1)
    ch = SSM_CHUNK

    @pl.when(c == 0)
    def _():
        h_ref[...] = h0_ref[...]

    xs = xs_ref[0].astype(F32)
    bc = bc_ref[0].astype(F32)

    dt = jax.nn.softplus(dt_ref[0] + dtb_ref[...])
    da = dt * (-jnp.exp(alog_ref[...]))
    row = _iota((ch, ch), 0)
    colv = _iota((ch, ch), 1)
    tril = (row >= colv).astype(F32)
    cs = _dot(tril, da, precision=HIGHEST)
    cs_t = cs.T
    causal = row >= colv

    heads = range(SSM_HEADS)
    hpg = SSM_HEADS // SSM_GROUPS
    bm = [bc[:, g * SSM_STATE:(g + 1) * SSM_STATE] for g in range(SSM_GROUPS)]
    cm = [bc[:, (SSM_GROUPS + g) * SSM_STATE:(SSM_GROUPS + g + 1) * SSM_STATE] for g in range(SSM_GROUPS)]
    cb = [_dot_nt(cm[g], bm[g]) for g in range(SSM_GROUPS)]
    dt_t = dt.T
    xs_t = xs.T
    w_t = dt_t * jnp.exp(cs_t[:, ch - 1:ch] - cs_t)
    e_last = jnp.exp(cs[ch - 1:ch, :])
    lane_lo = _iota((ch, LANES), 1) < SSM_HEAD_DIM
    row_lo = _iota((LANES, ch), 0) < SSM_HEAD_DIM
    pairs = range(SSM_HEADS // 2)
    csb = [jnp.broadcast_to(cs[:, h:h + 1], (ch, ch)) for h in heads]
    mix = [cb[h // hpg] * jnp.exp(jnp.where(causal, csb[h] - cs_t[h:h + 1, :], NEG)) * dt_t[h:h + 1, :]
           for h in heads]
    x_pair = [xs[:, j * LANES:(j + 1) * LANES] for j in pairs]
    y_intra = [jnp.where(lane_lo, _dot(mix[2 * j], x_pair[j]), _dot(mix[2 * j + 1], x_pair[j])) for j in pairs]
    h_pair = [h_ref[0, 2 * j:2 * j + 2].reshape(2 * SSM_HEAD_DIM, SSM_STATE) for j in pairs]
    y_inter = [_dot_nt(cm[2 * j // hpg], h_pair[j]) * jnp.exp(jnp.where(lane_lo, csb[2 * j], csb[2 * j + 1]))
               for j in pairs]
    xw_t = [xs_t[j * LANES:(j + 1) * LANES, :] * jnp.where(row_lo, w_t[2 * j:2 * j + 1, :], w_t[2 * j + 1:2 * j + 2, :])
            for j in pairs]
    st = [_dot(xw_t[j], bm[2 * j // hpg]) for j in pairs]
    for j in pairs:
        cols = slice(j * LANES, (j + 1) * LANES)
        ys_ref[:, cols] = y_intra[j] + y_inter[j] + d_ref[:, cols] * x_pair[j]
        decay = jnp.where(row_lo, e_last[:, 2 * j:2 * j + 1], e_last[:, 2 * j + 1:2 * j + 2])
        h_ref[0, 2 * j:2 * j + 2] = (h_pair[j] * decay + st[j]).reshape(2, SSM_HEAD_DIM, SSM_STATE)

    y = ys_ref[...] * jax.nn.silu(z_ref[0].astype(F32))
    half = SSM_D_INNER // SSM_GROUPS
    for g, yg in enumerate(_group_rms(y, nw_ref[...])):
        y_ref[0, :, g * half:(g + 1) * half] = yg.astype(y_ref.dtype)


def _small(a):
    return pl.BlockSpec(a.shape, lambda *_: (0,) * a.ndim)


def ssd_prompt(main, tail, h0, dt_bias, a_log, d_skip, norm_w, *, bsz, seq):
    nc = seq // SSM_CHUNK
    ch = SSM_CHUNK
    small = [_pad_tail(dt_bias.reshape(1, -1)), _pad_tail(a_log.reshape(1, -1)),
             jnp.repeat(d_skip, SSM_HEAD_DIM).reshape(1, -1), norm_w.reshape(1, -1)]
    return pl.pallas_call(
        _ssd_kernel,
        grid=(bsz, nc),
        in_specs=[
            pl.BlockSpec((1, ch, SSM_D_INNER), lambda b, c: (b, c, 0)),
            pl.BlockSpec((1, ch, SSM_D_INNER), lambda b, c: (b, c, 1)),
            pl.BlockSpec((1, ch, SSM_BC), lambda b, c: (b, c, HYB_BC0 // SSM_BC)),
            pl.BlockSpec((1, ch, LANES), lambda b, c: (b, c, 0)),
        ] + [_small(a) for a in small] + [
            pl.BlockSpec((1, SSM_HEADS, SSM_HEAD_DIM, SSM_STATE), lambda b, c: (b, 0, 0, 0)),
        ],
        out_specs=[
            pl.BlockSpec((1, ch, SSM_D_INNER), lambda b, c: (b, c, 0)),
            pl.BlockSpec((1, SSM_HEADS, SSM_HEAD_DIM, SSM_STATE), lambda b, c: (b, 0, 0, 0)),
        ],
        out_shape=[jax.ShapeDtypeStruct((bsz, seq, SSM_D_INNER), BF16),
                   jax.ShapeDtypeStruct((bsz, SSM_HEADS, SSM_HEAD_DIM, SSM_STATE), F32)],
        scratch_shapes=[pltpu.VMEM((ch, SSM_D_INNER), F32)],
        compiler_params=_cparams("parallel", "arbitrary"),
        name="ssd_prompt",
    )(main, main, main, tail, *small, h0)


def _row_to_col(row, eye):
    return jnp.sum(jnp.where(eye, row, 0.0), axis=1, keepdims=True)


def _col_to_row(col, eye):
    return jnp.sum(jnp.where(eye, col, 0.0), axis=0, keepdims=True)


def _conv_step(c0_ref, w_ref, x_row, c0, c1):
    acc = w_ref[SSM_CONV - 1:SSM_CONV, c0:c1] * x_row
    for i in range(SSM_CONV - 1):
        acc = acc + w_ref[i:i + 1, c0:c1] * c0_ref[0, i:i + 1, c0:c1]
    return acc


def _ssd_step_kernel(z_ref, xs_ref, bc_ref, dt_ref, c0_ref, cw_ref, cb_ref, dtb_ref, alog_ref, d_ref, nw_ref,
                     h0_ref, y_ref, co_ref, h_ref, ys_ref):
    xs_raw = xs_ref[0]
    bc_raw = bc_ref[0]
    xs = jax.nn.silu(_conv_step(c0_ref, cw_ref, xs_raw, 0, SSM_D_INNER) + cb_ref[:, 0:SSM_D_INNER])
    bc = jax.nn.silu(_conv_step(c0_ref, cw_ref, bc_raw, SSM_D_INNER, SSM_XBC) + cb_ref[:, SSM_D_INNER:SSM_XBC])
    co_ref[0, 0:SSM_CONV - 2, :] = c0_ref[0, 1:SSM_CONV - 1, :]
    co_ref[0, SSM_CONV - 2:SSM_CONV - 1, 0:SSM_D_INNER] = xs_raw
    co_ref[0, SSM_CONV - 2:SSM_CONV - 1, SSM_D_INNER:SSM_XBC] = bc_raw

    dt = jax.nn.softplus(dt_ref[0] + dtb_ref[...])
    dec = jnp.exp(dt * (-jnp.exp(alog_ref[...])))
    dskip = d_ref[...]
    eye = _iota((LANES, LANES), 0) == _iota((LANES, LANES), 1)
    upper = _iota((LANES, 1), 0) >= SSM_HEAD_DIM
    pairs = range(SSM_HEADS // 2)
    grp = [2 * j // (SSM_HEADS // SSM_GROUPS) for j in pairs]
    bm = [bc[:, g * SSM_STATE:(g + 1) * SSM_STATE] for g in grp]
    cm = [bc[:, (SSM_GROUPS + g) * SSM_STATE:(SSM_GROUPS + g + 1) * SSM_STATE] for g in grp]
    pick = lambda v, j: jnp.where(upper, v[:, 2 * j + 1:2 * j + 2], v[:, 2 * j:2 * j + 1])
    x_col = [_row_to_col(xs[:, j * LANES:(j + 1) * LANES], eye) for j in pairs]
    hp = [h0_ref[0, 2 * j:2 * j + 2].reshape(2 * SSM_HEAD_DIM, SSM_STATE) for j in pairs]
    hn = [hp[j] * pick(dec, j) + (x_col[j] * pick(dt, j)) * bm[j] for j in pairs]
    y_col = [jnp.sum(hn[j] * cm[j], axis=1, keepdims=True) + pick(dskip, j) * x_col[j] for j in pairs]
    y_row = [_col_to_row(y_col[j], eye) for j in pairs]
    for j in pairs:
        h_ref[0, 2 * j:2 * j + 2] = hn[j].reshape(2, SSM_HEAD_DIM, SSM_STATE)
        ys_ref[:, j * LANES:(j + 1) * LANES] = y_row[j]

    y = ys_ref[...] * jax.nn.silu(z_ref[0])
    half = SSM_D_INNER // SSM_GROUPS
    for g, yg in enumerate(_group_rms(y, nw_ref[...])):
        y_ref[0, :, g * half:(g + 1) * half] = yg


def ssd_step(main, tail, conv0, h0, conv_w, conv_b, dt_bias, a_log, d_skip, norm_w):
    bsz = main.shape[0]
    small = [conv_w, conv_b.reshape(1, -1), _pad_tail(dt_bias.reshape(1, -1)), _pad_tail(a_log.reshape(1, -1)),
             _pad_tail(d_skip.reshape(1, -1)), norm_w.reshape(1, -1)]
    hspec = pl.BlockSpec((1, SSM_HEADS, SSM_HEAD_DIM, SSM_STATE), lambda b: (b, 0, 0, 0))
    cspec = pl.BlockSpec((1, SSM_CONV - 1, SSM_XBC), lambda b: (b, 0, 0))
    return pl.pallas_call(
        _ssd_step_kernel,
        grid=(bsz,),
        in_specs=[
            pl.BlockSpec((1, 1, SSM_D_INNER), lambda b: (b, 0, 0)),
            pl.BlockSpec((1, 1, SSM_D_INNER), lambda b: (b, 0, 1)),
            pl.BlockSpec((1, 1, SSM_BC), lambda b: (b, 0, HYB_BC0 // SSM_BC)),
            pl.BlockSpec((1, 1, LANES), lambda b: (b, 0, 0)),
            cspec,
        ] + [_small(a) for a in small] + [hspec],
        out_specs=[pl.BlockSpec((1, 1, SSM_D_INNER), lambda b: (b, 0, 0)), cspec, hspec],
        out_shape=[jax.ShapeDtypeStruct((bsz, 1, SSM_D_INNER), F32),
                   jax.ShapeDtypeStruct(conv0.shape, F32),
                   jax.ShapeDtypeStruct(h0.shape, F32)],
        scratch_shapes=[pltpu.VMEM((1, SSM_D_INNER), F32)],
        compiler_params=_cparams("parallel"),
        name="ssd_step",
    )(main, main, main, tail, conv0, *small, h0)


def _l2norm(x):
    return x * lax.rsqrt(jnp.sum(x * x, axis=-1, keepdims=True) + EPS)


def _unit_lower_inverse(ns, eye):
    size = ns[0].shape[0]
    ps = [eye - n for n in ns]
    ms = [_dot(n, n) for n in ns]
    power = 2
    while 2 * power < size:
        pms = [_dot(jnp.concatenate([p, m], axis=0), m) for p, m in zip(ps, ms)]
        ps = [p + pm[:size] for p, pm in zip(ps, pms)]
        ms = [pm[size:] for pm in pms]
        power *= 2
    return [p + _dot(p, m) for p, m in zip(ps, ms)]


def _gdn_gates(ba, dtb_ref, alog_ref):
    beta = jax.nn.sigmoid(ba)
    g = -jnp.exp(alog_ref[...]) * jax.nn.softplus(ba + dtb_ref[...])
    return beta, g


def _gdn_kernel(q_ref, k_ref, v_ref, z_ref, ba_ref, dtb_ref, alog_ref, nw_ref, s0_ref, o_ref, s_ref):
    c = pl.program_id(1)
    ch = GDN_CHUNK
    nh = GDN_V_HEADS

    @pl.when(c == 0)
    def _():
        s_ref[...] = s0_ref[...]

    row = _iota((ch, ch), 0)
    colv = _iota((ch, ch), 1)
    incl = row >= colv
    strict = row > colv
    eye = (row == colv).astype(F32)
    rep = nh // GDN_QK_HEADS
    heads = range(nh)
    qk_heads = range(GDN_QK_HEADS)
    subs = range(GDN_STEP_CHUNKS)
    rows = {sc: slice(sc * ch, (sc + 1) * ch) for sc in subs}
    sh = [(sc, h) for sc in subs for h in heads]
    sj = [(sc, j) for sc in subs for j in qk_heads]

    gates = {sc: _gdn_gates(ba_ref[0, rows[sc], :], dtb_ref, alog_ref) for sc in subs}
    gcum = {sc: _dot(incl.astype(F32), gates[sc][1], precision=HIGHEST) for sc in subs}
    gcum_t = {sc: jnp.concatenate([gcum[sc], jnp.zeros((LANES - ch, LANES), F32)], axis=0).T for sc in subs}
    qn = {(sc, j): _l2norm(q_ref[0, rows[sc], j * GDN_DK:(j + 1) * GDN_DK].astype(F32)) * (GDN_DK ** -0.5)
          for sc, j in sj}
    kn = {(sc, j): _l2norm(k_ref[0, rows[sc], j * GDN_DK:(j + 1) * GDN_DK].astype(F32)) for sc, j in sj}
    kk = {k: _dot_nt(kn[k], kn[k]) for k in sj}
    qk = {k: _dot_nt(qn[k], kn[k]) for k in sj}
    gc_col = {(sc, h): gcum[sc][:, nh + h:nh + h + 1] for sc, h in sh}
    gc_last = {(sc, h): gcum[sc][ch - 1:ch, nh + h:nh + h + 1] for sc, h in sh}
    beta_col = {(sc, h): gates[sc][0][:, h:h + 1] for sc, h in sh}
    dec = {(sc, h): jnp.exp(jnp.where(incl, gc_col[sc, h] - gcum_t[sc][nh + h:nh + h + 1, 0:ch], NEG))
           for sc, h in sh}
    t_inv = dict(zip(sh, _unit_lower_inverse(
        [jnp.where(strict, kk[sc, h // rep] * dec[sc, h], 0.0) * beta_col[sc, h] for sc, h in sh], eye)))
    eg = {k: jnp.exp(gc_col[k]) for k in sh}
    lhs = {(sc, h): jnp.concatenate([kn[sc, h // rep] * (beta_col[sc, h] * eg[sc, h]), qn[sc, h // rep] * eg[sc, h]],
                                    axis=0) for sc, h in sh}
    vb = {(sc, h): v_ref[0, rows[sc], h * GDN_DV:(h + 1) * GDN_DV].astype(F32) * beta_col[sc, h] for sc, h in sh}
    attn = {(sc, h): qk[sc, h // rep] * dec[sc, h] for sc, h in sh}
    kdec = {(sc, h): kn[sc, h // rep] * jnp.exp(gc_last[sc, h] - gc_col[sc, h]) for sc, h in sh}

    state = {h: s_ref[0, h] for h in heads}
    for sc in subs:
        both = {h: _dot(lhs[sc, h], state[h]) for h in heads}
        u = {h: _dot(t_inv[sc, h], vb[sc, h] - both[h][:ch]) for h in heads}
        o = {h: both[h][ch:] + _dot(attn[sc, h], u[h]) for h in heads}
        state = {h: state[h] * jnp.exp(gc_last[sc, h]) + _dot_tn(kdec[sc, h], u[h]) for h in heads}
        for h in heads:
            z_h = z_ref[0, rows[sc], h * GDN_DV:(h + 1) * GDN_DV].astype(F32)
            o_ref[0, rows[sc], h * GDN_DV:(h + 1) * GDN_DV] = (
                _rms(o[h]) * nw_ref[...] * jax.nn.silu(z_h)).astype(o_ref.dtype)
    for h in heads:
        s_ref[0, h] = state[h]


def _gdn_gate_params(dt_bias, a_log):
    nh = GDN_V_HEADS
    dtb = jnp.zeros((1, LANES), F32).at[0, nh:2 * nh].set(dt_bias)
    alog = jnp.zeros((1, LANES), F32).at[0, nh:2 * nh].set(a_log)
    return dtb, alog


def gdn_prompt(main, tail, s0, dt_bias, a_log, norm_w, *, bsz, seq):
    ch = GDN_CHUNK * GDN_STEP_CHUNKS
    assert seq % ch == 0
    nc = seq // ch
    dtb, alog = _gdn_gate_params(dt_bias, a_log)
    small = [dtb, alog, norm_w.reshape(1, -1)]
    sspec = pl.BlockSpec((1, GDN_V_HEADS, GDN_DK, GDN_DV), lambda b, c: (b, 0, 0, 0))
    return pl.pallas_call(
        _gdn_kernel,
        grid=(bsz, nc),
        in_specs=[
            pl.BlockSpec((1, ch, GDN_QK_W), lambda b, c: (b, c, 0)),
            pl.BlockSpec((1, ch, GDN_QK_W), lambda b, c: (b, c, 1)),
            pl.BlockSpec((1, ch, GDN_VW), lambda b, c: (b, c, 1)),
            pl.BlockSpec((1, ch, GDN_VW), lambda b, c: (b, c, 2)),
            pl.BlockSpec((1, ch, LANES), lambda b, c: (b, c, 0)),
        ] + [_small(a) for a in small] + [sspec],
        out_specs=[pl.BlockSpec((1, ch, GDN_VW), lambda b, c: (b, c, 0)), sspec],
        out_shape=[jax.ShapeDtypeStruct((bsz, seq, GDN_VW), BF16),
                   jax.ShapeDtypeStruct((bsz, GDN_V_HEADS, GDN_DK, GDN_DV), F32)],
        compiler_params=_cparams("parallel", "arbitrary"),
        name="gdn_prompt",
    )(main, main, main, main, tail, *small, s0)


def _gdn_conv_step(c0_ref, w_ref, x_row, c0, c1):
    acc = w_ref[GDN_CONV - 1:GDN_CONV, c0:c1] * x_row
    for i in range(GDN_CONV - 1):
        acc = acc + w_ref[i:i + 1, c0:c1] * c0_ref[0, i:i + 1, c0:c1]
    return acc


def _gdn_step_kernel(q_ref, k_ref, v_ref, z_ref, ba_ref, c0_ref, cw_ref, dtb_ref, alog_ref, nw_ref, s0_ref,
                     o_ref, co_ref, s_ref):
    nh = GDN_V_HEADS
    q_raw, k_raw, v_raw = q_ref[0], k_ref[0], v_ref[0]
    q = jax.nn.silu(_gdn_conv_step(c0_ref, cw_ref, q_raw, 0, GDN_QK_W))
    k = jax.nn.silu(_gdn_conv_step(c0_ref, cw_ref, k_raw, GDN_QK_W, 2 * GDN_QK_W))
    v = jax.nn.silu(_gdn_conv_step(c0_ref, cw_ref, v_raw, 2 * GDN_QK_W, GDN_QKV))
    co_ref[0, 0:GDN_CONV - 2, :] = c0_ref[0, 1:GDN_CONV - 1, :]
    co_ref[0, GDN_CONV - 2:GDN_CONV - 1, 0:GDN_QK_W] = q_raw
    co_ref[0, GDN_CONV - 2:GDN_CONV - 1, GDN_QK_W:2 * GDN_QK_W] = k_raw
    co_ref[0, GDN_CONV - 2:GDN_CONV - 1, 2 * GDN_QK_W:GDN_QKV] = v_raw

    beta, g = _gdn_gates(ba_ref[0], dtb_ref, alog_ref)
    eg_all = jnp.exp(g)
    eye = _iota((LANES, LANES), 0) == _iota((LANES, LANES), 1)
    rep = nh // GDN_QK_HEADS
    heads = range(nh)
    qn = [_l2norm(q[:, j * GDN_DK:(j + 1) * GDN_DK]) * (GDN_DK ** -0.5) for j in range(GDN_QK_HEADS)]
    kn = [_l2norm(k[:, j * GDN_DK:(j + 1) * GDN_DK]) for j in range(GDN_QK_HEADS)]
    qk = [jnp.sum(a * b, axis=-1, keepdims=True) for a, b in zip(qn, kn)]
    q_col = [_row_to_col(a, eye) for a in qn]
    k_col = [_row_to_col(a, eye) for a in kn]
    b_h = [beta[:, h:h + 1] for h in heads]
    eg = [eg_all[:, nh + h:nh + h + 1] for h in heads]
    s_prev = [s0_ref[0, h] for h in heads]
    ks = [jnp.sum(s_prev[h] * k_col[h // rep], axis=0, keepdims=True) for h in heads]
    qs = [jnp.sum(s_prev[h] * q_col[h // rep], axis=0, keepdims=True) for h in heads]
    u = [v[:, h * GDN_DV:(h + 1) * GDN_DV] * b_h[h] - (b_h[h] * eg[h]) * ks[h] for h in heads]
    o = [eg[h] * qs[h] + qk[h // rep] * u[h] for h in heads]
    for h in heads:
        s_ref[0, h] = s_prev[h] * eg[h] + k_col[h // rep] * u[h]
        z_h = z_ref[0, :, h * GDN_DV:(h + 1) * GDN_DV]
        o_ref[0, :, h * GDN_DV:(h + 1) * GDN_DV] = _rms(o[h]) * nw_ref[...] * jax.nn.silu(z_h)


def gdn_step(main, tail, conv0, s0, conv_w, dt_bias, a_log, norm_w):
    bsz = main.shape[0]
    dtb, alog = _gdn_gate_params(dt_bias, a_log)
    small = [conv_w, dtb, alog, norm_w.reshape(1, -1)]
    sspec = pl.BlockSpec((1, GDN_V_HEADS, GDN_DK, GDN_DV), lambda b: (b, 0, 0, 0))
    cspec = pl.BlockSpec((1, GDN_CONV - 1, GDN_QKV), lambda b: (b, 0, 0))
    return pl.pallas_call(
        _gdn_step_kernel,
        grid=(bsz,),
        in_specs=[
            pl.BlockSpec((1, 1, GDN_QK_W), lambda b: (b, 0, 0)),
            pl.BlockSpec((1, 1, GDN_QK_W), lambda b: (b, 0, 1)),
            pl.BlockSpec((1, 1, GDN_VW), lambda b: (b, 0, 1)),
            pl.BlockSpec((1, 1, GDN_VW), lambda b: (b, 0, 2)),
            pl.BlockSpec((1, 1, LANES), lambda b: (b, 0, 0)),
            cspec,
        ] + [_small(a) for a in small] + [sspec],
        out_specs=[pl.BlockSpec((1, 1, GDN_VW), lambda b: (b, 0, 0)), cspec, sspec],
        out_shape=[jax.ShapeDtypeStruct((bsz, 1, GDN_VW), F32),
                   jax.ShapeDtypeStruct(conv0.shape, F32),
                   jax.ShapeDtypeStruct(s0.shape, F32)],
        compiler_params=_cparams("parallel"),
        name="gdn_step",
    )(main, main, main, main, tail, conv0, *small, s0)


def _pad_tail(w):
    return jnp.pad(w, ((0, 0), (0, LANES - w.shape[1])))


def _prep_hyb_in(w):
    a = A_WIDTH
    q, k, v = w[:, 0:a], w[:, a:2 * a], w[:, 2 * a:3 * a]
    z = w[:, 3 * a:3 * a + SSM_D_INNER]
    x0 = 3 * a + SSM_D_INNER
    xs = w[:, x0:x0 + SSM_D_INNER]
    bc = w[:, x0 + SSM_D_INNER:x0 + SSM_XBC]
    dt = w[:, x0 + SSM_XBC:]
    return jnp.concatenate([z, xs, q, k, v, bc], axis=1).astype(BF16), _pad_tail(dt).astype(BF16)


def _prep_gdn_in(w):
    return w[:, :GDN_MAIN].astype(BF16), _pad_tail(w[:, GDN_MAIN:]).astype(BF16)


HYB_CONV_COLS = ((SSM_D_INNER, 2 * SSM_D_INNER), (HYB_BC0, HYB_MAIN))


def _hyb_cols(a):
    out = jnp.zeros(a.shape[:-1] + (HYB_MAIN,), F32)
    (x0, x1), (b0, b1) = HYB_CONV_COLS
    return out.at[..., x0:x1].set(a[..., :SSM_D_INNER]).at[..., b0:b1].set(a[..., SSM_D_INNER:])


def _window_to_lanes(c):
    n, b, past, h, dh = c.shape
    return jnp.transpose(c, (0, 1, 3, 4, 2)).reshape(n, b, h * dh, past)


def _window_from_lanes(c):
    n, b, _, past = c.shape
    return jnp.transpose(c.reshape(n, b, A_HEADS, A_HEAD_DIM, past), (0, 1, 4, 2, 3))


def _row_tile(m, cap):
    return m if m <= cap else cap


def kernel(x_prompt, x_sample, cache_attn_k, cache_attn_v, state_ssm_conv, state_ssm, state_gdn_conv, state_gdn, rel_bias, norm_mix_pre, norm_mix_post, norm_ffn_pre, norm_ffn_post, w_hyb_in, ssm_conv_w, ssm_conv_b, ssm_dt_bias, ssm_a_log, ssm_d, ssm_norm_w, w_hyb_out, w_gdn_in, gdn_conv_w, gdn_dt_bias, gdn_a_log, gdn_norm_w, w_gdn_out, w_ffn_gate, w_ffn_up, w_ffn_down):
    depth = norm_mix_pre.shape[0]
    d_model = x_prompt.shape[-1]
    n_hyb, n_gdn = w_hyb_in.shape[0], w_gdn_in.shape[0]

    hyb_in = [_prep_hyb_in(w_hyb_in[i]) for i in range(n_hyb)]
    hyb_out = [(w_hyb_out[i, :A_WIDTH].astype(BF16), w_hyb_out[i, A_WIDTH:].astype(BF16)) for i in range(n_hyb)]
    gdn_in = [_prep_gdn_in(w_gdn_in[i]) for i in range(n_gdn)]
    gdn_out = [w_gdn_out[i].astype(BF16) for i in range(n_gdn)]
    ffn_w = [(w_ffn_gate[l].astype(BF16), w_ffn_up[l].astype(BF16), w_ffn_down[l].astype(BF16))
             for l in range(depth)]
    bias_tiles = _attn_bias_rows(rel_bias)

    def trunk(x3, k_pre, v_pre, sconv, sssm, gconv, gstate):
        bsz, seq, _ = x3.shape
        m = bsz * seq
        step = seq == 1
        tm_big = _row_tile(m, 2048)
        tm = _row_tile(m, 512)
        x = x3.reshape(m, d_model)
        nk, nv, nsc, nss, ngc, ngs = [], [], [], [], [], []
        rolled = None
        for l in range(depth):
            i = l // 2
            if l % 2 == 0:
                w_main, w_tail = hyb_in[i]
                ssm_args = (ssm_dt_bias[i], ssm_a_log[i], ssm_d[i], ssm_norm_w[i])
                if step:
                    main, tail = inproj(x, norm_mix_pre[l], w_main, w_tail, tm=tm_big, tn=512)
                    main3 = main.reshape(bsz, seq, HYB_MAIN)
                    tail3 = tail.reshape(bsz, seq, LANES)
                    col = lambda c0: main[:, c0:c0 + A_WIDTH].reshape(bsz, A_WIDTH, 1)
                    o_attn, *rolled = attention_step(
                        col(HYB_Q0), col(HYB_K0), col(HYB_V0), k_pre, v_pre, rolled,
                        _attn_logw(rel_bias, k_pre.shape[-1]), layer=i)
                    y, c_new, s_new = ssd_step(main3, tail3, sconv[i], sssm[i], ssm_conv_w[i], ssm_conv_b[i],
                                               *ssm_args)
                else:
                    main, tail, hist = inproj_conv(
                        x, norm_mix_pre[l], w_main, w_tail, _hyb_cols(ssm_conv_w[i]),
                        _hyb_cols(ssm_conv_b[i][None]), _hyb_cols(sconv[i]), seq=seq, conv_cols=HYB_CONV_COLS,
                        tm=tm_big, tn=512)
                    main3 = main.reshape(bsz, seq, HYB_MAIN)
                    tail3 = tail.reshape(bsz, seq, LANES)
                    c_new = jnp.concatenate([hist[..., a:b] for a, b in HYB_CONV_COLS], axis=-1)
                    o_attn = attention_prompt(main3, bias_tiles, bsz=bsz, seq=seq)
                    keep = min(A_PATTERNS[-1][0], seq)
                    k_new = main3[:, seq - keep:, HYB_K0:HYB_K0 + A_WIDTH].astype(F32)
                    v_new = main3[:, seq - keep:, HYB_V0:HYB_V0 + A_WIDTH].astype(F32)
                    y, s_new = ssd_prompt(main3, tail3, sssm[i], *ssm_args, bsz=bsz, seq=seq)
                    nk.append(k_new.reshape(bsz, -1, A_HEADS, A_HEAD_DIM))
                    nv.append(v_new.reshape(bsz, -1, A_HEADS, A_HEAD_DIM))
                nsc.append(c_new)
                nss.append(s_new)
                acts, w_outs = [o_attn.reshape(m, A_WIDTH), y.reshape(m, SSM_D_INNER)], list(hyb_out[i])
            else:
                w_main, w_tail = gdn_in[i]
                gdn_args = (gdn_dt_bias[i], gdn_a_log[i], gdn_norm_w[i])
                if step:
                    main, tail = inproj(x, norm_mix_pre[l], w_main, w_tail, tm=tm_big, tn=512)
                    o, c_new, s_new = gdn_step(main.reshape(bsz, seq, GDN_MAIN), tail.reshape(bsz, seq, LANES),
                                               gconv[i], gstate[i], gdn_conv_w[i], *gdn_args)
                else:
                    pad = lambda a: jnp.pad(a, [(0, 0)] * (a.ndim - 1) + [(0, GDN_MAIN - GDN_QKV)])
                    main, tail, hist = inproj_conv(
                        x, norm_mix_pre[l], w_main, w_tail, pad(gdn_conv_w[i]), jnp.zeros((1, GDN_MAIN), F32),
                        pad(gconv[i]), seq=seq, conv_cols=((0, GDN_QKV),), tm=tm_big, tn=512)
                    c_new = hist[..., :GDN_QKV]
                    o, s_new = gdn_prompt(main.reshape(bsz, seq, GDN_MAIN), tail.reshape(bsz, seq, LANES),
                                          gstate[i], *gdn_args, bsz=bsz, seq=seq)
                ngc.append(c_new)
                ngs.append(s_new)
                acts, w_outs = [o.reshape(m, GDN_VW)], [gdn_out[i]]
            wg, wu, wd = ffn_w[l]
            x = mix_ffn(acts, w_outs, x, norm_mix_post[l], norm_ffn_pre[l], wg, wu, wd, norm_ffn_post[l], tm=tm)
        k_out, v_out = [_window_from_lanes(r) for r in rolled] if step else (jnp.stack(nk), jnp.stack(nv))
        return (x.reshape(bsz, seq, d_model), k_out, v_out, jnp.stack(nsc), jnp.stack(nss),
                jnp.stack(ngc), jnp.stack(ngs))

    bsz = x_prompt.shape[0]
    dt_p = x_prompt.dtype
    p_sc0 = jnp.zeros((n_hyb, bsz, SSM_CONV - 1, SSM_XBC), dt_p)
    p_ss0 = jnp.zeros((n_hyb, bsz, SSM_HEADS, SSM_HEAD_DIM, SSM_STATE), F32)
    p_gc0 = jnp.zeros((n_gdn, bsz, GDN_CONV - 1, GDN_QKV), dt_p)
    p_gs0 = jnp.zeros((n_gdn, bsz, GDN_V_HEADS, GDN_DK, GDN_DV), F32)
    y_prompt, pk, pv, psc, pss, pgc, pgs = trunk(x_prompt, None, None, p_sc0, p_ss0, p_gc0, p_gs0)
    y_sample, sk, sv, ssc, sss, sgc, sgs = trunk(
        x_sample, _window_to_lanes(cache_attn_k), _window_to_lanes(cache_attn_v), state_ssm_conv, state_ssm,
        state_gdn_conv, state_gdn)
    return (y_prompt, y_sample, pk, pv, psc, pss, pgc, pgs, sk, sv, ssc, sss, sgc, sgs)
```

```python
import functools
import math

import numpy as np
import jax
import jax.numpy as jnp
from jax import lax
from jax.experimental import pallas as pl
from jax.experimental.pallas import tpu as pltpu

F32 = jnp.float32
BF16 = jnp.bfloat16
EPS = 1e-6
NEG = -1e30
HIGHEST = lax.Precision.HIGHEST

VMEM_LIMIT_BYTES = 56 * 1024 * 1024
LANES = 128
PROJ_ROW_TILE = 2048
PROJ_COL_TILE = 512
FFN_ROW_TILE = 512

A_HEADS = 8
A_HEAD_DIM = 64
A_WIDTH = A_HEADS * A_HEAD_DIM
A_PATTERNS = ((128, 1), (512, 4), (2048, 16))
A_BAND = 128
ATTN_GROUP = 8
REL_BUCKETS = 32
REL_MAX_DIST = 2048

SSM_D_INNER = 1024
SSM_HEAD_DIM = 64
SSM_HEADS = SSM_D_INNER // SSM_HEAD_DIM
SSM_GROUPS = 2
SSM_STATE = 128
SSM_CONV = 4
SSM_CHUNK = 128
SSM_BC = 2 * SSM_GROUPS * SSM_STATE
SSM_XBC = SSM_D_INNER + SSM_BC

GDN_QK_HEADS = 8
GDN_V_HEADS = 16
GDN_DK = 128
GDN_DV = 128
GDN_CONV = 4
GDN_CHUNK = 64
GDN_QK_W = GDN_QK_HEADS * GDN_DK
GDN_VW = GDN_V_HEADS * GDN_DV
GDN_QKV = 2 * GDN_QK_W + GDN_VW

HYB_MAIN = 2 * SSM_D_INNER + 3 * A_WIDTH + SSM_BC
HYB_Q0 = 2 * SSM_D_INNER
HYB_K0 = HYB_Q0 + A_WIDTH
HYB_V0 = HYB_K0 + A_WIDTH
HYB_BC0 = HYB_V0 + A_WIDTH
GDN_MAIN = GDN_QKV + GDN_VW


def _cparams(*sem):
    return pltpu.CompilerParams(dimension_semantics=sem, vmem_limit_bytes=VMEM_LIMIT_BYTES)


def _rms(x):
    return x * lax.rsqrt(jnp.mean(x * x, axis=-1, keepdims=True) + EPS)


def _dot(a, b, **kw):
    return jnp.dot(a, b, preferred_element_type=F32, **kw)


def _dot_nt(a, b, **kw):
    return lax.dot_general(a, b, (((1,), (1,)), ((), ())), preferred_element_type=F32, **kw)


def _dot_tn(a, b, **kw):
    return lax.dot_general(a, b, (((0,), (0,)), ((), ())), preferred_element_type=F32, **kw)


def _iota(shape, dim):
    return lax.broadcasted_iota(jnp.int32, shape, dim)


def _inproj_kernel(x_ref, g_ref, w_ref, wt_ref, o_ref, t_ref, h_ref):
    @pl.when(pl.program_id(1) == 0)
    def _():
        hb = (_rms(x_ref[...]) * g_ref[...]).astype(BF16)
        h_ref[...] = hb
        t_ref[...] = _dot(hb, wt_ref[...])

    o_ref[...] = _dot(h_ref[...], w_ref[...])


def inproj(x, g, w_main, w_tail, *, tm, tn):
    m, d = x.shape
    n = w_main.shape[1]
    return pl.pallas_call(
        _inproj_kernel,
        grid=(m // tm, n // tn),
        in_specs=[
            pl.BlockSpec((tm, d), lambda i, j: (i, 0)),
            pl.BlockSpec((1, d), lambda i, j: (0, 0)),
            pl.BlockSpec((d, tn), lambda i, j: (0, j)),
            pl.BlockSpec((d, LANES), lambda i, j: (0, 0)),
        ],
        out_specs=[
            pl.BlockSpec((tm, tn), lambda i, j: (i, j)),
            pl.BlockSpec((tm, LANES), lambda i, j: (i, 0)),
        ],
        out_shape=[jax.ShapeDtypeStruct((m, n), F32), jax.ShapeDtypeStruct((m, LANES), F32)],
        scratch_shapes=[pltpu.VMEM((tm, d), BF16)],
        compiler_params=_cparams("parallel", "arbitrary"),
        name="inproj",
    )(x, g.reshape(1, d), w_main, w_tail)


CONV_TAPS = 4
CONV_ROWS = 64
PROJ_K_SLICES = 4
CONV_BASE = 8


def _inproj_conv_kernel(x_ref, g_ref, w_ref, wt_ref, cw_ref, cb_ref, c0_ref, o_ref, t_ref, so_ref,
                        h_ref, xp0_ref, xp1_ref, carry_ref, *, n_col, conv_tiles, tiles_per_seq):
    s = pl.program_id(0)
    n_tiles = pl.num_programs(0) - 1
    tm = o_ref.shape[0]
    lo = CONV_BASE - (CONV_TAPS - 1)
    cur = jnp.minimum(s, n_tiles - 1)
    prv = jnp.maximum(s - 1, 0)
    pi, pj = prv // n_col, prv % n_col
    prv_conv = functools.reduce(jnp.logical_or, [jnp.logical_and(pj >= a, pj < b) for a, b in conv_tiles])
    first = pi % tiles_per_seq == 0
    both = lambda a, b: jnp.logical_and(a, b)

    @pl.when(both(cur % n_col == 0, s < n_tiles))
    def _():
        hb = (_rms(x_ref[...]) * g_ref[...]).astype(BF16)
        h_ref[...] = hb
        t_ref[...] = _dot(hb, wt_ref[...])

    for parity, (cur_ref, prv_ref) in enumerate(((xp0_ref, xp1_ref), (xp1_ref, xp0_ref))):
        here = s % 2 == parity

        def project(cur_ref=cur_ref):
            cur_ref[CONV_BASE:CONV_BASE + tm, :] = _dot(h_ref[...], w_ref[...])

        def raw_tail(prv_ref=prv_ref):
            tail = prv_ref[CONV_BASE + tm - (CONV_TAPS - 1):CONV_BASE + tm, :]
            so_ref[pj, pi] = tail
            return tail

        if parity == 0:
            @pl.when(s == 0)
            def _():
                project()

        @pl.when(both(here, both(s > 0, jnp.logical_not(prv_conv))))
        def _():
            project()
            raw_tail()
            o_ref[...] = prv_ref[CONV_BASE:CONV_BASE + tm, :].astype(o_ref.dtype)

        @pl.when(both(here, both(both(s > 0, prv_conv), first)))
        def _():
            prv_ref[lo:CONV_BASE, :] = c0_ref[pj, pi // tiles_per_seq]

        @pl.when(both(here, both(both(s > 0, prv_conv), jnp.logical_not(first))))
        def _():
            prv_ref[lo:CONV_BASE, :] = carry_ref[pj]

        @pl.when(both(here, both(s > 0, prv_conv)))
        def _():
            def conv_chunk(r0):
                ext = prv_ref[r0:r0 + CONV_BASE + CONV_ROWS, :]
                conv = cb_ref[pj] + cw_ref[pj, CONV_TAPS - 1:CONV_TAPS, :] * ext[CONV_BASE:]
                for back in range(1, CONV_TAPS):
                    tap = CONV_TAPS - 1 - back
                    conv = conv + cw_ref[pj, tap:tap + 1, :] * pltpu.roll(ext, back, 0)[CONV_BASE:]
                o_ref[r0:r0 + CONV_ROWS, :] = jax.nn.silu(conv).astype(o_ref.dtype)

            d = h_ref.shape[1]
            n_chunks = tm // CONV_ROWS
            acc = None
            for kq in range(PROJ_K_SLICES):
                ks = slice(kq * d // PROJ_K_SLICES, (kq + 1) * d // PROJ_K_SLICES)
                part = _dot(h_ref[:, ks], w_ref[ks, :])
                acc = part if acc is None else acc + part
                for c in range(kq * n_chunks // PROJ_K_SLICES, (kq + 1) * n_chunks // PROJ_K_SLICES):
                    conv_chunk(c * CONV_ROWS)
            cur_ref[CONV_BASE:CONV_BASE + tm, :] = acc
            carry_ref[pj] = raw_tail()


def inproj_conv(x, g, w_main, w_tail, conv_w, conv_b, conv0, *, seq, conv_cols, tm, tn):
    m, d = x.shape
    n = w_main.shape[1]
    assert seq % tm == 0 and all(a % tn == 0 and b % tn == 0 for a, b in conv_cols)
    tiles_per_seq = seq // tm
    n_row, n_col = m // tm, n // tn
    n_tiles = n_row * n_col
    conv_tiles = tuple((a // tn, b // tn) for a, b in conv_cols)
    hist = CONV_TAPS - 1
    cur = lambda s: jnp.minimum(s, n_tiles - 1)
    prv = lambda s: jnp.maximum(s - 1, 0)
    by_tile = lambda a: jnp.moveaxis(a.reshape(a.shape[:-1] + (n_col, tn)), -2, 0)
    main, tail, hist_rows = pl.pallas_call(
        functools.partial(_inproj_conv_kernel, n_col=n_col, conv_tiles=conv_tiles, tiles_per_seq=tiles_per_seq),
        grid=(n_tiles + 1,),
        in_specs=[
            pl.BlockSpec((tm, d), lambda s: (cur(s) // n_col, 0)),
            pl.BlockSpec((1, d), lambda s: (0, 0)),
            pl.BlockSpec((None, d, tn), lambda s: (cur(s) % n_col, 0, 0)),
            pl.BlockSpec((d, LANES), lambda s: (0, 0)),
            pl.BlockSpec((n_col, CONV_TAPS, tn), lambda s: (0, 0, 0)),
            pl.BlockSpec((n_col, 1, tn), lambda s: (0, 0, 0)),
            pl.BlockSpec((n_col, m // seq, hist, tn), lambda s: (0, 0, 0, 0)),
        ],
        out_specs=[
            pl.BlockSpec((tm, tn), lambda s: (prv(s) // n_col, prv(s) % n_col)),
            pl.BlockSpec((tm, LANES), lambda s: (cur(s) // n_col, 0)),
            pl.BlockSpec((n_col, n_row, hist, tn), lambda s: (0, 0, 0, 0)),
        ],
        out_shape=[jax.ShapeDtypeStruct((m, n), BF16), jax.ShapeDtypeStruct((m, LANES), F32),
                   jax.ShapeDtypeStruct((n_col, n_row, hist, tn), F32)],
        scratch_shapes=[pltpu.VMEM((tm, d), BF16), pltpu.VMEM((CONV_BASE + tm, tn), F32),
                        pltpu.VMEM((CONV_BASE + tm, tn), F32), pltpu.VMEM((n_col, hist, tn), F32)],
        compiler_params=_cparams("arbitrary"),
        name="inproj_conv",
    )(x, g.reshape(1, d), by_tile(w_main), w_tail, by_tile(conv_w), by_tile(conv_b), by_tile(conv0))
    hist_rows = jnp.moveaxis(hist_rows, 0, -2).reshape(n_row, hist, n)
    return main, tail, hist_rows[tiles_per_seq - 1::tiles_per_seq]


def _mix_ffn_kernel(*refs, n_in):
    a_refs, w_refs = refs[:n_in], refs[n_in:2 * n_in]
    x_ref, gm_ref, g1_ref, wg_ref, wu_ref, wd_ref, g2_ref, o_ref = refs[2 * n_in:]
    mixed = None
    for a_ref, w_ref in zip(a_refs, w_refs):
        t = _dot(a_ref[...].astype(BF16), w_ref[...])
        mixed = t if mixed is None else mixed + t
    x = x_ref[...] + _rms(mixed) * gm_ref[...]
    h = (_rms(x) * g1_ref[...]).astype(BF16)
    a = jax.nn.silu(_dot(h, wg_ref[...])) * _dot(h, wu_ref[...])
    o_ref[...] = x + _rms(_dot(a.astype(BF16), wd_ref[...])) * g2_ref[...]


def mix_ffn(acts, w_outs, x, g_mix, g1, wg, wu, wd, g2, *, tm):
    m, d = x.shape
    n_in = len(acts)
    resident = lambda a: pl.BlockSpec(a.shape, lambda i: (0, 0), pipeline_mode=pl.Buffered(1))
    vec = pl.BlockSpec((1, d), lambda i: (0, 0))
    return pl.pallas_call(
        functools.partial(_mix_ffn_kernel, n_in=n_in),
        grid=(m // tm,),
        in_specs=[pl.BlockSpec((tm, a.shape[1]), lambda i: (i, 0)) for a in acts]
        + [resident(w) for w in w_outs]
        + [pl.BlockSpec((tm, d), lambda i: (i, 0)), vec, vec, resident(wg), resident(wu), resident(wd), vec],
        out_specs=pl.BlockSpec((tm, d), lambda i: (i, 0)),
        out_shape=jax.ShapeDtypeStruct((m, d), F32),
        compiler_params=_cparams("parallel"),
        name="mix_ffn",
    )(*acts, *w_outs, x, g_mix.reshape(1, d), g1.reshape(1, d), wg, wu, wd, g2.reshape(1, d))


def _rel_buckets(dist):
    max_exact = REL_BUCKETS // 2
    n = np.maximum(dist, 1).astype(np.float32)
    large = max_exact + (np.log(n / max_exact) / math.log(REL_MAX_DIST / max_exact)
                         * (REL_BUCKETS - max_exact)).astype(np.int32)
    large = np.minimum(large, REL_BUCKETS - 1)
    return np.where(dist < max_exact, dist, large).astype(np.int32)


def _attn_bias_rows(rel_bias):
    u = np.arange(2 * A_BAND)
    valid = u <= A_BAND
    rows = []
    for (_, d) in A_PATTERNS:
        b = rel_bias[_rel_buckets(np.where(valid, A_BAND - u, 0) * d)]
        rows.append(jnp.where(valid[:, None], b.astype(F32), NEG))
    tl = jnp.transpose(jnp.stack(rows), (2, 0, 1))
    tl = tl.reshape(A_HEADS // 2, 2, len(A_PATTERNS), 2 * A_BAND)
    tl = jnp.transpose(tl, (0, 2, 1, 3))[:, :, :, None, :]
    return jnp.broadcast_to(tl, tl.shape[:3] + (8, 2 * A_BAND))


def _attn_kernel(qin_ref, kin_ref, vin_ref, brow_ref, o_ref, *scratch, seq):
    n_pat = len(A_PATTERNS)
    m_refs, l_refs, acc_refs = scratch[0:n_pat], scratch[n_pat:2 * n_pat], scratch[2 * n_pat:3 * n_pat]
    q_ref, k_ref, v_ref, qc_ref, kc_ref, vc_ref, b_ref = scratch[3 * n_pat:]
    n_tiles = seq // A_BAND
    lane = _iota((A_BAND, LANES), 1)
    head0 = lane < A_HEAD_DIM
    q_ref[...] = qin_ref[...].astype(F32) * (A_HEAD_DIM ** -0.5)
    k_ref[...] = kin_ref[...].astype(F32)
    v_ref[...] = vin_ref[...].astype(F32)
    for p in range(len(A_PATTERNS)):
        for h in range(2):
            row = jnp.broadcast_to(brow_ref[0, p, h, 0:1, :], (A_BAND, 2 * A_BAND))
            b_ref[0, p, h] = pltpu.roll(row, 0, 1, stride=1, stride_axis=0)

    first_p = max(range(len(A_PATTERNS)), key=lambda p: A_PATTERNS[p][1])
    d_first = A_PATTERNS[first_p][1]
    class_len = seq // d_first

    def to_class_major(r, carry):
        dst = pl.ds(pl.multiple_of(r * class_len, class_len), class_len)
        for src_ref, dst_ref in ((q_ref, qc_ref), (k_ref, kc_ref), (v_ref, vc_ref)):
            dst_ref[dst, :] = src_ref[pl.ds(r, class_len, stride=d_first), :]
        return carry

    lax.fori_loop(0, d_first, to_class_major, 0)

    for p, (_, d) in enumerate(A_PATTERNS):
        tiles_per_class = n_tiles // d
        class_major = p == first_p

        def load_tile(idx, d=d, tiles_per_class=tiles_per_class, class_major=class_major):
            r = idx // tiles_per_class
            t = idx % tiles_per_class
            start = r + t * (d * A_BAND)
            has_prev = t > 0
            rows = pl.ds(start, A_BAND, stride=d) if d > 1 else pl.ds(pl.multiple_of(start, A_BAND), A_BAND)
            if class_major:
                base = r * class_len + t * A_BAND
                crow = pl.ds(pl.multiple_of(base, A_BAND), A_BAND)
                cprev = pl.ds(pl.multiple_of(jnp.where(has_prev, base - A_BAND, base), A_BAND), A_BAND)
                return dict(
                    rows=rows, has_prev=has_prev, q=qc_ref[crow, :],
                    k2=jnp.concatenate([kc_ref[cprev, :], kc_ref[crow, :]], axis=0).astype(BF16),
                    v2=jnp.concatenate([vc_ref[cprev, :], vc_ref[crow, :]], axis=0).astype(BF16))
            prev = jnp.where(has_prev, start - d * A_BAND, start)
            prows = pl.ds(prev, A_BAND, stride=d) if d > 1 else pl.ds(pl.multiple_of(prev, A_BAND), A_BAND)
            return dict(
                rows=rows, has_prev=has_prev, q=q_ref[rows, :],
                k2=jnp.concatenate([k_ref[prows, :], k_ref[rows, :]], axis=0).astype(BF16),
                v2=jnp.concatenate([v_ref[prows, :], v_ref[rows, :]], axis=0).astype(BF16))

        def tile_group(idx, carry, load_tile=load_tile, p=p):
            tiles = [load_tile(idx + i * (n_tiles // ATTN_GROUP)) for i in range(ATTN_GROUP)]
            chains = [(tile, h) for tile in tiles for h in range(2)]
            n_chain = range(len(chains))
            col = _iota((A_BAND, 2 * A_BAND), 1)
            wide = lambda x: jnp.broadcast_to(x, (A_BAND, LANES))
            qh = [jnp.where(head0 if h == 0 else jnp.logical_not(head0), tile["q"], 0.0).astype(BF16)
                  for tile, h in chains]
            s = [_dot_nt(qh[c], tile["k2"])
                 + jnp.where(jnp.logical_and(col < A_BAND, jnp.logical_not(tile["has_prev"])), NEG, b_ref[0, p, h])
                 for c, (tile, h) in enumerate(chains)]
            m_new = [wide(jnp.max(s[c], axis=-1, keepdims=True)) for c in n_chain]
            pr = [jnp.exp(s[c] - jnp.concatenate([m_new[c], m_new[c]], axis=1)) for c in n_chain]
            l_new = [wide(jnp.sum(pr[c], axis=-1, keepdims=True)) for c in n_chain]
            acc_new = [_dot(pr[c].astype(BF16), tile["v2"]) for c, (tile, h) in enumerate(chains)]
            for c, (tile, h) in enumerate(chains):
                if h == 1:
                    m_refs[p][tile["rows"], :] = jnp.where(head0, m_new[c - 1], m_new[c])
                    l_refs[p][tile["rows"], :] = jnp.where(head0, l_new[c - 1], l_new[c])
                    acc_refs[p][tile["rows"], :] = jnp.where(head0, acc_new[c - 1], acc_new[c])
            return carry

        lax.fori_loop(0, n_tiles // ATTN_GROUP, tile_group, 0)

    m_all = functools.reduce(jnp.maximum, [m_ref[...] for m_ref in m_refs])
    wts = [jnp.exp(m_ref[...] - m_all) for m_ref in m_refs]
    num = functools.reduce(jnp.add, [w * acc_ref[...] for w, acc_ref in zip(wts, acc_refs)])
    den = functools.reduce(jnp.add, [w * l_ref[...] for w, l_ref in zip(wts, l_refs)])
    o_ref[...] = (num / den).astype(o_ref.dtype)


def attention_prompt(proj, bias_tiles, *, bsz, seq):
    hp = A_HEADS // 2
    qb, kb, vb = HYB_Q0 // LANES, HYB_K0 // LANES, HYB_V0 // LANES
    return pl.pallas_call(
        functools.partial(_attn_kernel, seq=seq),
        grid=(bsz, hp),
        in_specs=[
            pl.BlockSpec((None, seq, LANES), lambda b, h: (b, 0, qb + h)),
            pl.BlockSpec((None, seq, LANES), lambda b, h: (b, 0, kb + h)),
            pl.BlockSpec((None, seq, LANES), lambda b, h: (b, 0, vb + h)),
            pl.BlockSpec((1,) + bias_tiles.shape[1:], lambda b, h: (h, 0, 0, 0, 0)),
        ],
        out_specs=pl.BlockSpec((None, seq, LANES), lambda b, h: (b, 0, h)),
        out_shape=jax.ShapeDtypeStruct((bsz, seq, A_WIDTH), BF16),
        scratch_shapes=[pltpu.VMEM((seq, LANES), F32)] * (3 * len(A_PATTERNS) + 6)
        + [pltpu.VMEM((1, len(A_PATTERNS), 2, A_BAND, 2 * A_BAND), F32)],
        compiler_params=_cparams("parallel", "parallel"),
        name="attn_prompt",
    )(proj, proj, proj, bias_tiles)


def _attn_logw(rel_bias, past):
    dist = np.arange(past + 1)
    count = np.zeros(past + 1, np.float64)
    for (w, d) in A_PATTERNS:
        count += ((dist % d == 0) & (dist <= w)).astype(np.float64)
    logc = np.where(count > 0, np.log(np.maximum(count, 1.0)), 0.0).astype(np.float32)
    lw = rel_bias[_rel_buckets(dist)].astype(F32).T + logc[None, :]
    return jnp.where((count > 0)[None, :], lw, NEG)


def _step_scores(xk_ref, q_col, kn_col, lw_ref, lw0_ref, s_ref):
    qs = q_col * (A_HEAD_DIM ** -0.5)
    s_new = []
    for h in range(A_HEADS):
        rows = slice(h * A_HEAD_DIM, (h + 1) * A_HEAD_DIM)
        s_ref[h:h + 1, :] = jnp.sum(xk_ref[0, 0, rows, :] * qs[rows], axis=0, keepdims=True)
        s_new.append(jnp.sum(kn_col[rows] * qs[rows], axis=0, keepdims=True))
    s = s_ref[...] + lw_ref[...]
    s_new = jnp.concatenate(s_new, axis=0) + lw0_ref[:, :1]
    m = jnp.maximum(jnp.max(s, axis=-1, keepdims=True), s_new)
    p = jnp.exp(s - m)
    p_new = jnp.exp(s_new - m)
    den = jnp.sum(p, axis=-1, keepdims=True) + p_new
    return p, p_new, den


def _step_output(xv_ref, vn_col, p, p_new, den, o_ref):
    for h in range(A_HEADS):
        rows = slice(h * A_HEAD_DIM, (h + 1) * A_HEAD_DIM)
        pv = jnp.sum(xv_ref[0, 0, rows, :] * p[h:h + 1, :], axis=-1, keepdims=True)
        o_ref[0, rows, :] = (pv + p_new[h:h + 1, :] * vn_col[rows]) / den[h:h + 1, :]


def _attn_step_roll_kernel(q_ref, kn_ref, vn_ref, lw_ref, lw0_ref, xk_ref, xv_ref, o_ref, ko_ref, vo_ref, s_ref,
                           *, layer, past):
    is_layer = pl.program_id(0) == layer
    newest = _iota((A_HEAD_DIM, past), 1) == past - 1
    for x_ref, n_ref, out_ref in ((xk_ref, kn_ref, ko_ref), (xv_ref, vn_ref, vo_ref)):
        for h in range(A_HEADS):
            rows = slice(h * A_HEAD_DIM, (h + 1) * A_HEAD_DIM)
            rolled = pltpu.roll(x_ref[0, 0, rows, :], past - 1, 1)
            out_ref[0, 0, rows, :] = jnp.where(jnp.logical_and(newest, is_layer), n_ref[0, rows, :], rolled)

    @pl.when(is_layer)
    def _():
        p, p_new, den = _step_scores(xk_ref, q_ref[0], kn_ref[0], lw_ref, lw0_ref, s_ref)
        _step_output(xv_ref, vn_ref[0], p, p_new, den, o_ref.at[0])

    @pl.when(jnp.logical_not(is_layer))
    def _():
        o_ref[...] = jnp.zeros_like(o_ref)


def _attn_step_append_kernel(q_ref, kn_ref, vn_ref, lw_ref, lw0_ref, xk_ref, xv_ref, ko_in, vo_in,
                             o_ref, ko_ref, vo_ref, s_ref, *, past):
    del ko_in, vo_in
    p, p_new, den = _step_scores(xk_ref, q_ref[0], kn_ref[0], lw_ref, lw0_ref, s_ref)
    _step_output(xv_ref, vn_ref[0], p, p_new, den, o_ref)
    newest = _iota((A_WIDTH, LANES), 1) == LANES - 1
    for x_ref, n_ref, out_ref in ((xk_ref, kn_ref, ko_ref), (xv_ref, vn_ref, vo_ref)):
        rolled = pltpu.roll(x_ref[0, 0, :, past - LANES:past], LANES - 1, 1)
        out_ref[0, 0] = jnp.where(newest, n_ref[0], rolled)


def attention_step(q_col, kn_col, vn_col, cache_k, cache_v, rolled, logw, *, layer):
    n_layers, bsz, w, past = cache_k.shape
    lw_cache = logw[:, past:0:-1]
    lw_new = jnp.broadcast_to(logw[:, :1], (A_HEADS, LANES))
    out_shape = [jax.ShapeDtypeStruct((bsz, w, 1), F32),
                 jax.ShapeDtypeStruct(cache_k.shape, cache_k.dtype),
                 jax.ShapeDtypeStruct(cache_v.shape, cache_v.dtype)]
    scratch = [pltpu.VMEM((A_HEADS, past), F32)]
    if rolled is None:
        assert layer == 0
        col = pl.BlockSpec((1, w, 1), lambda l, b: (b, 0, 0))
        win = pl.BlockSpec((1, 1, w, past), lambda l, b: (l, b, 0, 0))
        o_all, rolled_k, rolled_v = pl.pallas_call(
            functools.partial(_attn_step_roll_kernel, layer=layer, past=past),
            grid=(n_layers, bsz),
            in_specs=[col, col, col, _small(lw_cache), _small(lw_new), win, win],
            out_specs=[pl.BlockSpec((1, 1, w, 1), lambda l, b: (l, b, 0, 0)), win, win],
            out_shape=[jax.ShapeDtypeStruct((n_layers, bsz, w, 1), F32)] + out_shape[1:],
            scratch_shapes=scratch,
            compiler_params=_cparams("arbitrary", "arbitrary"),
            name="attn_step_roll",
        )(q_col, kn_col, vn_col, lw_cache, lw_new, cache_k, cache_v)
        return o_all[layer], rolled_k, rolled_v
    col = pl.BlockSpec((1, w, 1), lambda b: (b, 0, 0))
    win = pl.BlockSpec((1, 1, w, past), lambda b: (layer, b, 0, 0))
    tail = pl.BlockSpec((1, 1, w, LANES), lambda b: (layer, b, 0, past // LANES - 1))
    return pl.pallas_call(
        functools.partial(_attn_step_append_kernel, past=past),
        grid=(bsz,),
        in_specs=[col, col, col, _small(lw_cache), _small(lw_new), win, win, tail, tail],
        out_specs=[col, tail, tail],
        out_shape=out_shape,
        scratch_shapes=scratch,
        input_output_aliases={7: 1, 8: 2},
        compiler_params=_cparams("arbitrary"),
        name="attn_step_append",
    )(q_col, kn_col, vn_col, lw_cache, lw_new, cache_k, cache_v, *rolled)


def _group_rms(y, w):
    half = SSM_D_INNER // SSM_GROUPS
    return [_rms(y[:, g * half:(g + 1) * half]) * w[:, g * half:(g + 1) * half] for g in range(SSM_GROUPS)]


def _ssd_kernel(z_ref, xs_ref, bc_ref, dt_ref, dtb_ref, alog_ref, d_ref, nw_ref, h0_ref, y_ref, h_ref, ys_ref):
    c = pl.program_id(1)
    ch = SSM_CHUNK

    @pl.when(c == 0)
    def _():
        h_ref[...] = h0_ref[...]

    xs = xs_ref[0].astype(F32)
    bc = bc_ref[0].astype(F32)

    dt = jax.nn.softplus(dt_ref[0] + dtb_ref[...])
    da = dt * (-jnp.exp(alog_ref[...]))
    row = _iota((ch, ch), 0)
    colv = _iota((ch, ch), 1)
    tril = (row >= colv).astype(F32)
    cs = _dot(tril, da, precision=HIGHEST)
    cs_t = cs.T
    causal = row >= colv

    heads = range(SSM_HEADS)
    hpg = SSM_HEADS // SSM_GROUPS
    bm = [bc[:, g * SSM_STATE:(g + 1) * SSM_STATE] for g in range(SSM_GROUPS)]
    cm = [bc[:, (SSM_GROUPS + g) * SSM_STATE:(SSM_GROUPS + g + 1) * SSM_STATE] for g in range(SSM_GROUPS)]
    cb = [_dot_nt(cm[g], bm[g]) for g in range(SSM_GROUPS)]
    dt_t = dt.T
    xs_t = xs.T
    w_t = dt_t * jnp.exp(cs_t[:, ch - 1:ch] - cs_t)
    e_last = jnp.exp(cs[ch - 1:ch, :])
    lane_lo = _iota((ch, LANES), 1) < SSM_HEAD_DIM
    row_lo = _iota((LANES, ch), 0) < SSM_HEAD_DIM
    pairs = range(SSM_HEADS // 2)
    csb = [jnp.broadcast_to(cs[:, h:h + 1], (ch, ch)) for h in heads]
    mix = [cb[h // hpg] * jnp.exp(jnp.where(causal, csb[h] - cs_t[h:h + 1, :], NEG)) * dt_t[h:h + 1, :]
           for h in heads]
    x_pair = [xs[:, j * LANES:(j + 1) * LANES] for j in pairs]
    y_intra = [jnp.where(lane_lo, _dot(mix[2 * j], x_pair[j]), _dot(mix[2 * j + 1], x_pair[j])) for j in pairs]
    h_pair = [h_ref[0, 2 * j:2 * j + 2].reshape(2 * SSM_HEAD_DIM, SSM_STATE) for j in pairs]
    y_inter = [_dot_nt(cm[2 * j // hpg], h_pair[j]) * jnp.exp(jnp.where(lane_lo, csb[2 * j], csb[2 * j + 1]))
               for j in pairs]
    xw_t = [xs_t[j * LANES:(j + 1) * LANES, :] * jnp.where(row_lo, w_t[2 * j:2 * j + 1, :], w_t[2 * j + 1:2 * j + 2, :])
            for j in pairs]
    st = [_dot(xw_t[j], bm[2 * j // hpg]) for j in pairs]
    for j in pairs:
        cols = slice(j * LANES, (j + 1) * LANES)
        ys_ref[:, cols] = y_intra[j] + y_inter[j] + d_ref[:, cols] * x_pair[j]
        decay = jnp.where(row_lo, e_last[:, 2 * j:2 * j + 1], e_last[:, 2 * j + 1:2 * j + 2])
        h_ref[0, 2 * j:2 * j + 2] = (h_pair[j] * decay + st[j]).reshape(2, SSM_HEAD_DIM, SSM_STATE)

    y = ys_ref[...] * jax.nn.silu(z_ref[0].astype(F32))
    half = SSM_D_INNER // SSM_GROUPS
    for g, yg in enumerate(_group_rms(y, nw_ref[...])):
        y_ref[0, :, g * half:(g + 1) * half] = yg.astype(y_ref.dtype)


def _small(a):
    return pl.BlockSpec(a.shape, lambda *_: (0,) * a.ndim)


def ssd_prompt(main, tail, h0, dt_bias, a_log, d_skip, norm_w, *, bsz, seq):
    nc = seq // SSM_CHUNK
    ch = SSM_CHUNK
    small = [_pad_tail(dt_bias.reshape(1, -1)), _pad_tail(a_log.reshape(1, -1)),
             jnp.repeat(d_skip, SSM_HEAD_DIM).reshape(1, -1), norm_w.reshape(1, -1)]
    return pl.pallas_call(
        _ssd_kernel,
        grid=(bsz, nc),
        in_specs=[
            pl.BlockSpec((1, ch, SSM_D_INNER), lambda b, c: (b, c, 0)),
            pl.BlockSpec((1, ch, SSM_D_INNER), lambda b, c: (b, c, 1)),
            pl.BlockSpec((1, ch, SSM_BC), lambda b, c: (b, c, HYB_BC0 // SSM_BC)),
            pl.BlockSpec((1, ch, LANES), lambda b, c: (b, c, 0)),
        ] + [_small(a) for a in small] + [
            pl.BlockSpec((1, SSM_HEADS, SSM_HEAD_DIM, SSM_STATE), lambda b, c: (b, 0, 0, 0)),
        ],
        out_specs=[
            pl.BlockSpec((1, ch, SSM_D_INNER), lambda b, c: (b, c, 0)),
            pl.BlockSpec((1, SSM_HEADS, SSM_HEAD_DIM, SSM_STATE), lambda b, c: (b, 0, 0, 0)),
        ],
        out_shape=[jax.ShapeDtypeStruct((bsz, seq, SSM_D_INNER), BF16),
                   jax.ShapeDtypeStruct((bsz, SSM_HEADS, SSM_HEAD_DIM, SSM_STATE), F32)],
        scratch_shapes=[pltpu.VMEM((ch, SSM_D_INNER), F32)],
        compiler_params=_cparams("parallel", "arbitrary"),
        name="ssd_prompt",
    )(main, main, main, tail, *small, h0)


def _row_to_col(row, eye):
    return jnp.sum(jnp.where(eye, row, 0.0), axis=1, keepdims=True)


def _col_to_row(col, eye):
    return jnp.sum(jnp.where(eye, col, 0.0), axis=0, keepdims=True)


def _conv_step(c0_ref, w_ref, x_row, c0, c1):
    acc = w_ref[SSM_CONV - 1:SSM_CONV, c0:c1] * x_row
    for i in range(SSM_CONV - 1):
        acc = acc + w_ref[i:i + 1, c0:c1] * c0_ref[0, i:i + 1, c0:c1]
    return acc


def _ssd_step_kernel(z_ref, xs_ref, bc_ref, dt_ref, c0_ref, cw_ref, cb_ref, dtb_ref, alog_ref, d_ref, nw_ref,
                     h0_ref, y_ref, co_ref, h_ref, ys_ref):
    xs_raw = xs_ref[0]
    bc_raw = bc_ref[0]
    xs = jax.nn.silu(_conv_step(c0_ref, cw_ref, xs_raw, 0, SSM_D_INNER) + cb_ref[:, 0:SSM_D_INNER])
    bc = jax.nn.silu(_conv_step(c0_ref, cw_ref, bc_raw, SSM_D_INNER, SSM_XBC) + cb_ref[:, SSM_D_INNER:SSM_XBC])
    co_ref[0, 0:SSM_CONV - 2, :] = c0_ref[0, 1:SSM_CONV - 1, :]
    co_ref[0, SSM_CONV - 2:SSM_CONV - 1, 0:SSM_D_INNER] = xs_raw
    co_ref[0, SSM_CONV - 2:SSM_CONV - 1, SSM_D_INNER:SSM_XBC] = bc_raw

    dt = jax.nn.softplus(dt_ref[0] + dtb_ref[...])
    dec = jnp.exp(dt * (-jnp.exp(alog_ref[...])))
    dskip = d_ref[...]
    eye = _iota((LANES, LANES), 0) == _iota((LANES, LANES), 1)
    upper = _iota((LANES, 1), 0) >= SSM_HEAD_DIM
    pairs = range(SSM_HEADS // 2)
    grp = [2 * j // (SSM_HEADS // SSM_GROUPS) for j in pairs]
    bm = [bc[:, g * SSM_STATE:(g + 1) * SSM_STATE] for g in grp]
    cm = [bc[:, (SSM_GROUPS + g) * SSM_STATE:(SSM_GROUPS + g + 1) * SSM_STATE] for g in grp]
    pick = lambda v, j: jnp.where(upper, v[:, 2 * j + 1:2 * j + 2], v[:, 2 * j:2 * j + 1])
    x_col = [_row_to_col(xs[:, j * LANES:(j + 1) * LANES], eye) for j in pairs]
    hp = [h0_ref[0, 2 * j:2 * j + 2].reshape(2 * SSM_HEAD_DIM, SSM_STATE) for j in pairs]
    hn = [hp[j] * pick(dec, j) + (x_col[j] * pick(dt, j)) * bm[j] for j in pairs]
    y_col = [jnp.sum(hn[j] * cm[j], axis=1, keepdims=True) + pick(dskip, j) * x_col[j] for j in pairs]
    y_row = [_col_to_row(y_col[j], eye) for j in pairs]
    for j in pairs:
        h_ref[0, 2 * j:2 * j + 2] = hn[j].reshape(2, SSM_HEAD_DIM, SSM_STATE)
        ys_ref[:, j * LANES:(j + 1) * LANES] = y_row[j]

    y = ys_ref[...] * jax.nn.silu(z_ref[0])
    half = SSM_D_INNER // SSM_GROUPS
    for g, yg in enumerate(_group_rms(y, nw_ref[...])):
        y_ref[0, :, g * half:(g + 1) * half] = yg


def ssd_step(main, tail, conv0, h0, conv_w, conv_b, dt_bias, a_log, d_skip, norm_w):
    bsz = main.shape[0]
    small = [conv_w, conv_b.reshape(1, -1), _pad_tail(dt_bias.reshape(1, -1)), _pad_tail(a_log.reshape(1, -1)),
             _pad_tail(d_skip.reshape(1, -1)), norm_w.reshape(1, -1)]
    hspec = pl.BlockSpec((1, SSM_HEADS, SSM_HEAD_DIM, SSM_STATE), lambda b: (b, 0, 0, 0))
    cspec = pl.BlockSpec((1, SSM_CONV - 1, SSM_XBC), lambda b: (b, 0, 0))
    return pl.pallas_call(
        _ssd_step_kernel,
        grid=(bsz,),
        in_specs=[
            pl.BlockSpec((1, 1, SSM_D_INNER), lambda b: (b, 0, 0)),
            pl.BlockSpec((1, 1, SSM_D_INNER), lambda b: (b, 0, 1)),
            pl.BlockSpec((1, 1, SSM_BC), lambda b: (b, 0, HYB_BC0 // SSM_BC)),
            pl.BlockSpec((1, 1, LANES), lambda b: (b, 0, 0)),
            cspec,
        ] + [_small(a) for a in small] + [hspec],
        out_specs=[pl.BlockSpec((1, 1, SSM_D_INNER), lambda b: (b, 0, 0)), cspec, hspec],
        out_shape=[jax.ShapeDtypeStruct((bsz, 1, SSM_D_INNER), F32),
                   jax.ShapeDtypeStruct(conv0.shape, F32),
                   jax.ShapeDtypeStruct(h0.shape, F32)],
        scratch_shapes=[pltpu.VMEM((1, SSM_D_INNER), F32)],
        compiler_params=_cparams("parallel"),
        name="ssd_step",
    )(main, main, main, tail, conv0, *small, h0)


def _l2norm(x):
    return x * lax.rsqrt(jnp.sum(x * x, axis=-1, keepdims=True) + EPS)


def _unit_lower_inverse(ns, eye):
    size = ns[0].shape[0]
    ps = [eye - n for n in ns]
    ms = [_dot(n, n) for n in ns]
    power = 2
    while 2 * power < size:
        pms = [_dot(jnp.concatenate([p, m], axis=0), m) for p, m in zip(ps, ms)]
        ps = [p + pm[:size] for p, pm in zip(ps, pms)]
        ms = [pm[size:] for pm in pms]
        power *= 2
    return [p + _dot(p, m) for p, m in zip(ps, ms)]


def _gdn_gates(ba, dtb_ref, alog_ref):
    beta = jax.nn.sigmoid(ba)
    g = -jnp.exp(alog_ref[...]) * jax.nn.softplus(ba + dtb_ref[...])
    return beta, g


def _gdn_kernel(q_ref, k_ref, v_ref, z_ref, ba_ref, dtb_ref, alog_ref, nw_ref, s0_ref, o_ref, s_ref):
    c = pl.program_id(1)
    ch = GDN_CHUNK
    nh = GDN_V_HEADS

    @pl.when(c == 0)
    def _():
        s_ref[...] = s0_ref[...]

    row = _iota((ch, ch), 0)
    colv = _iota((ch, ch), 1)
    incl = row >= colv
    strict = row > colv
    eye = (row == colv).astype(F32)
    rep = nh // GDN_QK_HEADS
    heads = range(nh)
    qk_heads = range(GDN_QK_HEADS)

    beta, g = _gdn_gates(ba_ref[0], dtb_ref, alog_ref)
    gcum = _dot(incl.astype(F32), g, precision=HIGHEST)
    gcum_t = jnp.concatenate([gcum, jnp.zeros((LANES - ch, LANES), F32)], axis=0).T
    qn = [_l2norm(q_ref[0, :, j * GDN_DK:(j + 1) * GDN_DK].astype(F32)) * (GDN_DK ** -0.5) for j in qk_heads]
    kn = [_l2norm(k_ref[0, :, j * GDN_DK:(j + 1) * GDN_DK].astype(F32)) for j in qk_heads]
    kk = [_dot_nt(kn[j], kn[j]) for j in qk_heads]
    qk = [_dot_nt(qn[j], kn[j]) for j in qk_heads]
    gc_col = [gcum[:, nh + h:nh + h + 1] for h in heads]
    gc_last = [gcum[ch - 1:ch, nh + h:nh + h + 1] for h in heads]
    beta_col = [beta[:, h:h + 1] for h in heads]
    dec = [jnp.exp(jnp.where(incl, gc_col[h] - gcum_t[nh + h:nh + h + 1, 0:ch], NEG)) for h in heads]
    t_inv = _unit_lower_inverse(
        [jnp.where(strict, kk[h // rep] * dec[h], 0.0) * beta_col[h] for h in heads], eye)
    eg = [jnp.exp(gc_col[h]) for h in heads]
    lhs = [jnp.concatenate([kn[h // rep] * (beta_col[h] * eg[h]), qn[h // rep] * eg[h]], axis=0) for h in heads]
    vb = [v_ref[0, :, h * GDN_DV:(h + 1) * GDN_DV].astype(F32) * beta_col[h] for h in heads]
    attn = [qk[h // rep] * dec[h] for h in heads]
    kdec = [kn[h // rep] * jnp.exp(gc_last[h] - gc_col[h]) for h in heads]

    s_prev = [s_ref[0, h] for h in heads]
    both = [_dot(lhs[h], s_prev[h]) for h in heads]
    u = [_dot(t_inv[h], vb[h] - both[h][:ch]) for h in heads]
    o = [both[h][ch:] + _dot(attn[h], u[h]) for h in heads]
    s_new = [s_prev[h] * jnp.exp(gc_last[h]) + _dot_tn(kdec[h], u[h]) for h in heads]
    for h in heads:
        s_ref[0, h] = s_new[h]
        z_h = z_ref[0, :, h * GDN_DV:(h + 1) * GDN_DV].astype(F32)
        o_ref[0, :, h * GDN_DV:(h + 1) * GDN_DV] = (
            _rms(o[h]) * nw_ref[...] * jax.nn.silu(z_h)).astype(o_ref.dtype)


def _gdn_gate_params(dt_bias, a_log):
    nh = GDN_V_HEADS
    dtb = jnp.zeros((1, LANES), F32).at[0, nh:2 * nh].set(dt_bias)
    alog = jnp.zeros((1, LANES), F32).at[0, nh:2 * nh].set(a_log)
    return dtb, alog


def gdn_prompt(main, tail, s0, dt_bias, a_log, norm_w, *, bsz, seq):
    ch = GDN_CHUNK
    assert seq % ch == 0
    nc = seq // ch
    dtb, alog = _gdn_gate_params(dt_bias, a_log)
    small = [dtb, alog, norm_w.reshape(1, -1)]
    sspec = pl.BlockSpec((1, GDN_V_HEADS, GDN_DK, GDN_DV), lambda b, c: (b, 0, 0, 0))
    return pl.pallas_call(
        _gdn_kernel,
        grid=(bsz, nc),
        in_specs=[
            pl.BlockSpec((1, ch, GDN_QK_W), lambda b, c: (b, c, 0)),
            pl.BlockSpec((1, ch, GDN_QK_W), lambda b, c: (b, c, 1)),
            pl.BlockSpec((1, ch, GDN_VW), lambda b, c: (b, c, 1)),
            pl.BlockSpec((1, ch, GDN_VW), lambda b, c: (b, c, 2)),
            pl.BlockSpec((1, ch, LANES), lambda b, c: (b, c, 0)),
        ] + [_small(a) for a in small] + [sspec],
        out_specs=[pl.BlockSpec((1, ch, GDN_VW), lambda b, c: (b, c, 0)), sspec],
        out_shape=[jax.ShapeDtypeStruct((bsz, seq, GDN_VW), BF16),
                   jax.ShapeDtypeStruct((bsz, GDN_V_HEADS, GDN_DK, GDN_DV), F32)],
        compiler_params=_cparams("parallel", "arbitrary"),
        name="gdn_prompt",
    )(main, main, main, main, tail, *small, s0)


def _gdn_conv_step(c0_ref, w_ref, x_row, c0, c1):
    acc = w_ref[GDN_CONV - 1:GDN_CONV, c0:c1] * x_row
    for i in range(GDN_CONV - 1):
        acc = acc + w_ref[i:i + 1, c0:c1] * c0_ref[0, i:i + 1, c0:c1]
    return acc


def _gdn_step_kernel(q_ref, k_ref, v_ref, z_ref, ba_ref, c0_ref, cw_ref, dtb_ref, alog_ref, nw_ref, s0_ref,
                     o_ref, co_ref, s_ref):
    nh = GDN_V_HEADS
    q_raw, k_raw, v_raw = q_ref[0], k_ref[0], v_ref[0]
    q = jax.nn.silu(_gdn_conv_step(c0_ref, cw_ref, q_raw, 0, GDN_QK_W))
    k = jax.nn.silu(_gdn_conv_step(c0_ref, cw_ref, k_raw, GDN_QK_W, 2 * GDN_QK_W))
    v = jax.nn.silu(_gdn_conv_step(c0_ref, cw_ref, v_raw, 2 * GDN_QK_W, GDN_QKV))
    co_ref[0, 0:GDN_CONV - 2, :] = c0_ref[0, 1:GDN_CONV - 1, :]
    co_ref[0, GDN_CONV - 2:GDN_CONV - 1, 0:GDN_QK_W] = q_raw
    co_ref[0, GDN_CONV - 2:GDN_CONV - 1, GDN_QK_W:2 * GDN_QK_W] = k_raw
    co_ref[0, GDN_CONV - 2:GDN_CONV - 1, 2 * GDN_QK_W:GDN_QKV] = v_raw

    beta, g = _gdn_gates(ba_ref[0], dtb_ref, alog_ref)
    eg_all = jnp.exp(g)
    eye = _iota((LANES, LANES), 0) == _iota((LANES, LANES), 1)
    rep = nh // GDN_QK_HEADS
    heads = range(nh)
    qn = [_l2norm(q[:, j * GDN_DK:(j + 1) * GDN_DK]) * (GDN_DK ** -0.5) for j in range(GDN_QK_HEADS)]
    kn = [_l2norm(k[:, j * GDN_DK:(j + 1) * GDN_DK]) for j in range(GDN_QK_HEADS)]
    qk = [jnp.sum(a * b, axis=-1, keepdims=True) for a, b in zip(qn, kn)]
    q_col = [_row_to_col(a, eye) for a in qn]
    k_col = [_row_to_col(a, eye) for a in kn]
    b_h = [beta[:, h:h + 1] for h in heads]
    eg = [eg_all[:, nh + h:nh + h + 1] for h in heads]
    s_prev = [s0_ref[0, h] for h in heads]
    ks = [jnp.sum(s_prev[h] * k_col[h // rep], axis=0, keepdims=True) for h in heads]
    qs = [jnp.sum(s_prev[h] * q_col[h // rep], axis=0, keepdims=True) for h in heads]
    u = [v[:, h * GDN_DV:(h + 1) * GDN_DV] * b_h[h] - (b_h[h] * eg[h]) * ks[h] for h in heads]
    o = [eg[h] * qs[h] + qk[h // rep] * u[h] for h in heads]
    for h in heads:
        s_ref[0, h] = s_prev[h] * eg[h] + k_col[h // rep] * u[h]
        z_h = z_ref[0, :, h * GDN_DV:(h + 1) * GDN_DV]
        o_ref[0, :, h * GDN_DV:(h + 1) * GDN_DV] = _rms(o[h]) * nw_ref[...] * jax.nn.silu(z_h)


def gdn_step(main, tail, conv0, s0, conv_w, dt_bias, a_log, norm_w):
    bsz = main.shape[0]
    dtb, alog = _gdn_gate_params(dt_bias, a_log)
    small = [conv_w, dtb, alog, norm_w.reshape(1, -1)]
    sspec = pl.BlockSpec((1, GDN_V_HEADS, GDN_DK, GDN_DV), lambda b: (b, 0, 0, 0))
    cspec = pl.BlockSpec((1, GDN_CONV - 1, GDN_QKV), lambda b: (b, 0, 0))
    return pl.pallas_call(
        _gdn_step_kernel,
        grid=(bsz,),
        in_specs=[
            pl.BlockSpec((1, 1, GDN_QK_W), lambda b: (b, 0, 0)),
            pl.BlockSpec((1, 1, GDN_QK_W), lambda b: (b, 0, 1)),
            pl.BlockSpec((1, 1, GDN_VW), lambda b: (b, 0, 1)),
            pl.BlockSpec((1, 1, GDN_VW), lambda b: (b, 0, 2)),
            pl.BlockSpec((1, 1, LANES), lambda b: (b, 0, 0)),
            cspec,
        ] + [_small(a) for a in small] + [sspec],
        out_specs=[pl.BlockSpec((1, 1, GDN_VW), lambda b: (b, 0, 0)), cspec, sspec],
        out_shape=[jax.ShapeDtypeStruct((bsz, 1, GDN_VW), F32),
                   jax.ShapeDtypeStruct(conv0.shape, F32),
                   jax.ShapeDtypeStruct(s0.shape, F32)],
        compiler_params=_cparams("parallel"),
        name="gdn_step",
    )(main, main, main, main, tail, conv0, *small, s0)


def _pad_tail(w):
    return jnp.pad(w, ((0, 0), (0, LANES - w.shape[1])))


def _prep_hyb_in(w):
    a = A_WIDTH
    q, k, v = w[:, 0:a], w[:, a:2 * a], w[:, 2 * a:3 * a]
    z = w[:, 3 * a:3 * a + SSM_D_INNER]
    x0 = 3 * a + SSM_D_INNER
    xs = w[:, x0:x0 + SSM_D_INNER]
    bc = w[:, x0 + SSM_D_INNER:x0 + SSM_XBC]
    dt = w[:, x0 + SSM_XBC:]
    return jnp.concatenate([z, xs, q, k, v, bc], axis=1).astype(BF16), _pad_tail(dt).astype(BF16)


def _prep_gdn_in(w):
    return w[:, :GDN_MAIN].astype(BF16), _pad_tail(w[:, GDN_MAIN:]).astype(BF16)


HYB_CONV_COLS = ((SSM_D_INNER, 2 * SSM_D_INNER), (HYB_BC0, HYB_MAIN))


def _hyb_cols(a):
    out = jnp.zeros(a.shape[:-1] + (HYB_MAIN,), F32)
    (x0, x1), (b0, b1) = HYB_CONV_COLS
    return out.at[..., x0:x1].set(a[..., :SSM_D_INNER]).at[..., b0:b1].set(a[..., SSM_D_INNER:])


def _window_to_lanes(c):
    n, b, past, h, dh = c.shape
    return jnp.transpose(c, (0, 1, 3, 4, 2)).reshape(n, b, h * dh, past)


def _window_from_lanes(c):
    n, b, _, past = c.shape
    return jnp.transpose(c.reshape(n, b, A_HEADS, A_HEAD_DIM, past), (0, 1, 4, 2, 3))


def _row_tile(m, cap):
    return m if m <= cap else cap


def kernel(x_prompt, x_sample, cache_attn_k, cache_attn_v, state_ssm_conv, state_ssm, state_gdn_conv, state_gdn, rel_bias, norm_mix_pre, norm_mix_post, norm_ffn_pre, norm_ffn_post, w_hyb_in, ssm_conv_w, ssm_conv_b, ssm_dt_bias, ssm_a_log, ssm_d, ssm_norm_w, w_hyb_out, w_gdn_in, gdn_conv_w, gdn_dt_bias, gdn_a_log, gdn_norm_w, w_gdn_out, w_ffn_gate, w_ffn_up, w_ffn_down):
    depth = norm_mix_pre.shape[0]
    d_model = x_prompt.shape[-1]
    n_hyb, n_gdn = w_hyb_in.shape[0], w_gdn_in.shape[0]

    hyb_in = [_prep_hyb_in(w_hyb_in[i]) for i in range(n_hyb)]
    hyb_out = [(w_hyb_out[i, :A_WIDTH].astype(BF16), w_hyb_out[i, A_WIDTH:].astype(BF16)) for i in range(n_hyb)]
    gdn_in = [_prep_gdn_in(w_gdn_in[i]) for i in range(n_gdn)]
    gdn_out = [w_gdn_out[i].astype(BF16) for i in range(n_gdn)]
    ffn_w = [(w_ffn_gate[l].astype(BF16), w_ffn_up[l].astype(BF16), w_ffn_down[l].astype(BF16))
             for l in range(depth)]
    bias_tiles = _attn_bias_rows(rel_bias)

    def trunk(x3, k_pre, v_pre, sconv, sssm, gconv, gstate):
        bsz, seq, _ = x3.shape
        m = bsz * seq
        step = seq == 1
        tm_big = _row_tile(m, PROJ_ROW_TILE)
        tm = _row_tile(m, FFN_ROW_TILE)
        x = x3.reshape(m, d_model)
        nk, nv, nsc, nss, ngc, ngs = [], [], [], [], [], []
        rolled = None
        for l in range(depth):
            i = l // 2
            if l % 2 == 0:
                w_main, w_tail = hyb_in[i]
                ssm_args = (ssm_dt_bias[i], ssm_a_log[i], ssm_d[i], ssm_norm_w[i])
                if step:
                    main, tail = inproj(x, norm_mix_pre[l], w_main, w_tail, tm=tm_big, tn=PROJ_COL_TILE)
                    main3 = main.reshape(bsz, seq, HYB_MAIN)
                    tail3 = tail.reshape(bsz, seq, LANES)
                    col = lambda c0: main[:, c0:c0 + A_WIDTH].reshape(bsz, A_WIDTH, 1)
                    o_attn, *rolled = attention_step(
                        col(HYB_Q0), col(HYB_K0), col(HYB_V0), k_pre, v_pre, rolled,
                        _attn_logw(rel_bias, k_pre.shape[-1]), layer=i)
                    y, c_new, s_new = ssd_step(main3, tail3, sconv[i], sssm[i], ssm_conv_w[i], ssm_conv_b[i],
                                               *ssm_args)
                else:
                    main, tail, hist = inproj_conv(
                        x, norm_mix_pre[l], w_main, w_tail, _hyb_cols(ssm_conv_w[i]),
                        _hyb_cols(ssm_conv_b[i][None]), _hyb_cols(sconv[i]), seq=seq, conv_cols=HYB_CONV_COLS,
                        tm=tm_big, tn=PROJ_COL_TILE)
                    main3 = main.reshape(bsz, seq, HYB_MAIN)
                    tail3 = tail.reshape(bsz, seq, LANES)
                    c_new = jnp.concatenate([hist[..., a:b] for a, b in HYB_CONV_COLS], axis=-1)
                    o_attn = attention_prompt(main3, bias_tiles, bsz=bsz, seq=seq)
                    keep = min(A_PATTERNS[-1][0], seq)
                    k_new = main3[:, seq - keep:, HYB_K0:HYB_K0 + A_WIDTH].astype(F32)
                    v_new = main3[:, seq - keep:, HYB_V0:HYB_V0 + A_WIDTH].astype(F32)
                    y, s_new = ssd_prompt(main3, tail3, sssm[i], *ssm_args, bsz=bsz, seq=seq)
                    nk.append(k_new.reshape(bsz, -1, A_HEADS, A_HEAD_DIM))
                    nv.append(v_new.reshape(bsz, -1, A_HEADS, A_HEAD_DIM))
                nsc.append(c_new)
                nss.append(s_new)
                acts, w_outs = [o_attn.reshape(m, A_WIDTH), y.reshape(m, SSM_D_INNER)], list(hyb_out[i])
            else:
                w_main, w_tail = gdn_in[i]
                gdn_args = (gdn_dt_bias[i], gdn_a_log[i], gdn_norm_w[i])
                if step:
                    main, tail = inproj(x, norm_mix_pre[l], w_main, w_tail, tm=tm_big, tn=PROJ_COL_TILE)
                    o, c_new, s_new = gdn_step(main.reshape(bsz, seq, GDN_MAIN), tail.reshape(bsz, seq, LANES),
                                               gconv[i], gstate[i], gdn_conv_w[i], *gdn_args)
                else:
                    pad = lambda a: jnp.pad(a, [(0, 0)] * (a.ndim - 1) + [(0, GDN_MAIN - GDN_QKV)])
                    main, tail, hist = inproj_conv(
                        x, norm_mix_pre[l], w_main, w_tail, pad(gdn_conv_w[i]), jnp.zeros((1, GDN_MAIN), F32),
                        pad(gconv[i]), seq=seq, conv_cols=((0, GDN_QKV),), tm=tm_big, tn=PROJ_COL_TILE)
                    c_new = hist[..., :GDN_QKV]
                    o, s_new = gdn_prompt(main.reshape(bsz, seq, GDN_MAIN), tail.reshape(bsz, seq, LANES),
                                          gstate[i], *gdn_args, bsz=bsz, seq=seq)
                ngc.append(c_new)
                ngs.append(s_new)
                acts, w_outs = [o.reshape(m, GDN_VW)], [gdn_out[i]]
            wg, wu, wd = ffn_w[l]
            x = mix_ffn(acts, w_outs, x, norm_mix_post[l], norm_ffn_pre[l], wg, wu, wd, norm_ffn_post[l], tm=tm)
        k_out, v_out = [_window_from_lanes(r) for r in rolled] if step else (jnp.stack(nk), jnp.stack(nv))
        return (x.reshape(bsz, seq, d_model), k_out, v_out, jnp.stack(nsc), jnp.stack(nss),
                jnp.stack(ngc), jnp.stack(ngs))

    bsz = x_prompt.shape[0]
    dt_p = x_prompt.dtype
    p_sc0 = jnp.zeros((n_hyb, bsz, SSM_CONV - 1, SSM_XBC), dt_p)
    p_ss0 = jnp.zeros((n_hyb, bsz, SSM_HEADS, SSM_HEAD_DIM, SSM_STATE), F32)
    p_gc0 = jnp.zeros((n_gdn, bsz, GDN_CONV - 1, GDN_QKV), dt_p)
    p_gs0 = jnp.zeros((n_gdn, bsz, GDN_V_HEADS, GDN_DK, GDN_DV), F32)
    y_prompt, pk, pv, psc, pss, pgc, pgs = trunk(x_prompt, None, None, p_sc0, p_ss0, p_gc0, p_gs0)
    y_sample, sk, sv, ssc, sss, sgc, sgs = trunk(
        x_sample, _window_to_lanes(cache_attn_k), _window_to_lanes(cache_attn_v), state_ssm_conv, state_ssm,
        state_gdn_conv, state_gdn)
    return (y_prompt, y_sample, pk, pv, psc, pss, pgc, pgs, sk, sv, ssc, sss, sgc, sgs)
```
